```python
import math
import jax, jax.numpy as jnp
from jax import lax
import numpy as np

D_MODEL = 1024
BATCH = 2
SEQ = 8192
DEPTH = 2

GRID_W = 64
CTX_LEN = 256
EPS = 1e-6
N_MOD = 6

FNET_GROUPS = 4
FNET_GROUP_DIM = 128
FNET_WIDTH = FNET_GROUPS * FNET_GROUP_DIM

SSD_HEADS = 16
SSD_HEAD_DIM = 64
SSD_INNER = SSD_HEADS * SSD_HEAD_DIM
SSD_GROUPS = 4
SSD_STATE = 128
SSD_CONV = 5
SSD_CHUNK = 128
SSD_CONV_DIM = SSD_INNER + 2 * SSD_GROUPS * SSD_STATE
EVEN_IN = FNET_WIDTH + SSD_INNER + SSD_CONV_DIM + 2 * SSD_HEADS
EVEN_OUT = FNET_WIDTH + SSD_INNER

MLA_HEADS = 16
MLA_NOPE = 64
MLA_ROPE = 32
MLA_QK = MLA_NOPE + MLA_ROPE
MLA_V = 64
MLA_Q_LORA = 384
MLA_KV_LORA = 256
ODD_IN = MLA_Q_LORA + MLA_KV_LORA + MLA_ROPE
ROPE_THETA = 10000.0
Q_BLOCK = 128

PEER_HEADS = 8
PEER_KEYS = 128
PEER_EXPERTS = PEER_KEYS * PEER_KEYS
PEER_QDIM = 256
PEER_TOPK = 16
PEER_TOKEN_BLOCK = 128

kernel_name = "hybrid_fnet_ssd_mla_peer_dit_block"


def rms_norm(x, g):
    xf = x.astype(jnp.float32)
    y = xf * lax.rsqrt(jnp.mean(xf * xf, axis=-1, keepdims=True) + EPS)
    return (y * g.astype(jnp.float32)).astype(x.dtype)


def ada_params(cond, w_ada, b_ada):
    mod = jax.nn.silu(cond) @ w_ada + b_ada
    return jnp.split(mod[..., None, :], N_MOD, axis=-1)


def modulate(x, g, shift, scale):
    return rms_norm(x, g) * (1 + scale) + shift


def fourier_mix(f):
    b, l, _ = f.shape
    fg = f.astype(jnp.float32).reshape(b, l, FNET_GROUPS, FNET_GROUP_DIM)
    out = jnp.fft.fft2(fg, axes=(1, 3), norm="ortho").real
    return out.reshape(b, l, FNET_WIDTH).astype(f.dtype)


def centred_dwconv(u, w, bias):
    pad = SSD_CONV // 2
    y = lax.conv_general_dilated(
        u, w[:, None, :].astype(u.dtype), window_strides=(1,), padding=[(pad, pad)],
        dimension_numbers=("NWC", "WIO", "NWC"), feature_group_count=u.shape[-1])
    return y + bias


def ssd_chunked(x, dt, a, bm, cm, h0):
    b, l, h, p = x.shape
    g, n = bm.shape[2], bm.shape[3]
    r = h // g
    nc, cl = l // SSD_CHUNK, SSD_CHUNK
    xd = (x.astype(jnp.float32) * dt[..., None]).reshape(b, nc, cl, g, r, p)
    a_cs = jnp.cumsum((dt * a).reshape(b, nc, cl, g, r), axis=2)
    bc = bm.astype(jnp.float32).reshape(b, nc, cl, g, n)
    cc = cm.astype(jnp.float32).reshape(b, nc, cl, g, n)
    tri = (jnp.arange(cl)[:, None] >= jnp.arange(cl)[None, :])[None, None, :, :, None, None]
    seg = a_cs[:, :, :, None] - a_cs[:, :, None, :]
    decay = jnp.exp(jnp.where(tri, seg, -jnp.inf))
    cb = jnp.einsum("bclgn,bcsgn->bclsg", cc, bc)
    y_diag = jnp.einsum("bclsgr,bcsgrp->bclgrp", cb[..., None] * decay, xd)
    to_end = jnp.exp(a_cs[:, :, -1:] - a_cs)
    states = jnp.einsum("bclgn,bclgr,bclgrp->bcgrpn", bc, to_end, xd)
    chunk_decay = jnp.exp(a_cs[:, :, -1])

    def step(hs, inp):
        s, d = inp
        return hs * d[..., None, None] + s, hs

    h_fin, h_prev = lax.scan(step, h0, (jnp.moveaxis(states, 1, 0), jnp.moveaxis(chunk_decay, 1, 0)))
    h_prev = jnp.moveaxis(h_prev, 0, 1)
    y_off = jnp.einsum("bclgn,bcgrpn,bclgr->bclgrp", cc, h_prev, jnp.exp(a_cs))
    return (y_diag + y_off).reshape(b, l, h, p), h_fin


def split_xbc(u):
    b, l, _ = u.shape
    gn = SSD_GROUPS * SSD_STATE
    xs = u[..., :SSD_INNER].reshape(b, l, SSD_HEADS, SSD_HEAD_DIM)
    bm = u[..., SSD_INNER:SSD_INNER + gn].reshape(b, l, SSD_GROUPS, SSD_STATE)
    cm = u[..., SSD_INNER + gn:].reshape(b, l, SSD_GROUPS, SSD_STATE)
    return xs, bm, cm


def ssd_bidir(u_ctx, u_lat, dt_ctx, dt_lat, p):
    a = -jnp.exp(p["a_log"].astype(jnp.float32))
    dt_bias = p["dt_bias"].astype(jnp.float32)
    d_skip = p["d_skip"].astype(jnp.float32)
    xc, bc, cc = split_xbc(u_ctx)
    xl, bl, cl = split_xbc(u_lat)
    b = u_lat.shape[0]
    y_ctx, y_lat = 0.0, 0.0
    for d in range(2):
        flip = (lambda t: jnp.flip(t, axis=1)) if d == 1 else (lambda t: t)
        dtc = jax.nn.softplus(dt_ctx[:, :, d].astype(jnp.float32) + dt_bias[d])
        dtl = jax.nn.softplus(dt_lat[:, :, d].astype(jnp.float32) + dt_bias[d])
        h0 = jnp.zeros((b, SSD_GROUPS, SSD_HEADS // SSD_GROUPS, SSD_HEAD_DIM, SSD_STATE), jnp.float32)
        yc, hc = ssd_chunked(flip(xc), flip(dtc), a[d], flip(bc), flip(cc), h0)
        yl, _ = ssd_chunked(flip(xl), flip(dtl), a[d], flip(bl), flip(cl), hc)
        y_ctx = y_ctx + flip(yc) + d_skip[d][:, None] * xc.astype(jnp.float32)
        y_lat = y_lat + flip(yl) + d_skip[d][:, None] * xl.astype(jnp.float32)
    return y_ctx, y_lat


def even_mixer(h_lat, h_ctx, p, need_ctx):
    def parts(h):
        pr = h @ p["w_in"]
        o1 = FNET_WIDTH
        o2 = o1 + SSD_INNER
        o3 = o2 + SSD_CONV_DIM
        dt = pr[..., o3:].reshape(pr.shape[0], pr.shape[1], 2, SSD_HEADS)
        u = jax.nn.silu(centred_dwconv(pr[..., o2:o3], p["conv_w"], p["conv_b"]))
        return pr[..., :o1], pr[..., o1:o2], u, dt

    f_l, z_l, u_l, dt_l = parts(h_lat)
    f_c, z_c, u_c, dt_c = parts(h_ctx)
    y_c, y_l = ssd_bidir(u_c, u_l, dt_c, dt_l, p)

    def ssd_out(y, z):
        b, l = z.shape[:2]
        gated = y.reshape(b, l, SSD_INNER) * jax.nn.silu(z.astype(jnp.float32))
        gs = SSD_INNER // SSD_GROUPS
        normed = rms_norm(gated.reshape(b, l, SSD_GROUPS, gs), p["ssd_norm"].reshape(SSD_GROUPS, gs))
        return normed.reshape(b, l, SSD_INNER).astype(z.dtype)

    def merge(f, y, z):
        return jnp.concatenate([fourier_mix(f), ssd_out(y, z)], axis=-1) @ p["w_out"]

    out_lat = merge(f_l, y_l, z_l)
    out_ctx = merge(f_c, y_c, z_c) if need_ctx else None
    return out_lat, out_ctx


def axial_rope(n_rows):
    rows = jnp.repeat(jnp.arange(n_rows), GRID_W)
    cols = jnp.tile(jnp.arange(GRID_W), n_rows)
    per_axis = MLA_ROPE // 2
    inv = ROPE_THETA ** (-jnp.arange(0, per_axis, 2, dtype=jnp.float32) / per_axis)
    ang = jnp.concatenate([rows[:, None] * inv, cols[:, None] * inv], axis=-1)
    return jnp.cos(ang), jnp.sin(ang)


def apply_rope(t, cos, sin):
    nope, rp = t[..., :MLA_NOPE], t[..., MLA_NOPE:]
    half = MLA_ROPE // 2
    x1, x2 = rp[..., :half], rp[..., half:]
    c, s = cos[:, None, :], sin[:, None, :]
    rot = jnp.concatenate([x1 * c - x2 * s, x1 * s + x2 * c], axis=-1)
    return jnp.concatenate([nope, rot.astype(t.dtype)], axis=-1)


def mla_q(dq, p, rope):
    b, l, _ = dq.shape
    q = (rms_norm(dq, p["q_a_norm"]) @ p["w_uq"]).reshape(b, l, MLA_HEADS, MLA_QK)
    q = rms_norm(q, p["q_norm"])
    return q if rope is None else apply_rope(q, *rope)


def mla_kv(dkv, p, rope):
    b, l, _ = dkv.shape
    kv = (rms_norm(dkv[..., :MLA_KV_LORA], p["kv_a_norm"]) @ p["w_ukv"]).reshape(b, l, MLA_HEADS, MLA_NOPE + MLA_V)
    k_pe = jnp.broadcast_to(dkv[..., None, MLA_KV_LORA:], (b, l, MLA_HEADS, MLA_ROPE))
    k = rms_norm(jnp.concatenate([kv[..., :MLA_NOPE], k_pe], axis=-1), p["k_norm"])
    if rope is not None:
        k = apply_rope(k, *rope)
    return k, kv[..., MLA_NOPE:]


def softmax_attend(q, k, v):
    s = jnp.einsum("bqhd,bkhd->bhqk", q, k).astype(jnp.float32) * (MLA_QK ** -0.5)
    pr = jax.nn.softmax(s, axis=-1).astype(v.dtype)
    return jnp.einsum("bhqk,bkhd->bqhd", pr, v)


def odd_mixer(h_lat, h_ctx, p, rope, need_ctx):
    d_l = h_lat @ p["w_dqkv"]
    q_l = mla_q(d_l[..., :MLA_Q_LORA], p, rope)
    k_l, v_l = mla_kv(d_l[..., MLA_Q_LORA:], p, rope)
    if need_ctx:
        d_c = h_ctx @ p["w_dqkv"]
        q_c = mla_q(d_c[..., :MLA_Q_LORA], p, None)
        k_c, v_c = mla_kv(d_c[..., MLA_Q_LORA:], p, None)
    else:
        k_c, v_c = mla_kv(h_ctx @ p["w_dqkv"][:, MLA_Q_LORA:], p, None)
    k_all = jnp.concatenate([k_c, k_l], axis=1)
    v_all = jnp.concatenate([v_c, v_l], axis=1)
    b, l = q_l.shape[:2]
    nb = l // Q_BLOCK
    qb = jnp.moveaxis(q_l.reshape(b, nb, Q_BLOCK, MLA_HEADS, MLA_QK), 1, 0)
    o = lax.map(lambda qi: softmax_attend(qi, k_all, v_all), qb)
    out_lat = jnp.moveaxis(o, 0, 1).reshape(b, l, MLA_HEADS * MLA_V) @ p["w_o"]
    out_ctx = None
    if need_ctx:
        o_c = softmax_attend(q_c, k_c, v_c)
        out_ctx = o_c.reshape(b, -1, MLA_HEADS * MLA_V) @ p["w_o"]
    return out_lat, out_ctx


def peer(h, p):
    t, dm = h.shape
    blocks = h.reshape(t // PEER_TOKEN_BLOCK, PEER_TOKEN_BLOCK, dm)

    def one(hb):
        q = (hb @ p["peer_wq"]).reshape(PEER_TOKEN_BLOCK, PEER_HEADS, 2, PEER_QDIM // 2)
        s = jnp.einsum("thsd,hskd->thsk", q, p["peer_keys"]).astype(jnp.float32)
        s_top, i_top = lax.top_k(s, PEER_TOPK)
        cand = (s_top[:, :, 0, :, None] + s_top[:, :, 1, None, :]).reshape(
            PEER_TOKEN_BLOCK, PEER_HEADS, PEER_TOPK * PEER_TOPK)
        g_score, c_idx = lax.top_k(cand, PEER_TOPK)
        e1 = jnp.take_along_axis(i_top[:, :, 0], c_idx // PEER_TOPK, axis=-1)
        e2 = jnp.take_along_axis(i_top[:, :, 1], c_idx % PEER_TOPK, axis=-1)
        experts = e1 * PEER_KEYS + e2
        gates = jax.nn.softmax(g_score, axis=-1)
        u = jnp.take(p["peer_u"], experts, axis=0)
        act = jax.nn.gelu(jnp.einsum("thkd,td->thk", u, hb).astype(jnp.float32), approximate=False) * gates
        v = jnp.take(p["peer_v"], experts, axis=0)
        return jnp.einsum("thk,thkd->td", act.astype(v.dtype), v)

    return lax.map(one, blocks).reshape(t, dm)


def setup_inputs(seed: int = 0) -> dict:
    key = jax.random.key(seed)
    ks = iter(jax.random.split(key, 64))

    def nrm(shape, scale):
        return jax.random.normal(next(ks), shape, jnp.float32) * scale

    def gain(n):
        return 1.0 + nrm((n,), 0.02)

    inp = {}
    inp["x"] = nrm((BATCH, SEQ, D_MODEL), 1.0)
    inp["c"] = nrm((BATCH, D_MODEL), 1.0)
    inp["ctx"] = nrm((BATCH, CTX_LEN, D_MODEL), 1.0)
    inp["c_ctx"] = nrm((D_MODEL,), 1.0)
    for i in range(DEPTH):
        pre = "l%d_" % i
        inp[pre + "w_ada"] = nrm((D_MODEL, N_MOD * D_MODEL), 0.5 * D_MODEL ** -0.5)
        inp[pre + "b_ada"] = nrm((N_MOD * D_MODEL,), 0.02)
        inp[pre + "norm1"] = gain(D_MODEL)
        if i % 2 == 0:
            inp[pre + "w_in"] = nrm((D_MODEL, EVEN_IN), D_MODEL ** -0.5)
            inp[pre + "conv_w"] = nrm((SSD_CONV, SSD_CONV_DIM), SSD_CONV ** -0.5)
            inp[pre + "conv_b"] = nrm((SSD_CONV_DIM,), 0.02)
            dt0 = jnp.exp(jax.random.uniform(next(ks), (2, SSD_HEADS)) * (math.log(0.1) - math.log(0.001))
                          + math.log(0.001))
            inp[pre + "dt_bias"] = dt0 + jnp.log(-jnp.expm1(-dt0))
            inp[pre + "a_log"] = jnp.log(jax.random.uniform(next(ks), (2, SSD_HEADS), minval=1.0, maxval=16.0))
            inp[pre + "d_skip"] = 1.0 + nrm((2, SSD_HEADS), 0.1)
            inp[pre + "ssd_norm"] = gain(SSD_INNER)
            inp[pre + "w_out"] = nrm((EVEN_OUT, D_MODEL), EVEN_OUT ** -0.5)
        else:
            inp[pre + "w_dqkv"] = nrm((D_MODEL, ODD_IN), D_MODEL ** -0.5)
            inp[pre + "q_a_norm"] = gain(MLA_Q_LORA)
            inp[pre + "kv_a_norm"] = gain(MLA_KV_LORA)
            inp[pre + "w_uq"] = nrm((MLA_Q_LORA, MLA_HEADS * MLA_QK), MLA_Q_LORA ** -0.5)
            inp[pre + "w_ukv"] = nrm((MLA_KV_LORA, MLA_HEADS * (MLA_NOPE + MLA_V)), MLA_KV_LORA ** -0.5)
            inp[pre + "q_norm"] = gain(MLA_QK)
            inp[pre + "k_norm"] = gain(MLA_QK)
            inp[pre + "w_o"] = nrm((MLA_HEADS * MLA_V, D_MODEL), (MLA_HEADS * MLA_V) ** -0.5)
        inp[pre + "norm2"] = gain(D_MODEL)
        inp[pre + "peer_wq"] = nrm((D_MODEL, PEER_HEADS * PEER_QDIM), D_MODEL ** -0.5)
        inp[pre + "peer_keys"] = nrm((PEER_HEADS, 2, PEER_KEYS, PEER_QDIM // 2), (PEER_QDIM // 2) ** -0.5)
        inp[pre + "peer_u"] = nrm((PEER_EXPERTS, D_MODEL), D_MODEL ** -0.5)
        inp[pre + "peer_v"] = nrm((PEER_EXPERTS, D_MODEL), 1.0)
    return inp


def reference(x, c, ctx, c_ctx,
              l0_w_ada, l0_b_ada, l0_norm1, l0_w_in, l0_conv_w, l0_conv_b, l0_dt_bias, l0_a_log,
              l0_d_skip, l0_ssd_norm, l0_w_out, l0_norm2, l0_peer_wq, l0_peer_keys, l0_peer_u, l0_peer_v,
              l1_w_ada, l1_b_ada, l1_norm1, l1_w_dqkv, l1_q_a_norm, l1_kv_a_norm, l1_w_uq, l1_w_ukv,
              l1_q_norm, l1_k_norm, l1_w_o, l1_norm2, l1_peer_wq, l1_peer_keys, l1_peer_u, l1_peer_v):
    layers = (
        dict(w_ada=l0_w_ada, b_ada=l0_b_ada, norm1=l0_norm1, w_in=l0_w_in, conv_w=l0_conv_w, conv_b=l0_conv_b,
             dt_bias=l0_dt_bias, a_log=l0_a_log, d_skip=l0_d_skip, ssd_norm=l0_ssd_norm, w_out=l0_w_out,
             norm2=l0_norm2, peer_wq=l0_peer_wq, peer_keys=l0_peer_keys, peer_u=l0_peer_u, peer_v=l0_peer_v),
        dict(w_ada=l1_w_ada, b_ada=l1_b_ada, norm1=l1_norm1, w_dqkv=l1_w_dqkv, q_a_norm=l1_q_a_norm,
             kv_a_norm=l1_kv_a_norm, w_uq=l1_w_uq, w_ukv=l1_w_ukv, q_norm=l1_q_norm, k_norm=l1_k_norm,
             w_o=l1_w_o, norm2=l1_norm2, peer_wq=l1_peer_wq, peer_keys=l1_peer_keys, peer_u=l1_peer_u,
             peer_v=l1_peer_v),
    )
    b, l, dm = x.shape
    n_rows = l // GRID_W
    rope = axial_rope(n_rows)
    x_lat, x_ctx = x, ctx
    for i in range(DEPTH):
        p = layers[i]
        need_ctx = i < DEPTH - 1
        sh1, sc1, g1, sh2, sc2, g2 = ada_params(c, p["w_ada"], p["b_ada"])
        csh1, csc1, cg1, csh2, csc2, cg2 = ada_params(c_ctx, p["w_ada"], p["b_ada"])
        h_lat = modulate(x_lat, p["norm1"], sh1, sc1)
        h_ctx = modulate(x_ctx, p["norm1"], csh1, csc1)
        if i % 2 == 0:
            y_lat, y_ctx = even_mixer(h_lat, h_ctx, p, need_ctx)
        else:
            y_lat, y_ctx = odd_mixer(h_lat, h_ctx, p, rope, need_ctx)
        x_lat = x_lat + g1 * y_lat
        h2_lat = modulate(x_lat, p["norm2"], sh2, sc2)
        if need_ctx:
            x_ctx = x_ctx + cg1 * y_ctx
            h2_ctx = modulate(x_ctx, p["norm2"], csh2, csc2)
            tokens = jnp.concatenate([h2_lat.reshape(-1, dm), h2_ctx.reshape(-1, dm)], axis=0)
            ff = peer(tokens, p)
            x_lat = x_lat + g2 * ff[: b * l].reshape(b, l, dm)
            x_ctx = x_ctx + cg2 * ff[b * l:].reshape(b, -1, dm)
        else:
            x_lat = x_lat + g2 * peer(h2_lat.reshape(-1, dm), p).reshape(b, l, dm)
    return x_lat
```

```python
import functools
import math

import numpy as np
import jax
import jax.numpy as jnp
from jax import lax
from jax.experimental import pallas as pl
from jax.experimental.pallas import tpu as pltpu

D_MODEL = 1024
BATCH = 2
SEQ = 8192
GRID_W = 64
CTX_LEN = 256
EPS = 1e-6
N_MOD = 6
LAT_ROWS = BATCH * SEQ
ALL_ROWS = LAT_ROWS + BATCH * CTX_LEN

FNET_GROUPS = 4
FNET_GROUP_DIM = 128
FNET_WIDTH = FNET_GROUPS * FNET_GROUP_DIM
FFT_N1 = 64
FFT_N2 = 128

SSD_HEADS = 16
SSD_HEAD_DIM = 64
SSD_INNER = SSD_HEADS * SSD_HEAD_DIM
SSD_GROUPS = 4
SSD_STATE = 128
SSD_CONV = 5
SSD_CHUNK = 128
SSD_CONV_DIM = SSD_INNER + 2 * SSD_GROUPS * SSD_STATE
SSD_GROUP_W = SSD_INNER // SSD_GROUPS

MLA_HEADS = 16
MLA_NOPE = 64
MLA_ROPE = 32
MLA_QK = MLA_NOPE + MLA_ROPE
MLA_V = 64
MLA_Q_LORA = 384
MLA_KV_LORA = 256
ROPE_THETA = 10000.0
HEAD_PAD = 128

PEER_HEADS = 8
PEER_KEYS = 128
PEER_EXPERTS = PEER_KEYS * PEER_KEYS
PEER_QDIM = 256
PEER_TOPK = 16

LANES = 128
SUBLANES = 8
VMEM_LIMIT = 56 * 1024 * 1024

ROW_TILE = 256
PEER_TILE = 256
PEER_CHUNK = 512
ATTN_TQ = 512
ATTN_TK = 512

F32 = jnp.float32
BF16 = jnp.bfloat16
HIGHEST = lax.Precision.HIGHEST
NEG_INF = float("-inf")


def _params(*sem):
    return pltpu.CompilerParams(dimension_semantics=sem, vmem_limit_bytes=VMEM_LIMIT)


def _mod_row(i, tile):
    return jnp.minimum((i * tile) // SEQ, BATCH)


def _full(shape):
    return pl.BlockSpec(shape, lambda *_: (0,) * len(shape))


def _silu(x):
    return x * (1.0 / (1.0 + jnp.exp(-x)))


def _modulated_norm(x, gain, mod, k_shift, k_scale):
    shift = mod[:, k_shift * D_MODEL:(k_shift + 1) * D_MODEL]
    scale = mod[:, k_scale * D_MODEL:(k_scale + 1) * D_MODEL]
    ms = jnp.mean(x * x, axis=-1, keepdims=True)
    return x * lax.rsqrt(ms + EPS) * gain * (1.0 + scale) + shift


def _bdot(a, b):
    return jnp.dot(a.astype(BF16), b.astype(BF16), preferred_element_type=F32)


def _hdot(a, b):
    return jnp.dot(a, b, precision=HIGHEST, preferred_element_type=F32)


def _ada_kernel(c_ref, w_ref, b_ref, o_ref):
    o_ref[...] = _hdot(_silu(c_ref[...]), w_ref[...]) + b_ref[...]


def ada_table(c, c_ctx, w_ada, b_ada):
    cond = jnp.concatenate([c, c_ctx[None, :], jnp.zeros((SUBLANES - BATCH - 1, D_MODEL), F32)], axis=0)
    tn = 512
    out = pl.pallas_call(
        _ada_kernel,
        grid=(N_MOD * D_MODEL // tn,),
        in_specs=[_full((SUBLANES, D_MODEL)),
                  pl.BlockSpec((D_MODEL, tn), lambda j: (0, j)),
                  pl.BlockSpec((1, tn), lambda j: (0, j))],
        out_specs=pl.BlockSpec((SUBLANES, tn), lambda j: (0, j)),
        out_shape=jax.ShapeDtypeStruct((SUBLANES, N_MOD * D_MODEL), F32),
        compiler_params=_params("arbitrary"),
        name="ada_table",
    )(cond, w_ada, b_ada[None, :])
    return out[:BATCH + 1].reshape(BATCH + 1, 1, N_MOD * D_MODEL)


def _even_in_kernel(x_ref, mod_ref, g_ref, wf_ref, wz_ref, wx_ref, wd_ref, f_ref, z_ref, xbc_ref, dt_ref):
    h = _modulated_norm(x_ref[...], g_ref[...], mod_ref[0], 0, 1).astype(BF16)
    f_ref[...] = jnp.dot(h, wf_ref[...], preferred_element_type=F32)
    z_ref[...] = jnp.dot(h, wz_ref[...], preferred_element_type=F32)
    xbc_ref[...] = jnp.dot(h, wx_ref[...], preferred_element_type=F32)
    dt_ref[...] = jnp.dot(h, wd_ref[...], preferred_element_type=F32)


def even_in(x_all, mod, norm1, w_in):
    rows = x_all.shape[0]
    o1, o2, o3 = FNET_WIDTH, FNET_WIDTH + SSD_INNER, FNET_WIDTH + SSD_INNER + SSD_CONV_DIM
    wf = w_in[:, :o1].astype(BF16)
    wz = w_in[:, o1:o2].astype(BF16)
    wx = w_in[:, o2:o3].astype(BF16)
    wd = jnp.pad(w_in[:, o3:], ((0, 0), (0, LANES - 2 * SSD_HEADS))).astype(BF16)
    tm = ROW_TILE
    row = lambda n: pl.BlockSpec((tm, n), lambda i: (i, 0))
    return pl.pallas_call(
        _even_in_kernel,
        grid=(rows // tm,),
        in_specs=[row(D_MODEL),
                  pl.BlockSpec((1, 1, N_MOD * D_MODEL), lambda i: (_mod_row(i, tm), 0, 0)),
                  _full((1, D_MODEL)),
                  _full(wf.shape), _full(wz.shape), _full(wx.shape), _full(wd.shape)],
        out_specs=[row(FNET_WIDTH), row(SSD_INNER), row(SSD_CONV_DIM), row(LANES)],
        out_shape=[jax.ShapeDtypeStruct((rows, FNET_WIDTH), F32),
                   jax.ShapeDtypeStruct((rows, SSD_INNER), F32),
                   jax.ShapeDtypeStruct((rows, SSD_CONV_DIM), F32),
                   jax.ShapeDtypeStruct((rows, LANES), F32)],
        compiler_params=_params("arbitrary"),
        name="even_in",
    )(x_all, mod, norm1[None, :], wf, wz, wx, wd)


def _conv_kernel(x_ref, prev_ref, next_ref, w_ref, b_ref, o_ref, ext_ref, *, tm):
    row0 = pl.program_id(0) * tm
    in_lat = row0 < LAT_ROWS
    first = jnp.where(in_lat, row0 % SEQ == 0, (row0 - LAT_ROWS) % CTX_LEN == 0)
    last = jnp.where(in_lat, (row0 + tm) % SEQ == 0, (row0 + tm - LAT_ROWS) % CTX_LEN == 0)
    ext_ref[0:SUBLANES, :] = prev_ref[...] * jnp.where(first, 0.0, 1.0)
    ext_ref[SUBLANES:SUBLANES + tm, :] = x_ref[...]
    ext_ref[SUBLANES + tm:2 * SUBLANES + tm, :] = next_ref[...] * jnp.where(last, 0.0, 1.0)
    pad = SSD_CONV // 2
    acc = b_ref[...] + w_ref[0:1, :] * ext_ref[pl.ds(SUBLANES - pad, tm), :]
    for k in range(1, SSD_CONV):
        acc = acc + w_ref[k:k + 1, :] * ext_ref[pl.ds(SUBLANES - pad + k, tm), :]
    o_ref[...] = _silu(acc)


def conv_silu(xbc, conv_w, conv_b):
    rows, ch = xbc.shape
    tm, tc = ROW_TILE, 1024
    assert CTX_LEN % tm == 0 and SEQ % tm == 0
    per = tm // SUBLANES
    n_small = rows // SUBLANES
    w = jnp.pad(conv_w, ((0, SUBLANES - SSD_CONV), (0, 0)))
    return pl.pallas_call(
        functools.partial(_conv_kernel, tm=tm),
        grid=(rows // tm, ch // tc),
        in_specs=[pl.BlockSpec((tm, tc), lambda i, j: (i, j)),
                  pl.BlockSpec((SUBLANES, tc), lambda i, j: (jnp.maximum(i * per - 1, 0), j)),
                  pl.BlockSpec((SUBLANES, tc), lambda i, j: (jnp.minimum((i + 1) * per, n_small - 1), j)),
                  pl.BlockSpec((SUBLANES, tc), lambda i, j: (0, j)),
                  pl.BlockSpec((1, tc), lambda i, j: (0, j))],
        out_specs=pl.BlockSpec((tm, tc), lambda i, j: (i, j)),
        out_shape=jax.ShapeDtypeStruct((rows, ch), F32),
        scratch_shapes=[pltpu.VMEM((tm + 2 * SUBLANES, tc), F32)],
        compiler_params=_params("arbitrary", "arbitrary"),
        name="conv_silu",
    )(xbc, xbc, xbc, w, conv_b[None, :])


def _softplus(x):
    return jnp.maximum(x, 0.0) + jnp.log1p(jnp.exp(-jnp.abs(x)))


def _ssd_direction(u_ref, dt_ref, dtb_ref, alog_ref, exp_ref, state_ref, y_ref, direction):
    cl = SSD_CHUNK
    xs = u_ref[:, :SSD_INNER]
    dtv = _softplus(dt_ref[...] + dtb_ref[...])
    dta = dtv * (-jnp.exp(alog_ref[...]))
    r = lax.broadcasted_iota(jnp.int32, (cl, cl), 0)
    c = lax.broadcasted_iota(jnp.int32, (cl, cl), 1)
    tri = (r >= c) if direction == 0 else (r <= c)
    cs = _hdot(tri.astype(F32), dta)
    cs_t = cs.T
    edge = cs[cl - 1:cl, :] if direction == 0 else cs[0:1, :]
    to_end = jnp.exp(edge - cs)
    from_start = jnp.exp(cs)
    spread = _hdot(jnp.concatenate([dtv, to_end, from_start], axis=0), exp_ref[direction])
    dt_x, te_x, fs_x = spread[:cl], spread[cl:2 * cl], spread[2 * cl:]
    chunk_decay = fs_x[cl - 1:cl, :] if direction == 0 else fs_x[0:1, :]
    xd = xs * dt_x
    xd_b = xd.astype(BF16)
    xte_b = (xd * te_x).astype(BF16)
    lane = lax.broadcasted_iota(jnp.int32, (cl, LANES), 1)
    low = lane < SSD_HEAD_DIM
    for g in range(SSD_GROUPS):
        bg = u_ref[:, SSD_INNER + g * SSD_STATE:SSD_INNER + (g + 1) * SSD_STATE]
        cg = u_ref[:, SSD_INNER + (SSD_GROUPS + g) * SSD_STATE:SSD_INNER + (SSD_GROUPS + g + 1) * SSD_STATE]
        bg_b, cg_b = bg.astype(BF16), cg.astype(BF16)
        cb = lax.dot_general(cg_b, bg_b, (((1,), (1,)), ((), ())), preferred_element_type=F32)
        gs = slice(g * SSD_GROUP_W, (g + 1) * SSD_GROUP_W)
        state = state_ref[g]
        y_off = jnp.dot(cg_b, state.astype(BF16), preferred_element_type=F32) * fs_x[:, gs]
        new_state = jnp.dot(bg.T.astype(BF16), xte_b[:, gs], preferred_element_type=F32)
        state_ref[g] = state * chunk_decay[:, gs] + new_state
        for pair in range(2):
            blk = xd_b[:, g * SSD_GROUP_W + pair * LANES:g * SSD_GROUP_W + (pair + 1) * LANES]
            y_pair = None
            for sub in range(2):
                col = direction * SSD_HEADS + g * 4 + pair * 2 + sub
                seg = cs[:, col:col + 1] - cs_t[col:col + 1, :]
                decay = jnp.exp(jnp.where(tri, seg, NEG_INF))
                m = (cb * decay).astype(BF16)
                half = jnp.where(low if sub == 0 else jnp.logical_not(low), blk, jnp.zeros_like(blk))
                part = jnp.dot(m, half, preferred_element_type=F32)
                y_pair = part if y_pair is None else y_pair + part
            lo = g * SSD_GROUP_W + pair * LANES
            y_ref[:, lo:lo + LANES] = y_pair + y_off[:, pair * LANES:(pair + 1) * LANES]


def _ssd_kernel(uf_ref, ub_ref, dtf_ref, dtb_in_ref, bias_ref, alog_ref, exp_ref, yf_ref, yb_ref, sf_ref, sb_ref):
    @pl.when(pl.program_id(1) == 0)
    def _():
        sf_ref[...] = jnp.zeros_like(sf_ref)
        sb_ref[...] = jnp.zeros_like(sb_ref)

    _ssd_direction(uf_ref, dtf_ref, bias_ref, alog_ref, exp_ref, sf_ref, yf_ref, 0)
    _ssd_direction(ub_ref, dtb_in_ref, bias_ref, alog_ref, exp_ref, sb_ref, yb_ref, 1)


def ssd_scan(u, dt, dt_bias, a_log):
    rows = u.shape[0]
    cl = SSD_CHUNK
    lat_chunks, ctx_chunks = SEQ // cl, CTX_LEN // cl
    steps = ctx_chunks + lat_chunks
    ctx0 = LAT_ROWS // cl

    def fwd_chunk(b, t):
        return jnp.where(t < ctx_chunks, ctx0 + b * ctx_chunks + t, b * lat_chunks + t - ctx_chunks)

    def bwd_chunk(b, t):
        return jnp.where(t < ctx_chunks, ctx0 + b * ctx_chunks + (ctx_chunks - 1 - t),
                         b * lat_chunks + (lat_chunks - 1) - (t - ctx_chunks))

    pad = LANES - 2 * SSD_HEADS
    bias = jnp.pad(dt_bias.reshape(1, -1), ((0, 0), (0, pad)))
    alog = jnp.pad(a_log.reshape(1, -1), ((0, 0), (0, pad)))
    expand = np.zeros((2, LANES, SSD_INNER), np.float32)
    for d in range(2):
        for h in range(SSD_HEADS):
            expand[d, d * SSD_HEADS + h, h * SSD_HEAD_DIM:(h + 1) * SSD_HEAD_DIM] = 1.0
    spec = lambda n, fn: pl.BlockSpec((cl, n), lambda b, t: (fn(b, t), 0))
    return pl.pallas_call(
        _ssd_kernel,
        grid=(BATCH, steps),
        in_specs=[spec(SSD_CONV_DIM, fwd_chunk), spec(SSD_CONV_DIM, bwd_chunk),
                  spec(LANES, fwd_chunk), spec(LANES, bwd_chunk),
                  _full((1, LANES)), _full((1, LANES)), _full((2, LANES, SSD_INNER))],
        out_specs=[spec(SSD_INNER, fwd_chunk), spec(SSD_INNER, bwd_chunk)],
        out_shape=[jax.ShapeDtypeStruct((rows, SSD_INNER), F32)] * 2,
        scratch_shapes=[pltpu.VMEM((SSD_GROUPS, SSD_STATE, SSD_GROUP_W), F32)] * 2,
        compiler_params=_params("arbitrary", "arbitrary"),
        name="ssd_scan",
    )(u, u, dt, dt, bias, alog, jnp.asarray(expand))


def _dft_cos_sin(n):
    k = np.arange(n)
    ang = 2.0 * np.pi * ((k[:, None] * k[None, :]) % n) / n
    return np.cos(ang), np.sin(ang)


def _fft1_kernel(x_ref, m_ref, o_ref):
    o_ref[0] = _hdot(m_ref[...], x_ref[...])


def _fft2_kernel(ar_ref, ai_ref, twr_ref, twi_ref, m2_ref, mc_ref, o_ref):
    ar, ai = ar_ref[0, 0], ai_ref[0, 0]
    twr = jnp.concatenate([twr_ref[0]] * FNET_GROUPS, axis=1)
    twi = jnp.concatenate([twi_ref[0]] * FNET_GROUPS, axis=1)
    p = ar * twr - ai * twi
    q = ar * twi + ai * twr
    uv = _hdot(m2_ref[...], jnp.concatenate([p, q], axis=0))
    n2 = FFT_N2
    for g in range(FNET_GROUPS):
        gs = slice(g * FNET_GROUP_DIM, (g + 1) * FNET_GROUP_DIM)
        o_ref[0, :, gs] = _hdot(jnp.concatenate([uv[:n2, gs], uv[n2:, gs]], axis=1), mc_ref[...])


def _fft_ctx_kernel(x_ref, mc_ref, mp_ref, o_ref):
    x = x_ref[...]
    for g in range(FNET_GROUPS):
        gs = slice(g * FNET_GROUP_DIM, (g + 1) * FNET_GROUP_DIM)
        cs = _hdot(x[:, gs], mc_ref[...])
        stacked = jnp.concatenate([cs[:, :FNET_GROUP_DIM], cs[:, FNET_GROUP_DIM:]], axis=0)
        o_ref[:, gs] = _hdot(mp_ref[...], stacked)


def fourier_mix(f_all):
    rows = f_all.shape[0]
    n1, n2, gd = FFT_N1, FFT_N2, FNET_GROUP_DIM
    row_w = n2 * FNET_WIDTH
    c1, s1 = _dft_cos_sin(n1)
    m1 = jnp.asarray(np.concatenate([c1, -s1], axis=0), F32)
    tn = 8192
    stage1 = pl.pallas_call(
        _fft1_kernel,
        grid=(BATCH, row_w // tn),
        in_specs=[pl.BlockSpec((n1, tn), lambda b, j: (b, j)), _full((2 * n1, n1))],
        out_specs=pl.BlockSpec((1, 2 * n1, tn), lambda b, j: (b, 0, j)),
        out_shape=jax.ShapeDtypeStruct((BATCH, 2 * n1, row_w), F32),
        compiler_params=_params("arbitrary", "arbitrary"),
        name="fft_stage1",
    )(f_all.reshape(rows // n2, row_w), m1)

    k1 = np.arange(n1)[:, None]
    l2 = np.arange(n2)[None, :]
    tw = 2.0 * np.pi * (k1 * l2) / SEQ
    twr = jnp.asarray(np.repeat(np.cos(tw)[:, :, None], gd, axis=2), F32)
    twi = jnp.asarray(np.repeat(-np.sin(tw)[:, :, None], gd, axis=2), F32)
    c2, s2 = _dft_cos_sin(n2)
    m2 = jnp.asarray(np.block([[c2, s2], [-s2, c2]]), F32)
    cc, sc = _dft_cos_sin(gd)
    mc = jnp.asarray(np.concatenate([cc, sc], axis=0) / math.sqrt(SEQ * gd), F32)
    a4 = stage1.reshape(BATCH, 2 * n1, n2, FNET_WIDTH)
    lat = pl.pallas_call(
        _fft2_kernel,
        grid=(BATCH, n1),
        in_specs=[pl.BlockSpec((1, 1, n2, FNET_WIDTH), lambda b, k: (b, k, 0, 0)),
                  pl.BlockSpec((1, 1, n2, FNET_WIDTH), lambda b, k: (b, n1 + k, 0, 0)),
                  pl.BlockSpec((1, n2, gd), lambda b, k: (k, 0, 0)),
                  pl.BlockSpec((1, n2, gd), lambda b, k: (k, 0, 0)),
                  _full((2 * n2, 2 * n2)), _full((2 * gd, gd))],
        out_specs=pl.BlockSpec((1, n2, FNET_WIDTH), lambda b, k: (b, 0, k)),
        out_shape=jax.ShapeDtypeStruct((BATCH, n2, n1 * FNET_WIDTH), F32),
        compiler_params=_params("arbitrary", "arbitrary"),
        name="fft_stage2",
    )(a4, a4, twr, twi, m2, mc)

    cp, sp = _dft_cos_sin(CTX_LEN)
    mp = jnp.asarray(np.concatenate([cp, -sp], axis=1) / math.sqrt(CTX_LEN * gd), F32)
    mcc = jnp.asarray(np.concatenate([cc, sc], axis=1), F32)
    ctx0 = LAT_ROWS // CTX_LEN
    ctx = pl.pallas_call(
        _fft_ctx_kernel,
        grid=(BATCH,),
        in_specs=[pl.BlockSpec((CTX_LEN, FNET_WIDTH), lambda b: (ctx0 + b, 0)),
                  _full((gd, 2 * gd)), _full((CTX_LEN, 2 * CTX_LEN))],
        out_specs=pl.BlockSpec((CTX_LEN, FNET_WIDTH), lambda b: (b, 0)),
        out_shape=jax.ShapeDtypeStruct((BATCH * CTX_LEN, FNET_WIDTH), F32),
        compiler_params=_params("arbitrary"),
        name="fft_ctx",
    )(f_all, mcc, mp)
    return jnp.concatenate([lat.reshape(LAT_ROWS, FNET_WIDTH), ctx], axis=0)


def _even_out_kernel(x_ref, mod_ref, four_ref, yf_ref, yb_ref, xs_ref, z_ref, dsk_ref, gn_ref, wof_ref, wos_ref,
                     o_ref):
    y = yf_ref[...] + yb_ref[...] + dsk_ref[...] * xs_ref[...]
    gated = y * _silu(z_ref[...])
    out = jnp.dot(four_ref[...].astype(BF16), wof_ref[...], preferred_element_type=F32)
    for g in range(SSD_GROUPS):
        gs = slice(g * SSD_GROUP_W, (g + 1) * SSD_GROUP_W)
        v = gated[:, gs]
        normed = v * lax.rsqrt(jnp.mean(v * v, axis=-1, keepdims=True) + EPS) * gn_ref[:, gs]
        out = out + jnp.dot(normed.astype(BF16), wos_ref[gs, :], preferred_element_type=F32)
    gate = mod_ref[0][:, 2 * D_MODEL:3 * D_MODEL]
    o_ref[...] = x_ref[...] + gate * out


def even_out(x_all, mod, four, yf, yb, u, z, d_skip, ssd_norm, w_out):
    rows = x_all.shape[0]
    tm = ROW_TILE
    dsk = jnp.repeat(d_skip[0] + d_skip[1], SSD_HEAD_DIM)[None, :]
    wof = w_out[:FNET_WIDTH].astype(BF16)
    wos = w_out[FNET_WIDTH:].astype(BF16)
    row = lambda n: pl.BlockSpec((tm, n), lambda i: (i, 0))
    return pl.pallas_call(
        _even_out_kernel,
        grid=(rows // tm,),
        in_specs=[row(D_MODEL),
                  pl.BlockSpec((1, 1, N_MOD * D_MODEL), lambda i: (_mod_row(i, tm), 0, 0)),
                  row(FNET_WIDTH), row(SSD_INNER), row(SSD_INNER), row(SSD_INNER), row(SSD_INNER),
                  _full((1, SSD_INNER)), _full((1, SSD_INNER)), _full(wof.shape), _full(wos.shape)],
        out_specs=row(D_MODEL),
        out_shape=jax.ShapeDtypeStruct((rows, D_MODEL), F32),
        compiler_params=_params("arbitrary"),
        name="even_out",
    )(x_all, mod, four, yf, yb, u, z, dsk, ssd_norm[None, :], wof, wos)


def _erf_gelu(z):
    return 0.5 * z * (1.0 + lax.erf(z * (1.0 / math.sqrt(2.0))))


def _top_values(score, k):
    work, vals = score, []
    for _ in range(k):
        m = jnp.max(work, axis=0, keepdims=True)
        vals.append(m)
        work = jnp.where(work == m, NEG_INF, work)
    return jnp.concatenate(vals, axis=0)


def _peer_kernel(x_ref, mod_ref, g_ref, wq_ref, keys_ref, u_ref, vt_ref, o_ref,
                 ht_ref, q_ref, top_ref, a1_ref, a2_ref, w1_ref, w2_ref, tau_ref, acc_ref, *, tm, chunk):
    c = pl.program_id(1)
    k = PEER_TOPK

    @pl.when(c == 0)
    def _prologue():
        h2 = _modulated_norm(x_ref[...], g_ref[...], mod_ref[0], 3, 4)
        ht = h2.T.astype(BF16)
        ht_ref[...] = ht
        q = jnp.dot(wq_ref[...], ht, preferred_element_type=F32)
        q_ref[...] = q.reshape(2 * PEER_HEADS, PEER_QDIM // 2, tm)
        acc_ref[...] = jnp.zeros_like(acc_ref)

        def half_body(hs, carry):
            s = jnp.dot(keys_ref[hs], q_ref[hs].astype(BF16), preferred_element_type=F32)
            top = _top_values(s, k)
            top_ref[hs] = top
            a = jnp.where(s >= top[k - 1:k, :], s, NEG_INF)
            w = jnp.exp(s - top[0:1, :])
            h = hs // 2

            @pl.when(hs % 2 == 0)
            def _():
                a1_ref[h] = a
                w1_ref[h] = w

            @pl.when(hs % 2 == 1)
            def _():
                a2_ref[h] = a
                w2_ref[h] = w

            return carry

        lax.fori_loop(0, 2 * PEER_HEADS, half_body, 0)

        row8 = lax.broadcasted_iota(jnp.int32, (SUBLANES, tm), 0)

        def head_body(h, carry):
            v1, v2 = top_ref[2 * h], top_ref[2 * h + 1]
            groups = [v1[0:1] + v2[0:8], v1[0:1] + v2[8:16], v1[1:2] + v2[0:8], v1[8:16] + v2[0:1]]
            for i in range(2, 8):
                groups.append(jnp.where(row8 < k // (i + 1), v1[i:i + 1] + v2[0:8], NEG_INF))
            cand = jnp.concatenate(groups, axis=0)
            tau = _top_values(cand, k)[k - 1:k, :]
            zsum = jnp.sum(jnp.where(cand >= tau, jnp.exp(cand - (v1[0:1] + v2[0:1])), 0.0), axis=0, keepdims=True)
            tau_ref[h] = jnp.broadcast_to(tau, (SUBLANES, tm))
            w1_ref[h] = w1_ref[h] / zsum
            return carry

        lax.fori_loop(0, PEER_HEADS, head_body, 0)

    z = jnp.dot(u_ref[...], ht_ref[...], preferred_element_type=F32)
    act = _erf_gelu(z)
    parts = []
    for j in range(chunk // PEER_KEYS):
        e1 = c * (chunk // PEER_KEYS) + j
        gate = jnp.zeros((PEER_KEYS, tm), F32)
        for h in range(PEER_HEADS):
            a1 = a1_ref[h, pl.ds(e1, 1), :]
            w1 = w1_ref[h, pl.ds(e1, 1), :]
            sel = (a2_ref[h] + a1) >= tau_ref[h, 0:1, :]
            gate = gate + jnp.where(sel, w2_ref[h], 0.0) * w1
        parts.append((act[j * PEER_KEYS:(j + 1) * PEER_KEYS, :] * gate).astype(BF16))
    a = jnp.concatenate(parts, axis=0)
    acc_ref[...] += jnp.dot(vt_ref[...], a, preferred_element_type=F32)

    @pl.when(c == pl.num_programs(1) - 1)
    def _epilogue():
        gate2 = mod_ref[0][:, 5 * D_MODEL:6 * D_MODEL]
        o_ref[...] = x_ref[...] + gate2 * acc_ref[...].T


def peer_layer(x_rows, mod, norm2, peer_wq, peer_keys, peer_u, peer_v):
    rows = x_rows.shape[0]
    tm, chunk = PEER_TILE, PEER_CHUNK
    wq_t = peer_wq.T.astype(BF16)
    keys = peer_keys.reshape(2 * PEER_HEADS, PEER_KEYS, PEER_QDIM // 2).astype(BF16)
    u = peer_u.astype(BF16)
    vt = peer_v.astype(BF16).T
    head_buf = pltpu.VMEM((PEER_HEADS, PEER_KEYS, tm), F32)
    return pl.pallas_call(
        functools.partial(_peer_kernel, tm=tm, chunk=chunk),
        grid=(rows // tm, PEER_EXPERTS // chunk),
        in_specs=[pl.BlockSpec((tm, D_MODEL), lambda i, c: (i, 0)),
                  pl.BlockSpec((1, 1, N_MOD * D_MODEL), lambda i, c: (_mod_row(i, tm), 0, 0)),
                  _full((1, D_MODEL)), _full(wq_t.shape), _full(keys.shape),
                  pl.BlockSpec((chunk, D_MODEL), lambda i, c: (c, 0)),
                  pl.BlockSpec((D_MODEL, chunk), lambda i, c: (0, c))],
        out_specs=pl.BlockSpec((tm, D_MODEL), lambda i, c: (i, 0)),
        out_shape=jax.ShapeDtypeStruct((rows, D_MODEL), F32),
        scratch_shapes=[pltpu.VMEM((D_MODEL, tm), BF16),
                        pltpu.VMEM((2 * PEER_HEADS, PEER_QDIM // 2, tm), F32),
                        pltpu.VMEM((2 * PEER_HEADS, PEER_TOPK, tm), F32),
                        head_buf, head_buf, head_buf, head_buf,
                        pltpu.VMEM((PEER_HEADS, SUBLANES, tm), F32),
                        pltpu.VMEM((D_MODEL, tm), F32)],
        compiler_params=_params("arbitrary", "arbitrary"),
        name="peer",
    )(x_rows, mod, norm2[None, :], wq_t, keys, u, vt)


def _rope_tables(rows):
    l = jnp.arange(SEQ)
    per_axis = MLA_ROPE // 2
    inv = ROPE_THETA ** (-jnp.arange(0, per_axis, 2, dtype=F32) / per_axis)
    ang = jnp.concatenate([(l // GRID_W)[:, None] * inv, (l % GRID_W)[:, None] * inv], axis=-1)
    cos, sin = jnp.cos(ang), jnp.sin(ang)
    one = jnp.ones((SEQ, MLA_NOPE), F32)
    zero16 = jnp.zeros((SEQ, per_axis), F32)
    tail = jnp.zeros((SEQ, HEAD_PAD - MLA_QK), F32)
    zero64 = jnp.zeros((SEQ, MLA_NOPE), F32)
    cos_t = jnp.concatenate([one, cos, cos, tail], axis=1)
    sin_dn = jnp.concatenate([zero64, -sin, zero16, tail], axis=1)
    sin_up = jnp.concatenate([zero64, zero16, sin, tail], axis=1)
    n_ctx = rows - LAT_ROWS
    ident = jnp.concatenate([jnp.ones((n_ctx, MLA_QK), F32), jnp.zeros((n_ctx, HEAD_PAD - MLA_QK), F32)], axis=1)
    zeros = jnp.zeros((n_ctx, HEAD_PAD), F32)
    tile = lambda t, c: jnp.concatenate([t] * BATCH + [c], axis=0)
    return tile(cos_t, ident), tile(sin_up, zeros), tile(sin_dn, zeros)


def _head_norm_rope(t, gain, cos, sin_up, sin_dn):
    ms = jnp.sum(t * t, axis=-1, keepdims=True) * (1.0 / MLA_QK)
    t = t * lax.rsqrt(ms + EPS) * gain
    per_axis = MLA_ROPE // 2
    return t * cos + pltpu.roll(t, per_axis, 1) * sin_up + pltpu.roll(t, HEAD_PAD - per_axis, 1) * sin_dn


def _mla_in_kernel(x_ref, mod_ref, g_ref, wdq_ref, wdkv_ref, wpe_ref, qan_ref, kvan_ref, wuq_ref, wuk_ref, wuv_ref,
                   qn_ref, kn_ref, cos_ref, sup_ref, sdn_ref, q_ref, k_ref, v_ref):
    h = _modulated_norm(x_ref[...], g_ref[...], mod_ref[0], 0, 1).astype(BF16)
    dq = jnp.dot(h, wdq_ref[...], preferred_element_type=F32)
    dkv = jnp.dot(h, wdkv_ref[...], preferred_element_type=F32)
    kpe = jnp.dot(h, wpe_ref[...], preferred_element_type=F32)
    qa = dq * lax.rsqrt(jnp.mean(dq * dq, axis=-1, keepdims=True) + EPS) * qan_ref[...]
    kva = (dkv * lax.rsqrt(jnp.mean(dkv * dkv, axis=-1, keepdims=True) + EPS) * kvan_ref[...]).astype(BF16)
    q = jnp.dot(qa.astype(BF16), wuq_ref[...], preferred_element_type=F32)
    kn = jnp.dot(kva, wuk_ref[...], preferred_element_type=F32)
    v_ref[...] = jnp.dot(kva, wuv_ref[...], preferred_element_type=F32).astype(BF16)
    cos, sup, sdn = cos_ref[...], sup_ref[...], sdn_ref[...]
    scale = MLA_QK ** -0.5
    for hd in range(MLA_HEADS):
        hs = slice(hd * HEAD_PAD, (hd + 1) * HEAD_PAD)
        q_ref[:, hs] = (_head_norm_rope(q[:, hs], qn_ref[...], cos, sup, sdn) * scale).astype(BF16)
        k_ref[:, hs] = _head_norm_rope(kn[:, hs] + kpe, kn_ref[...], cos, sup, sdn).astype(BF16)


def mla_in(x_all, mod, p):
    rows = x_all.shape[0]
    tm = ROW_TILE
    w = p["w_dqkv"]
    wdq = w[:, :MLA_Q_LORA].astype(BF16)
    wdkv = w[:, MLA_Q_LORA:MLA_Q_LORA + MLA_KV_LORA].astype(BF16)
    wpe = jnp.pad(w[:, MLA_Q_LORA + MLA_KV_LORA:], ((0, 0), (MLA_NOPE, HEAD_PAD - MLA_QK))).astype(BF16)
    wuq = jnp.pad(p["w_uq"].reshape(MLA_Q_LORA, MLA_HEADS, MLA_QK), ((0, 0), (0, 0), (0, HEAD_PAD - MLA_QK)))
    wuq = wuq.reshape(MLA_Q_LORA, MLA_HEADS * HEAD_PAD).astype(BF16)
    wukv = p["w_ukv"].reshape(MLA_KV_LORA, MLA_HEADS, MLA_NOPE + MLA_V)
    wuk = jnp.pad(wukv[:, :, :MLA_NOPE], ((0, 0), (0, 0), (0, HEAD_PAD - MLA_NOPE)))
    wuk = wuk.reshape(MLA_KV_LORA, MLA_HEADS * HEAD_PAD).astype(BF16)
    wuv = wukv[:, :, MLA_NOPE:].reshape(MLA_KV_LORA, MLA_HEADS * MLA_V).astype(BF16)
    padg = lambda g: jnp.pad(g, (0, HEAD_PAD - MLA_QK))[None, :]
    cos, sup, sdn = _rope_tables(rows)
    row = lambda n: pl.BlockSpec((tm, n), lambda i: (i, 0))
    weights = [wdq, wdkv, wpe, p["q_a_norm"][None, :], p["kv_a_norm"][None, :], wuq, wuk, wuv,
               padg(p["q_norm"]), padg(p["k_norm"])]
    return pl.pallas_call(
        _mla_in_kernel,
        grid=(rows // tm,),
        in_specs=[row(D_MODEL),
                  pl.BlockSpec((1, 1, N_MOD * D_MODEL), lambda i: (_mod_row(i, tm), 0, 0)),
                  _full((1, D_MODEL))] + [_full(a.shape) for a in weights] + [row(HEAD_PAD)] * 3,
        out_specs=[row(MLA_HEADS * HEAD_PAD), row(MLA_HEADS * HEAD_PAD), row(MLA_HEADS * MLA_V)],
        out_shape=[jax.ShapeDtypeStruct((rows, MLA_HEADS * HEAD_PAD), BF16),
                   jax.ShapeDtypeStruct((rows, MLA_HEADS * HEAD_PAD), BF16),
                   jax.ShapeDtypeStruct((rows, MLA_HEADS * MLA_V), BF16)],
        compiler_params=_params("arbitrary"),
        name="mla_in",
    )(x_all, mod, p["norm1"][None, :], *weights, cos, sup, sdn)


def _attn_kernel(q_ref, kc_ref, kl_ref, vc_ref, vl_ref, o_ref, *, tq, tk):
    lane = lax.broadcasted_iota(jnp.int32, (tq, 2 * MLA_V), 1)
    outs = []
    for hh in range(2):
        hs = slice(hh * HEAD_PAD, (hh + 1) * HEAD_PAD)
        q = q_ref[:, hs]

        def step(k, v, carry):
            m, l, acc = carry
            s = lax.dot_general(q, k, (((1,), (1,)), ((), ())), preferred_element_type=F32)
            m_new = jnp.maximum(m, jnp.max(s, axis=-1, keepdims=True))
            alpha = jnp.exp(m - m_new)
            p = jnp.exp(s - m_new)
            l = alpha * l + jnp.sum(p, axis=-1, keepdims=True)
            acc = alpha * acc + jnp.dot(p.astype(BF16), v, preferred_element_type=F32)
            return m_new, l, acc

        init = (jnp.full((tq, 1), NEG_INF, F32), jnp.zeros((tq, 1), F32), jnp.zeros((tq, 2 * MLA_V), F32))
        carry = step(kc_ref[:, hs], vc_ref[...], init)

        def body(j, carry):
            off = pl.multiple_of(j * tk, tk)
            return step(kl_ref[pl.ds(off, tk), hs], vl_ref[pl.ds(off, tk), :], carry)

        m, l, acc = lax.fori_loop(0, SEQ // tk, body, carry)
        outs.append(acc / l)
    o_ref[...] = jnp.where(lane < MLA_V, outs[0], outs[1]).astype(o_ref.dtype)


def attention(q, k, v):
    tq, tk = ATTN_TQ, ATTN_TK
    nq = SEQ // tq
    ctx0 = LAT_ROWS // CTX_LEN
    return pl.pallas_call(
        functools.partial(_attn_kernel, tq=tq, tk=tk),
        grid=(BATCH, MLA_HEADS // 2, nq),
        in_specs=[pl.BlockSpec((tq, 2 * HEAD_PAD), lambda b, h, i: (b * nq + i, h)),
                  pl.BlockSpec((CTX_LEN, 2 * HEAD_PAD), lambda b, h, i: (ctx0 + b, h)),
                  pl.BlockSpec((SEQ, 2 * HEAD_PAD), lambda b, h, i: (b, h)),
                  pl.BlockSpec((CTX_LEN, 2 * MLA_V), lambda b, h, i: (ctx0 + b, h)),
                  pl.BlockSpec((SEQ, 2 * MLA_V), lambda b, h, i: (b, h))],
        out_specs=pl.BlockSpec((tq, 2 * MLA_V), lambda b, h, i: (b * nq + i, h)),
        out_shape=jax.ShapeDtypeStruct((LAT_ROWS, MLA_HEADS * MLA_V), BF16),
        compiler_params=_params("arbitrary", "arbitrary", "arbitrary"),
        name="attention",
    )(q, k, k, v, v)


def _attn_out_kernel(x_ref, mod_ref, o_ref_in, wo_ref, out_ref):
    gate = mod_ref[0][:, 2 * D_MODEL:3 * D_MODEL]
    out_ref[...] = x_ref[...] + gate * jnp.dot(o_ref_in[...], wo_ref[...], preferred_element_type=F32)


def attn_out(x_all, mod, o, w_o):
    tm = ROW_TILE
    row = lambda n: pl.BlockSpec((tm, n), lambda i: (i, 0))
    return pl.pallas_call(
        _attn_out_kernel,
        grid=(LAT_ROWS // tm,),
        in_specs=[row(D_MODEL), pl.BlockSpec((1, 1, N_MOD * D_MODEL), lambda i: (_mod_row(i, tm), 0, 0)),
                  row(MLA_HEADS * MLA_V), _full(w_o.shape)],
        out_specs=row(D_MODEL),
        out_shape=jax.ShapeDtypeStruct((LAT_ROWS, D_MODEL), F32),
        compiler_params=_params("arbitrary"),
        name="attn_out",
    )(x_all, mod, o, w_o.astype(BF16))


def kernel(x, c, ctx, c_ctx, l0_w_ada, l0_b_ada, l0_norm1, l0_w_in, l0_conv_w, l0_conv_b, l0_dt_bias, l0_a_log, l0_d_skip, l0_ssd_norm, l0_w_out, l0_norm2, l0_peer_wq, l0_peer_keys, l0_peer_u, l0_peer_v, l1_w_ada, l1_b_ada, l1_norm1, l1_w_dqkv, l1_q_a_norm, l1_kv_a_norm, l1_w_uq, l1_w_ukv, l1_q_norm, l1_k_norm, l1_w_o, l1_norm2, l1_peer_wq, l1_peer_keys, l1_peer_u, l1_peer_v):
    x_all = jnp.concatenate([x.reshape(LAT_ROWS, D_MODEL), ctx.reshape(BATCH * CTX_LEN, D_MODEL)], axis=0)

    mod0 = ada_table(c, c_ctx, l0_w_ada, l0_b_ada)
    f, z, xbc, dt = even_in(x_all, mod0, l0_norm1, l0_w_in)
    u = conv_silu(xbc, l0_conv_w, l0_conv_b)
    yf, yb = ssd_scan(u, dt, l0_dt_bias, l0_a_log)
    four = fourier_mix(f)
    x_all = even_out(x_all, mod0, four, yf, yb, u, z, l0_d_skip, l0_ssd_norm, l0_w_out)
    x_all = peer_layer(x_all, mod0, l0_norm2, l0_peer_wq, l0_peer_keys, l0_peer_u, l0_peer_v)

    mod1 = ada_table(c, c_ctx, l1_w_ada, l1_b_ada)
    p1 = dict(norm1=l1_norm1, w_dqkv=l1_w_dqkv, q_a_norm=l1_q_a_norm, kv_a_norm=l1_kv_a_norm, w_uq=l1_w_uq,
              w_ukv=l1_w_ukv, q_norm=l1_q_norm, k_norm=l1_k_norm)
    q, k, v = mla_in(x_all, mod1, p1)
    o = attention(q, k, v)
    x_lat = attn_out(x_all, mod1, o, l1_w_o)
    x_lat = peer_layer(x_lat, mod1, l1_norm2, l1_peer_wq, l1_peer_keys, l1_peer_u, l1_peer_v)
    return x_lat.reshape(BATCH, SEQ, D_MODEL)
```

```python
import functools
import math

import numpy as np
import jax
import jax.numpy as jnp
from jax import lax
from jax.experimental import pallas as pl
from jax.experimental.pallas import tpu as pltpu

D_MODEL = 1024
BATCH = 2
SEQ = 8192
GRID_W = 64
CTX_LEN = 256
EPS = 1e-6
N_MOD = 6
LAT_ROWS = BATCH * SEQ
ALL_ROWS = LAT_ROWS + BATCH * CTX_LEN

FNET_GROUPS = 4
FNET_GROUP_DIM = 128
FNET_WIDTH = FNET_GROUPS * FNET_GROUP_DIM
FFT_N1 = 64
FFT_N2 = 128

SSD_HEADS = 16
SSD_HEAD_DIM = 64
SSD_INNER = SSD_HEADS * SSD_HEAD_DIM
SSD_GROUPS = 4
SSD_STATE = 128
SSD_CONV = 5
SSD_CHUNK = 128
SSD_CONV_DIM = SSD_INNER + 2 * SSD_GROUPS * SSD_STATE
SSD_GROUP_W = SSD_INNER // SSD_GROUPS

MLA_HEADS = 16
MLA_NOPE = 64
MLA_ROPE = 32
MLA_QK = MLA_NOPE + MLA_ROPE
MLA_V = 64
MLA_Q_LORA = 384
MLA_KV_LORA = 256
ROPE_THETA = 10000.0
HEAD_PAD = 128

PEER_HEADS = 8
PEER_KEYS = 128
PEER_EXPERTS = PEER_KEYS * PEER_KEYS
PEER_QDIM = 256
PEER_TOPK = 16

LANES = 128
SUBLANES = 8
VMEM_LIMIT = 56 * 1024 * 1024

ROW_TILE = 256
PEER_TILE = 512
PEER_CHUNK = 1024
PEER_SUB = 256
PEER_SELECT_TOKENS = 256
ATTN_TQ = 512
ATTN_TK = 512

F32 = jnp.float32
BF16 = jnp.bfloat16
HIGHEST = lax.Precision.HIGHEST
NEG_INF = float("-inf")


def _params(*sem):
    return pltpu.CompilerParams(dimension_semantics=sem, vmem_limit_bytes=VMEM_LIMIT)


def _mod_row(i, tile):
    return jnp.minimum((i * tile) // SEQ, BATCH)


def _full(shape):
    return pl.BlockSpec(shape, lambda *_: (0,) * len(shape))


def _silu(x):
    return x * (1.0 / (1.0 + jnp.exp(-x)))


def _modulated_norm(x, gain, mod, k_shift, k_scale):
    shift = mod[:, k_shift * D_MODEL:(k_shift + 1) * D_MODEL]
    scale = mod[:, k_scale * D_MODEL:(k_scale + 1) * D_MODEL]
    ms = jnp.mean(x * x, axis=-1, keepdims=True)
    return x * lax.rsqrt(ms + EPS) * gain * (1.0 + scale) + shift


def _bdot(a, b):
    return jnp.dot(a.astype(BF16), b.astype(BF16), preferred_element_type=F32)


def _hdot(a, b):
    return jnp.dot(a, b, precision=HIGHEST, preferred_element_type=F32)


def _ada_kernel(c_ref, w_ref, b_ref, o_ref):
    o_ref[...] = _hdot(_silu(c_ref[...]), w_ref[...]) + b_ref[...]


def ada_table(c, c_ctx, w_ada, b_ada):
    cond = jnp.concatenate([c, c_ctx[None, :], jnp.zeros((SUBLANES - BATCH - 1, D_MODEL), F32)], axis=0)
    tn = 512
    out = pl.pallas_call(
        _ada_kernel,
        grid=(N_MOD * D_MODEL // tn,),
        in_specs=[_full((SUBLANES, D_MODEL)),
                  pl.BlockSpec((D_MODEL, tn), lambda j: (0, j)),
                  pl.BlockSpec((1, tn), lambda j: (0, j))],
        out_specs=pl.BlockSpec((SUBLANES, tn), lambda j: (0, j)),
        out_shape=jax.ShapeDtypeStruct((SUBLANES, N_MOD * D_MODEL), F32),
        compiler_params=_params("arbitrary"),
        name="ada_table",
    )(cond, w_ada, b_ada[None, :])
    return out[:BATCH + 1].reshape(BATCH + 1, 1, N_MOD * D_MODEL)


def _even_in_kernel(x_ref, mod_ref, g_ref, wf_ref, wz_ref, wx_ref, wd_ref, f_ref, z_ref, xbc_ref, dt_ref):
    h = _modulated_norm(x_ref[...], g_ref[...], mod_ref[0], 0, 1).astype(BF16)
    f_ref[...] = jnp.dot(h, wf_ref[...], preferred_element_type=F32)
    z_ref[...] = jnp.dot(h, wz_ref[...], preferred_element_type=F32)
    xbc_ref[...] = jnp.dot(h, wx_ref[...], preferred_element_type=F32)
    dt_ref[...] = jnp.dot(h, wd_ref[...], preferred_element_type=F32)


def even_in(x_all, mod, norm1, w_in):
    rows = x_all.shape[0]
    o1, o2, o3 = FNET_WIDTH, FNET_WIDTH + SSD_INNER, FNET_WIDTH + SSD_INNER + SSD_CONV_DIM
    wf = w_in[:, :o1].astype(BF16)
    wz = w_in[:, o1:o2].astype(BF16)
    wx = w_in[:, o2:o3].astype(BF16)
    wd = jnp.pad(w_in[:, o3:], ((0, 0), (0, LANES - 2 * SSD_HEADS))).astype(BF16)
    tm = ROW_TILE
    row = lambda n: pl.BlockSpec((tm, n), lambda i: (i, 0))
    return pl.pallas_call(
        _even_in_kernel,
        grid=(rows // tm,),
        in_specs=[row(D_MODEL),
                  pl.BlockSpec((1, 1, N_MOD * D_MODEL), lambda i: (_mod_row(i, tm), 0, 0)),
                  _full((1, D_MODEL)),
                  _full(wf.shape), _full(wz.shape), _full(wx.shape), _full(wd.shape)],
        out_specs=[row(FNET_WIDTH), row(SSD_INNER), row(SSD_CONV_DIM), row(LANES)],
        out_shape=[jax.ShapeDtypeStruct((rows, FNET_WIDTH), F32),
                   jax.ShapeDtypeStruct((rows, SSD_INNER), F32),
                   jax.ShapeDtypeStruct((rows, SSD_CONV_DIM), F32),
                   jax.ShapeDtypeStruct((rows, LANES), F32)],
        compiler_params=_params("arbitrary"),
        name="even_in",
    )(x_all, mod, norm1[None, :], wf, wz, wx, wd)


def _conv_kernel(x_ref, prev_ref, next_ref, w_ref, b_ref, o_ref, ext_ref, *, tm):
    row0 = pl.program_id(0) * tm
    in_lat = row0 < LAT_ROWS
    first = jnp.where(in_lat, row0 % SEQ == 0, (row0 - LAT_ROWS) % CTX_LEN == 0)
    last = jnp.where(in_lat, (row0 + tm) % SEQ == 0, (row0 + tm - LAT_ROWS) % CTX_LEN == 0)
    ext_ref[0:SUBLANES, :] = prev_ref[...] * jnp.where(first, 0.0, 1.0)
    ext_ref[SUBLANES:SUBLANES + tm, :] = x_ref[...]
    ext_ref[SUBLANES + tm:2 * SUBLANES + tm, :] = next_ref[...] * jnp.where(last, 0.0, 1.0)
    pad = SSD_CONV // 2
    acc = b_ref[...] + w_ref[0:1, :] * ext_ref[pl.ds(SUBLANES - pad, tm), :]
    for k in range(1, SSD_CONV):
        acc = acc + w_ref[k:k + 1, :] * ext_ref[pl.ds(SUBLANES - pad + k, tm), :]
    o_ref[...] = _silu(acc)


def conv_silu(xbc, conv_w, conv_b):
    rows, ch = xbc.shape
    tm, tc = ROW_TILE, 1024
    assert CTX_LEN % tm == 0 and SEQ % tm == 0
    per = tm // SUBLANES
    n_small = rows // SUBLANES
    w = jnp.pad(conv_w, ((0, SUBLANES - SSD_CONV), (0, 0)))
    return pl.pallas_call(
        functools.partial(_conv_kernel, tm=tm),
        grid=(rows // tm, ch // tc),
        in_specs=[pl.BlockSpec((tm, tc), lambda i, j: (i, j)),
                  pl.BlockSpec((SUBLANES, tc), lambda i, j: (jnp.maximum(i * per - 1, 0), j)),
                  pl.BlockSpec((SUBLANES, tc), lambda i, j: (jnp.minimum((i + 1) * per, n_small - 1), j)),
                  pl.BlockSpec((SUBLANES, tc), lambda i, j: (0, j)),
                  pl.BlockSpec((1, tc), lambda i, j: (0, j))],
        out_specs=pl.BlockSpec((tm, tc), lambda i, j: (i, j)),
        out_shape=jax.ShapeDtypeStruct((rows, ch), F32),
        scratch_shapes=[pltpu.VMEM((tm + 2 * SUBLANES, tc), F32)],
        compiler_params=_params("arbitrary", "arbitrary"),
        name="conv_silu",
    )(xbc, xbc, xbc, w, conv_b[None, :])


def _softplus(x):
    return jnp.maximum(x, 0.0) + jnp.log1p(jnp.exp(-jnp.abs(x)))


def _ssd_direction(u_ref, dt_ref, dtb_ref, alog_ref, exp_ref, state_ref, y_ref, direction):
    cl = SSD_CHUNK
    xs = u_ref[:, :SSD_INNER]
    dtv = _softplus(dt_ref[...] + dtb_ref[...])
    dta = dtv * (-jnp.exp(alog_ref[...]))
    r = lax.broadcasted_iota(jnp.int32, (cl, cl), 0)
    c = lax.broadcasted_iota(jnp.int32, (cl, cl), 1)
    tri = (r >= c) if direction == 0 else (r <= c)
    cs = _hdot(tri.astype(F32), dta)
    cs_t = cs.T
    edge = cs[cl - 1:cl, :] if direction == 0 else cs[0:1, :]
    to_end = jnp.exp(edge - cs)
    from_start = jnp.exp(cs)
    spread = _hdot(jnp.concatenate([dtv, to_end, from_start], axis=0), exp_ref[direction])
    dt_x, te_x, fs_x = spread[:cl], spread[cl:2 * cl], spread[2 * cl:]
    chunk_decay = fs_x[cl - 1:cl, :] if direction == 0 else fs_x[0:1, :]
    xd = xs * dt_x
    xd_b = xd.astype(BF16)
    xte_b = (xd * te_x).astype(BF16)
    lane = lax.broadcasted_iota(jnp.int32, (cl, LANES), 1)
    low = lane < SSD_HEAD_DIM
    for g in range(SSD_GROUPS):
        bg = u_ref[:, SSD_INNER + g * SSD_STATE:SSD_INNER + (g + 1) * SSD_STATE]
        cg = u_ref[:, SSD_INNER + (SSD_GROUPS + g) * SSD_STATE:SSD_INNER + (SSD_GROUPS + g + 1) * SSD_STATE]
        bg_b, cg_b = bg.astype(BF16), cg.astype(BF16)
        cb = lax.dot_general(cg_b, bg_b, (((1,), (1,)), ((), ())), preferred_element_type=F32)
        gs = slice(g * SSD_GROUP_W, (g + 1) * SSD_GROUP_W)
        state = state_ref[g]
        y_off = jnp.dot(cg_b, state.astype(BF16), preferred_element_type=F32) * fs_x[:, gs]
        new_state = jnp.dot(bg.T.astype(BF16), xte_b[:, gs], preferred_element_type=F32)
        state_ref[g] = state * chunk_decay[:, gs] + new_state
        for pair in range(2):
            blk = xd_b[:, g * SSD_GROUP_W + pair * LANES:g * SSD_GROUP_W + (pair + 1) * LANES]
            y_pair = None
            for sub in range(2):
                col = direction * SSD_HEADS + g * 4 + pair * 2 + sub
                seg = cs[:, col:col + 1] - cs_t[col:col + 1, :]
                decay = jnp.exp(jnp.where(tri, seg, NEG_INF))
                m = (cb * decay).astype(BF16)
                half = jnp.where(low if sub == 0 else jnp.logical_not(low), blk, jnp.zeros_like(blk))
                part = jnp.dot(m, half, preferred_element_type=F32)
                y_pair = part if y_pair is None else y_pair + part
            lo = g * SSD_GROUP_W + pair * LANES
            y_ref[:, lo:lo + LANES] = y_pair + y_off[:, pair * LANES:(pair + 1) * LANES]


def _ssd_kernel(uf_ref, ub_ref, dtf_ref, dtb_in_ref, bias_ref, alog_ref, exp_ref, yf_ref, yb_ref, sf_ref, sb_ref):
    @pl.when(pl.program_id(1) == 0)
    def _():
        sf_ref[...] = jnp.zeros_like(sf_ref)
        sb_ref[...] = jnp.zeros_like(sb_ref)

    _ssd_direction(uf_ref, dtf_ref, bias_ref, alog_ref, exp_ref, sf_ref, yf_ref, 0)
    _ssd_direction(ub_ref, dtb_in_ref, bias_ref, alog_ref, exp_ref, sb_ref, yb_ref, 1)


def ssd_scan(u, dt, dt_bias, a_log):
    rows = u.shape[0]
    cl = SSD_CHUNK
    lat_chunks, ctx_chunks = SEQ // cl, CTX_LEN // cl
    steps = ctx_chunks + lat_chunks
    ctx0 = LAT_ROWS // cl

    def fwd_chunk(b, t):
        return jnp.where(t < ctx_chunks, ctx0 + b * ctx_chunks + t, b * lat_chunks + t - ctx_chunks)

    def bwd_chunk(b, t):
        return jnp.where(t < ctx_chunks, ctx0 + b * ctx_chunks + (ctx_chunks - 1 - t),
                         b * lat_chunks + (lat_chunks - 1) - (t - ctx_chunks))

    pad = LANES - 2 * SSD_HEADS
    bias = jnp.pad(dt_bias.reshape(1, -1), ((0, 0), (0, pad)))
    alog = jnp.pad(a_log.reshape(1, -1), ((0, 0), (0, pad)))
    expand = np.zeros((2, LANES, SSD_INNER), np.float32)
    for d in range(2):
        for h in range(SSD_HEADS):
            expand[d, d * SSD_HEADS + h, h * SSD_HEAD_DIM:(h + 1) * SSD_HEAD_DIM] = 1.0
    spec = lambda n, fn: pl.BlockSpec((cl, n), lambda b, t: (fn(b, t), 0))
    return pl.pallas_call(
        _ssd_kernel,
        grid=(BATCH, steps),
        in_specs=[spec(SSD_CONV_DIM, fwd_chunk), spec(SSD_CONV_DIM, bwd_chunk),
                  spec(LANES, fwd_chunk), spec(LANES, bwd_chunk),
                  _full((1, LANES)), _full((1, LANES)), _full((2, LANES, SSD_INNER))],
        out_specs=[spec(SSD_INNER, fwd_chunk), spec(SSD_INNER, bwd_chunk)],
        out_shape=[jax.ShapeDtypeStruct((rows, SSD_INNER), F32)] * 2,
        scratch_shapes=[pltpu.VMEM((SSD_GROUPS, SSD_STATE, SSD_GROUP_W), F32)] * 2,
        compiler_params=_params("arbitrary", "arbitrary"),
        name="ssd_scan",
    )(u, u, dt, dt, bias, alog, jnp.asarray(expand))


def _dft_cos_sin(n):
    k = np.arange(n)
    ang = 2.0 * np.pi * ((k[:, None] * k[None, :]) % n) / n
    return np.cos(ang), np.sin(ang)


def _fft1_kernel(x_ref, m_ref, o_ref):
    o_ref[0] = _hdot(m_ref[...], x_ref[...])


def _fft2_kernel(ar_ref, ai_ref, twr_ref, twi_ref, m2_ref, mc_ref, o_ref):
    ar, ai = ar_ref[0, 0], ai_ref[0, 0]
    twr = jnp.concatenate([twr_ref[0]] * FNET_GROUPS, axis=1)
    twi = jnp.concatenate([twi_ref[0]] * FNET_GROUPS, axis=1)
    p = ar * twr - ai * twi
    q = ar * twi + ai * twr
    uv = _hdot(m2_ref[...], jnp.concatenate([p, q], axis=0))
    n2 = FFT_N2
    for g in range(FNET_GROUPS):
        gs = slice(g * FNET_GROUP_DIM, (g + 1) * FNET_GROUP_DIM)
        o_ref[0, :, gs] = _hdot(jnp.concatenate([uv[:n2, gs], uv[n2:, gs]], axis=1), mc_ref[...])


def _fft_ctx_kernel(x_ref, mc_ref, mp_ref, o_ref):
    x = x_ref[...]
    for g in range(FNET_GROUPS):
        gs = slice(g * FNET_GROUP_DIM, (g + 1) * FNET_GROUP_DIM)
        cs = _hdot(x[:, gs], mc_ref[...])
        stacked = jnp.concatenate([cs[:, :FNET_GROUP_DIM], cs[:, FNET_GROUP_DIM:]], axis=0)
        o_ref[:, gs] = _hdot(mp_ref[...], stacked)


def fourier_mix(f_all):
    rows = f_all.shape[0]
    n1, n2, gd = FFT_N1, FFT_N2, FNET_GROUP_DIM
    row_w = n2 * FNET_WIDTH
    c1, s1 = _dft_cos_sin(n1)
    m1 = jnp.asarray(np.concatenate([c1, -s1], axis=0), F32)
    tn = 8192
    stage1 = pl.pallas_call(
        _fft1_kernel,
        grid=(BATCH, row_w // tn),
        in_specs=[pl.BlockSpec((n1, tn), lambda b, j: (b, j)), _full((2 * n1, n1))],
        out_specs=pl.BlockSpec((1, 2 * n1, tn), lambda b, j: (b, 0, j)),
        out_shape=jax.ShapeDtypeStruct((BATCH, 2 * n1, row_w), F32),
        compiler_params=_params("arbitrary", "arbitrary"),
        name="fft_stage1",
    )(f_all.reshape(rows // n2, row_w), m1)

    k1 = np.arange(n1)[:, None]
    l2 = np.arange(n2)[None, :]
    tw = 2.0 * np.pi * (k1 * l2) / SEQ
    twr = jnp.asarray(np.repeat(np.cos(tw)[:, :, None], gd, axis=2), F32)
    twi = jnp.asarray(np.repeat(-np.sin(tw)[:, :, None], gd, axis=2), F32)
    c2, s2 = _dft_cos_sin(n2)
    m2 = jnp.asarray(np.block([[c2, s2], [-s2, c2]]), F32)
    cc, sc = _dft_cos_sin(gd)
    mc = jnp.asarray(np.concatenate([cc, sc], axis=0) / math.sqrt(SEQ * gd), F32)
    a4 = stage1.reshape(BATCH, 2 * n1, n2, FNET_WIDTH)
    lat = pl.pallas_call(
        _fft2_kernel,
        grid=(BATCH, n1),
        in_specs=[pl.BlockSpec((1, 1, n2, FNET_WIDTH), lambda b, k: (b, k, 0, 0)),
                  pl.BlockSpec((1, 1, n2, FNET_WIDTH), lambda b, k: (b, n1 + k, 0, 0)),
                  pl.BlockSpec((1, n2, gd), lambda b, k: (k, 0, 0)),
                  pl.BlockSpec((1, n2, gd), lambda b, k: (k, 0, 0)),
                  _full((2 * n2, 2 * n2)), _full((2 * gd, gd))],
        out_specs=pl.BlockSpec((1, n2, FNET_WIDTH), lambda b, k: (b, 0, k)),
        out_shape=jax.ShapeDtypeStruct((BATCH, n2, n1 * FNET_WIDTH), F32),
        compiler_params=_params("arbitrary", "arbitrary"),
        name="fft_stage2",
    )(a4, a4, twr, twi, m2, mc)

    cp, sp = _dft_cos_sin(CTX_LEN)
    mp = jnp.asarray(np.concatenate([cp, -sp], axis=1) / math.sqrt(CTX_LEN * gd), F32)
    mcc = jnp.asarray(np.concatenate([cc, sc], axis=1), F32)
    ctx0 = LAT_ROWS // CTX_LEN
    ctx = pl.pallas_call(
        _fft_ctx_kernel,
        grid=(BATCH,),
        in_specs=[pl.BlockSpec((CTX_LEN, FNET_WIDTH), lambda b: (ctx0 + b, 0)),
                  _full((gd, 2 * gd)), _full((CTX_LEN, 2 * CTX_LEN))],
        out_specs=pl.BlockSpec((CTX_LEN, FNET_WIDTH), lambda b: (b, 0)),
        out_shape=jax.ShapeDtypeStruct((BATCH * CTX_LEN, FNET_WIDTH), F32),
        compiler_params=_params("arbitrary"),
        name="fft_ctx",
    )(f_all, mcc, mp)
    return jnp.concatenate([lat.reshape(LAT_ROWS, FNET_WIDTH), ctx], axis=0)


def _even_out_kernel(x_ref, mod_ref, four_ref, yf_ref, yb_ref, xs_ref, z_ref, dsk_ref, gn_ref, wof_ref, wos_ref,
                     o_ref):
    y = yf_ref[...] + yb_ref[...] + dsk_ref[...] * xs_ref[...]
    gated = y * _silu(z_ref[...])
    out = jnp.dot(four_ref[...].astype(BF16), wof_ref[...], preferred_element_type=F32)
    for g in range(SSD_GROUPS):
        gs = slice(g * SSD_GROUP_W, (g + 1) * SSD_GROUP_W)
        v = gated[:, gs]
        normed = v * lax.rsqrt(jnp.mean(v * v, axis=-1, keepdims=True) + EPS) * gn_ref[:, gs]
        out = out + jnp.dot(normed.astype(BF16), wos_ref[gs, :], preferred_element_type=F32)
    gate = mod_ref[0][:, 2 * D_MODEL:3 * D_MODEL]
    o_ref[...] = x_ref[...] + gate * out


def even_out(x_all, mod, four, yf, yb, u, z, d_skip, ssd_norm, w_out):
    rows = x_all.shape[0]
    tm = ROW_TILE
    dsk = jnp.repeat(d_skip[0] + d_skip[1], SSD_HEAD_DIM)[None, :]
    wof = w_out[:FNET_WIDTH].astype(BF16)
    wos = w_out[FNET_WIDTH:].astype(BF16)
    row = lambda n: pl.BlockSpec((tm, n), lambda i: (i, 0))
    return pl.pallas_call(
        _even_out_kernel,
        grid=(rows // tm,),
        in_specs=[row(D_MODEL),
                  pl.BlockSpec((1, 1, N_MOD * D_MODEL), lambda i: (_mod_row(i, tm), 0, 0)),
                  row(FNET_WIDTH), row(SSD_INNER), row(SSD_INNER), row(SSD_INNER), row(SSD_INNER),
                  _full((1, SSD_INNER)), _full((1, SSD_INNER)), _full(wof.shape), _full(wos.shape)],
        out_specs=row(D_MODEL),
        out_shape=jax.ShapeDtypeStruct((rows, D_MODEL), F32),
        compiler_params=_params("arbitrary"),
        name="even_out",
    )(x_all, mod, four, yf, yb, u, z, dsk, ssd_norm[None, :], wof, wos)


BF16_ROWS = 2 * SUBLANES
NOT_TOP = 64.0


def _erf_gelu(z):
    return 0.5 * z * (1.0 + lax.erf(z * (1.0 / math.sqrt(2.0))))


def _top_ranks(score, k):
    work, vals = score, []
    rank = jnp.full(score.shape, NOT_TOP, F32)
    for i in range(k):
        m = jnp.max(work, axis=0, keepdims=True)
        hit = work == m
        vals.append(m)
        rank = jnp.where(hit, float(i), rank)
        work = jnp.where(hit, NEG_INF, work)
    return jnp.concatenate(vals, axis=0), rank


def _peer_kernel(x_ref, mod_ref, g_ref, wq_ref, keys_ref, u_ref, vt_ref, o_ref,
                 ht_ref, q_ref, lim_ref, w1_ref, rank2_ref, w2_ref, acc_ref, *, tm, chunk):
    c = pl.program_id(1)
    k = PEER_TOPK
    packed = (PEER_KEYS // BF16_ROWS, BF16_ROWS, tm)
    tsel = PEER_SELECT_TOKENS

    @pl.when(c == 0)
    def _prologue():
        h2 = _modulated_norm(x_ref[...], g_ref[...], mod_ref[0], 3, 4)
        ht = h2.T.astype(BF16)
        ht_ref[...] = ht
        q = jnp.dot(wq_ref[...], ht, preferred_element_type=F32)
        q_ref[...] = q.reshape(2 * PEER_HEADS, PEER_QDIM // 2, tm)
        acc_ref[...] = jnp.zeros_like(acc_ref)
        row8 = lax.broadcasted_iota(jnp.int32, (SUBLANES, tsel), 0)

        def head_body(h, carry):
            for part in range(tm // tsel):
                ts = slice(part * tsel, (part + 1) * tsel)
                s1 = jnp.dot(keys_ref[2 * h], q_ref[2 * h, :, ts].astype(BF16), preferred_element_type=F32)
                s2 = jnp.dot(keys_ref[2 * h + 1], q_ref[2 * h + 1, :, ts].astype(BF16), preferred_element_type=F32)
                v1, r1 = _top_ranks(s1, k)
                v2, r2 = _top_ranks(s2, k)
                groups = [v1[0:1] + v2[0:8], v1[0:1] + v2[8:16], v1[1:2] + v2[0:8], v1[8:16] + v2[0:1]]
                for i in range(2, 8):
                    groups.append(jnp.where(row8 < k // (i + 1), v1[i:i + 1] + v2[0:8], NEG_INF))
                cand = jnp.concatenate(groups, axis=0)
                tau = _top_ranks(cand, k)[0][k - 1:k, :]
                top = v1[0:1] + v2[0:1]
                zsum = jnp.sum(jnp.where(cand >= tau, jnp.exp(cand - top), 0.0), axis=0, keepdims=True)
                a1 = jnp.where(r1 < k, s1, NEG_INF)
                lim = jnp.zeros_like(s1)
                for j in range(k):
                    lim = lim + jnp.where(a1 + v2[j:j + 1] >= tau, 1.0, 0.0)
                lim_ref[h, :, ts] = lim
                w1_ref[h, :, ts] = jnp.exp(s1 - v1[0:1]) / zsum
                small = (PEER_KEYS // BF16_ROWS, BF16_ROWS, tsel)
                rank2_ref[h, :, :, ts] = r2.reshape(small).astype(BF16)
                w2_ref[h, :, :, ts] = jnp.exp(s2 - v2[0:1]).reshape(small).astype(BF16)
            return carry

        lax.fori_loop(0, PEER_HEADS, head_body, 0)

    per_key = PEER_KEYS // BF16_ROWS
    keys_per_sub = PEER_SUB // PEER_KEYS
    total = None
    n_sub = chunk // PEER_SUB
    sub_rows = lambda j: slice(j * PEER_SUB, (j + 1) * PEER_SUB)
    first_dot = lambda j: jnp.dot(u_ref[sub_rows(j), :], ht_ref[...], preferred_element_type=F32)
    z_next = first_dot(0)
    for j in range(n_sub):
        rows = sub_rows(j)
        z = z_next
        if j + 1 < n_sub:
            z_next = first_dot(j + 1)
        act = _erf_gelu(z).reshape(PEER_SUB // BF16_ROWS, BF16_ROWS, tm).astype(BF16)
        parts = []
        for jj in range(keys_per_sub):
            e1 = c * (chunk // PEER_KEYS) + j * keys_per_sub + jj
            gate = jnp.zeros(packed, BF16)
            for h in range(PEER_HEADS):
                lim = jnp.broadcast_to(lim_ref[h, pl.ds(e1, 1), :], (BF16_ROWS, tm)).astype(BF16)
                w1 = jnp.broadcast_to(w1_ref[h, pl.ds(e1, 1), :], (BF16_ROWS, tm)).astype(BF16)
                picked = jnp.where(rank2_ref[h] < lim[None], w2_ref[h], jnp.zeros(packed, BF16))
                gate = gate + picked * w1[None]
            parts.append(act[jj * per_key:(jj + 1) * per_key] * gate)
        a = jnp.concatenate(parts, axis=0).reshape(PEER_SUB, tm)
        d = jnp.dot(vt_ref[:, rows], a, preferred_element_type=F32)
        total = d if total is None else total + d
    acc_ref[...] += total

    @pl.when(c == pl.num_programs(1) - 1)
    def _epilogue():
        gate2 = mod_ref[0][:, 5 * D_MODEL:6 * D_MODEL]
        o_ref[...] = x_ref[...] + gate2 * acc_ref[...].T


def peer_layer(x_rows, mod, norm2, peer_wq, peer_keys, peer_u, peer_v):
    rows = x_rows.shape[0]
    tm, chunk = PEER_TILE, PEER_CHUNK
    wq_t = peer_wq.T.astype(BF16)
    keys = peer_keys.reshape(2 * PEER_HEADS, PEER_KEYS, PEER_QDIM // 2).astype(BF16)
    u = peer_u.astype(BF16)
    vt = peer_v.astype(BF16).T
    head_f32 = pltpu.VMEM((PEER_HEADS, PEER_KEYS, tm), F32)
    head_bf16 = pltpu.VMEM((PEER_HEADS, PEER_KEYS // BF16_ROWS, BF16_ROWS, tm), BF16)
    return pl.pallas_call(
        functools.partial(_peer_kernel, tm=tm, chunk=chunk),
        grid=(rows // tm, PEER_EXPERTS // chunk),
        in_specs=[pl.BlockSpec((tm, D_MODEL), lambda i, c: (i, 0)),
                  pl.BlockSpec((1, 1, N_MOD * D_MODEL), lambda i, c: (_mod_row(i, tm), 0, 0)),
                  _full((1, D_MODEL)), _full(wq_t.shape), _full(keys.shape),
                  pl.BlockSpec((chunk, D_MODEL), lambda i, c: (c, 0)),
                  pl.BlockSpec((D_MODEL, chunk), lambda i, c: (0, c))],
        out_specs=pl.BlockSpec((tm, D_MODEL), lambda i, c: (i, 0)),
        out_shape=jax.ShapeDtypeStruct((rows, D_MODEL), F32),
        scratch_shapes=[pltpu.VMEM((D_MODEL, tm), BF16),
                        pltpu.VMEM((2 * PEER_HEADS, PEER_QDIM // 2, tm), F32),
                        head_f32, head_f32, head_bf16, head_bf16,
                        pltpu.VMEM((D_MODEL, tm), F32)],
        compiler_params=_params("arbitrary", "arbitrary"),
        name="peer",
    )(x_rows, mod, norm2[None, :], wq_t, keys, u, vt)


def _rope_tables(rows):
    l = jnp.arange(SEQ)
    per_axis = MLA_ROPE // 2
    inv = ROPE_THETA ** (-jnp.arange(0, per_axis, 2, dtype=F32) / per_axis)
    ang = jnp.concatenate([(l // GRID_W)[:, None] * inv, (l % GRID_W)[:, None] * inv], axis=-1)
    cos, sin = jnp.cos(ang), jnp.sin(ang)
    one = jnp.ones((SEQ, MLA_NOPE), F32)
    zero16 = jnp.zeros((SEQ, per_axis), F32)
    tail = jnp.zeros((SEQ, HEAD_PAD - MLA_QK), F32)
    zero64 = jnp.zeros((SEQ, MLA_NOPE), F32)
    cos_t = jnp.concatenate([one, cos, cos, tail], axis=1)
    sin_dn = jnp.concatenate([zero64, -sin, zero16, tail], axis=1)
    sin_up = jnp.concatenate([zero64, zero16, sin, tail], axis=1)
    n_ctx = rows - LAT_ROWS
    ident = jnp.concatenate([jnp.ones((n_ctx, MLA_QK), F32), jnp.zeros((n_ctx, HEAD_PAD - MLA_QK), F32)], axis=1)
    zeros = jnp.zeros((n_ctx, HEAD_PAD), F32)
    tile = lambda t, c: jnp.concatenate([t] * BATCH + [c], axis=0)
    return tile(cos_t, ident), tile(sin_up, zeros), tile(sin_dn, zeros)


def _head_norm_rope(t, gain, cos, sin_up, sin_dn):
    ms = jnp.sum(t * t, axis=-1, keepdims=True) * (1.0 / MLA_QK)
    t = t * lax.rsqrt(ms + EPS) * gain
    per_axis = MLA_ROPE // 2
    return t * cos + pltpu.roll(t, per_axis, 1) * sin_up + pltpu.roll(t, HEAD_PAD - per_axis, 1) * sin_dn


def _mla_in_kernel(x_ref, mod_ref, g_ref, wdq_ref, wdkv_ref, wpe_ref, qan_ref, kvan_ref, wuq_ref, wuk_ref, wuv_ref,
                   qn_ref, kn_ref, cos_ref, sup_ref, sdn_ref, q_ref, k_ref, v_ref):
    h = _modulated_norm(x_ref[...], g_ref[...], mod_ref[0], 0, 1).astype(BF16)
    dq = jnp.dot(h, wdq_ref[...], preferred_element_type=F32)
    dkv = jnp.dot(h, wdkv_ref[...], preferred_element_type=F32)
    kpe = jnp.dot(h, wpe_ref[...], preferred_element_type=F32)
    qa = dq * lax.rsqrt(jnp.mean(dq * dq, axis=-1, keepdims=True) + EPS) * qan_ref[...]
    kva = (dkv * lax.rsqrt(jnp.mean(dkv * dkv, axis=-1, keepdims=True) + EPS) * kvan_ref[...]).astype(BF16)
    q = jnp.dot(qa.astype(BF16), wuq_ref[...], preferred_element_type=F32)
    kn = jnp.dot(kva, wuk_ref[...], preferred_element_type=F32)
    v_ref[...] = jnp.dot(kva, wuv_ref[...], preferred_element_type=F32).astype(BF16)
    cos, sup, sdn = cos_ref[...], sup_ref[...], sdn_ref[...]
    scale = MLA_QK ** -0.5
    for hd in range(MLA_HEADS):
        hs = slice(hd * HEAD_PAD, (hd + 1) * HEAD_PAD)
        q_ref[:, hs] = (_head_norm_rope(q[:, hs], qn_ref[...], cos, sup, sdn) * scale).astype(BF16)
        k_ref[:, hs] = _head_norm_rope(kn[:, hs] + kpe, kn_ref[...], cos, sup, sdn).astype(BF16)


def mla_in(x_all, mod, p):
    rows = x_all.shape[0]
    tm = ROW_TILE
    w = p["w_dqkv"]
    wdq = w[:, :MLA_Q_LORA].astype(BF16)
    wdkv = w[:, MLA_Q_LORA:MLA_Q_LORA + MLA_KV_LORA].astype(BF16)
    wpe = jnp.pad(w[:, MLA_Q_LORA + MLA_KV_LORA:], ((0, 0), (MLA_NOPE, HEAD_PAD - MLA_QK))).astype(BF16)
    wuq = jnp.pad(p["w_uq"].reshape(MLA_Q_LORA, MLA_HEADS, MLA_QK), ((0, 0), (0, 0), (0, HEAD_PAD - MLA_QK)))
    wuq = wuq.reshape(MLA_Q_LORA, MLA_HEADS * HEAD_PAD).astype(BF16)
    wukv = p["w_ukv"].reshape(MLA_KV_LORA, MLA_HEADS, MLA_NOPE + MLA_V)
    wuk = jnp.pad(wukv[:, :, :MLA_NOPE], ((0, 0), (0, 0), (0, HEAD_PAD - MLA_NOPE)))
    wuk = wuk.reshape(MLA_KV_LORA, MLA_HEADS * HEAD_PAD).astype(BF16)
    wuv = wukv[:, :, MLA_NOPE:].reshape(MLA_KV_LORA, MLA_HEADS * MLA_V).astype(BF16)
    padg = lambda g: jnp.pad(g, (0, HEAD_PAD - MLA_QK))[None, :]
    cos, sup, sdn = _rope_tables(rows)
    row = lambda n: pl.BlockSpec((tm, n), lambda i: (i, 0))
    weights = [wdq, wdkv, wpe, p["q_a_norm"][None, :], p["kv_a_norm"][None, :], wuq, wuk, wuv,
               padg(p["q_norm"]), padg(p["k_norm"])]
    return pl.pallas_call(
        _mla_in_kernel,
        grid=(rows // tm,),
        in_specs=[row(D_MODEL),
                  pl.BlockSpec((1, 1, N_MOD * D_MODEL), lambda i: (_mod_row(i, tm), 0, 0)),
                  _full((1, D_MODEL))] + [_full(a.shape) for a in weights] + [row(HEAD_PAD)] * 3,
        out_specs=[row(MLA_HEADS * HEAD_PAD), row(MLA_HEADS * HEAD_PAD), row(MLA_HEADS * MLA_V)],
        out_shape=[jax.ShapeDtypeStruct((rows, MLA_HEADS * HEAD_PAD), BF16),
                   jax.ShapeDtypeStruct((rows, MLA_HEADS * HEAD_PAD), BF16),
                   jax.ShapeDtypeStruct((rows, MLA_HEADS * MLA_V), BF16)],
        compiler_params=_params("arbitrary"),
        name="mla_in",
    )(x_all, mod, p["norm1"][None, :], *weights, cos, sup, sdn)


def _attn_kernel(q_ref, kc_ref, kl_ref, vc_ref, vl_ref, o_ref, *, tq, tk):
    lane = lax.broadcasted_iota(jnp.int32, (tq, 2 * MLA_V), 1)
    outs = []
    for hh in range(2):
        hs = slice(hh * HEAD_PAD, (hh + 1) * HEAD_PAD)
        q = q_ref[:, hs]

        def step(k, v, carry):
            m, l, acc = carry
            s = lax.dot_general(q, k, (((1,), (1,)), ((), ())), preferred_element_type=F32)
            m_new = jnp.maximum(m, jnp.max(s, axis=-1, keepdims=True))
            alpha = jnp.exp(m - m_new)
            p = jnp.exp(s - m_new)
            l = alpha * l + jnp.sum(p, axis=-1, keepdims=True)
            acc = alpha * acc + jnp.dot(p.astype(BF16), v, preferred_element_type=F32)
            return m_new, l, acc

        init = (jnp.full((tq, 1), NEG_INF, F32), jnp.zeros((tq, 1), F32), jnp.zeros((tq, 2 * MLA_V), F32))
        carry = step(kc_ref[:, hs], vc_ref[...], init)

        def body(j, carry):
            off = pl.multiple_of(j * tk, tk)
            return step(kl_ref[pl.ds(off, tk), hs], vl_ref[pl.ds(off, tk), :], carry)

        m, l, acc = lax.fori_loop(0, SEQ // tk, body, carry)
        outs.append(acc / l)
    o_ref[...] = jnp.where(lane < MLA_V, outs[0], outs[1]).astype(o_ref.dtype)


def attention(q, k, v):
    tq, tk = ATTN_TQ, ATTN_TK
    nq = SEQ // tq
    ctx0 = LAT_ROWS // CTX_LEN
    return pl.pallas_call(
        functools.partial(_attn_kernel, tq=tq, tk=tk),
        grid=(BATCH, MLA_HEADS // 2, nq),
        in_specs=[pl.BlockSpec((tq, 2 * HEAD_PAD), lambda b, h, i: (b * nq + i, h)),
                  pl.BlockSpec((CTX_LEN, 2 * HEAD_PAD), lambda b, h, i: (ctx0 + b, h)),
                  pl.BlockSpec((SEQ, 2 * HEAD_PAD), lambda b, h, i: (b, h)),
                  pl.BlockSpec((CTX_LEN, 2 * MLA_V), lambda b, h, i: (ctx0 + b, h)),
                  pl.BlockSpec((SEQ, 2 * MLA_V), lambda b, h, i: (b, h))],
        out_specs=pl.BlockSpec((tq, 2 * MLA_V), lambda b, h, i: (b * nq + i, h)),
        out_shape=jax.ShapeDtypeStruct((LAT_ROWS, MLA_HEADS * MLA_V), BF16),
        compiler_params=_params("arbitrary", "arbitrary", "arbitrary"),
        name="attention",
    )(q, k, k, v, v)


def _attn_out_kernel(x_ref, mod_ref, o_ref_in, wo_ref, out_ref):
    gate = mod_ref[0][:, 2 * D_MODEL:3 * D_MODEL]
    out_ref[...] = x_ref[...] + gate * jnp.dot(o_ref_in[...], wo_ref[...], preferred_element_type=F32)


def attn_out(x_all, mod, o, w_o):
    tm = ROW_TILE
    row = lambda n: pl.BlockSpec((tm, n), lambda i: (i, 0))
    return pl.pallas_call(
        _attn_out_kernel,
        grid=(LAT_ROWS // tm,),
        in_specs=[row(D_MODEL), pl.BlockSpec((1, 1, N_MOD * D_MODEL), lambda i: (_mod_row(i, tm), 0, 0)),
                  row(MLA_HEADS * MLA_V), _full(w_o.shape)],
        out_specs=row(D_MODEL),
        out_shape=jax.ShapeDtypeStruct((LAT_ROWS, D_MODEL), F32),
        compiler_params=_params("arbitrary"),
        name="attn_out",
    )(x_all, mod, o, w_o.astype(BF16))


def kernel(x, c, ctx, c_ctx, l0_w_ada, l0_b_ada, l0_norm1, l0_w_in, l0_conv_w, l0_conv_b, l0_dt_bias, l0_a_log, l0_d_skip, l0_ssd_norm, l0_w_out, l0_norm2, l0_peer_wq, l0_peer_keys, l0_peer_u, l0_peer_v, l1_w_ada, l1_b_ada, l1_norm1, l1_w_dqkv, l1_q_a_norm, l1_kv_a_norm, l1_w_uq, l1_w_ukv, l1_q_norm, l1_k_norm, l1_w_o, l1_norm2, l1_peer_wq, l1_peer_keys, l1_peer_u, l1_peer_v):
    x_all = jnp.concatenate([x.reshape(LAT_ROWS, D_MODEL), ctx.reshape(BATCH * CTX_LEN, D_MODEL)], axis=0)

    mod0 = ada_table(c, c_ctx, l0_w_ada, l0_b_ada)
    f, z, xbc, dt = even_in(x_all, mod0, l0_norm1, l0_w_in)
    u = conv_silu(xbc, l0_conv_w, l0_conv_b)
    yf, yb = ssd_scan(u, dt, l0_dt_bias, l0_a_log)
    four = fourier_mix(f)
    x_all = even_out(x_all, mod0, four, yf, yb, u, z, l0_d_skip, l0_ssd_norm, l0_w_out)
    x_all = peer_layer(x_all, mod0, l0_norm2, l0_peer_wq, l0_peer_keys, l0_peer_u, l0_peer_v)

    mod1 = ada_table(c, c_ctx, l1_w_ada, l1_b_ada)
    p1 = dict(norm1=l1_norm1, w_dqkv=l1_w_dqkv, q_a_norm=l1_q_a_norm, kv_a_norm=l1_kv_a_norm, w_uq=l1_w_uq,
              w_ukv=l1_w_ukv, q_norm=l1_q_norm, k_norm=l1_k_norm)
    q, k, v = mla_in(x_all, mod1, p1)
    o = attention(q, k, v)
    x_lat = attn_out(x_all, mod1, o, l1_w_o)
    x_lat = peer_layer(x_lat, mod1, l1_norm2, l1_peer_wq, l1_peer_keys, l1_peer_u, l1_peer_v)
    return x_lat.reshape(BATCH, SEQ, D_MODEL)
```

```python
import functools
import math

import numpy as np
import jax
import jax.numpy as jnp
from jax import lax
from jax.experimental import pallas as pl
from jax.experimental.pallas import tpu as pltpu

D_MODEL = 1024
BATCH = 2
SEQ = 8192
GRID_W = 64
CTX_LEN = 256
EPS = 1e-6
N_MOD = 6
LAT_ROWS = BATCH * SEQ
ALL_ROWS = LAT_ROWS + BATCH * CTX_LEN

FNET_GROUPS = 4
FNET_GROUP_DIM = 128
FNET_WIDTH = FNET_GROUPS * FNET_GROUP_DIM
FFT_N1 = 64
FFT_N2 = 128

SSD_HEADS = 16
SSD_HEAD_DIM = 64
SSD_INNER = SSD_HEADS * SSD_HEAD_DIM
SSD_GROUPS = 4
SSD_STATE = 128
SSD_CONV = 5
SSD_CHUNK = 128
SSD_CONV_DIM = SSD_INNER + 2 * SSD_GROUPS * SSD_STATE
SSD_GROUP_W = SSD_INNER // SSD_GROUPS

MLA_HEADS = 16
MLA_NOPE = 64
MLA_ROPE = 32
MLA_QK = MLA_NOPE + MLA_ROPE
MLA_V = 64
MLA_Q_LORA = 384
MLA_KV_LORA = 256
ROPE_THETA = 10000.0
HEAD_PAD = 128

PEER_HEADS = 8
PEER_KEYS = 128
PEER_EXPERTS = PEER_KEYS * PEER_KEYS
PEER_QDIM = 256
PEER_TOPK = 16

LANES = 128
SUBLANES = 8
VMEM_LIMIT = 56 * 1024 * 1024

ROW_TILE = 256
PEER_TILE = 512
PEER_CHUNK = 512
PEER_SUB = 256
PEER_SELECT_TOKENS = 256
ATTN_TQ = 512
ATTN_TK = 512

F32 = jnp.float32
BF16 = jnp.bfloat16
HIGHEST = lax.Precision.HIGHEST
NEG_INF = float("-inf")


def _params(*sem):
    return pltpu.CompilerParams(dimension_semantics=sem, vmem_limit_bytes=VMEM_LIMIT)


def _mod_row(i, tile):
    return jnp.minimum((i * tile) // SEQ, BATCH)


def _full(shape):
    return pl.BlockSpec(shape, lambda *_: (0,) * len(shape))


def _silu(x):
    return x * (1.0 / (1.0 + jnp.exp(-x)))


def _modulated_norm(x, gain, mod, k_shift, k_scale):
    shift = mod[:, k_shift * D_MODEL:(k_shift + 1) * D_MODEL]
    scale = mod[:, k_scale * D_MODEL:(k_scale + 1) * D_MODEL]
    ms = jnp.mean(x * x, axis=-1, keepdims=True)
    return x * lax.rsqrt(ms + EPS) * gain * (1.0 + scale) + shift


def _bdot(a, b):
    return jnp.dot(a.astype(BF16), b.astype(BF16), preferred_element_type=F32)


def _hdot(a, b):
    return jnp.dot(a, b, precision=HIGHEST, preferred_element_type=F32)


def _ada_kernel(c_ref, w_ref, b_ref, o_ref):
    o_ref[...] = _hdot(_silu(c_ref[...]), w_ref[...]) + b_ref[...]


def ada_table(c, c_ctx, w_ada, b_ada):
    cond = jnp.concatenate([c, c_ctx[None, :], jnp.zeros((SUBLANES - BATCH - 1, D_MODEL), F32)], axis=0)
    tn = 512
    out = pl.pallas_call(
        _ada_kernel,
        grid=(N_MOD * D_MODEL // tn,),
        in_specs=[_full((SUBLANES, D_MODEL)),
                  pl.BlockSpec((D_MODEL, tn), lambda j: (0, j)),
                  pl.BlockSpec((1, tn), lambda j: (0, j))],
        out_specs=pl.BlockSpec((SUBLANES, tn), lambda j: (0, j)),
        out_shape=jax.ShapeDtypeStruct((SUBLANES, N_MOD * D_MODEL), F32),
        compiler_params=_params("arbitrary"),
        name="ada_table",
    )(cond, w_ada, b_ada[None, :])
    return out[:BATCH + 1].reshape(BATCH + 1, 1, N_MOD * D_MODEL)


def _even_in_kernel(x_ref, mod_ref, g_ref, wf_ref, wz_ref, wx_ref, wd_ref, f_ref, z_ref, xbc_ref, dt_ref):
    h = _modulated_norm(x_ref[...], g_ref[...], mod_ref[0], 0, 1).astype(BF16)
    f_ref[...] = jnp.dot(h, wf_ref[...], preferred_element_type=F32)
    z_ref[...] = jnp.dot(h, wz_ref[...], preferred_element_type=F32)
    xbc_ref[...] = jnp.dot(h, wx_ref[...], preferred_element_type=F32)
    dt_ref[...] = jnp.dot(h, wd_ref[...], preferred_element_type=F32)


def even_in(x_all, mod, norm1, w_in):
    rows = x_all.shape[0]
    o1, o2, o3 = FNET_WIDTH, FNET_WIDTH + SSD_INNER, FNET_WIDTH + SSD_INNER + SSD_CONV_DIM
    wf = w_in[:, :o1].astype(BF16)
    wz = w_in[:, o1:o2].astype(BF16)
    wx = w_in[:, o2:o3].astype(BF16)
    wd = jnp.pad(w_in[:, o3:], ((0, 0), (0, LANES - 2 * SSD_HEADS))).astype(BF16)
    tm = ROW_TILE
    row = lambda n: pl.BlockSpec((tm, n), lambda i: (i, 0))
    return pl.pallas_call(
        _even_in_kernel,
        grid=(rows // tm,),
        in_specs=[row(D_MODEL),
                  pl.BlockSpec((1, 1, N_MOD * D_MODEL), lambda i: (_mod_row(i, tm), 0, 0)),
                  _full((1, D_MODEL)),
                  _full(wf.shape), _full(wz.shape), _full(wx.shape), _full(wd.shape)],
        out_specs=[row(FNET_WIDTH), row(SSD_INNER), row(SSD_CONV_DIM), row(LANES)],
        out_shape=[jax.ShapeDtypeStruct((rows, FNET_WIDTH), F32),
                   jax.ShapeDtypeStruct((rows, SSD_INNER), F32),
                   jax.ShapeDtypeStruct((rows, SSD_CONV_DIM), F32),
                   jax.ShapeDtypeStruct((rows, LANES), F32)],
        compiler_params=_params("arbitrary"),
        name="even_in",
    )(x_all, mod, norm1[None, :], wf, wz, wx, wd)


def _conv_kernel(x_ref, prev_ref, next_ref, w_ref, b_ref, o_ref, ext_ref, *, tm):
    row0 = pl.program_id(0) * tm
    in_lat = row0 < LAT_ROWS
    first = jnp.where(in_lat, row0 % SEQ == 0, (row0 - LAT_ROWS) % CTX_LEN == 0)
    last = jnp.where(in_lat, (row0 + tm) % SEQ == 0, (row0 + tm - LAT_ROWS) % CTX_LEN == 0)
    ext_ref[0:SUBLANES, :] = prev_ref[...] * jnp.where(first, 0.0, 1.0)
    ext_ref[SUBLANES:SUBLANES + tm, :] = x_ref[...]
    ext_ref[SUBLANES + tm:2 * SUBLANES + tm, :] = next_ref[...] * jnp.where(last, 0.0, 1.0)
    pad = SSD_CONV // 2
    acc = b_ref[...] + w_ref[0:1, :] * ext_ref[pl.ds(SUBLANES - pad, tm), :]
    for k in range(1, SSD_CONV):
        acc = acc + w_ref[k:k + 1, :] * ext_ref[pl.ds(SUBLANES - pad + k, tm), :]
    o_ref[...] = _silu(acc)


def conv_silu(xbc, conv_w, conv_b):
    rows, ch = xbc.shape
    tm, tc = ROW_TILE, 1024
    assert CTX_LEN % tm == 0 and SEQ % tm == 0
    per = tm // SUBLANES
    n_small = rows // SUBLANES
    w = jnp.pad(conv_w, ((0, SUBLANES - SSD_CONV), (0, 0)))
    return pl.pallas_call(
        functools.partial(_conv_kernel, tm=tm),
        grid=(rows // tm, ch // tc),
        in_specs=[pl.BlockSpec((tm, tc), lambda i, j: (i, j)),
                  pl.BlockSpec((SUBLANES, tc), lambda i, j: (jnp.maximum(i * per - 1, 0), j)),
                  pl.BlockSpec((SUBLANES, tc), lambda i, j: (jnp.minimum((i + 1) * per, n_small - 1), j)),
                  pl.BlockSpec((SUBLANES, tc), lambda i, j: (0, j)),
                  pl.BlockSpec((1, tc), lambda i, j: (0, j))],
        out_specs=pl.BlockSpec((tm, tc), lambda i, j: (i, j)),
        out_shape=jax.ShapeDtypeStruct((rows, ch), F32),
        scratch_shapes=[pltpu.VMEM((tm + 2 * SUBLANES, tc), F32)],
        compiler_params=_params("arbitrary", "arbitrary"),
        name="conv_silu",
    )(xbc, xbc, xbc, w, conv_b[None, :])


def _softplus(x):
    return jnp.maximum(x, 0.0) + jnp.log1p(jnp.exp(-jnp.abs(x)))


def _ssd_direction(u_ref, dt_ref, dtb_ref, alog_ref, exp_ref, state_ref, y_ref, direction):
    cl = SSD_CHUNK
    xs = u_ref[:, :SSD_INNER]
    dtv = _softplus(dt_ref[...] + dtb_ref[...])
    dta = dtv * (-jnp.exp(alog_ref[...]))
    r = lax.broadcasted_iota(jnp.int32, (cl, cl), 0)
    c = lax.broadcasted_iota(jnp.int32, (cl, cl), 1)
    tri = (r >= c) if direction == 0 else (r <= c)
    cs = _hdot(tri.astype(F32), dta)
    cs_t = cs.T
    edge = cs[cl - 1:cl, :] if direction == 0 else cs[0:1, :]
    to_end = jnp.exp(edge - cs)
    from_start = jnp.exp(cs)
    spread = _hdot(jnp.concatenate([dtv, to_end, from_start], axis=0), exp_ref[direction])
    dt_x, te_x, fs_x = spread[:cl], spread[cl:2 * cl], spread[2 * cl:]
    chunk_decay = fs_x[cl - 1:cl, :] if direction == 0 else fs_x[0:1, :]
    xd = xs * dt_x
    xd_b = xd.astype(BF16)
    xte_b = (xd * te_x).astype(BF16)
    lane = lax.broadcasted_iota(jnp.int32, (cl, LANES), 1)
    low = lane < SSD_HEAD_DIM
    for g in range(SSD_GROUPS):
        bg = u_ref[:, SSD_INNER + g * SSD_STATE:SSD_INNER + (g + 1) * SSD_STATE]
        cg = u_ref[:, SSD_INNER + (SSD_GROUPS + g) * SSD_STATE:SSD_INNER + (SSD_GROUPS + g + 1) * SSD_STATE]
        bg_b, cg_b = bg.astype(BF16), cg.astype(BF16)
        cb = lax.dot_general(cg_b, bg_b, (((1,), (1,)), ((), ())), preferred_element_type=F32)
        gs = slice(g * SSD_GROUP_W, (g + 1) * SSD_GROUP_W)
        state = state_ref[g]
        y_off = jnp.dot(cg_b, state.astype(BF16), preferred_element_type=F32) * fs_x[:, gs]
        new_state = jnp.dot(bg.T.astype(BF16), xte_b[:, gs], preferred_element_type=F32)
        state_ref[g] = state * chunk_decay[:, gs] + new_state
        for pair in range(2):
            blk = xd_b[:, g * SSD_GROUP_W + pair * LANES:g * SSD_GROUP_W + (pair + 1) * LANES]
            y_pair = None
            for sub in range(2):
                col = direction * SSD_HEADS + g * 4 + pair * 2 + sub
                seg = cs[:, col:col + 1] - cs_t[col:col + 1, :]
                decay = jnp.exp(jnp.where(tri, seg, NEG_INF))
                m = (cb * decay).astype(BF16)
                half = jnp.where(low if sub == 0 else jnp.logical_not(low), blk, jnp.zeros_like(blk))
                part = jnp.dot(m, half, preferred_element_type=F32)
                y_pair = part if y_pair is None else y_pair + part
            lo = g * SSD_GROUP_W + pair * LANES
            y_ref[:, lo:lo + LANES] = y_pair + y_off[:, pair * LANES:(pair + 1) * LANES]


def _ssd_kernel(uf_ref, ub_ref, dtf_ref, dtb_in_ref, bias_ref, alog_ref, exp_ref, yf_ref, yb_ref, sf_ref, sb_ref):
    @pl.when(pl.program_id(1) == 0)
    def _():
        sf_ref[...] = jnp.zeros_like(sf_ref)
        sb_ref[...] = jnp.zeros_like(sb_ref)

    _ssd_direction(uf_ref, dtf_ref, bias_ref, alog_ref, exp_ref, sf_ref, yf_ref, 0)
    _ssd_direction(ub_ref, dtb_in_ref, bias_ref, alog_ref, exp_ref, sb_ref, yb_ref, 1)


def ssd_scan(u, dt, dt_bias, a_log):
    rows = u.shape[0]
    cl = SSD_CHUNK
    lat_chunks, ctx_chunks = SEQ // cl, CTX_LEN // cl
    steps = ctx_chunks + lat_chunks
    ctx0 = LAT_ROWS // cl

    def fwd_chunk(b, t):
        return jnp.where(t < ctx_chunks, ctx0 + b * ctx_chunks + t, b * lat_chunks + t - ctx_chunks)

    def bwd_chunk(b, t):
        return jnp.where(t < ctx_chunks, ctx0 + b * ctx_chunks + (ctx_chunks - 1 - t),
                         b * lat_chunks + (lat_chunks - 1) - (t - ctx_chunks))

    pad = LANES - 2 * SSD_HEADS
    bias = jnp.pad(dt_bias.reshape(1, -1), ((0, 0), (0, pad)))
    alog = jnp.pad(a_log.reshape(1, -1), ((0, 0), (0, pad)))
    expand = np.zeros((2, LANES, SSD_INNER), np.float32)
    for d in range(2):
        for h in range(SSD_HEADS):
            expand[d, d * SSD_HEADS + h, h * SSD_HEAD_DIM:(h + 1) * SSD_HEAD_DIM] = 1.0
    spec = lambda n, fn: pl.BlockSpec((cl, n), lambda b, t: (fn(b, t), 0))
    return pl.pallas_call(
        _ssd_kernel,
        grid=(BATCH, steps),
        in_specs=[spec(SSD_CONV_DIM, fwd_chunk), spec(SSD_CONV_DIM, bwd_chunk),
                  spec(LANES, fwd_chunk), spec(LANES, bwd_chunk),
                  _full((1, LANES)), _full((1, LANES)), _full((2, LANES, SSD_INNER))],
        out_specs=[spec(SSD_INNER, fwd_chunk), spec(SSD_INNER, bwd_chunk)],
        out_shape=[jax.ShapeDtypeStruct((rows, SSD_INNER), F32)] * 2,
        scratch_shapes=[pltpu.VMEM((SSD_GROUPS, SSD_STATE, SSD_GROUP_W), F32)] * 2,
        compiler_params=_params("arbitrary", "arbitrary"),
        name="ssd_scan",
    )(u, u, dt, dt, bias, alog, jnp.asarray(expand))


def _dft_cos_sin(n):
    k = np.arange(n)
    ang = 2.0 * np.pi * ((k[:, None] * k[None, :]) % n) / n
    return np.cos(ang), np.sin(ang)


def _fft1_kernel(x_ref, m_ref, o_ref):
    o_ref[0] = _hdot(m_ref[...], x_ref[...])


def _fft2_kernel(ar_ref, ai_ref, twr_ref, twi_ref, m2_ref, mc_ref, o_ref):
    ar, ai = ar_ref[0, 0], ai_ref[0, 0]
    twr = jnp.concatenate([twr_ref[0]] * FNET_GROUPS, axis=1)
    twi = jnp.concatenate([twi_ref[0]] * FNET_GROUPS, axis=1)
    p = ar * twr - ai * twi
    q = ar * twi + ai * twr
    uv = _hdot(m2_ref[...], jnp.concatenate([p, q], axis=0))
    n2 = FFT_N2
    for g in range(FNET_GROUPS):
        gs = slice(g * FNET_GROUP_DIM, (g + 1) * FNET_GROUP_DIM)
        o_ref[0, :, gs] = _hdot(jnp.concatenate([uv[:n2, gs], uv[n2:, gs]], axis=1), mc_ref[...])


def _fft_ctx_kernel(x_ref, mc_ref, mp_ref, o_ref):
    x = x_ref[...]
    for g in range(FNET_GROUPS):
        gs = slice(g * FNET_GROUP_DIM, (g + 1) * FNET_GROUP_DIM)
        cs = _hdot(x[:, gs], mc_ref[...])
        stacked = jnp.concatenate([cs[:, :FNET_GROUP_DIM], cs[:, FNET_GROUP_DIM:]], axis=0)
        o_ref[:, gs] = _hdot(mp_ref[...], stacked)


def fourier_mix(f_all):
    rows = f_all.shape[0]
    n1, n2, gd = FFT_N1, FFT_N2, FNET_GROUP_DIM
    row_w = n2 * FNET_WIDTH
    c1, s1 = _dft_cos_sin(n1)
    m1 = jnp.asarray(np.concatenate([c1, -s1], axis=0), F32)
    tn = 8192
    stage1 = pl.pallas_call(
        _fft1_kernel,
        grid=(BATCH, row_w // tn),
        in_specs=[pl.BlockSpec((n1, tn), lambda b, j: (b, j)), _full((2 * n1, n1))],
        out_specs=pl.BlockSpec((1, 2 * n1, tn), lambda b, j: (b, 0, j)),
        out_shape=jax.ShapeDtypeStruct((BATCH, 2 * n1, row_w), F32),
        compiler_params=_params("arbitrary", "arbitrary"),
        name="fft_stage1",
    )(f_all.reshape(rows // n2, row_w), m1)

    k1 = np.arange(n1)[:, None]
    l2 = np.arange(n2)[None, :]
    tw = 2.0 * np.pi * (k1 * l2) / SEQ
    twr = jnp.asarray(np.repeat(np.cos(tw)[:, :, None], gd, axis=2), F32)
    twi = jnp.asarray(np.repeat(-np.sin(tw)[:, :, None], gd, axis=2), F32)
    c2, s2 = _dft_cos_sin(n2)
    m2 = jnp.asarray(np.block([[c2, s2], [-s2, c2]]), F32)
    cc, sc = _dft_cos_sin(gd)
    mc = jnp.asarray(np.concatenate([cc, sc], axis=0) / math.sqrt(SEQ * gd), F32)
    a4 = stage1.reshape(BATCH, 2 * n1, n2, FNET_WIDTH)
    lat = pl.pallas_call(
        _fft2_kernel,
        grid=(BATCH, n1),
        in_specs=[pl.BlockSpec((1, 1, n2, FNET_WIDTH), lambda b, k: (b, k, 0, 0)),
                  pl.BlockSpec((1, 1, n2, FNET_WIDTH), lambda b, k: (b, n1 + k, 0, 0)),
                  pl.BlockSpec((1, n2, gd), lambda b, k: (k, 0, 0)),
                  pl.BlockSpec((1, n2, gd), lambda b, k: (k, 0, 0)),
                  _full((2 * n2, 2 * n2)), _full((2 * gd, gd))],
        out_specs=pl.BlockSpec((1, n2, FNET_WIDTH), lambda b, k: (b, 0, k)),
        out_shape=jax.ShapeDtypeStruct((BATCH, n2, n1 * FNET_WIDTH), F32),
        compiler_params=_params("arbitrary", "arbitrary"),
        name="fft_stage2",
    )(a4, a4, twr, twi, m2, mc)

    cp, sp = _dft_cos_sin(CTX_LEN)
    mp = jnp.asarray(np.concatenate([cp, -sp], axis=1) / math.sqrt(CTX_LEN * gd), F32)
    mcc = jnp.asarray(np.concatenate([cc, sc], axis=1), F32)
    ctx0 = LAT_ROWS // CTX_LEN
    ctx = pl.pallas_call(
        _fft_ctx_kernel,
        grid=(BATCH,),
        in_specs=[pl.BlockSpec((CTX_LEN, FNET_WIDTH), lambda b: (ctx0 + b, 0)),
                  _full((gd, 2 * gd)), _full((CTX_LEN, 2 * CTX_LEN))],
        out_specs=pl.BlockSpec((CTX_LEN, FNET_WIDTH), lambda b: (b, 0)),
        out_shape=jax.ShapeDtypeStruct((BATCH * CTX_LEN, FNET_WIDTH), F32),
        compiler_params=_params("arbitrary"),
        name="fft_ctx",
    )(f_all, mcc, mp)
    return jnp.concatenate([lat.reshape(LAT_ROWS, FNET_WIDTH), ctx], axis=0)


def _even_out_kernel(x_ref, mod_ref, four_ref, yf_ref, yb_ref, xs_ref, z_ref, dsk_ref, gn_ref, wof_ref, wos_ref,
                     o_ref):
    y = yf_ref[...] + yb_ref[...] + dsk_ref[...] * xs_ref[...]
    gated = y * _silu(z_ref[...])
    out = jnp.dot(four_ref[...].astype(BF16), wof_ref[...], preferred_element_type=F32)
    for g in range(SSD_GROUPS):
        gs = slice(g * SSD_GROUP_W, (g + 1) * SSD_GROUP_W)
        v = gated[:, gs]
        normed = v * lax.rsqrt(jnp.mean(v * v, axis=-1, keepdims=True) + EPS) * gn_ref[:, gs]
        out = out + jnp.dot(normed.astype(BF16), wos_ref[gs, :], preferred_element_type=F32)
    gate = mod_ref[0][:, 2 * D_MODEL:3 * D_MODEL]
    o_ref[...] = x_ref[...] + gate * out


def even_out(x_all, mod, four, yf, yb, u, z, d_skip, ssd_norm, w_out):
    rows = x_all.shape[0]
    tm = ROW_TILE
    dsk = jnp.repeat(d_skip[0] + d_skip[1], SSD_HEAD_DIM)[None, :]
    wof = w_out[:FNET_WIDTH].astype(BF16)
    wos = w_out[FNET_WIDTH:].astype(BF16)
    row = lambda n: pl.BlockSpec((tm, n), lambda i: (i, 0))
    return pl.pallas_call(
        _even_out_kernel,
        grid=(rows // tm,),
        in_specs=[row(D_MODEL),
                  pl.BlockSpec((1, 1, N_MOD * D_MODEL), lambda i: (_mod_row(i, tm), 0, 0)),
                  row(FNET_WIDTH), row(SSD_INNER), row(SSD_INNER), row(SSD_INNER), row(SSD_INNER),
                  _full((1, SSD_INNER)), _full((1, SSD_INNER)), _full(wof.shape), _full(wos.shape)],
        out_specs=row(D_MODEL),
        out_shape=jax.ShapeDtypeStruct((rows, D_MODEL), F32),
        compiler_params=_params("arbitrary"),
        name="even_out",
    )(x_all, mod, four, yf, yb, u, z, dsk, ssd_norm[None, :], wof, wos)


BF16_ROWS = 2 * SUBLANES
NOT_TOP = 64.0


def _erf_gelu(z):
    return 0.5 * z * (1.0 + lax.erf(z * (1.0 / math.sqrt(2.0))))


def _top_ranks(score, k, want_rank=True):
    work, vals = score, []
    rank = jnp.full(score.shape, NOT_TOP, F32) if want_rank else None
    for i in range(k):
        m = jnp.max(work, axis=0, keepdims=True)
        hit = work == m
        vals.append(m)
        if want_rank:
            rank = jnp.where(hit, float(i), rank)
        work = jnp.where(hit, NEG_INF, work)
    return jnp.concatenate(vals, axis=0), rank


def _peer_kernel(x_ref, mod_ref, g_ref, wq_ref, keys_ref, u0_ref, ua_ref, ub_ref, vt_ref, o_ref,
                 ht_ref, q_ref, lim_ref, w1_ref, rank2_ref, w2_ref, acc_ref, za_ref, zb_ref, *, tm, chunk):
    c = pl.program_id(1)
    k = PEER_TOPK
    packed = (PEER_KEYS // BF16_ROWS, BF16_ROWS, tm)
    tsel = PEER_SELECT_TOKENS

    @pl.when(c == 0)
    def _prologue():
        h2 = _modulated_norm(x_ref[...], g_ref[...], mod_ref[0], 3, 4)
        ht = h2.T.astype(BF16)
        ht_ref[...] = ht
        q = jnp.dot(wq_ref[...], ht, preferred_element_type=F32)
        q_ref[...] = q.reshape(2 * PEER_HEADS, PEER_QDIM // 2, tm)
        acc_ref[...] = jnp.zeros_like(acc_ref)
        za_ref[...] = jnp.dot(u0_ref[...], ht, preferred_element_type=F32)
        row8 = lax.broadcasted_iota(jnp.int32, (SUBLANES, tsel), 0)

        def head_body(h, carry):
            for part in range(tm // tsel):
                ts = slice(part * tsel, (part + 1) * tsel)
                s1 = jnp.dot(keys_ref[2 * h], q_ref[2 * h, :, ts].astype(BF16), preferred_element_type=F32)
                s2 = jnp.dot(keys_ref[2 * h + 1], q_ref[2 * h + 1, :, ts].astype(BF16), preferred_element_type=F32)
                v1, _ = _top_ranks(s1, k, want_rank=False)
                v2, r2 = _top_ranks(s2, k)
                groups = [v1[0:1] + v2[0:8], v1[0:1] + v2[8:16], v1[1:2] + v2[0:8], v1[8:16] + v2[0:1]]
                for i in range(2, 8):
                    groups.append(jnp.where(row8 < k // (i + 1), v1[i:i + 1] + v2[0:8], NEG_INF))
                cand = jnp.concatenate(groups, axis=0)
                tau = _top_ranks(cand, k, want_rank=False)[0][k - 1:k, :]
                top = v1[0:1] + v2[0:1]
                zsum = jnp.sum(jnp.where(cand >= tau, jnp.exp(cand - top), 0.0), axis=0, keepdims=True)
                a1 = jnp.where(s1 >= v1[k - 1:k], s1, NEG_INF)
                lim = jnp.zeros_like(s1)
                for j in range(k // 2):
                    lim = lim + jnp.where(a1 + v2[j:j + 1] >= tau, 1.0, 0.0)
                best = jnp.zeros_like(tau)
                for j in range(k // 2, k):
                    best = best + jnp.where(v1[0:1] + v2[j:j + 1] >= tau, 1.0, 0.0)
                lim = lim + jnp.where(s1 == v1[0:1], best, 0.0)
                lim_ref[h, :, ts] = lim
                w1_ref[h, :, ts] = jnp.exp(s1 - v1[0:1]) / zsum
                small = (PEER_KEYS // BF16_ROWS, BF16_ROWS, tsel)
                rank2_ref[h, :, :, ts] = r2.reshape(small).astype(BF16)
                w2_ref[h, :, :, ts] = jnp.exp(s2 - v2[0:1]).reshape(small).astype(BF16)
            return carry

        lax.fori_loop(0, PEER_HEADS, head_body, 0)

    per_key = PEER_KEYS // BF16_ROWS
    keys_per_sub = PEER_SUB // PEER_KEYS

    def gated_values(z_ref, which):
        total = None
        for j in range(chunk // PEER_SUB):
            rows = slice(j * PEER_SUB, (j + 1) * PEER_SUB)
            act = _erf_gelu(z_ref[rows, :]).reshape(PEER_SUB // BF16_ROWS, BF16_ROWS, tm).astype(BF16)
            parts = []
            for jj in range(keys_per_sub):
                e1 = (2 * c + which) * (chunk // PEER_KEYS) + j * keys_per_sub + jj
                gate = jnp.zeros(packed, BF16)
                for h in range(PEER_HEADS):
                    lim = jnp.broadcast_to(lim_ref[h, pl.ds(e1, 1), :], (BF16_ROWS, tm)).astype(BF16)
                    w1 = jnp.broadcast_to(w1_ref[h, pl.ds(e1, 1), :], (BF16_ROWS, tm)).astype(BF16)
                    picked = jnp.where(rank2_ref[h] < lim[None], w2_ref[h], jnp.zeros(packed, BF16))
                    gate = gate + picked * w1[None]
                parts.append(act[jj * per_key:(jj + 1) * per_key] * gate)
            a = jnp.concatenate(parts, axis=0).reshape(PEER_SUB, tm)
            cols = slice(which * chunk + j * PEER_SUB, which * chunk + (j + 1) * PEER_SUB)
            d = jnp.dot(vt_ref[:, cols], a, preferred_element_type=F32)
            total = d if total is None else total + d
        return total

    zb_ref[...] = jnp.dot(ua_ref[...], ht_ref[...], preferred_element_type=F32)
    first = gated_values(za_ref, 0)
    za_ref[...] = jnp.dot(ub_ref[...], ht_ref[...], preferred_element_type=F32)
    acc_ref[...] += first + gated_values(zb_ref, 1)

    @pl.when(c == pl.num_programs(1) - 1)
    def _epilogue():
        gate2 = mod_ref[0][:, 5 * D_MODEL:6 * D_MODEL]
        o_ref[...] = x_ref[...] + gate2 * acc_ref[...].T


def peer_layer(x_rows, mod, norm2, peer_wq, peer_keys, peer_u, peer_v):
    rows = x_rows.shape[0]
    tm, chunk = PEER_TILE, PEER_CHUNK
    wq_t = peer_wq.T.astype(BF16)
    keys = peer_keys.reshape(2 * PEER_HEADS, PEER_KEYS, PEER_QDIM // 2).astype(BF16)
    u = peer_u.astype(BF16)
    vt = peer_v.astype(BF16).T
    head_f32 = pltpu.VMEM((PEER_HEADS, PEER_KEYS, tm), F32)
    head_bf16 = pltpu.VMEM((PEER_HEADS, PEER_KEYS // BF16_ROWS, BF16_ROWS, tm), BF16)
    n_chunks = PEER_EXPERTS // chunk
    ahead = lambda k: pl.BlockSpec((chunk, D_MODEL), lambda i, c: (jnp.minimum(2 * c + k, n_chunks - 1), 0))
    return pl.pallas_call(
        functools.partial(_peer_kernel, tm=tm, chunk=chunk),
        grid=(rows // tm, n_chunks // 2),
        in_specs=[pl.BlockSpec((tm, D_MODEL), lambda i, c: (i, 0)),
                  pl.BlockSpec((1, 1, N_MOD * D_MODEL), lambda i, c: (_mod_row(i, tm), 0, 0)),
                  _full((1, D_MODEL)), _full(wq_t.shape), _full(keys.shape),
                  pl.BlockSpec((chunk, D_MODEL), lambda i, c: (0, 0)), ahead(1), ahead(2),
                  pl.BlockSpec((D_MODEL, 2 * chunk), lambda i, c: (0, c))],
        out_specs=pl.BlockSpec((tm, D_MODEL), lambda i, c: (i, 0)),
        out_shape=jax.ShapeDtypeStruct((rows, D_MODEL), F32),
        scratch_shapes=[pltpu.VMEM((D_MODEL, tm), BF16),
                        pltpu.VMEM((2 * PEER_HEADS, PEER_QDIM // 2, tm), F32),
                        head_f32, head_f32, head_bf16, head_bf16,
                        pltpu.VMEM((D_MODEL, tm), F32),
                        pltpu.VMEM((chunk, tm), F32), pltpu.VMEM((chunk, tm), F32)],
        compiler_params=_params("arbitrary", "arbitrary"),
        name="peer",
    )(x_rows, mod, norm2[None, :], wq_t, keys, u, u, u, vt)


def _rope_tables(rows):
    l = jnp.arange(SEQ)
    per_axis = MLA_ROPE // 2
    inv = ROPE_THETA ** (-jnp.arange(0, per_axis, 2, dtype=F32) / per_axis)
    ang = jnp.concatenate([(l // GRID_W)[:, None] * inv, (l % GRID_W)[:, None] * inv], axis=-1)
    cos, sin = jnp.cos(ang), jnp.sin(ang)
    one = jnp.ones((SEQ, MLA_NOPE), F32)
    tail = jnp.zeros((SEQ, HEAD_PAD - MLA_QK), F32)
    zero64 = jnp.zeros((SEQ, MLA_NOPE), F32)
    cos_t = jnp.concatenate([one, cos, cos, tail], axis=1)
    sin_t = jnp.concatenate([zero64, -sin, sin, tail], axis=1)
    n_ctx = rows - LAT_ROWS
    ident = jnp.concatenate([jnp.ones((n_ctx, MLA_QK), F32), jnp.zeros((n_ctx, HEAD_PAD - MLA_QK), F32)], axis=1)
    zeros = jnp.zeros((n_ctx, HEAD_PAD), F32)
    tile = lambda t, c: jnp.concatenate([t] * BATCH + [c], axis=0)
    return tile(cos_t, ident), tile(sin_t, zeros)


def _rope_swap_matrix():
    per_axis = MLA_ROPE // 2
    swap = np.zeros((HEAD_PAD, HEAD_PAD), np.float32)
    for j in range(per_axis):
        swap[MLA_NOPE + per_axis + j, MLA_NOPE + j] = 1.0
        swap[MLA_NOPE + j, MLA_NOPE + per_axis + j] = 1.0
    return swap


def _head_norm_rope(t, gain, cos, sin, ones, swap):
    ms = jnp.dot((t * t).astype(BF16), ones, preferred_element_type=F32) * (1.0 / MLA_QK)
    t = t * lax.rsqrt(ms + EPS) * gain
    return t * cos + jnp.dot(t.astype(BF16), swap, preferred_element_type=F32) * sin


def _mla_in_kernel(x_ref, mod_ref, g_ref, wdq_ref, wdkv_ref, wpe_ref, qan_ref, kvan_ref, wuq_ref, wuk_ref, wuv_ref,
                   qn_ref, kn_ref, ones_ref, swap_ref, cos_ref, sin_ref, q_ref, k_ref, v_ref):
    h = _modulated_norm(x_ref[...], g_ref[...], mod_ref[0], 0, 1).astype(BF16)
    dq = jnp.dot(h, wdq_ref[...], preferred_element_type=F32)
    dkv = jnp.dot(h, wdkv_ref[...], preferred_element_type=F32)
    kpe = jnp.dot(h, wpe_ref[...], preferred_element_type=F32)
    qa = dq * lax.rsqrt(jnp.mean(dq * dq, axis=-1, keepdims=True) + EPS) * qan_ref[...]
    kva = (dkv * lax.rsqrt(jnp.mean(dkv * dkv, axis=-1, keepdims=True) + EPS) * kvan_ref[...]).astype(BF16)
    q = jnp.dot(qa.astype(BF16), wuq_ref[...], preferred_element_type=F32)
    kn = jnp.dot(kva, wuk_ref[...], preferred_element_type=F32)
    v_ref[...] = jnp.dot(kva, wuv_ref[...], preferred_element_type=F32).astype(BF16)
    cos, sin, ones, swap = cos_ref[...], sin_ref[...], ones_ref[...], swap_ref[...]
    scale = MLA_QK ** -0.5 * math.log2(math.e)
    for hd in range(MLA_HEADS):
        hs = slice(hd * HEAD_PAD, (hd + 1) * HEAD_PAD)
        q_ref[:, hs] = (_head_norm_rope(q[:, hs], qn_ref[...], cos, sin, ones, swap) * scale).astype(BF16)
        k_ref[:, hs] = _head_norm_rope(kn[:, hs] + kpe, kn_ref[...], cos, sin, ones, swap).astype(BF16)


def mla_in(x_all, mod, p):
    rows = x_all.shape[0]
    tm = ROW_TILE
    w = p["w_dqkv"]
    wdq = w[:, :MLA_Q_LORA].astype(BF16)
    wdkv = w[:, MLA_Q_LORA:MLA_Q_LORA + MLA_KV_LORA].astype(BF16)
    wpe = jnp.pad(w[:, MLA_Q_LORA + MLA_KV_LORA:], ((0, 0), (MLA_NOPE, HEAD_PAD - MLA_QK))).astype(BF16)
    wuq = jnp.pad(p["w_uq"].reshape(MLA_Q_LORA, MLA_HEADS, MLA_QK), ((0, 0), (0, 0), (0, HEAD_PAD - MLA_QK)))
    wuq = wuq.reshape(MLA_Q_LORA, MLA_HEADS * HEAD_PAD).astype(BF16)
    wukv = p["w_ukv"].reshape(MLA_KV_LORA, MLA_HEADS, MLA_NOPE + MLA_V)
    wuk = jnp.pad(wukv[:, :, :MLA_NOPE], ((0, 0), (0, 0), (0, HEAD_PAD - MLA_NOPE)))
    wuk = wuk.reshape(MLA_KV_LORA, MLA_HEADS * HEAD_PAD).astype(BF16)
    wuv = wukv[:, :, MLA_NOPE:].reshape(MLA_KV_LORA, MLA_HEADS * MLA_V).astype(BF16)
    padg = lambda g: jnp.pad(g, (0, HEAD_PAD - MLA_QK))[None, :]
    cos, sin = _rope_tables(rows)
    row = lambda n: pl.BlockSpec((tm, n), lambda i: (i, 0))
    weights = [wdq, wdkv, wpe, p["q_a_norm"][None, :], p["kv_a_norm"][None, :], wuq, wuk, wuv,
               padg(p["q_norm"]), padg(p["k_norm"]),
               jnp.ones((HEAD_PAD, HEAD_PAD), BF16), jnp.asarray(_rope_swap_matrix(), BF16)]
    return pl.pallas_call(
        _mla_in_kernel,
        grid=(rows // tm,),
        in_specs=[row(D_MODEL),
                  pl.BlockSpec((1, 1, N_MOD * D_MODEL), lambda i: (_mod_row(i, tm), 0, 0)),
                  _full((1, D_MODEL))] + [_full(a.shape) for a in weights] + [row(HEAD_PAD)] * 2,
        out_specs=[row(MLA_HEADS * HEAD_PAD), row(MLA_HEADS * HEAD_PAD), row(MLA_HEADS * MLA_V)],
        out_shape=[jax.ShapeDtypeStruct((rows, MLA_HEADS * HEAD_PAD), BF16),
                   jax.ShapeDtypeStruct((rows, MLA_HEADS * HEAD_PAD), BF16),
                   jax.ShapeDtypeStruct((rows, MLA_HEADS * MLA_V), BF16)],
        compiler_params=_params("arbitrary"),
        name="mla_in",
    )(x_all, mod, p["norm1"][None, :], *weights, cos, sin)


def _attn_kernel(q_ref, kc_ref, kl_ref, vc_ref, vl_ref, o_ref, *, tq, tk):
    heads = (slice(0, HEAD_PAD), slice(HEAD_PAD, 2 * HEAD_PAD))
    qs = [q_ref[:, hs] for hs in heads]

    def values(v, hh):
        lane = lax.broadcasted_iota(jnp.int32, v.shape, 1)
        own = (lane < MLA_V) if hh == 0 else (lane >= MLA_V)
        return jnp.where(own, v, jnp.ones_like(v))

    def step(k2, v, carry):
        new = []
        for hh in range(2):
            m, acc = carry[hh]
            s = lax.dot_general(qs[hh], k2[:, heads[hh]], (((1,), (1,)), ((), ())), preferred_element_type=F32)
            m_new = jnp.maximum(m, jnp.max(s, axis=-1, keepdims=True))
            p = jnp.exp2(s - m_new).astype(BF16)
            acc = jnp.exp2(m - m_new) * acc + jnp.dot(p, values(v, hh), preferred_element_type=F32)
            new.append((m_new, acc))
        return tuple(new)

    init = tuple((jnp.full((tq, 1), NEG_INF, F32), jnp.zeros((tq, 2 * MLA_V), F32)) for _ in range(2))
    carry = step(kc_ref[...], vc_ref[...], init)

    def body(j, carry):
        off = pl.multiple_of(j * tk, tk)
        return step(kl_ref[pl.ds(off, tk), :], vl_ref[pl.ds(off, tk), :], carry)

    carry = lax.fori_loop(0, SEQ // tk, body, carry, unroll=2)
    lane = lax.broadcasted_iota(jnp.int32, (tq, 2 * MLA_V), 1)
    outs = [acc / pltpu.roll(acc, MLA_V, 1) for _, acc in carry]
    o_ref[...] = jnp.where(lane < MLA_V, outs[0], outs[1]).astype(o_ref.dtype)


def attention(q, k, v):
    tq, tk = ATTN_TQ, ATTN_TK
    nq = SEQ // tq
    ctx0 = LAT_ROWS // CTX_LEN
    return pl.pallas_call(
        functools.partial(_attn_kernel, tq=tq, tk=tk),
        grid=(BATCH, MLA_HEADS // 2, nq),
        in_specs=[pl.BlockSpec((tq, 2 * HEAD_PAD), lambda b, h, i: (b * nq + i, h)),
                  pl.BlockSpec((CTX_LEN, 2 * HEAD_PAD), lambda b, h, i: (ctx0 + b, h)),
                  pl.BlockSpec((SEQ, 2 * HEAD_PAD), lambda b, h, i: (b, h)),
                  pl.BlockSpec((CTX_LEN, 2 * MLA_V), lambda b, h, i: (ctx0 + b, h)),
                  pl.BlockSpec((SEQ, 2 * MLA_V), lambda b, h, i: (b, h))],
        out_specs=pl.BlockSpec((tq, 2 * MLA_V), lambda b, h, i: (b * nq + i, h)),
        out_shape=jax.ShapeDtypeStruct((LAT_ROWS, MLA_HEADS * MLA_V), BF16),
        compiler_params=_params("arbitrary", "arbitrary", "arbitrary"),
        name="attention",
    )(q, k, k, v, v)


def _attn_out_kernel(x_ref, mod_ref, o_ref_in, wo_ref, out_ref):
    gate = mod_ref[0][:, 2 * D_MODEL:3 * D_MODEL]
    out_ref[...] = x_ref[...] + gate * jnp.dot(o_ref_in[...], wo_ref[...], preferred_element_type=F32)


def attn_out(x_all, mod, o, w_o):
    tm = ROW_TILE
    row = lambda n: pl.BlockSpec((tm, n), lambda i: (i, 0))
    return pl.pallas_call(
        _attn_out_kernel,
        grid=(LAT_ROWS // tm,),
        in_specs=[row(D_MODEL), pl.BlockSpec((1, 1, N_MOD * D_MODEL), lambda i: (_mod_row(i, tm), 0, 0)),
                  row(MLA_HEADS * MLA_V), _full(w_o.shape)],
        out_specs=row(D_MODEL),
        out_shape=jax.ShapeDtypeStruct((LAT_ROWS, D_MODEL), F32),
        compiler_params=_params("arbitrary"),
        name="attn_out",
    )(x_all, mod, o, w_o.astype(BF16))


def kernel(x, c, ctx, c_ctx, l0_w_ada, l0_b_ada, l0_norm1, l0_w_in, l0_conv_w, l0_conv_b, l0_dt_bias, l0_a_log, l0_d_skip, l0_ssd_norm, l0_w_out, l0_norm2, l0_peer_wq, l0_peer_keys, l0_peer_u, l0_peer_v, l1_w_ada, l1_b_ada, l1_norm1, l1_w_dqkv, l1_q_a_norm, l1_kv_a_norm, l1_w_uq, l1_w_ukv, l1_q_norm, l1_k_norm, l1_w_o, l1_norm2, l1_peer_wq, l1_peer_keys, l1_peer_u, l1_peer_v):
    x_all = jnp.concatenate([x.reshape(LAT_ROWS, D_MODEL), ctx.reshape(BATCH * CTX_LEN, D_MODEL)], axis=0)

    mod0 = ada_table(c, c_ctx, l0_w_ada, l0_b_ada)
    f, z, xbc, dt = even_in(x_all, mod0, l0_norm1, l0_w_in)
    u = conv_silu(xbc, l0_conv_w, l0_conv_b)
    yf, yb = ssd_scan(u, dt, l0_dt_bias, l0_a_log)
    four = fourier_mix(f)
    x_all = even_out(x_all, mod0, four, yf, yb, u, z, l0_d_skip, l0_ssd_norm, l0_w_out)
    x_all = peer_layer(x_all, mod0, l0_norm2, l0_peer_wq, l0_peer_keys, l0_peer_u, l0_peer_v)

    mod1 = ada_table(c, c_ctx, l1_w_ada, l1_b_ada)
    p1 = dict(norm1=l1_norm1, w_dqkv=l1_w_dqkv, q_a_norm=l1_q_a_norm, kv_a_norm=l1_kv_a_norm, w_uq=l1_w_uq,
              w_ukv=l1_w_ukv, q_norm=l1_q_norm, k_norm=l1_k_norm)
    q, k, v = mla_in(x_all, mod1, p1)
    o = attention(q, k, v)
    x_lat = attn_out(x_all, mod1, o, l1_w_o)
    x_lat = peer_layer(x_lat, mod1, l1_norm2, l1_peer_wq, l1_peer_keys, l1_peer_u, l1_peer_v)
    return x_lat.reshape(BATCH, SEQ, D_MODEL)
```

```python
import functools
import math

import numpy as np
import jax
import jax.numpy as jnp
from jax import lax
from jax.experimental import pallas as pl
from jax.experimental.pallas import tpu as pltpu

D_MODEL = 1024
BATCH = 2
SEQ = 8192
GRID_W = 64
CTX_LEN = 256
EPS = 1e-6
N_MOD = 6
LAT_ROWS = BATCH * SEQ
ALL_ROWS = LAT_ROWS + BATCH * CTX_LEN

FNET_GROUPS = 4
FNET_GROUP_DIM = 128
FNET_WIDTH = FNET_GROUPS * FNET_GROUP_DIM
FFT_N1 = 64
FFT_N2 = 128

SSD_HEADS = 16
SSD_HEAD_DIM = 64
SSD_INNER = SSD_HEADS * SSD_HEAD_DIM
SSD_GROUPS = 4
SSD_STATE = 128
SSD_CONV = 5
SSD_CHUNK = 128
SSD_CONV_DIM = SSD_INNER + 2 * SSD_GROUPS * SSD_STATE
SSD_GROUP_W = SSD_INNER // SSD_GROUPS

MLA_HEADS = 16
MLA_NOPE = 64
MLA_ROPE = 32
MLA_QK = MLA_NOPE + MLA_ROPE
MLA_V = 64
MLA_Q_LORA = 384
MLA_KV_LORA = 256
ROPE_THETA = 10000.0
HEAD_PAD = 128

PEER_HEADS = 8
PEER_KEYS = 128
PEER_EXPERTS = PEER_KEYS * PEER_KEYS
PEER_QDIM = 256
PEER_TOPK = 16

LANES = 128
SUBLANES = 8
VMEM_LIMIT = 56 * 1024 * 1024

ROW_TILE = 256
PEER_TILE = 512
PEER_CHUNK = 512
PEER_SUB = 256
PEER_SELECT_TOKENS = 256
ATTN_TQ = 512
ATTN_TK = 512

F32 = jnp.float32
BF16 = jnp.bfloat16
HIGHEST = lax.Precision.HIGHEST
NEG_INF = float("-inf")


def _params(*sem):
    return pltpu.CompilerParams(dimension_semantics=sem, vmem_limit_bytes=VMEM_LIMIT)


def _mod_row(i, tile):
    return jnp.minimum((i * tile) // SEQ, BATCH)


def _full(shape):
    return pl.BlockSpec(shape, lambda *_: (0,) * len(shape))


def _silu(x):
    return x * (1.0 / (1.0 + jnp.exp(-x)))


def _modulated_norm(x, gain, mod, k_shift, k_scale):
    shift = mod[:, k_shift * D_MODEL:(k_shift + 1) * D_MODEL]
    scale = mod[:, k_scale * D_MODEL:(k_scale + 1) * D_MODEL]
    ms = jnp.mean(x * x, axis=-1, keepdims=True)
    return x * lax.rsqrt(ms + EPS) * gain * (1.0 + scale) + shift


def _bdot(a, b):
    return jnp.dot(a.astype(BF16), b.astype(BF16), preferred_element_type=F32)


def _hdot(a, b):
    return jnp.dot(a, b, precision=HIGHEST, preferred_element_type=F32)


def _bf16_terms(x, n):
    terms = []
    for _ in range(n):
        t = x.astype(BF16)
        terms.append(t)
        x = x - t.astype(F32)
    return terms


def _dot_exact_rhs(a, b01):
    return sum(jnp.dot(t, b01, preferred_element_type=F32) for t in _bf16_terms(a, 3))


def _dot_exact_lhs(a01, b):
    return sum(jnp.dot(a01, t, preferred_element_type=F32) for t in _bf16_terms(b, 3))


def _dot3(a, b):
    a_hi, a_lo = _bf16_terms(a, 2)
    b_hi, b_lo = _bf16_terms(b, 2)
    dot = lambda x, y: jnp.dot(x, y, preferred_element_type=F32)
    return dot(a_hi, b_hi) + (dot(a_hi, b_lo) + dot(a_lo, b_hi))


def _ada_kernel(c_ref, w_ref, b_ref, o_ref):
    o_ref[...] = _hdot(_silu(c_ref[...]), w_ref[...]) + b_ref[...]


def ada_table(c, c_ctx, w_ada, b_ada):
    cond = jnp.concatenate([c, c_ctx[None, :], jnp.zeros((SUBLANES - BATCH - 1, D_MODEL), F32)], axis=0)
    tn = 512
    out = pl.pallas_call(
        _ada_kernel,
        grid=(N_MOD * D_MODEL // tn,),
        in_specs=[_full((SUBLANES, D_MODEL)),
                  pl.BlockSpec((D_MODEL, tn), lambda j: (0, j)),
                  pl.BlockSpec((1, tn), lambda j: (0, j))],
        out_specs=pl.BlockSpec((SUBLANES, tn), lambda j: (0, j)),
        out_shape=jax.ShapeDtypeStruct((SUBLANES, N_MOD * D_MODEL), F32),
        compiler_params=_params("arbitrary"),
        name="ada_table",
    )(cond, w_ada, b_ada[None, :])
    return out[:BATCH + 1].reshape(BATCH + 1, 1, N_MOD * D_MODEL)


def _even_in_kernel(x_ref, mod_ref, g_ref, wf_ref, wz_ref, wx_ref, wd_ref, f_ref, z_ref, xbc_ref, dt_ref):
    h = _modulated_norm(x_ref[...], g_ref[...], mod_ref[0], 0, 1).astype(BF16)
    f_ref[...] = jnp.dot(h, wf_ref[...], preferred_element_type=F32)
    z_ref[...] = jnp.dot(h, wz_ref[...], preferred_element_type=F32)
    xbc_ref[...] = jnp.dot(h, wx_ref[...], preferred_element_type=F32)
    dt_ref[...] = jnp.dot(h, wd_ref[...], preferred_element_type=F32)


def even_in(x_all, mod, norm1, w_in):
    rows = x_all.shape[0]
    o1, o2, o3 = FNET_WIDTH, FNET_WIDTH + SSD_INNER, FNET_WIDTH + SSD_INNER + SSD_CONV_DIM
    wf = w_in[:, :o1].astype(BF16)
    wz = w_in[:, o1:o2].astype(BF16)
    wx = w_in[:, o2:o3].astype(BF16)
    wd = jnp.pad(w_in[:, o3:], ((0, 0), (0, LANES - 2 * SSD_HEADS))).astype(BF16)
    tm = ROW_TILE
    row = lambda n: pl.BlockSpec((tm, n), lambda i: (i, 0))
    return pl.pallas_call(
        _even_in_kernel,
        grid=(rows // tm,),
        in_specs=[row(D_MODEL),
                  pl.BlockSpec((1, 1, N_MOD * D_MODEL), lambda i: (_mod_row(i, tm), 0, 0)),
                  _full((1, D_MODEL)),
                  _full(wf.shape), _full(wz.shape), _full(wx.shape), _full(wd.shape)],
        out_specs=[row(FNET_WIDTH), row(SSD_INNER), row(SSD_CONV_DIM), row(LANES)],
        out_shape=[jax.ShapeDtypeStruct((rows, FNET_WIDTH), F32),
                   jax.ShapeDtypeStruct((rows, SSD_INNER), F32),
                   jax.ShapeDtypeStruct((rows, SSD_CONV_DIM), F32),
                   jax.ShapeDtypeStruct((rows, LANES), F32)],
        compiler_params=_params("arbitrary"),
        name="even_in",
    )(x_all, mod, norm1[None, :], wf, wz, wx, wd)


def _conv_kernel(x_ref, prev_ref, next_ref, w_ref, b_ref, o_ref, ext_ref, *, tm):
    row0 = pl.program_id(0) * tm
    in_lat = row0 < LAT_ROWS
    first = jnp.where(in_lat, row0 % SEQ == 0, (row0 - LAT_ROWS) % CTX_LEN == 0)
    last = jnp.where(in_lat, (row0 + tm) % SEQ == 0, (row0 + tm - LAT_ROWS) % CTX_LEN == 0)
    ext_ref[0:SUBLANES, :] = prev_ref[...] * jnp.where(first, 0.0, 1.0)
    ext_ref[SUBLANES:SUBLANES + tm, :] = x_ref[...]
    ext_ref[SUBLANES + tm:2 * SUBLANES + tm, :] = next_ref[...] * jnp.where(last, 0.0, 1.0)
    pad = SSD_CONV // 2
    acc = b_ref[...] + w_ref[0:1, :] * ext_ref[pl.ds(SUBLANES - pad, tm), :]
    for k in range(1, SSD_CONV):
        acc = acc + w_ref[k:k + 1, :] * ext_ref[pl.ds(SUBLANES - pad + k, tm), :]
    o_ref[...] = _silu(acc)


def conv_silu(xbc, conv_w, conv_b):
    rows, ch = xbc.shape
    tm, tc = ROW_TILE, 1024
    assert CTX_LEN % tm == 0 and SEQ % tm == 0
    per = tm // SUBLANES
    n_small = rows // SUBLANES
    w = jnp.pad(conv_w, ((0, SUBLANES - SSD_CONV), (0, 0)))
    return pl.pallas_call(
        functools.partial(_conv_kernel, tm=tm),
        grid=(rows // tm, ch // tc),
        in_specs=[pl.BlockSpec((tm, tc), lambda i, j: (i, j)),
                  pl.BlockSpec((SUBLANES, tc), lambda i, j: (jnp.maximum(i * per - 1, 0), j)),
                  pl.BlockSpec((SUBLANES, tc), lambda i, j: (jnp.minimum((i + 1) * per, n_small - 1), j)),
                  pl.BlockSpec((SUBLANES, tc), lambda i, j: (0, j)),
                  pl.BlockSpec((1, tc), lambda i, j: (0, j))],
        out_specs=pl.BlockSpec((tm, tc), lambda i, j: (i, j)),
        out_shape=jax.ShapeDtypeStruct((rows, ch), F32),
        scratch_shapes=[pltpu.VMEM((tm + 2 * SUBLANES, tc), F32)],
        compiler_params=_params("arbitrary", "arbitrary"),
        name="conv_silu",
    )(xbc, xbc, xbc, w, conv_b[None, :])


def _softplus(x):
    return jnp.maximum(x, 0.0) + jnp.log1p(jnp.exp(-jnp.abs(x)))


def _ssd_direction(u_ref, dt_ref, dtb_ref, alog_ref, exp_ref, state_ref, y_ref, direction):
    cl = SSD_CHUNK
    xs = u_ref[:, :SSD_INNER]
    dtv = _softplus(dt_ref[...] + dtb_ref[...])
    dta = dtv * (-jnp.exp(alog_ref[...]))
    r = lax.broadcasted_iota(jnp.int32, (cl, cl), 0)
    c = lax.broadcasted_iota(jnp.int32, (cl, cl), 1)
    tri = (r >= c) if direction == 0 else (r <= c)
    cs = _dot_exact_lhs(tri.astype(F32).astype(BF16), dta)
    cs_t = cs.T
    edge = cs[cl - 1:cl, :] if direction == 0 else cs[0:1, :]
    to_end = jnp.exp(edge - cs)
    from_start = jnp.exp(cs)
    spread = _dot_exact_rhs(jnp.concatenate([dtv, to_end, from_start], axis=0), exp_ref[direction])
    dt_x, te_x, fs_x = spread[:cl], spread[cl:2 * cl], spread[2 * cl:]
    chunk_decay = fs_x[cl - 1:cl, :] if direction == 0 else fs_x[0:1, :]
    xd = xs * dt_x
    xd_b = xd.astype(BF16)
    xte_b = (xd * te_x).astype(BF16)
    lane = lax.broadcasted_iota(jnp.int32, (cl, LANES), 1)
    low = lane < SSD_HEAD_DIM
    for g in range(SSD_GROUPS):
        bg = u_ref[:, SSD_INNER + g * SSD_STATE:SSD_INNER + (g + 1) * SSD_STATE]
        cg = u_ref[:, SSD_INNER + (SSD_GROUPS + g) * SSD_STATE:SSD_INNER + (SSD_GROUPS + g + 1) * SSD_STATE]
        bg_b, cg_b = bg.astype(BF16), cg.astype(BF16)
        cb = lax.dot_general(cg_b, bg_b, (((1,), (1,)), ((), ())), preferred_element_type=F32)
        gs = slice(g * SSD_GROUP_W, (g + 1) * SSD_GROUP_W)
        state = state_ref[g]
        y_off = jnp.dot(cg_b, state.astype(BF16), preferred_element_type=F32) * fs_x[:, gs]
        new_state = jnp.dot(bg.T.astype(BF16), xte_b[:, gs], preferred_element_type=F32)
        state_ref[g] = state * chunk_decay[:, gs] + new_state
        for pair in range(2):
            blk = xd_b[:, g * SSD_GROUP_W + pair * LANES:g * SSD_GROUP_W + (pair + 1) * LANES]
            y_pair = None
            for sub in range(2):
                col = direction * SSD_HEADS + g * 4 + pair * 2 + sub
                seg = cs[:, col:col + 1] - cs_t[col:col + 1, :]
                decay = jnp.exp(jnp.where(tri, seg, NEG_INF))
                m = (cb * decay).astype(BF16)
                half = jnp.where(low if sub == 0 else jnp.logical_not(low), blk, jnp.zeros_like(blk))
                part = jnp.dot(m, half, preferred_element_type=F32)
                y_pair = part if y_pair is None else y_pair + part
            lo = g * SSD_GROUP_W + pair * LANES
            y_ref[:, lo:lo + LANES] = y_pair + y_off[:, pair * LANES:(pair + 1) * LANES]


def _ssd_kernel(uf_ref, ub_ref, dtf_ref, dtb_in_ref, bias_ref, alog_ref, exp_ref, yf_ref, yb_ref, sf_ref, sb_ref):
    @pl.when(pl.program_id(1) == 0)
    def _():
        sf_ref[...] = jnp.zeros_like(sf_ref)
        sb_ref[...] = jnp.zeros_like(sb_ref)

    _ssd_direction(uf_ref, dtf_ref, bias_ref, alog_ref, exp_ref, sf_ref, yf_ref, 0)
    _ssd_direction(ub_ref, dtb_in_ref, bias_ref, alog_ref, exp_ref, sb_ref, yb_ref, 1)


def ssd_scan(u, dt, dt_bias, a_log):
    rows = u.shape[0]
    cl = SSD_CHUNK
    lat_chunks, ctx_chunks = SEQ // cl, CTX_LEN // cl
    steps = ctx_chunks + lat_chunks
    ctx0 = LAT_ROWS // cl

    def fwd_chunk(b, t):
        return jnp.where(t < ctx_chunks, ctx0 + b * ctx_chunks + t, b * lat_chunks + t - ctx_chunks)

    def bwd_chunk(b, t):
        return jnp.where(t < ctx_chunks, ctx0 + b * ctx_chunks + (ctx_chunks - 1 - t),
                         b * lat_chunks + (lat_chunks - 1) - (t - ctx_chunks))

    pad = LANES - 2 * SSD_HEADS
    bias = jnp.pad(dt_bias.reshape(1, -1), ((0, 0), (0, pad)))
    alog = jnp.pad(a_log.reshape(1, -1), ((0, 0), (0, pad)))
    expand = np.zeros((2, LANES, SSD_INNER), np.float32)
    for d in range(2):
        for h in range(SSD_HEADS):
            expand[d, d * SSD_HEADS + h, h * SSD_HEAD_DIM:(h + 1) * SSD_HEAD_DIM] = 1.0
    spec = lambda n, fn: pl.BlockSpec((cl, n), lambda b, t: (fn(b, t), 0))
    return pl.pallas_call(
        _ssd_kernel,
        grid=(BATCH, steps),
        in_specs=[spec(SSD_CONV_DIM, fwd_chunk), spec(SSD_CONV_DIM, bwd_chunk),
                  spec(LANES, fwd_chunk), spec(LANES, bwd_chunk),
                  _full((1, LANES)), _full((1, LANES)), _full((2, LANES, SSD_INNER))],
        out_specs=[spec(SSD_INNER, fwd_chunk), spec(SSD_INNER, bwd_chunk)],
        out_shape=[jax.ShapeDtypeStruct((rows, SSD_INNER), F32)] * 2,
        scratch_shapes=[pltpu.VMEM((SSD_GROUPS, SSD_STATE, SSD_GROUP_W), F32)] * 2,
        compiler_params=_params("arbitrary", "arbitrary"),
        name="ssd_scan",
    )(u, u, dt, dt, bias, alog, jnp.asarray(expand, BF16))


def _dft_cos_sin(n):
    k = np.arange(n)
    ang = 2.0 * np.pi * ((k[:, None] * k[None, :]) % n) / n
    return np.cos(ang), np.sin(ang)


def _fft1_kernel(x_ref, m_ref, o_ref):
    o_ref[0] = _dot3(m_ref[...], x_ref[...])


def _fft2_kernel(ar_ref, ai_ref, twr_ref, twi_ref, m2_ref, mc_ref, o_ref):
    ar, ai = ar_ref[0, 0], ai_ref[0, 0]
    twr = jnp.concatenate([twr_ref[0]] * FNET_GROUPS, axis=1)
    twi = jnp.concatenate([twi_ref[0]] * FNET_GROUPS, axis=1)
    p = ar * twr - ai * twi
    q = ar * twi + ai * twr
    uv = _dot3(m2_ref[...], jnp.concatenate([p, q], axis=0))
    n2 = FFT_N2
    for g in range(FNET_GROUPS):
        gs = slice(g * FNET_GROUP_DIM, (g + 1) * FNET_GROUP_DIM)
        o_ref[0, :, gs] = _dot3(jnp.concatenate([uv[:n2, gs], uv[n2:, gs]], axis=1), mc_ref[...])


def _fft_ctx_kernel(x_ref, mc_ref, mp_ref, o_ref):
    x = x_ref[...]
    for g in range(FNET_GROUPS):
        gs = slice(g * FNET_GROUP_DIM, (g + 1) * FNET_GROUP_DIM)
        cs = _dot3(x[:, gs], mc_ref[...])
        stacked = jnp.concatenate([cs[:, :FNET_GROUP_DIM], cs[:, FNET_GROUP_DIM:]], axis=0)
        o_ref[:, gs] = _dot3(mp_ref[...], stacked)


def fourier_mix(f_all):
    rows = f_all.shape[0]
    n1, n2, gd = FFT_N1, FFT_N2, FNET_GROUP_DIM
    row_w = n2 * FNET_WIDTH
    c1, s1 = _dft_cos_sin(n1)
    m1 = jnp.asarray(np.concatenate([c1, -s1], axis=0), F32)
    tn = 8192
    stage1 = pl.pallas_call(
        _fft1_kernel,
        grid=(BATCH, row_w // tn),
        in_specs=[pl.BlockSpec((n1, tn), lambda b, j: (b, j)), _full((2 * n1, n1))],
        out_specs=pl.BlockSpec((1, 2 * n1, tn), lambda b, j: (b, 0, j)),
        out_shape=jax.ShapeDtypeStruct((BATCH, 2 * n1, row_w), F32),
        compiler_params=_params("arbitrary", "arbitrary"),
        name="fft_stage1",
    )(f_all.reshape(rows // n2, row_w), m1)

    k1 = np.arange(n1)[:, None]
    l2 = np.arange(n2)[None, :]
    tw = 2.0 * np.pi * (k1 * l2) / SEQ
    twr = jnp.asarray(np.repeat(np.cos(tw)[:, :, None], gd, axis=2), F32)
    twi = jnp.asarray(np.repeat(-np.sin(tw)[:, :, None], gd, axis=2), F32)
    c2, s2 = _dft_cos_sin(n2)
    m2 = jnp.asarray(np.block([[c2, s2], [-s2, c2]]), F32)
    cc, sc = _dft_cos_sin(gd)
    mc = jnp.asarray(np.concatenate([cc, sc], axis=0) / math.sqrt(SEQ * gd), F32)
    a4 = stage1.reshape(BATCH, 2 * n1, n2, FNET_WIDTH)
    lat = pl.pallas_call(
        _fft2_kernel,
        grid=(BATCH, n1),
        in_specs=[pl.BlockSpec((1, 1, n2, FNET_WIDTH), lambda b, k: (b, k, 0, 0)),
                  pl.BlockSpec((1, 1, n2, FNET_WIDTH), lambda b, k: (b, n1 + k, 0, 0)),
                  pl.BlockSpec((1, n2, gd), lambda b, k: (k, 0, 0)),
                  pl.BlockSpec((1, n2, gd), lambda b, k: (k, 0, 0)),
                  _full((2 * n2, 2 * n2)), _full((2 * gd, gd))],
        out_specs=pl.BlockSpec((1, n2, FNET_WIDTH), lambda b, k: (b, 0, k)),
        out_shape=jax.ShapeDtypeStruct((BATCH, n2, n1 * FNET_WIDTH), F32),
        compiler_params=_params("arbitrary", "arbitrary"),
        name="fft_stage2",
    )(a4, a4, twr, twi, m2, mc)

    cp, sp = _dft_cos_sin(CTX_LEN)
    mp = jnp.asarray(np.concatenate([cp, -sp], axis=1) / math.sqrt(CTX_LEN * gd), F32)
    mcc = jnp.asarray(np.concatenate([cc, sc], axis=1), F32)
    ctx0 = LAT_ROWS // CTX_LEN
    ctx = pl.pallas_call(
        _fft_ctx_kernel,
        grid=(BATCH,),
        in_specs=[pl.BlockSpec((CTX_LEN, FNET_WIDTH), lambda b: (ctx0 + b, 0)),
                  _full((gd, 2 * gd)), _full((CTX_LEN, 2 * CTX_LEN))],
        out_specs=pl.BlockSpec((CTX_LEN, FNET_WIDTH), lambda b: (b, 0)),
        out_shape=jax.ShapeDtypeStruct((BATCH * CTX_LEN, FNET_WIDTH), F32),
        compiler_params=_params("arbitrary"),
        name="fft_ctx",
    )(f_all, mcc, mp)
    return jnp.concatenate([lat.reshape(LAT_ROWS, FNET_WIDTH), ctx], axis=0)


def _even_out_kernel(x_ref, mod_ref, four_ref, yf_ref, yb_ref, xs_ref, z_ref, dsk_ref, gn_ref, wof_ref, wos_ref,
                     o_ref):
    y = yf_ref[...] + yb_ref[...] + dsk_ref[...] * xs_ref[...]
    gated = y * _silu(z_ref[...])
    out = jnp.dot(four_ref[...].astype(BF16), wof_ref[...], preferred_element_type=F32)
    for g in range(SSD_GROUPS):
        gs = slice(g * SSD_GROUP_W, (g + 1) * SSD_GROUP_W)
        v = gated[:, gs]
        normed = v * lax.rsqrt(jnp.mean(v * v, axis=-1, keepdims=True) + EPS) * gn_ref[:, gs]
        out = out + jnp.dot(normed.astype(BF16), wos_ref[gs, :], preferred_element_type=F32)
    gate = mod_ref[0][:, 2 * D_MODEL:3 * D_MODEL]
    o_ref[...] = x_ref[...] + gate * out


def even_out(x_all, mod, four, yf, yb, u, z, d_skip, ssd_norm, w_out):
    rows = x_all.shape[0]
    tm = ROW_TILE
    dsk = jnp.repeat(d_skip[0] + d_skip[1], SSD_HEAD_DIM)[None, :]
    wof = w_out[:FNET_WIDTH].astype(BF16)
    wos = w_out[FNET_WIDTH:].astype(BF16)
    row = lambda n: pl.BlockSpec((tm, n), lambda i: (i, 0))
    return pl.pallas_call(
        _even_out_kernel,
        grid=(rows // tm,),
        in_specs=[row(D_MODEL),
                  pl.BlockSpec((1, 1, N_MOD * D_MODEL), lambda i: (_mod_row(i, tm), 0, 0)),
                  row(FNET_WIDTH), row(SSD_INNER), row(SSD_INNER), row(SSD_INNER), row(SSD_INNER),
                  _full((1, SSD_INNER)), _full((1, SSD_INNER)), _full(wof.shape), _full(wos.shape)],
        out_specs=row(D_MODEL),
        out_shape=jax.ShapeDtypeStruct((rows, D_MODEL), F32),
        compiler_params=_params("arbitrary"),
        name="even_out",
    )(x_all, mod, four, yf, yb, u, z, dsk, ssd_norm[None, :], wof, wos)


BF16_ROWS = 2 * SUBLANES
NOT_TOP = 64.0


def _erf_gelu(z):
    return 0.5 * z * (1.0 + lax.erf(z * (1.0 / math.sqrt(2.0))))


def _top_ranks(score, k, want_rank=True):
    work, vals = score, []
    rank = jnp.full(score.shape, NOT_TOP, F32) if want_rank else None
    for i in range(k):
        m = jnp.max(work, axis=0, keepdims=True)
        hit = work == m
        vals.append(m)
        if want_rank:
            rank = jnp.where(hit, float(i), rank)
        work = jnp.where(hit, NEG_INF, work)
    return jnp.concatenate(vals, axis=0), rank


def _peer_kernel(x_ref, mod_ref, g_ref, wq_ref, keys_ref, u0_ref, ua_ref, ub_ref, vt_ref, o_ref,
                 ht_ref, q_ref, lim_ref, w1_ref, rank2_ref, w2_ref, acc_ref, za_ref, zb_ref, *, tm, chunk):
    c = pl.program_id(1)
    k = PEER_TOPK
    packed = (PEER_KEYS // BF16_ROWS, BF16_ROWS, tm)
    tsel = PEER_SELECT_TOKENS

    @pl.when(c == 0)
    def _prologue():
        h2 = _modulated_norm(x_ref[...], g_ref[...], mod_ref[0], 3, 4)
        ht = h2.T.astype(BF16)
        ht_ref[...] = ht
        q = jnp.dot(wq_ref[...], ht, preferred_element_type=F32)
        q_ref[...] = q.reshape(2 * PEER_HEADS, PEER_QDIM // 2, tm)
        acc_ref[...] = jnp.zeros_like(acc_ref)
        za_ref[...] = jnp.dot(u0_ref[...], ht, preferred_element_type=F32)
        row8 = lax.broadcasted_iota(jnp.int32, (SUBLANES, tsel), 0)

        def head_body(h, carry):
            for part in range(tm // tsel):
                ts = slice(part * tsel, (part + 1) * tsel)
                s1 = jnp.dot(keys_ref[2 * h], q_ref[2 * h, :, ts].astype(BF16), preferred_element_type=F32)
                s2 = jnp.dot(keys_ref[2 * h + 1], q_ref[2 * h + 1, :, ts].astype(BF16), preferred_element_type=F32)
                v1, _ = _top_ranks(s1, k, want_rank=False)
                v2, r2 = _top_ranks(s2, k)
                groups = [v1[0:1] + v2[0:8], v1[0:1] + v2[8:16], v1[1:2] + v2[0:8], v1[8:16] + v2[0:1]]
                for i in range(2, 8):
                    groups.append(jnp.where(row8 < k // (i + 1), v1[i:i + 1] + v2[0:8], NEG_INF))
                cand = jnp.concatenate(groups, axis=0)
                tau = _top_ranks(cand, k, want_rank=False)[0][k - 1:k, :]
                top = v1[0:1] + v2[0:1]
                zsum = jnp.sum(jnp.where(cand >= tau, jnp.exp(cand - top), 0.0), axis=0, keepdims=True)
                a1 = jnp.where(s1 >= v1[k - 1:k], s1, NEG_INF)
                lim = jnp.zeros_like(s1)
                for j in range(k // 2):
                    lim = lim + jnp.where(a1 + v2[j:j + 1] >= tau, 1.0, 0.0)
                best = jnp.zeros_like(tau)
                for j in range(k // 2, k):
                    best = best + jnp.where(v1[0:1] + v2[j:j + 1] >= tau, 1.0, 0.0)
                lim = lim + jnp.where(s1 == v1[0:1], best, 0.0)
                lim_ref[h, :, ts] = lim
                w1_ref[h, :, ts] = jnp.exp(s1 - v1[0:1]) / zsum
                small = (PEER_KEYS // BF16_ROWS, BF16_ROWS, tsel)
                rank2_ref[h, :, :, ts] = r2.reshape(small).astype(BF16)
                w2_ref[h, :, :, ts] = jnp.exp(s2 - v2[0:1]).reshape(small).astype(BF16)
            return carry

        lax.fori_loop(0, PEER_HEADS, head_body, 0)

    per_key = PEER_KEYS // BF16_ROWS
    keys_per_sub = PEER_SUB // PEER_KEYS

    def gated_values(z_ref, which):
        total = None
        for j in range(chunk // PEER_SUB):
            rows = slice(j * PEER_SUB, (j + 1) * PEER_SUB)
            act = _erf_gelu(z_ref[rows, :]).reshape(PEER_SUB // BF16_ROWS, BF16_ROWS, tm).astype(BF16)
            parts = []
            for jj in range(keys_per_sub):
                e1 = (2 * c + which) * (chunk // PEER_KEYS) + j * keys_per_sub + jj
                gate = jnp.zeros(packed, BF16)
                for h in range(PEER_HEADS):
                    lim = jnp.broadcast_to(lim_ref[h, pl.ds(e1, 1), :], (BF16_ROWS, tm)).astype(BF16)
                    w1 = jnp.broadcast_to(w1_ref[h, pl.ds(e1, 1), :], (BF16_ROWS, tm)).astype(BF16)
                    picked = jnp.where(rank2_ref[h] < lim[None], w2_ref[h], jnp.zeros(packed, BF16))
                    gate = gate + picked * w1[None]
                parts.append(act[jj * per_key:(jj + 1) * per_key] * gate)
            a = jnp.concatenate(parts, axis=0).reshape(PEER_SUB, tm)
            cols = slice(which * chunk + j * PEER_SUB, which * chunk + (j + 1) * PEER_SUB)
            d = jnp.dot(vt_ref[:, cols], a, preferred_element_type=F32)
            total = d if total is None else total + d
        return total

    zb_ref[...] = jnp.dot(ua_ref[...], ht_ref[...], preferred_element_type=F32)
    first = gated_values(za_ref, 0)
    za_ref[...] = jnp.dot(ub_ref[...], ht_ref[...], preferred_element_type=F32)
    acc_ref[...] += first + gated_values(zb_ref, 1)

    @pl.when(c == pl.num_programs(1) - 1)
    def _epilogue():
        gate2 = mod_ref[0][:, 5 * D_MODEL:6 * D_MODEL]
        o_ref[...] = x_ref[...] + gate2 * acc_ref[...].T


def peer_layer(x_rows, mod, norm2, peer_wq, peer_keys, peer_u, peer_v):
    rows = x_rows.shape[0]
    tm, chunk = PEER_TILE, PEER_CHUNK
    wq_t = peer_wq.T.astype(BF16)
    keys = peer_keys.reshape(2 * PEER_HEADS, PEER_KEYS, PEER_QDIM // 2).astype(BF16)
    u = peer_u.astype(BF16)
    vt = peer_v.astype(BF16).T
    head_f32 = pltpu.VMEM((PEER_HEADS, PEER_KEYS, tm), F32)
    head_bf16 = pltpu.VMEM((PEER_HEADS, PEER_KEYS // BF16_ROWS, BF16_ROWS, tm), BF16)
    n_chunks = PEER_EXPERTS // chunk
    ahead = lambda k: pl.BlockSpec((chunk, D_MODEL), lambda i, c: (jnp.minimum(2 * c + k, n_chunks - 1), 0))
    return pl.pallas_call(
        functools.partial(_peer_kernel, tm=tm, chunk=chunk),
        grid=(rows // tm, n_chunks // 2),
        in_specs=[pl.BlockSpec((tm, D_MODEL), lambda i, c: (i, 0)),
                  pl.BlockSpec((1, 1, N_MOD * D_MODEL), lambda i, c: (_mod_row(i, tm), 0, 0)),
                  _full((1, D_MODEL)), _full(wq_t.shape), _full(keys.shape),
                  pl.BlockSpec((chunk, D_MODEL), lambda i, c: (0, 0)), ahead(1), ahead(2),
                  pl.BlockSpec((D_MODEL, 2 * chunk), lambda i, c: (0, c))],
        out_specs=pl.BlockSpec((tm, D_MODEL), lambda i, c: (i, 0)),
        out_shape=jax.ShapeDtypeStruct((rows, D_MODEL), F32),
        scratch_shapes=[pltpu.VMEM((D_MODEL, tm), BF16),
                        pltpu.VMEM((2 * PEER_HEADS, PEER_QDIM // 2, tm), F32),
                        head_f32, head_f32, head_bf16, head_bf16,
                        pltpu.VMEM((D_MODEL, tm), F32),
                        pltpu.VMEM((chunk, tm), F32), pltpu.VMEM((chunk, tm), F32)],
        compiler_params=_params("arbitrary", "arbitrary"),
        name="peer",
    )(x_rows, mod, norm2[None, :], wq_t, keys, u, u, u, vt)


def _rope_tables(rows):
    l = jnp.arange(SEQ)
    per_axis = MLA_ROPE // 2
    inv = ROPE_THETA ** (-jnp.arange(0, per_axis, 2, dtype=F32) / per_axis)
    ang = jnp.concatenate([(l // GRID_W)[:, None] * inv, (l % GRID_W)[:, None] * inv], axis=-1)
    cos, sin = jnp.cos(ang), jnp.sin(ang)
    one = jnp.ones((SEQ, MLA_NOPE), F32)
    tail = jnp.zeros((SEQ, HEAD_PAD - MLA_QK), F32)
    zero64 = jnp.zeros((SEQ, MLA_NOPE), F32)
    cos_t = jnp.concatenate([one, cos, cos, tail], axis=1)
    sin_t = jnp.concatenate([zero64, -sin, sin, tail], axis=1)
    n_ctx = rows - LAT_ROWS
    ident = jnp.concatenate([jnp.ones((n_ctx, MLA_QK), F32), jnp.zeros((n_ctx, HEAD_PAD - MLA_QK), F32)], axis=1)
    zeros = jnp.zeros((n_ctx, HEAD_PAD), F32)
    tile = lambda t, c: jnp.concatenate([t] * BATCH + [c], axis=0)
    return tile(cos_t, ident), tile(sin_t, zeros)


def _rope_swap_matrix():
    per_axis = MLA_ROPE // 2
    swap = np.zeros((HEAD_PAD, HEAD_PAD), np.float32)
    for j in range(per_axis):
        swap[MLA_NOPE + per_axis + j, MLA_NOPE + j] = 1.0
        swap[MLA_NOPE + j, MLA_NOPE + per_axis + j] = 1.0
    return swap


def _head_norm_rope(t, gain, cos, sin, ones, swap):
    ms = jnp.dot((t * t).astype(BF16), ones, preferred_element_type=F32) * (1.0 / MLA_QK)
    t = t * lax.rsqrt(ms + EPS) * gain
    return t * cos + jnp.dot(t.astype(BF16), swap, preferred_element_type=F32) * sin


def _mla_in_kernel(x_ref, mod_ref, g_ref, wdq_ref, wdkv_ref, wpe_ref, qan_ref, kvan_ref, wuq_ref, wuk_ref, wuv_ref,
                   qn_ref, kn_ref, ones_ref, swap_ref, cos_ref, sin_ref, q_ref, k_ref, v_ref):
    h = _modulated_norm(x_ref[...], g_ref[...], mod_ref[0], 0, 1).astype(BF16)
    dq = jnp.dot(h, wdq_ref[...], preferred_element_type=F32)
    dkv = jnp.dot(h, wdkv_ref[...], preferred_element_type=F32)
    kpe = jnp.dot(h, wpe_ref[...], preferred_element_type=F32)
    qa = dq * lax.rsqrt(jnp.mean(dq * dq, axis=-1, keepdims=True) + EPS) * qan_ref[...]
    kva = (dkv * lax.rsqrt(jnp.mean(dkv * dkv, axis=-1, keepdims=True) + EPS) * kvan_ref[...]).astype(BF16)
    q = jnp.dot(qa.astype(BF16), wuq_ref[...], preferred_element_type=F32)
    kn = jnp.dot(kva, wuk_ref[...], preferred_element_type=F32)
    v_ref[...] = jnp.dot(kva, wuv_ref[...], preferred_element_type=F32).astype(BF16)
    cos, sin, ones, swap = cos_ref[...], sin_ref[...], ones_ref[...], swap_ref[...]
    scale = MLA_QK ** -0.5 * math.log2(math.e)
    for hd in range(MLA_HEADS):
        hs = slice(hd * HEAD_PAD, (hd + 1) * HEAD_PAD)
        q_ref[:, hs] = (_head_norm_rope(q[:, hs], qn_ref[...], cos, sin, ones, swap) * scale).astype(BF16)
        k_ref[:, hs] = _head_norm_rope(kn[:, hs] + kpe, kn_ref[...], cos, sin, ones, swap).astype(BF16)


def mla_in(x_all, mod, p):
    rows = x_all.shape[0]
    tm = ROW_TILE
    w = p["w_dqkv"]
    wdq = w[:, :MLA_Q_LORA].astype(BF16)
    wdkv = w[:, MLA_Q_LORA:MLA_Q_LORA + MLA_KV_LORA].astype(BF16)
    wpe = jnp.pad(w[:, MLA_Q_LORA + MLA_KV_LORA:], ((0, 0), (MLA_NOPE, HEAD_PAD - MLA_QK))).astype(BF16)
    wuq = jnp.pad(p["w_uq"].reshape(MLA_Q_LORA, MLA_HEADS, MLA_QK), ((0, 0), (0, 0), (0, HEAD_PAD - MLA_QK)))
    wuq = wuq.reshape(MLA_Q_LORA, MLA_HEADS * HEAD_PAD).astype(BF16)
    wukv = p["w_ukv"].reshape(MLA_KV_LORA, MLA_HEADS, MLA_NOPE + MLA_V)
    wuk = jnp.pad(wukv[:, :, :MLA_NOPE], ((0, 0), (0, 0), (0, HEAD_PAD - MLA_NOPE)))
    wuk = wuk.reshape(MLA_KV_LORA, MLA_HEADS * HEAD_PAD).astype(BF16)
    wuv = wukv[:, :, MLA_NOPE:].reshape(MLA_KV_LORA, MLA_HEADS * MLA_V).astype(BF16)
    padg = lambda g: jnp.pad(g, (0, HEAD_PAD - MLA_QK))[None, :]
    cos, sin = _rope_tables(rows)
    row = lambda n: pl.BlockSpec((tm, n), lambda i: (i, 0))
    weights = [wdq, wdkv, wpe, p["q_a_norm"][None, :], p["kv_a_norm"][None, :], wuq, wuk, wuv,
               padg(p["q_norm"]), padg(p["k_norm"]),
               jnp.ones((HEAD_PAD, HEAD_PAD), BF16), jnp.asarray(_rope_swap_matrix(), BF16)]
    return pl.pallas_call(
        _mla_in_kernel,
        grid=(rows // tm,),
        in_specs=[row(D_MODEL),
                  pl.BlockSpec((1, 1, N_MOD * D_MODEL), lambda i: (_mod_row(i, tm), 0, 0)),
                  _full((1, D_MODEL))] + [_full(a.shape) for a in weights] + [row(HEAD_PAD)] * 2,
        out_specs=[row(MLA_HEADS * HEAD_PAD), row(MLA_HEADS * HEAD_PAD), row(MLA_HEADS * MLA_V)],
        out_shape=[jax.ShapeDtypeStruct((rows, MLA_HEADS * HEAD_PAD), BF16),
                   jax.ShapeDtypeStruct((rows, MLA_HEADS * HEAD_PAD), BF16),
                   jax.ShapeDtypeStruct((rows, MLA_HEADS * MLA_V), BF16)],
        compiler_params=_params("arbitrary"),
        name="mla_in",
    )(x_all, mod, p["norm1"][None, :], *weights, cos, sin)


ATTN_SAFE_SHIFT = 50.0


def _attn_kernel(q_ref, kc_ref, kl_ref, vc_ref, vl_ref, o_ref, knorm_ref, *, tq, tk):
    heads = (slice(0, HEAD_PAD), slice(HEAD_PAD, 2 * HEAD_PAD))
    qs = [q_ref[:, hs] for hs in heads]
    n_lat = SEQ // tk

    @pl.when(pl.program_id(2) == 0)
    def _key_norms():
        def sq_max(k2, hh):
            kf = k2[:, heads[hh]].astype(F32)
            return jnp.max(jnp.sum(kf * kf, axis=-1, keepdims=True), axis=0, keepdims=True)

        for hh in range(2):
            best = sq_max(kc_ref[...], hh)
            best = lax.fori_loop(
                0, n_lat, lambda j, b: jnp.maximum(b, sq_max(kl_ref[pl.ds(pl.multiple_of(j * tk, tk), tk), :], hh)), best)
            knorm_ref[hh] = jnp.broadcast_to(jnp.sqrt(best), (SUBLANES, LANES))

    def values(v, hh):
        lane = lax.broadcasted_iota(jnp.int32, v.shape, 1)
        own = (lane < MLA_V) if hh == 0 else (lane >= MLA_V)
        return jnp.where(own, v, jnp.ones_like(v))

    def scores(hh, k2):
        return lax.dot_general(qs[hh], k2[:, heads[hh]], (((1,), (1,)), ((), ())), preferred_element_type=F32)

    def finish(accs):
        lane = lax.broadcasted_iota(jnp.int32, (tq, 2 * MLA_V), 1)
        outs = [acc / pltpu.roll(acc, MLA_V, 1) for acc in accs]
        o_ref[...] = jnp.where(lane < MLA_V, outs[0], outs[1]).astype(o_ref.dtype)

    def lat_chunk(j):
        off = pl.multiple_of(j * tk, tk)
        return kl_ref[pl.ds(off, tk), :], vl_ref[pl.ds(off, tk), :]

    bounds = []
    for hh in range(2):
        qf = qs[hh].astype(F32)
        qn = jnp.sqrt(jnp.sum(qf * qf, axis=-1, keepdims=True))
        bounds.append(qn * knorm_ref[hh, 0:1, 0:1] * 1.001 + 1e-3)
    widest = jnp.max(jnp.maximum(bounds[0], bounds[1]))

    def fixed_shift():
        def step(k2, v, accs):
            return tuple(accs[hh] + jnp.dot(jnp.exp2(scores(hh, k2) - bounds[hh]).astype(BF16), values(v, hh),
                                            preferred_element_type=F32) for hh in range(2))

        zero = jnp.zeros((tq, 2 * MLA_V), F32)
        accs = step(kc_ref[...], vc_ref[...], (zero, zero))
        finish(lax.fori_loop(0, n_lat, lambda j, a: step(*lat_chunk(j), a), accs, unroll=4))

    def running_max():
        def step(k2, v, carry):
            new = []
            for hh in range(2):
                m, acc = carry[hh]
                s = scores(hh, k2)
                m_new = jnp.maximum(m, jnp.max(s, axis=-1, keepdims=True))
                p = jnp.exp2(s - m_new).astype(BF16)
                acc = jnp.exp2(m - m_new) * acc + jnp.dot(p, values(v, hh), preferred_element_type=F32)
                new.append((m_new, acc))
            return tuple(new)

        init = tuple((jnp.full((tq, 1), NEG_INF, F32), jnp.zeros((tq, 2 * MLA_V), F32)) for _ in range(2))
        carry = step(kc_ref[...], vc_ref[...], init)
        carry = lax.fori_loop(0, n_lat, lambda j, c: step(*lat_chunk(j), c), carry)
        finish([acc for _, acc in carry])

    lax.cond(widest <= ATTN_SAFE_SHIFT, fixed_shift, running_max)


def attention(q, k, v):
    tq, tk = ATTN_TQ, ATTN_TK
    nq = SEQ // tq
    ctx0 = LAT_ROWS // CTX_LEN
    return pl.pallas_call(
        functools.partial(_attn_kernel, tq=tq, tk=tk),
        grid=(BATCH, MLA_HEADS // 2, nq),
        in_specs=[pl.BlockSpec((tq, 2 * HEAD_PAD), lambda b, h, i: (b * nq + i, h)),
                  pl.BlockSpec((CTX_LEN, 2 * HEAD_PAD), lambda b, h, i: (ctx0 + b, h)),
                  pl.BlockSpec((SEQ, 2 * HEAD_PAD), lambda b, h, i: (b, h)),
                  pl.BlockSpec((CTX_LEN, 2 * MLA_V), lambda b, h, i: (ctx0 + b, h)),
                  pl.BlockSpec((SEQ, 2 * MLA_V), lambda b, h, i: (b, h))],
        out_specs=pl.BlockSpec((tq, 2 * MLA_V), lambda b, h, i: (b * nq + i, h)),
        out_shape=jax.ShapeDtypeStruct((LAT_ROWS, MLA_HEADS * MLA_V), BF16),
        scratch_shapes=[pltpu.VMEM((2, SUBLANES, LANES), F32)],
        compiler_params=_params("arbitrary", "arbitrary", "arbitrary"),
        name="attention",
    )(q, k, k, v, v)


def _attn_out_kernel(x_ref, mod_ref, o_ref_in, wo_ref, out_ref):
    gate = mod_ref[0][:, 2 * D_MODEL:3 * D_MODEL]
    out_ref[...] = x_ref[...] + gate * jnp.dot(o_ref_in[...], wo_ref[...], preferred_element_type=F32)


def attn_out(x_all, mod, o, w_o):
    tm = ROW_TILE
    row = lambda n: pl.BlockSpec((tm, n), lambda i: (i, 0))
    return pl.pallas_call(
        _attn_out_kernel,
        grid=(LAT_ROWS // tm,),
        in_specs=[row(D_MODEL), pl.BlockSpec((1, 1, N_MOD * D_MODEL), lambda i: (_mod_row(i, tm), 0, 0)),
                  row(MLA_HEADS * MLA_V), _full(w_o.shape)],
        out_specs=row(D_MODEL),
        out_shape=jax.ShapeDtypeStruct((LAT_ROWS, D_MODEL), F32),
        compiler_params=_params("arbitrary"),
        name="attn_out",
    )(x_all, mod, o, w_o.astype(BF16))


def kernel(x, c, ctx, c_ctx, l0_w_ada, l0_b_ada, l0_norm1, l0_w_in, l0_conv_w, l0_conv_b, l0_dt_bias, l0_a_log, l0_d_skip, l0_ssd_norm, l0_w_out, l0_norm2, l0_peer_wq, l0_peer_keys, l0_peer_u, l0_peer_v, l1_w_ada, l1_b_ada, l1_norm1, l1_w_dqkv, l1_q_a_norm, l1_kv_a_norm, l1_w_uq, l1_w_ukv, l1_q_norm, l1_k_norm, l1_w_o, l1_norm2, l1_peer_wq, l1_peer_keys, l1_peer_u, l1_peer_v):
    x_all = jnp.concatenate([x.reshape(LAT_ROWS, D_MODEL), ctx.reshape(BATCH * CTX_LEN, D_MODEL)], axis=0)

    mod0 = ada_table(c, c_ctx, l0_w_ada, l0_b_ada)
    f, z, xbc, dt = even_in(x_all, mod0, l0_norm1, l0_w_in)
    u = conv_silu(xbc, l0_conv_w, l0_conv_b)
    yf, yb = ssd_scan(u, dt, l0_dt_bias, l0_a_log)
    four = fourier_mix(f)
    x_all = even_out(x_all, mod0, four, yf, yb, u, z, l0_d_skip, l0_ssd_norm, l0_w_out)
    x_all = peer_layer(x_all, mod0, l0_norm2, l0_peer_wq, l0_peer_keys, l0_peer_u, l0_peer_v)

    mod1 = ada_table(c, c_ctx, l1_w_ada, l1_b_ada)
    p1 = dict(norm1=l1_norm1, w_dqkv=l1_w_dqkv, q_a_norm=l1_q_a_norm, kv_a_norm=l1_kv_a_norm, w_uq=l1_w_uq,
              w_ukv=l1_w_ukv, q_norm=l1_q_norm, k_norm=l1_k_norm)
    q, k, v = mla_in(x_all, mod1, p1)
    o = attention(q, k, v)
    x_lat = attn_out(x_all, mod1, o, l1_w_o)
    x_lat = peer_layer(x_lat, mod1, l1_norm2, l1_peer_wq, l1_peer_keys, l1_peer_u, l1_peer_v)
    return x_lat.reshape(BATCH, SEQ, D_MODEL)
```

```python
import functools
import math

import numpy as np
import jax
import jax.numpy as jnp
from jax import lax
from jax.experimental import pallas as pl
from jax.experimental.pallas import tpu as pltpu

D_MODEL = 1024
BATCH = 2
SEQ = 8192
GRID_W = 64
CTX_LEN = 256
EPS = 1e-6
N_MOD = 6
LAT_ROWS = BATCH * SEQ
ALL_ROWS = LAT_ROWS + BATCH * CTX_LEN

FNET_GROUPS = 4
FNET_GROUP_DIM = 128
FNET_WIDTH = FNET_GROUPS * FNET_GROUP_DIM
FFT_N1 = 64
FFT_N2 = 128

SSD_HEADS = 16
SSD_HEAD_DIM = 64
SSD_INNER = SSD_HEADS * SSD_HEAD_DIM
SSD_GROUPS = 4
SSD_STATE = 128
SSD_CONV = 5
SSD_CHUNK = 128
SSD_CONV_DIM = SSD_INNER + 2 * SSD_GROUPS * SSD_STATE
SSD_GROUP_W = SSD_INNER // SSD_GROUPS

MLA_HEADS = 16
MLA_NOPE = 64
MLA_ROPE = 32
MLA_QK = MLA_NOPE + MLA_ROPE
MLA_V = 64
MLA_Q_LORA = 384
MLA_KV_LORA = 256
ROPE_THETA = 10000.0
HEAD_PAD = 128

PEER_HEADS = 8
PEER_KEYS = 128
PEER_EXPERTS = PEER_KEYS * PEER_KEYS
PEER_QDIM = 256
PEER_TOPK = 16

LANES = 128
SUBLANES = 8
VMEM_LIMIT = 56 * 1024 * 1024

ROW_TILE = 256
PEER_TILE = 512
PEER_CHUNK = 512
PEER_CHUNKS_PER_STEP = 2
PEER_SUB = 256
PEER_SELECT_TOKENS = 128
ATTN_TQ = 512
ATTN_TK = 512

F32 = jnp.float32
BF16 = jnp.bfloat16
HIGHEST = lax.Precision.HIGHEST
NEG_INF = float("-inf")


def _params(*sem):
    return pltpu.CompilerParams(dimension_semantics=sem, vmem_limit_bytes=VMEM_LIMIT)


def _mod_row(i, tile):
    return jnp.minimum((i * tile) // SEQ, BATCH)


def _full(shape):
    return pl.BlockSpec(shape, lambda *_: (0,) * len(shape))


def _silu(x):
    return x * (1.0 / (1.0 + jnp.exp(-x)))


def _modulated_norm(x, gain, mod, k_shift, k_scale):
    shift = mod[:, k_shift * D_MODEL:(k_shift + 1) * D_MODEL]
    scale = mod[:, k_scale * D_MODEL:(k_scale + 1) * D_MODEL]
    ms = jnp.mean(x * x, axis=-1, keepdims=True)
    return x * lax.rsqrt(ms + EPS) * gain * (1.0 + scale) + shift


def _bdot(a, b):
    return jnp.dot(a.astype(BF16), b.astype(BF16), preferred_element_type=F32)


def _hdot(a, b):
    return jnp.dot(a, b, precision=HIGHEST, preferred_element_type=F32)


def _bf16_terms(x, n):
    terms = []
    for _ in range(n):
        t = x.astype(BF16)
        terms.append(t)
        x = x - t.astype(F32)
    return terms


def _dot_exact_rhs(a, b01, terms=3):
    return sum(jnp.dot(t, b01, preferred_element_type=F32) for t in _bf16_terms(a, terms))


def _dot_exact_lhs(a01, b):
    return sum(jnp.dot(a01, t, preferred_element_type=F32) for t in _bf16_terms(b, 3))


def _dot3(a, b):
    a_hi, a_lo = _bf16_terms(a, 2)
    b_hi, b_lo = _bf16_terms(b, 2)
    dot = lambda x, y: jnp.dot(x, y, preferred_element_type=F32)
    return dot(a_hi, b_hi) + (dot(a_hi, b_lo) + dot(a_lo, b_hi))


def _ada_kernel(c_ref, w_ref, b_ref, o_ref):
    o_ref[...] = _hdot(_silu(c_ref[...]), w_ref[...]) + b_ref[...]


def ada_table(c, c_ctx, w_ada, b_ada):
    cond = jnp.concatenate([c, c_ctx[None, :], jnp.zeros((SUBLANES - BATCH - 1, D_MODEL), F32)], axis=0)
    tn = 512
    out = pl.pallas_call(
        _ada_kernel,
        grid=(N_MOD * D_MODEL // tn,),
        in_specs=[_full((SUBLANES, D_MODEL)),
                  pl.BlockSpec((D_MODEL, tn), lambda j: (0, j)),
                  pl.BlockSpec((1, tn), lambda j: (0, j))],
        out_specs=pl.BlockSpec((SUBLANES, tn), lambda j: (0, j)),
        out_shape=jax.ShapeDtypeStruct((SUBLANES, N_MOD * D_MODEL), F32),
        compiler_params=_params("arbitrary"),
        name="ada_table",
    )(cond, w_ada, b_ada[None, :])
    return out[:BATCH + 1].reshape(BATCH + 1, 1, N_MOD * D_MODEL)


def _even_in_kernel(x_ref, mod_ref, g_ref, wf_ref, wz_ref, wx_ref, wd_ref, f_ref, z_ref, xbc_ref, dt_ref):
    h = _modulated_norm(x_ref[...], g_ref[...], mod_ref[0], 0, 1).astype(BF16)
    f_ref[...] = jnp.dot(h, wf_ref[...], preferred_element_type=F32)
    z_ref[...] = jnp.dot(h, wz_ref[...], preferred_element_type=F32)
    xbc_ref[...] = jnp.dot(h, wx_ref[...], preferred_element_type=F32)
    dt_ref[...] = jnp.dot(h, wd_ref[...], preferred_element_type=F32)


def even_in(x_all, mod, norm1, w_in):
    rows = x_all.shape[0]
    o1, o2, o3 = FNET_WIDTH, FNET_WIDTH + SSD_INNER, FNET_WIDTH + SSD_INNER + SSD_CONV_DIM
    wf = w_in[:, :o1].astype(BF16)
    wz = w_in[:, o1:o2].astype(BF16)
    wx = w_in[:, o2:o3].astype(BF16)
    wd = jnp.pad(w_in[:, o3:], ((0, 0), (0, LANES - 2 * SSD_HEADS))).astype(BF16)
    tm = ROW_TILE
    row = lambda n: pl.BlockSpec((tm, n), lambda i: (i, 0))
    return pl.pallas_call(
        _even_in_kernel,
        grid=(rows // tm,),
        in_specs=[row(D_MODEL),
                  pl.BlockSpec((1, 1, N_MOD * D_MODEL), lambda i: (_mod_row(i, tm), 0, 0)),
                  _full((1, D_MODEL)),
                  _full(wf.shape), _full(wz.shape), _full(wx.shape), _full(wd.shape)],
        out_specs=[row(FNET_WIDTH), row(SSD_INNER), row(SSD_CONV_DIM), row(LANES)],
        out_shape=[jax.ShapeDtypeStruct((rows, FNET_WIDTH), F32),
                   jax.ShapeDtypeStruct((rows, SSD_INNER), F32),
                   jax.ShapeDtypeStruct((rows, SSD_CONV_DIM), F32),
                   jax.ShapeDtypeStruct((rows, LANES), F32)],
        compiler_params=_params("arbitrary"),
        name="even_in",
    )(x_all, mod, norm1[None, :], wf, wz, wx, wd)


def _conv_kernel(x_ref, prev_ref, next_ref, w_ref, b_ref, o_ref, ext_ref, *, tm):
    row0 = pl.program_id(0) * tm
    in_lat = row0 < LAT_ROWS
    first = jnp.where(in_lat, row0 % SEQ == 0, (row0 - LAT_ROWS) % CTX_LEN == 0)
    last = jnp.where(in_lat, (row0 + tm) % SEQ == 0, (row0 + tm - LAT_ROWS) % CTX_LEN == 0)
    ext_ref[0:SUBLANES, :] = prev_ref[...] * jnp.where(first, 0.0, 1.0)
    ext_ref[SUBLANES:SUBLANES + tm, :] = x_ref[...]
    ext_ref[SUBLANES + tm:2 * SUBLANES + tm, :] = next_ref[...] * jnp.where(last, 0.0, 1.0)
    pad = SSD_CONV // 2
    acc = b_ref[...] + w_ref[0:1, :] * ext_ref[pl.ds(SUBLANES - pad, tm), :]
    for k in range(1, SSD_CONV):
        acc = acc + w_ref[k:k + 1, :] * ext_ref[pl.ds(SUBLANES - pad + k, tm), :]
    o_ref[...] = _silu(acc)


def conv_silu(xbc, conv_w, conv_b):
    rows, ch = xbc.shape
    tm, tc = ROW_TILE, ch
    assert CTX_LEN % tm == 0 and SEQ % tm == 0
    per = tm // SUBLANES
    n_small = rows // SUBLANES
    w = jnp.pad(conv_w, ((0, SUBLANES - SSD_CONV), (0, 0)))
    return pl.pallas_call(
        functools.partial(_conv_kernel, tm=tm),
        grid=(rows // tm, ch // tc),
        in_specs=[pl.BlockSpec((tm, tc), lambda i, j: (i, j)),
                  pl.BlockSpec((SUBLANES, tc), lambda i, j: (jnp.maximum(i * per - 1, 0), j)),
                  pl.BlockSpec((SUBLANES, tc), lambda i, j: (jnp.minimum((i + 1) * per, n_small - 1), j)),
                  pl.BlockSpec((SUBLANES, tc), lambda i, j: (0, j)),
                  pl.BlockSpec((1, tc), lambda i, j: (0, j))],
        out_specs=pl.BlockSpec((tm, tc), lambda i, j: (i, j)),
        out_shape=jax.ShapeDtypeStruct((rows, ch), F32),
        scratch_shapes=[pltpu.VMEM((tm + 2 * SUBLANES, tc), F32)],
        compiler_params=_params("arbitrary", "arbitrary"),
        name="conv_silu",
    )(xbc, xbc, xbc, w, conv_b[None, :])


def _softplus(x):
    return jnp.maximum(x, 0.0) + jnp.log1p(jnp.exp(-jnp.abs(x)))


def _ssd_direction(u_ref, dt_ref, dtb_ref, alog_ref, exp_ref, state_ref, y_ref, direction):
    cl = SSD_CHUNK
    xs = u_ref[:, :SSD_INNER]
    dtv = _softplus(dt_ref[...] + dtb_ref[...])
    dta = dtv * (-jnp.exp(alog_ref[...]))
    r = lax.broadcasted_iota(jnp.int32, (cl, cl), 0)
    c = lax.broadcasted_iota(jnp.int32, (cl, cl), 1)
    tri = (r >= c) if direction == 0 else (r <= c)
    cs = _dot_exact_lhs(tri.astype(F32).astype(BF16), dta)
    cs_t = cs.T
    edge = cs[cl - 1:cl, :] if direction == 0 else cs[0:1, :]
    to_end = jnp.exp(edge - cs)
    from_start = jnp.exp(cs)
    spread = _dot_exact_rhs(jnp.concatenate([dtv, to_end, from_start], axis=0), exp_ref[direction], terms=2)
    dt_x, te_x, fs_x = spread[:cl], spread[cl:2 * cl], spread[2 * cl:]
    chunk_decay = fs_x[cl - 1:cl, :] if direction == 0 else fs_x[0:1, :]
    xd = xs * dt_x
    xd_b = xd.astype(BF16)
    xte_b = (xd * te_x).astype(BF16)
    lane = lax.broadcasted_iota(jnp.int32, (cl, LANES), 1)
    low = lane < SSD_HEAD_DIM
    for g in range(SSD_GROUPS):
        bg = u_ref[:, SSD_INNER + g * SSD_STATE:SSD_INNER + (g + 1) * SSD_STATE]
        cg = u_ref[:, SSD_INNER + (SSD_GROUPS + g) * SSD_STATE:SSD_INNER + (SSD_GROUPS + g + 1) * SSD_STATE]
        bg_b, cg_b = bg.astype(BF16), cg.astype(BF16)
        cb = lax.dot_general(cg_b, bg_b, (((1,), (1,)), ((), ())), preferred_element_type=F32)
        gs = slice(g * SSD_GROUP_W, (g + 1) * SSD_GROUP_W)
        state = state_ref[g]
        y_off = jnp.dot(cg_b, state.astype(BF16), preferred_element_type=F32) * fs_x[:, gs]
        new_state = jnp.dot(bg.T.astype(BF16), xte_b[:, gs], preferred_element_type=F32)
        state_ref[g] = state * chunk_decay[:, gs] + new_state
        for pair in range(2):
            blk = xd_b[:, g * SSD_GROUP_W + pair * LANES:g * SSD_GROUP_W + (pair + 1) * LANES]
            y_pair = None
            for sub in range(2):
                col = direction * SSD_HEADS + g * 4 + pair * 2 + sub
                seg = cs[:, col:col + 1] - cs_t[col:col + 1, :]
                decay = jnp.exp(jnp.where(tri, seg, NEG_INF))
                m = (cb * decay).astype(BF16)
                half = jnp.where(low if sub == 0 else jnp.logical_not(low), blk, jnp.zeros_like(blk))
                part = jnp.dot(m, half, preferred_element_type=F32)
                y_pair = part if y_pair is None else y_pair + part
            lo = g * SSD_GROUP_W + pair * LANES
            y_ref[:, lo:lo + LANES] = y_pair + y_off[:, pair * LANES:(pair + 1) * LANES]


def _ssd_kernel(uf_ref, ub_ref, dtf_ref, dtb_in_ref, bias_ref, alog_ref, exp_ref, yf_ref, yb_ref, sf_ref, sb_ref):
    @pl.when(pl.program_id(1) == 0)
    def _():
        sf_ref[...] = jnp.zeros_like(sf_ref)
        sb_ref[...] = jnp.zeros_like(sb_ref)

    _ssd_direction(uf_ref, dtf_ref, bias_ref, alog_ref, exp_ref, sf_ref, yf_ref, 0)
    _ssd_direction(ub_ref, dtb_in_ref, bias_ref, alog_ref, exp_ref, sb_ref, yb_ref, 1)


def ssd_scan(u, dt, dt_bias, a_log):
    rows = u.shape[0]
    cl = SSD_CHUNK
    lat_chunks, ctx_chunks = SEQ // cl, CTX_LEN // cl
    steps = ctx_chunks + lat_chunks
    ctx0 = LAT_ROWS // cl

    def fwd_chunk(b, t):
        return jnp.where(t < ctx_chunks, ctx0 + b * ctx_chunks + t, b * lat_chunks + t - ctx_chunks)

    def bwd_chunk(b, t):
        return jnp.where(t < ctx_chunks, ctx0 + b * ctx_chunks + (ctx_chunks - 1 - t),
                         b * lat_chunks + (lat_chunks - 1) - (t - ctx_chunks))

    pad = LANES - 2 * SSD_HEADS
    bias = jnp.pad(dt_bias.reshape(1, -1), ((0, 0), (0, pad)))
    alog = jnp.pad(a_log.reshape(1, -1), ((0, 0), (0, pad)))
    expand = np.zeros((2, LANES, SSD_INNER), np.float32)
    for d in range(2):
        for h in range(SSD_HEADS):
            expand[d, d * SSD_HEADS + h, h * SSD_HEAD_DIM:(h + 1) * SSD_HEAD_DIM] = 1.0
    spec = lambda n, fn: pl.BlockSpec((cl, n), lambda b, t: (fn(b, t), 0))
    return pl.pallas_call(
        _ssd_kernel,
        grid=(BATCH, steps),
        in_specs=[spec(SSD_CONV_DIM, fwd_chunk), spec(SSD_CONV_DIM, bwd_chunk),
                  spec(LANES, fwd_chunk), spec(LANES, bwd_chunk),
                  _full((1, LANES)), _full((1, LANES)), _full((2, LANES, SSD_INNER))],
        out_specs=[spec(SSD_INNER, fwd_chunk), spec(SSD_INNER, bwd_chunk)],
        out_shape=[jax.ShapeDtypeStruct((rows, SSD_INNER), F32)] * 2,
        scratch_shapes=[pltpu.VMEM((SSD_GROUPS, SSD_STATE, SSD_GROUP_W), F32)] * 2,
        compiler_params=_params("arbitrary", "arbitrary"),
        name="ssd_scan",
    )(u, u, dt, dt, bias, alog, jnp.asarray(expand, BF16))


def _dft_cos_sin(n):
    k = np.arange(n)
    ang = 2.0 * np.pi * ((k[:, None] * k[None, :]) % n) / n
    return np.cos(ang), np.sin(ang)


def _fft1_kernel(x_ref, m_ref, o_ref):
    o_ref[0] = _dot3(m_ref[...], x_ref[...])


def _fft2_kernel(ar_ref, ai_ref, twr_ref, twi_ref, m2_ref, mc_ref, o_ref):
    ar, ai = ar_ref[0, 0], ai_ref[0, 0]
    twr = jnp.concatenate([twr_ref[0]] * FNET_GROUPS, axis=1)
    twi = jnp.concatenate([twi_ref[0]] * FNET_GROUPS, axis=1)
    p = ar * twr - ai * twi
    q = ar * twi + ai * twr
    uv = _dot3(m2_ref[...], jnp.concatenate([p, q], axis=0))
    n2 = FFT_N2
    for g in range(FNET_GROUPS):
        gs = slice(g * FNET_GROUP_DIM, (g + 1) * FNET_GROUP_DIM)
        o_ref[0, :, gs] = _dot3(jnp.concatenate([uv[:n2, gs], uv[n2:, gs]], axis=1), mc_ref[...])


def _fft_ctx_kernel(x_ref, mc_ref, mp_ref, o_ref):
    x = x_ref[...]
    for g in range(FNET_GROUPS):
        gs = slice(g * FNET_GROUP_DIM, (g + 1) * FNET_GROUP_DIM)
        cs = _dot3(x[:, gs], mc_ref[...])
        stacked = jnp.concatenate([cs[:, :FNET_GROUP_DIM], cs[:, FNET_GROUP_DIM:]], axis=0)
        o_ref[:, gs] = _dot3(mp_ref[...], stacked)


def fourier_mix(f_all):
    rows = f_all.shape[0]
    n1, n2, gd = FFT_N1, FFT_N2, FNET_GROUP_DIM
    row_w = n2 * FNET_WIDTH
    c1, s1 = _dft_cos_sin(n1)
    m1 = jnp.asarray(np.concatenate([c1, -s1], axis=0), F32)
    tn = 8192
    stage1 = pl.pallas_call(
        _fft1_kernel,
        grid=(BATCH, row_w // tn),
        in_specs=[pl.BlockSpec((n1, tn), lambda b, j: (b, j)), _full((2 * n1, n1))],
        out_specs=pl.BlockSpec((1, 2 * n1, tn), lambda b, j: (b, 0, j)),
        out_shape=jax.ShapeDtypeStruct((BATCH, 2 * n1, row_w), F32),
        compiler_params=_params("arbitrary", "arbitrary"),
        name="fft_stage1",
    )(f_all.reshape(rows // n2, row_w), m1)

    k1 = np.arange(n1)[:, None]
    l2 = np.arange(n2)[None, :]
    tw = 2.0 * np.pi * (k1 * l2) / SEQ
    twr = jnp.asarray(np.repeat(np.cos(tw)[:, :, None], gd, axis=2), F32)
    twi = jnp.asarray(np.repeat(-np.sin(tw)[:, :, None], gd, axis=2), F32)
    c2, s2 = _dft_cos_sin(n2)
    m2 = jnp.asarray(np.block([[c2, s2], [-s2, c2]]), F32)
    cc, sc = _dft_cos_sin(gd)
    mc = jnp.asarray(np.concatenate([cc, sc], axis=0) / math.sqrt(SEQ * gd), F32)
    a4 = stage1.reshape(BATCH, 2 * n1, n2, FNET_WIDTH)
    lat = pl.pallas_call(
        _fft2_kernel,
        grid=(BATCH, n1),
        in_specs=[pl.BlockSpec((1, 1, n2, FNET_WIDTH), lambda b, k: (b, k, 0, 0)),
                  pl.BlockSpec((1, 1, n2, FNET_WIDTH), lambda b, k: (b, n1 + k, 0, 0)),
                  pl.BlockSpec((1, n2, gd), lambda b, k: (k, 0, 0)),
                  pl.BlockSpec((1, n2, gd), lambda b, k: (k, 0, 0)),
                  _full((2 * n2, 2 * n2)), _full((2 * gd, gd))],
        out_specs=pl.BlockSpec((1, n2, FNET_WIDTH), lambda b, k: (b, 0, k)),
        out_shape=jax.ShapeDtypeStruct((BATCH, n2, n1 * FNET_WIDTH), F32),
        compiler_params=_params("arbitrary", "arbitrary"),
        name="fft_stage2",
    )(a4, a4, twr, twi, m2, mc)

    cp, sp = _dft_cos_sin(CTX_LEN)
    mp = jnp.asarray(np.concatenate([cp, -sp], axis=1) / math.sqrt(CTX_LEN * gd), F32)
    mcc = jnp.asarray(np.concatenate([cc, sc], axis=1), F32)
    ctx0 = LAT_ROWS // CTX_LEN
    ctx = pl.pallas_call(
        _fft_ctx_kernel,
        grid=(BATCH,),
        in_specs=[pl.BlockSpec((CTX_LEN, FNET_WIDTH), lambda b: (ctx0 + b, 0)),
                  _full((gd, 2 * gd)), _full((CTX_LEN, 2 * CTX_LEN))],
        out_specs=pl.BlockSpec((CTX_LEN, FNET_WIDTH), lambda b: (b, 0)),
        out_shape=jax.ShapeDtypeStruct((BATCH * CTX_LEN, FNET_WIDTH), F32),
        compiler_params=_params("arbitrary"),
        name="fft_ctx",
    )(f_all, mcc, mp)
    return jnp.concatenate([lat.reshape(LAT_ROWS, FNET_WIDTH), ctx], axis=0)


def _even_out_kernel(x_ref, mod_ref, four_ref, yf_ref, yb_ref, xs_ref, z_ref, dsk_ref, gn_ref, wof_ref, wos_ref,
                     o_ref):
    y = yf_ref[...] + yb_ref[...] + dsk_ref[...] * xs_ref[...]
    gated = y * _silu(z_ref[...])
    out = jnp.dot(four_ref[...].astype(BF16), wof_ref[...], preferred_element_type=F32)
    for g in range(SSD_GROUPS):
        gs = slice(g * SSD_GROUP_W, (g + 1) * SSD_GROUP_W)
        v = gated[:, gs]
        normed = v * lax.rsqrt(jnp.mean(v * v, axis=-1, keepdims=True) + EPS) * gn_ref[:, gs]
        out = out + jnp.dot(normed.astype(BF16), wos_ref[gs, :], preferred_element_type=F32)
    gate = mod_ref[0][:, 2 * D_MODEL:3 * D_MODEL]
    o_ref[...] = x_ref[...] + gate * out


def even_out(x_all, mod, four, yf, yb, u, z, d_skip, ssd_norm, w_out):
    rows = x_all.shape[0]
    tm = ROW_TILE
    dsk = jnp.repeat(d_skip[0] + d_skip[1], SSD_HEAD_DIM)[None, :]
    wof = w_out[:FNET_WIDTH].astype(BF16)
    wos = w_out[FNET_WIDTH:].astype(BF16)
    row = lambda n: pl.BlockSpec((tm, n), lambda i: (i, 0))
    return pl.pallas_call(
        _even_out_kernel,
        grid=(rows // tm,),
        in_specs=[row(D_MODEL),
                  pl.BlockSpec((1, 1, N_MOD * D_MODEL), lambda i: (_mod_row(i, tm), 0, 0)),
                  row(FNET_WIDTH), row(SSD_INNER), row(SSD_INNER), row(SSD_INNER), row(SSD_INNER),
                  _full((1, SSD_INNER)), _full((1, SSD_INNER)), _full(wof.shape), _full(wos.shape)],
        out_specs=row(D_MODEL),
        out_shape=jax.ShapeDtypeStruct((rows, D_MODEL), F32),
        compiler_params=_params("arbitrary"),
        name="even_out",
    )(x_all, mod, four, yf, yb, u, z, dsk, ssd_norm[None, :], wof, wos)


BF16_ROWS = 2 * SUBLANES
NOT_TOP = 64.0


def _erf_gelu(z):
    return 0.5 * z * (1.0 + lax.erf(z * (1.0 / math.sqrt(2.0))))


def _top_ranks(score, k, want_rank=True):
    work, vals = score, []
    rank = jnp.full(score.shape, NOT_TOP, F32) if want_rank else None
    for i in range(k):
        m = jnp.max(work, axis=0, keepdims=True)
        hit = work == m
        vals.append(m)
        if want_rank:
            rank = jnp.where(hit, float(i), rank)
        work = jnp.where(hit, NEG_INF, work)
    return jnp.concatenate(vals, axis=0), rank


def _peer_kernel(x_ref, mod_ref, g_ref, wq_ref, keys_ref, u0_ref, *rest, tm, chunk, per_step):
    u_refs = rest[:per_step]
    (vt_ref, o_ref, ht_ref, q_ref, lim_ref, w1_ref, rank2_ref, w2_ref, acc_ref, za_ref, zb_ref) = rest[per_step:]
    c = pl.program_id(1)
    k = PEER_TOPK
    packed = (PEER_KEYS // BF16_ROWS, BF16_ROWS, tm)
    tsel = PEER_SELECT_TOKENS

    @pl.when(c == 0)
    def _prologue():
        h2 = _modulated_norm(x_ref[...], g_ref[...], mod_ref[0], 3, 4)
        ht = h2.T.astype(BF16)
        ht_ref[...] = ht
        q = jnp.dot(wq_ref[...], ht, preferred_element_type=F32)
        q_ref[...] = q.reshape(2 * PEER_HEADS, PEER_QDIM // 2, tm)
        acc_ref[...] = jnp.zeros_like(acc_ref)
        za_ref[...] = jnp.dot(u0_ref[...], ht, preferred_element_type=F32)
        row8 = lax.broadcasted_iota(jnp.int32, (SUBLANES, tsel), 0)

        def head_body(h, carry):
            for part in range(tm // tsel):
                ts = slice(part * tsel, (part + 1) * tsel)
                s1 = jnp.dot(keys_ref[2 * h], q_ref[2 * h, :, ts].astype(BF16), preferred_element_type=F32)
                s2 = jnp.dot(keys_ref[2 * h + 1], q_ref[2 * h + 1, :, ts].astype(BF16), preferred_element_type=F32)
                v1, _ = _top_ranks(s1, k, want_rank=False)
                v2, r2 = _top_ranks(s2, k)
                groups = [v1[0:1] + v2[0:8], v1[0:1] + v2[8:16], v1[1:2] + v2[0:8], v1[8:16] + v2[0:1]]
                for i in range(2, 8):
                    groups.append(jnp.where(row8 < k // (i + 1), v1[i:i + 1] + v2[0:8], NEG_INF))
                cand = jnp.concatenate(groups, axis=0)
                tau = _top_ranks(cand, k, want_rank=False)[0][k - 1:k, :]
                top = v1[0:1] + v2[0:1]
                zsum = jnp.sum(jnp.where(cand >= tau, jnp.exp(cand - top), 0.0), axis=0, keepdims=True)
                a1 = jnp.where(s1 >= v1[k - 1:k], s1, NEG_INF)
                lim = jnp.zeros_like(s1)
                for j in range(k // 2):
                    lim = lim + jnp.where(a1 + v2[j:j + 1] >= tau, 1.0, 0.0)
                best = jnp.zeros_like(tau)
                for j in range(k // 2, k):
                    best = best + jnp.where(v1[0:1] + v2[j:j + 1] >= tau, 1.0, 0.0)
                lim = lim + jnp.where(s1 == v1[0:1], best, 0.0)
                lim_ref[h, :, ts] = lim
                w1_ref[h, :, ts] = jnp.exp(s1 - v1[0:1]) / zsum
                small = (PEER_KEYS // BF16_ROWS, BF16_ROWS, tsel)
                rank2_ref[h, :, :, ts] = r2.reshape(small).astype(BF16)
                w2_ref[h, :, :, ts] = jnp.exp(s2 - v2[0:1]).reshape(small).astype(BF16)
            return carry

        lax.fori_loop(0, PEER_HEADS, head_body, 0)

    per_key = PEER_KEYS // BF16_ROWS
    keys_per_sub = PEER_SUB // PEER_KEYS
    step_keys = per_step * chunk // PEER_KEYS
    key0 = pl.multiple_of(c * step_keys, step_keys)
    step_lim = [lim_ref[h, pl.ds(key0, step_keys), :] for h in range(PEER_HEADS)]
    step_w1 = [w1_ref[h, pl.ds(key0, step_keys), :] for h in range(PEER_HEADS)]

    def gated_values(z_ref, which):
        total = None
        for j in range(chunk // PEER_SUB):
            rows = slice(j * PEER_SUB, (j + 1) * PEER_SUB)
            act = _erf_gelu(z_ref[rows, :]).reshape(PEER_SUB // BF16_ROWS, BF16_ROWS, tm).astype(BF16)
            parts = []
            for jj in range(keys_per_sub):
                e1 = which * (chunk // PEER_KEYS) + j * keys_per_sub + jj
                gate = jnp.zeros(packed, BF16)
                for h in range(PEER_HEADS):
                    lim = jnp.broadcast_to(step_lim[h][e1:e1 + 1, :], (BF16_ROWS, tm)).astype(BF16)
                    w1 = jnp.broadcast_to(step_w1[h][e1:e1 + 1, :], (BF16_ROWS, tm)).astype(BF16)
                    picked = jnp.where(rank2_ref[h] < lim[None], w2_ref[h], jnp.zeros(packed, BF16))
                    gate = gate + picked * w1[None]
                parts.append(act[jj * per_key:(jj + 1) * per_key] * gate)
            a = jnp.concatenate(parts, axis=0).reshape(PEER_SUB, tm)
            cols = slice(which * chunk + j * PEER_SUB, which * chunk + (j + 1) * PEER_SUB)
            d = jnp.dot(vt_ref[:, cols], a, preferred_element_type=F32)
            total = d if total is None else total + d
        return total

    bufs = (za_ref, zb_ref)
    total = None
    for i in range(per_step):
        bufs[(i + 1) % 2][...] = jnp.dot(u_refs[i][...], ht_ref[...], preferred_element_type=F32)
        d = gated_values(bufs[i % 2], i)
        total = d if total is None else total + d
    acc_ref[...] += total

    @pl.when(c == pl.num_programs(1) - 1)
    def _epilogue():
        gate2 = mod_ref[0][:, 5 * D_MODEL:6 * D_MODEL]
        o_ref[...] = x_ref[...] + gate2 * acc_ref[...].T


def peer_layer(x_rows, mod, norm2, peer_wq, peer_keys, peer_u, peer_v):
    rows = x_rows.shape[0]
    tm, chunk = PEER_TILE, PEER_CHUNK
    wq_t = peer_wq.T.astype(BF16)
    keys = peer_keys.reshape(2 * PEER_HEADS, PEER_KEYS, PEER_QDIM // 2).astype(BF16)
    u = peer_u.astype(BF16)
    vt = peer_v.astype(BF16).T
    head_f32 = pltpu.VMEM((PEER_HEADS, PEER_KEYS, tm), F32)
    head_bf16 = pltpu.VMEM((PEER_HEADS, PEER_KEYS // BF16_ROWS, BF16_ROWS, tm), BF16)
    n_chunks = PEER_EXPERTS // chunk
    per_step = PEER_CHUNKS_PER_STEP
    ahead = lambda k: pl.BlockSpec((chunk, D_MODEL), lambda i, c: (jnp.minimum(per_step * c + k, n_chunks - 1), 0))
    return pl.pallas_call(
        functools.partial(_peer_kernel, tm=tm, chunk=chunk, per_step=per_step),
        grid=(rows // tm, n_chunks // per_step),
        in_specs=[pl.BlockSpec((tm, D_MODEL), lambda i, c: (i, 0)),
                  pl.BlockSpec((1, 1, N_MOD * D_MODEL), lambda i, c: (_mod_row(i, tm), 0, 0)),
                  _full((1, D_MODEL)), _full(wq_t.shape), _full(keys.shape),
                  pl.BlockSpec((chunk, D_MODEL), lambda i, c: (0, 0))]
                 + [ahead(k) for k in range(1, per_step + 1)]
                 + [pl.BlockSpec((D_MODEL, per_step * chunk), lambda i, c: (0, c))],
        out_specs=pl.BlockSpec((tm, D_MODEL), lambda i, c: (i, 0)),
        out_shape=jax.ShapeDtypeStruct((rows, D_MODEL), F32),
        scratch_shapes=[pltpu.VMEM((D_MODEL, tm), BF16),
                        pltpu.VMEM((2 * PEER_HEADS, PEER_QDIM // 2, tm), F32),
                        head_f32, head_f32, head_bf16, head_bf16,
                        pltpu.VMEM((D_MODEL, tm), F32),
                        pltpu.VMEM((chunk, tm), F32), pltpu.VMEM((chunk, tm), F32)],
        compiler_params=_params("arbitrary", "arbitrary"),
        name="peer",
    )(x_rows, mod, norm2[None, :], wq_t, keys, u, *([u] * per_step), vt)


def _rope_tables(rows):
    l = jnp.arange(SEQ)
    per_axis = MLA_ROPE // 2
    inv = ROPE_THETA ** (-jnp.arange(0, per_axis, 2, dtype=F32) / per_axis)
    ang = jnp.concatenate([(l // GRID_W)[:, None] * inv, (l % GRID_W)[:, None] * inv], axis=-1)
    cos, sin = jnp.cos(ang), jnp.sin(ang)
    one = jnp.ones((SEQ, MLA_NOPE), F32)
    tail = jnp.zeros((SEQ, HEAD_PAD - MLA_QK), F32)
    zero64 = jnp.zeros((SEQ, MLA_NOPE), F32)
    cos_t = jnp.concatenate([one, cos, cos, tail], axis=1)
    sin_t = jnp.concatenate([zero64, -sin, sin, tail], axis=1)
    n_ctx = rows - LAT_ROWS
    ident = jnp.concatenate([jnp.ones((n_ctx, MLA_QK), F32), jnp.zeros((n_ctx, HEAD_PAD - MLA_QK), F32)], axis=1)
    zeros = jnp.zeros((n_ctx, HEAD_PAD), F32)
    tile = lambda t, c: jnp.concatenate([t] * BATCH + [c], axis=0)
    return tile(cos_t, ident), tile(sin_t, zeros)


def _rope_swap_matrix():
    per_axis = MLA_ROPE // 2
    swap = np.zeros((HEAD_PAD, HEAD_PAD), np.float32)
    for j in range(per_axis):
        swap[MLA_NOPE + per_axis + j, MLA_NOPE + j] = 1.0
        swap[MLA_NOPE + j, MLA_NOPE + per_axis + j] = 1.0
    return swap


def _head_norm_rope(t, gain, cos, sin, ones, swap):
    ms = jnp.dot((t * t).astype(BF16), ones, preferred_element_type=F32) * (1.0 / MLA_QK)
    t = t * lax.rsqrt(ms + EPS) * gain
    return t * cos + jnp.dot(t.astype(BF16), swap, preferred_element_type=F32) * sin


def _mla_in_kernel(x_ref, mod_ref, g_ref, wdq_ref, wdkv_ref, wpe_ref, qan_ref, kvan_ref, wuq_ref, wuk_ref, wuv_ref,
                   qn_ref, kn_ref, ones_ref, swap_ref, cos_ref, sin_ref, q_ref, k_ref, v_ref):
    h = _modulated_norm(x_ref[...], g_ref[...], mod_ref[0], 0, 1).astype(BF16)
    dq = jnp.dot(h, wdq_ref[...], preferred_element_type=F32)
    dkv = jnp.dot(h, wdkv_ref[...], preferred_element_type=F32)
    kpe = jnp.dot(h, wpe_ref[...], preferred_element_type=F32)
    qa = dq * lax.rsqrt(jnp.mean(dq * dq, axis=-1, keepdims=True) + EPS) * qan_ref[...]
    kva = (dkv * lax.rsqrt(jnp.mean(dkv * dkv, axis=-1, keepdims=True) + EPS) * kvan_ref[...]).astype(BF16)
    q = jnp.dot(qa.astype(BF16), wuq_ref[...], preferred_element_type=F32)
    kn = jnp.dot(kva, wuk_ref[...], preferred_element_type=F32)
    v_ref[...] = jnp.dot(kva, wuv_ref[...], preferred_element_type=F32).astype(BF16)
    cos, sin, ones, swap = cos_ref[...], sin_ref[...], ones_ref[...], swap_ref[...]
    scale = MLA_QK ** -0.5 * math.log2(math.e)
    for hd in range(MLA_HEADS):
        hs = slice(hd * HEAD_PAD, (hd + 1) * HEAD_PAD)
        q_ref[:, hs] = (_head_norm_rope(q[:, hs], qn_ref[...], cos, sin, ones, swap) * scale).astype(BF16)
        k_ref[:, hs] = _head_norm_rope(kn[:, hs] + kpe, kn_ref[...], cos, sin, ones, swap).astype(BF16)


def mla_in(x_all, mod, p):
    rows = x_all.shape[0]
    tm = ROW_TILE
    w = p["w_dqkv"]
    wdq = w[:, :MLA_Q_LORA].astype(BF16)
    wdkv = w[:, MLA_Q_LORA:MLA_Q_LORA + MLA_KV_LORA].astype(BF16)
    wpe = jnp.pad(w[:, MLA_Q_LORA + MLA_KV_LORA:], ((0, 0), (MLA_NOPE, HEAD_PAD - MLA_QK))).astype(BF16)
    wuq = jnp.pad(p["w_uq"].reshape(MLA_Q_LORA, MLA_HEADS, MLA_QK), ((0, 0), (0, 0), (0, HEAD_PAD - MLA_QK)))
    wuq = wuq.reshape(MLA_Q_LORA, MLA_HEADS * HEAD_PAD).astype(BF16)
    wukv = p["w_ukv"].reshape(MLA_KV_LORA, MLA_HEADS, MLA_NOPE + MLA_V)
    wuk = jnp.pad(wukv[:, :, :MLA_NOPE], ((0, 0), (0, 0), (0, HEAD_PAD - MLA_NOPE)))
    wuk = wuk.reshape(MLA_KV_LORA, MLA_HEADS * HEAD_PAD).astype(BF16)
    wuv = wukv[:, :, MLA_NOPE:].reshape(MLA_KV_LORA, MLA_HEADS * MLA_V).astype(BF16)
    padg = lambda g: jnp.pad(g, (0, HEAD_PAD - MLA_QK))[None, :]
    cos, sin = _rope_tables(rows)
    row = lambda n: pl.BlockSpec((tm, n), lambda i: (i, 0))
    weights = [wdq, wdkv, wpe, p["q_a_norm"][None, :], p["kv_a_norm"][None, :], wuq, wuk, wuv,
               padg(p["q_norm"]), padg(p["k_norm"]),
               jnp.ones((HEAD_PAD, HEAD_PAD), BF16), jnp.asarray(_rope_swap_matrix(), BF16)]
    return pl.pallas_call(
        _mla_in_kernel,
        grid=(rows // tm,),
        in_specs=[row(D_MODEL),
                  pl.BlockSpec((1, 1, N_MOD * D_MODEL), lambda i: (_mod_row(i, tm), 0, 0)),
                  _full((1, D_MODEL))] + [_full(a.shape) for a in weights] + [row(HEAD_PAD)] * 2,
        out_specs=[row(MLA_HEADS * HEAD_PAD), row(MLA_HEADS * HEAD_PAD), row(MLA_HEADS * MLA_V)],
        out_shape=[jax.ShapeDtypeStruct((rows, MLA_HEADS * HEAD_PAD), BF16),
                   jax.ShapeDtypeStruct((rows, MLA_HEADS * HEAD_PAD), BF16),
                   jax.ShapeDtypeStruct((rows, MLA_HEADS * MLA_V), BF16)],
        compiler_params=_params("arbitrary"),
        name="mla_in",
    )(x_all, mod, p["norm1"][None, :], *weights, cos, sin)


ATTN_SAFE_SHIFT = 50.0


def _attn_kernel(q_ref, kc_ref, kl_ref, vc_ref, vl_ref, o_ref, knorm_ref, *, tq, tk):
    heads = (slice(0, HEAD_PAD), slice(HEAD_PAD, 2 * HEAD_PAD))
    qs = [q_ref[:, hs] for hs in heads]
    n_lat = SEQ // tk

    @pl.when(pl.program_id(2) == 0)
    def _key_norms():
        def sq_max(k2, hh):
            kf = k2[:, heads[hh]].astype(F32)
            return jnp.max(jnp.sum(kf * kf, axis=-1, keepdims=True), axis=0, keepdims=True)

        for hh in range(2):
            best = sq_max(kc_ref[...], hh)
            best = lax.fori_loop(
                0, n_lat, lambda j, b: jnp.maximum(b, sq_max(kl_ref[pl.ds(pl.multiple_of(j * tk, tk), tk), :], hh)), best)
            knorm_ref[hh] = jnp.broadcast_to(jnp.sqrt(best), (SUBLANES, LANES))

    def values(v, hh):
        lane = lax.broadcasted_iota(jnp.int32, v.shape, 1)
        own = (lane < MLA_V) if hh == 0 else (lane >= MLA_V)
        return jnp.where(own, v, jnp.ones_like(v))

    def scores(hh, k2):
        return lax.dot_general(qs[hh], k2[:, heads[hh]], (((1,), (1,)), ((), ())), preferred_element_type=F32)

    def finish(accs):
        lane = lax.broadcasted_iota(jnp.int32, (tq, 2 * MLA_V), 1)
        outs = [acc / pltpu.roll(acc, MLA_V, 1) for acc in accs]
        o_ref[...] = jnp.where(lane < MLA_V, outs[0], outs[1]).astype(o_ref.dtype)

    def lat_chunk(j):
        off = pl.multiple_of(j * tk, tk)
        return kl_ref[pl.ds(off, tk), :], vl_ref[pl.ds(off, tk), :]

    bounds = []
    for hh in range(2):
        qf = qs[hh].astype(F32)
        qn = jnp.sqrt(jnp.sum(qf * qf, axis=-1, keepdims=True))
        bounds.append(qn * knorm_ref[hh, 0:1, 0:1] * 1.001 + 1e-3)
    widest = jnp.max(jnp.maximum(bounds[0], bounds[1]))

    def fixed_shift():
        def step(k2, v, accs):
            return tuple(accs[hh] + jnp.dot(jnp.exp2(scores(hh, k2) - bounds[hh]).astype(BF16), values(v, hh),
                                            preferred_element_type=F32) for hh in range(2))

        zero = jnp.zeros((tq, 2 * MLA_V), F32)
        accs = step(kc_ref[...], vc_ref[...], (zero, zero))
        finish(lax.fori_loop(0, n_lat, lambda j, a: step(*lat_chunk(j), a), accs, unroll=4))

    def running_max():
        def step(k2, v, carry):
            new = []
            for hh in range(2):
                m, acc = carry[hh]
                s = scores(hh, k2)
                m_new = jnp.maximum(m, jnp.max(s, axis=-1, keepdims=True))
                p = jnp.exp2(s - m_new).astype(BF16)
                acc = jnp.exp2(m - m_new) * acc + jnp.dot(p, values(v, hh), preferred_element_type=F32)
                new.append((m_new, acc))
            return tuple(new)

        init = tuple((jnp.full((tq, 1), NEG_INF, F32), jnp.zeros((tq, 2 * MLA_V), F32)) for _ in range(2))
        carry = step(kc_ref[...], vc_ref[...], init)
        carry = lax.fori_loop(0, n_lat, lambda j, c: step(*lat_chunk(j), c), carry)
        finish([acc for _, acc in carry])

    lax.cond(widest <= ATTN_SAFE_SHIFT, fixed_shift, running_max)


def attention(q, k, v):
    tq, tk = ATTN_TQ, ATTN_TK
    nq = SEQ // tq
    ctx0 = LAT_ROWS // CTX_LEN
    return pl.pallas_call(
        functools.partial(_attn_kernel, tq=tq, tk=tk),
        grid=(BATCH, MLA_HEADS // 2, nq),
        in_specs=[pl.BlockSpec((tq, 2 * HEAD_PAD), lambda b, h, i: (b * nq + i, h)),
                  pl.BlockSpec((CTX_LEN, 2 * HEAD_PAD), lambda b, h, i: (ctx0 + b, h)),
                  pl.BlockSpec((SEQ, 2 * HEAD_PAD), lambda b, h, i: (b, h)),
                  pl.BlockSpec((CTX_LEN, 2 * MLA_V), lambda b, h, i: (ctx0 + b, h)),
                  pl.BlockSpec((SEQ, 2 * MLA_V), lambda b, h, i: (b, h))],
        out_specs=pl.BlockSpec((tq, 2 * MLA_V), lambda b, h, i: (b * nq + i, h)),
        out_shape=jax.ShapeDtypeStruct((LAT_ROWS, MLA_HEADS * MLA_V), BF16),
        scratch_shapes=[pltpu.VMEM((2, SUBLANES, LANES), F32)],
        compiler_params=_params("arbitrary", "arbitrary", "arbitrary"),
        name="attention",
    )(q, k, k, v, v)


def _attn_out_kernel(x_ref, mod_ref, o_ref_in, wo_ref, out_ref):
    gate = mod_ref[0][:, 2 * D_MODEL:3 * D_MODEL]
    out_ref[...] = x_ref[...] + gate * jnp.dot(o_ref_in[...], wo_ref[...], preferred_element_type=F32)


def attn_out(x_all, mod, o, w_o):
    tm = ROW_TILE
    row = lambda n: pl.BlockSpec((tm, n), lambda i: (i, 0))
    return pl.pallas_call(
        _attn_out_kernel,
        grid=(LAT_ROWS // tm,),
        in_specs=[row(D_MODEL), pl.BlockSpec((1, 1, N_MOD * D_MODEL), lambda i: (_mod_row(i, tm), 0, 0)),
                  row(MLA_HEADS * MLA_V), _full(w_o.shape)],
        out_specs=row(D_MODEL),
        out_shape=jax.ShapeDtypeStruct((LAT_ROWS, D_MODEL), F32),
        compiler_params=_params("arbitrary"),
        name="attn_out",
    )(x_all, mod, o, w_o.astype(BF16))


def kernel(x, c, ctx, c_ctx, l0_w_ada, l0_b_ada, l0_norm1, l0_w_in, l0_conv_w, l0_conv_b, l0_dt_bias, l0_a_log, l0_d_skip, l0_ssd_norm, l0_w_out, l0_norm2, l0_peer_wq, l0_peer_keys, l0_peer_u, l0_peer_v, l1_w_ada, l1_b_ada, l1_norm1, l1_w_dqkv, l1_q_a_norm, l1_kv_a_norm, l1_w_uq, l1_w_ukv, l1_q_norm, l1_k_norm, l1_w_o, l1_norm2, l1_peer_wq, l1_peer_keys, l1_peer_u, l1_peer_v):
    x_all = jnp.concatenate([x.reshape(LAT_ROWS, D_MODEL), ctx.reshape(BATCH * CTX_LEN, D_MODEL)], axis=0)

    mod0 = ada_table(c, c_ctx, l0_w_ada, l0_b_ada)
    f, z, xbc, dt = even_in(x_all, mod0, l0_norm1, l0_w_in)
    u = conv_silu(xbc, l0_conv_w, l0_conv_b)
    yf, yb = ssd_scan(u, dt, l0_dt_bias, l0_a_log)
    four = fourier_mix(f)
    x_all = even_out(x_all, mod0, four, yf, yb, u, z, l0_d_skip, l0_ssd_norm, l0_w_out)
    x_all = peer_layer(x_all, mod0, l0_norm2, l0_peer_wq, l0_peer_keys, l0_peer_u, l0_peer_v)

    mod1 = ada_table(c, c_ctx, l1_w_ada, l1_b_ada)
    p1 = dict(norm1=l1_norm1, w_dqkv=l1_w_dqkv, q_a_norm=l1_q_a_norm, kv_a_norm=l1_kv_a_norm, w_uq=l1_w_uq,
              w_ukv=l1_w_ukv, q_norm=l1_q_norm, k_norm=l1_k_norm)
    q, k, v = mla_in(x_all, mod1, p1)
    o = attention(q, k, v)
    x_lat = attn_out(x_all, mod1, o, l1_w_o)
    x_lat = peer_layer(x_lat, mod1, l1_norm2, l1_peer_wq, l1_peer_keys, l1_peer_u, l1_peer_v)
    return x_lat.reshape(BATCH, SEQ, D_MODEL)
```

```python
import functools
import math

import numpy as np
import jax
import jax.numpy as jnp
from jax import lax
from jax.experimental import pallas as pl
from jax.experimental.pallas import tpu as pltpu

D_MODEL = 1024
BATCH = 2
SEQ = 8192
GRID_W = 64
CTX_LEN = 256
EPS = 1e-6
N_MOD = 6
LAT_ROWS = BATCH * SEQ
ALL_ROWS = LAT_ROWS + BATCH * CTX_LEN

FNET_GROUPS = 4
FNET_GROUP_DIM = 128
FNET_WIDTH = FNET_GROUPS * FNET_GROUP_DIM
FFT_N1 = 64
FFT_N2 = 128

SSD_HEADS = 16
SSD_HEAD_DIM = 64
SSD_INNER = SSD_HEADS * SSD_HEAD_DIM
SSD_GROUPS = 4
SSD_STATE = 128
SSD_CONV = 5
SSD_CHUNK = 128
SSD_CONV_DIM = SSD_INNER + 2 * SSD_GROUPS * SSD_STATE
SSD_GROUP_W = SSD_INNER // SSD_GROUPS

MLA_HEADS = 16
MLA_NOPE = 64
MLA_ROPE = 32
MLA_QK = MLA_NOPE + MLA_ROPE
MLA_V = 64
MLA_Q_LORA = 384
MLA_KV_LORA = 256
ROPE_THETA = 10000.0
HEAD_PAD = 128

PEER_HEADS = 8
PEER_KEYS = 128
PEER_EXPERTS = PEER_KEYS * PEER_KEYS
PEER_QDIM = 256
PEER_TOPK = 16

LANES = 128
SUBLANES = 8
VMEM_LIMIT = 56 * 1024 * 1024

ROW_TILE = 256
PEER_TILE = 512
PEER_CHUNK = 512
PEER_CHUNKS_PER_STEP = 2
PEER_SUB = 256
PEER_SELECT_TOKENS = 128
ATTN_TQ = 512
ATTN_TK = 512

F32 = jnp.float32
BF16 = jnp.bfloat16
HIGHEST = lax.Precision.HIGHEST
NEG_INF = float("-inf")


def _params(*sem):
    return pltpu.CompilerParams(dimension_semantics=sem, vmem_limit_bytes=VMEM_LIMIT)


def _mod_row(i, tile):
    return jnp.minimum((i * tile) // SEQ, BATCH)


def _full(shape):
    return pl.BlockSpec(shape, lambda *_: (0,) * len(shape))


def _silu(x):
    return x * (1.0 / (1.0 + jnp.exp(-x)))


def _modulated_norm(x, gain, mod, k_shift, k_scale):
    shift = mod[:, k_shift * D_MODEL:(k_shift + 1) * D_MODEL]
    scale = mod[:, k_scale * D_MODEL:(k_scale + 1) * D_MODEL]
    ms = jnp.mean(x * x, axis=-1, keepdims=True)
    return x * lax.rsqrt(ms + EPS) * gain * (1.0 + scale) + shift


def _bdot(a, b):
    return jnp.dot(a.astype(BF16), b.astype(BF16), preferred_element_type=F32)


def _hdot(a, b):
    return jnp.dot(a, b, precision=HIGHEST, preferred_element_type=F32)


def _bf16_terms(x, n):
    terms = []
    for _ in range(n):
        t = x.astype(BF16)
        terms.append(t)
        x = x - t.astype(F32)
    return terms


def _dot_exact_rhs(a, b01, terms=3):
    return sum(jnp.dot(t, b01, preferred_element_type=F32) for t in _bf16_terms(a, terms))


def _dot_exact_lhs(a01, b):
    return sum(jnp.dot(a01, t, preferred_element_type=F32) for t in _bf16_terms(b, 3))


def _dot3(a, b):
    a_hi, a_lo = _bf16_terms(a, 2)
    b_hi, b_lo = _bf16_terms(b, 2)
    dot = lambda x, y: jnp.dot(x, y, preferred_element_type=F32)
    return dot(a_hi, b_hi) + (dot(a_hi, b_lo) + dot(a_lo, b_hi))


def _ada_kernel(c_ref, w_ref, b_ref, o_ref):
    o_ref[...] = _hdot(_silu(c_ref[...]), w_ref[...]) + b_ref[...]


def ada_table(c, c_ctx, w_ada, b_ada):
    cond = jnp.concatenate([c, c_ctx[None, :], jnp.zeros((SUBLANES - BATCH - 1, D_MODEL), F32)], axis=0)
    tn = 512
    out = pl.pallas_call(
        _ada_kernel,
        grid=(N_MOD * D_MODEL // tn,),
        in_specs=[_full((SUBLANES, D_MODEL)),
                  pl.BlockSpec((D_MODEL, tn), lambda j: (0, j)),
                  pl.BlockSpec((1, tn), lambda j: (0, j))],
        out_specs=pl.BlockSpec((SUBLANES, tn), lambda j: (0, j)),
        out_shape=jax.ShapeDtypeStruct((SUBLANES, N_MOD * D_MODEL), F32),
        compiler_params=_params("arbitrary"),
        name="ada_table",
    )(cond, w_ada, b_ada[None, :])
    return out[:BATCH + 1].reshape(BATCH + 1, 1, N_MOD * D_MODEL)


def _even_in_kernel(x_ref, mod_ref, g_ref, wf_ref, wz_ref, wx_ref, wd_ref, f_ref, z_ref, xbc_ref, dt_ref):
    h = _modulated_norm(x_ref[...], g_ref[...], mod_ref[0], 0, 1).astype(BF16)
    f_ref[...] = jnp.dot(h, wf_ref[...], preferred_element_type=F32)
    z_ref[...] = jnp.dot(h, wz_ref[...], preferred_element_type=F32)
    xbc_ref[...] = jnp.dot(h, wx_ref[...], preferred_element_type=F32)
    dt_ref[...] = jnp.dot(h, wd_ref[...], preferred_element_type=F32)


def even_in(x_all, mod, norm1, w_in):
    rows = x_all.shape[0]
    o1, o2, o3 = FNET_WIDTH, FNET_WIDTH + SSD_INNER, FNET_WIDTH + SSD_INNER + SSD_CONV_DIM
    wf = w_in[:, :o1].astype(BF16)
    wz = w_in[:, o1:o2].astype(BF16)
    wx = w_in[:, o2:o3].astype(BF16)
    wd = jnp.pad(w_in[:, o3:], ((0, 0), (0, LANES - 2 * SSD_HEADS))).astype(BF16)
    tm = ROW_TILE
    row = lambda n: pl.BlockSpec((tm, n), lambda i: (i, 0))
    return pl.pallas_call(
        _even_in_kernel,
        grid=(rows // tm,),
        in_specs=[row(D_MODEL),
                  pl.BlockSpec((1, 1, N_MOD * D_MODEL), lambda i: (_mod_row(i, tm), 0, 0)),
                  _full((1, D_MODEL)),
                  _full(wf.shape), _full(wz.shape), _full(wx.shape), _full(wd.shape)],
        out_specs=[row(FNET_WIDTH), row(SSD_INNER), row(SSD_CONV_DIM), row(LANES)],
        out_shape=[jax.ShapeDtypeStruct((rows, FNET_WIDTH), F32),
                   jax.ShapeDtypeStruct((rows, SSD_INNER), F32),
                   jax.ShapeDtypeStruct((rows, SSD_CONV_DIM), F32),
                   jax.ShapeDtypeStruct((rows, LANES), F32)],
        compiler_params=_params("arbitrary"),
        name="even_in",
    )(x_all, mod, norm1[None, :], wf, wz, wx, wd)


def _conv_kernel(x_ref, prev_ref, next_ref, w_ref, b_ref, o_ref, ext_ref, *, tm):
    row0 = pl.program_id(0) * tm
    in_lat = row0 < LAT_ROWS
    first = jnp.where(in_lat, row0 % SEQ == 0, (row0 - LAT_ROWS) % CTX_LEN == 0)
    last = jnp.where(in_lat, (row0 + tm) % SEQ == 0, (row0 + tm - LAT_ROWS) % CTX_LEN == 0)
    ext_ref[0:SUBLANES, :] = prev_ref[...] * jnp.where(first, 0.0, 1.0)
    ext_ref[SUBLANES:SUBLANES + tm, :] = x_ref[...]
    ext_ref[SUBLANES + tm:2 * SUBLANES + tm, :] = next_ref[...] * jnp.where(last, 0.0, 1.0)
    pad = SSD_CONV // 2
    acc = b_ref[...] + w_ref[0:1, :] * ext_ref[pl.ds(SUBLANES - pad, tm), :]
    for k in range(1, SSD_CONV):
        acc = acc + w_ref[k:k + 1, :] * ext_ref[pl.ds(SUBLANES - pad + k, tm), :]
    o_ref[...] = _silu(acc)


def conv_silu(xbc, conv_w, conv_b):
    rows, ch = xbc.shape
    tm, tc = ROW_TILE, ch
    assert CTX_LEN % tm == 0 and SEQ % tm == 0
    per = tm // SUBLANES
    n_small = rows // SUBLANES
    w = jnp.pad(conv_w, ((0, SUBLANES - SSD_CONV), (0, 0)))
    return pl.pallas_call(
        functools.partial(_conv_kernel, tm=tm),
        grid=(rows // tm, ch // tc),
        in_specs=[pl.BlockSpec((tm, tc), lambda i, j: (i, j)),
                  pl.BlockSpec((SUBLANES, tc), lambda i, j: (jnp.maximum(i * per - 1, 0), j)),
                  pl.BlockSpec((SUBLANES, tc), lambda i, j: (jnp.minimum((i + 1) * per, n_small - 1), j)),
                  pl.BlockSpec((SUBLANES, tc), lambda i, j: (0, j)),
                  pl.BlockSpec((1, tc), lambda i, j: (0, j))],
        out_specs=pl.BlockSpec((tm, tc), lambda i, j: (i, j)),
        out_shape=jax.ShapeDtypeStruct((rows, ch), F32),
        scratch_shapes=[pltpu.VMEM((tm + 2 * SUBLANES, tc), F32)],
        compiler_params=_params("arbitrary", "arbitrary"),
        name="conv_silu",
    )(xbc, xbc, xbc, w, conv_b[None, :])


def _softplus(x):
    return jnp.maximum(x, 0.0) + jnp.log1p(jnp.exp(-jnp.abs(x)))


def _ssd_direction(u_ref, dt_ref, dtb_ref, alog_ref, exp_ref, state_ref, y_ref, direction):
    cl = SSD_CHUNK
    xs = u_ref[:, :SSD_INNER]
    dtv = _softplus(dt_ref[...] + dtb_ref[...])
    dta = dtv * (-jnp.exp(alog_ref[...]))
    r = lax.broadcasted_iota(jnp.int32, (cl, cl), 0)
    c = lax.broadcasted_iota(jnp.int32, (cl, cl), 1)
    tri = (r >= c) if direction == 0 else (r <= c)
    cs = _dot_exact_lhs(tri.astype(F32).astype(BF16), dta)
    cs_t = cs.T
    edge = cs[cl - 1:cl, :] if direction == 0 else cs[0:1, :]
    to_end = jnp.exp(edge - cs)
    from_start = jnp.exp(cs)
    spread = _dot_exact_rhs(jnp.concatenate([dtv, to_end, from_start], axis=0), exp_ref[direction], terms=2)
    dt_x, te_x, fs_x = spread[:cl], spread[cl:2 * cl], spread[2 * cl:]
    chunk_decay = fs_x[cl - 1:cl, :] if direction == 0 else fs_x[0:1, :]
    xd = xs * dt_x
    xd_b = xd.astype(BF16)
    xte_b = (xd * te_x).astype(BF16)
    lane = lax.broadcasted_iota(jnp.int32, (cl, LANES), 1)
    low = lane < SSD_HEAD_DIM
    for g in range(SSD_GROUPS):
        bg = u_ref[:, SSD_INNER + g * SSD_STATE:SSD_INNER + (g + 1) * SSD_STATE]
        cg = u_ref[:, SSD_INNER + (SSD_GROUPS + g) * SSD_STATE:SSD_INNER + (SSD_GROUPS + g + 1) * SSD_STATE]
        bg_b, cg_b = bg.astype(BF16), cg.astype(BF16)
        cb = lax.dot_general(cg_b, bg_b, (((1,), (1,)), ((), ())), preferred_element_type=F32)
        gs = slice(g * SSD_GROUP_W, (g + 1) * SSD_GROUP_W)
        state = state_ref[g]
        y_off = jnp.dot(cg_b, state.astype(BF16), preferred_element_type=F32) * fs_x[:, gs]
        new_state = jnp.dot(bg.T.astype(BF16), xte_b[:, gs], preferred_element_type=F32)
        state_ref[g] = state * chunk_decay[:, gs] + new_state
        for pair in range(2):
            blk = xd_b[:, g * SSD_GROUP_W + pair * LANES:g * SSD_GROUP_W + (pair + 1) * LANES]
            y_pair = None
            for sub in range(2):
                col = direction * SSD_HEADS + g * 4 + pair * 2 + sub
                seg = cs[:, col:col + 1] - cs_t[col:col + 1, :]
                decay = jnp.exp(jnp.where(tri, seg, NEG_INF))
                m = (cb * decay).astype(BF16)
                half = jnp.where(low if sub == 0 else jnp.logical_not(low), blk, jnp.zeros_like(blk))
                part = jnp.dot(m, half, preferred_element_type=F32)
                y_pair = part if y_pair is None else y_pair + part
            lo = g * SSD_GROUP_W + pair * LANES
            y_ref[:, lo:lo + LANES] = y_pair + y_off[:, pair * LANES:(pair + 1) * LANES]


def _ssd_kernel(uf_ref, ub_ref, dtf_ref, dtb_in_ref, bias_ref, alog_ref, exp_ref, yf_ref, yb_ref, sf_ref, sb_ref):
    @pl.when(pl.program_id(1) == 0)
    def _():
        sf_ref[...] = jnp.zeros_like(sf_ref)
        sb_ref[...] = jnp.zeros_like(sb_ref)

    _ssd_direction(uf_ref, dtf_ref, bias_ref, alog_ref, exp_ref, sf_ref, yf_ref, 0)
    _ssd_direction(ub_ref, dtb_in_ref, bias_ref, alog_ref, exp_ref, sb_ref, yb_ref, 1)


def ssd_scan(u, dt, dt_bias, a_log):
    rows = u.shape[0]
    cl = SSD_CHUNK
    lat_chunks, ctx_chunks = SEQ // cl, CTX_LEN // cl
    steps = ctx_chunks + lat_chunks
    ctx0 = LAT_ROWS // cl

    def fwd_chunk(b, t):
        return jnp.where(t < ctx_chunks, ctx0 + b * ctx_chunks + t, b * lat_chunks + t - ctx_chunks)

    def bwd_chunk(b, t):
        return jnp.where(t < ctx_chunks, ctx0 + b * ctx_chunks + (ctx_chunks - 1 - t),
                         b * lat_chunks + (lat_chunks - 1) - (t - ctx_chunks))

    pad = LANES - 2 * SSD_HEADS
    bias = jnp.pad(dt_bias.reshape(1, -1), ((0, 0), (0, pad)))
    alog = jnp.pad(a_log.reshape(1, -1), ((0, 0), (0, pad)))
    expand = np.zeros((2, LANES, SSD_INNER), np.float32)
    for d in range(2):
        for h in range(SSD_HEADS):
            expand[d, d * SSD_HEADS + h, h * SSD_HEAD_DIM:(h + 1) * SSD_HEAD_DIM] = 1.0
    spec = lambda n, fn: pl.BlockSpec((cl, n), lambda b, t: (fn(b, t), 0))
    return pl.pallas_call(
        _ssd_kernel,
        grid=(BATCH, steps),
        in_specs=[spec(SSD_CONV_DIM, fwd_chunk), spec(SSD_CONV_DIM, bwd_chunk),
                  spec(LANES, fwd_chunk), spec(LANES, bwd_chunk),
                  _full((1, LANES)), _full((1, LANES)), _full((2, LANES, SSD_INNER))],
        out_specs=[spec(SSD_INNER, fwd_chunk), spec(SSD_INNER, bwd_chunk)],
        out_shape=[jax.ShapeDtypeStruct((rows, SSD_INNER), F32)] * 2,
        scratch_shapes=[pltpu.VMEM((SSD_GROUPS, SSD_STATE, SSD_GROUP_W), F32)] * 2,
        compiler_params=_params("arbitrary", "arbitrary"),
        name="ssd_scan",
    )(u, u, dt, dt, bias, alog, jnp.asarray(expand, BF16))


def _dft_cos_sin(n):
    k = np.arange(n)
    ang = 2.0 * np.pi * ((k[:, None] * k[None, :]) % n) / n
    return np.cos(ang), np.sin(ang)


def _fft1_kernel(x_ref, m_ref, o_ref):
    o_ref[0] = _dot3(m_ref[...], x_ref[...])


def _fft2_kernel(ar_ref, ai_ref, twr_ref, twi_ref, m2_ref, mc_ref, o_ref):
    ar, ai = ar_ref[0, 0], ai_ref[0, 0]
    twr = jnp.concatenate([twr_ref[0]] * FNET_GROUPS, axis=1)
    twi = jnp.concatenate([twi_ref[0]] * FNET_GROUPS, axis=1)
    p = ar * twr - ai * twi
    q = ar * twi + ai * twr
    uv = _dot3(m2_ref[...], jnp.concatenate([p, q], axis=0))
    n2 = FFT_N2
    for g in range(FNET_GROUPS):
        gs = slice(g * FNET_GROUP_DIM, (g + 1) * FNET_GROUP_DIM)
        o_ref[0, :, gs] = _dot3(jnp.concatenate([uv[:n2, gs], uv[n2:, gs]], axis=1), mc_ref[...])


def _fft_ctx_kernel(x_ref, mc_ref, mp_ref, o_ref):
    x = x_ref[...]
    for g in range(FNET_GROUPS):
        gs = slice(g * FNET_GROUP_DIM, (g + 1) * FNET_GROUP_DIM)
        cs = _dot3(x[:, gs], mc_ref[...])
        stacked = jnp.concatenate([cs[:, :FNET_GROUP_DIM], cs[:, FNET_GROUP_DIM:]], axis=0)
        o_ref[:, gs] = _dot3(mp_ref[...], stacked)


def fourier_mix(f_all):
    rows = f_all.shape[0]
    n1, n2, gd = FFT_N1, FFT_N2, FNET_GROUP_DIM
    row_w = n2 * FNET_WIDTH
    c1, s1 = _dft_cos_sin(n1)
    m1 = jnp.asarray(np.concatenate([c1, -s1], axis=0), F32)
    tn = 8192
    stage1 = pl.pallas_call(
        _fft1_kernel,
        grid=(BATCH, row_w // tn),
        in_specs=[pl.BlockSpec((n1, tn), lambda b, j: (b, j)), _full((2 * n1, n1))],
        out_specs=pl.BlockSpec((1, 2 * n1, tn), lambda b, j: (b, 0, j)),
        out_shape=jax.ShapeDtypeStruct((BATCH, 2 * n1, row_w), F32),
        compiler_params=_params("arbitrary", "arbitrary"),
        name="fft_stage1",
    )(f_all.reshape(rows // n2, row_w), m1)

    k1 = np.arange(n1)[:, None]
    l2 = np.arange(n2)[None, :]
    tw = 2.0 * np.pi * (k1 * l2) / SEQ
    twr = jnp.asarray(np.repeat(np.cos(tw)[:, :, None], gd, axis=2), F32)
    twi = jnp.asarray(np.repeat(-np.sin(tw)[:, :, None], gd, axis=2), F32)
    c2, s2 = _dft_cos_sin(n2)
    m2 = jnp.asarray(np.block([[c2, s2], [-s2, c2]]), F32)
    cc, sc = _dft_cos_sin(gd)
    mc = jnp.asarray(np.concatenate([cc, sc], axis=0) / math.sqrt(SEQ * gd), F32)
    a4 = stage1.reshape(BATCH, 2 * n1, n2, FNET_WIDTH)
    lat = pl.pallas_call(
        _fft2_kernel,
        grid=(BATCH, n1),
        in_specs=[pl.BlockSpec((1, 1, n2, FNET_WIDTH), lambda b, k: (b, k, 0, 0)),
                  pl.BlockSpec((1, 1, n2, FNET_WIDTH), lambda b, k: (b, n1 + k, 0, 0)),
                  pl.BlockSpec((1, n2, gd), lambda b, k: (k, 0, 0)),
                  pl.BlockSpec((1, n2, gd), lambda b, k: (k, 0, 0)),
                  _full((2 * n2, 2 * n2)), _full((2 * gd, gd))],
        out_specs=pl.BlockSpec((1, n2, FNET_WIDTH), lambda b, k: (b, 0, k)),
        out_shape=jax.ShapeDtypeStruct((BATCH, n2, n1 * FNET_WIDTH), F32),
        compiler_params=_params("arbitrary", "arbitrary"),
        name="fft_stage2",
    )(a4, a4, twr, twi, m2, mc)

    cp, sp = _dft_cos_sin(CTX_LEN)
    mp = jnp.asarray(np.concatenate([cp, -sp], axis=1) / math.sqrt(CTX_LEN * gd), F32)
    mcc = jnp.asarray(np.concatenate([cc, sc], axis=1), F32)
    ctx0 = LAT_ROWS // CTX_LEN
    ctx = pl.pallas_call(
        _fft_ctx_kernel,
        grid=(BATCH,),
        in_specs=[pl.BlockSpec((CTX_LEN, FNET_WIDTH), lambda b: (ctx0 + b, 0)),
                  _full((gd, 2 * gd)), _full((CTX_LEN, 2 * CTX_LEN))],
        out_specs=pl.BlockSpec((CTX_LEN, FNET_WIDTH), lambda b: (b, 0)),
        out_shape=jax.ShapeDtypeStruct((BATCH * CTX_LEN, FNET_WIDTH), F32),
        compiler_params=_params("arbitrary"),
        name="fft_ctx",
    )(f_all, mcc, mp)
    return jnp.concatenate([lat.reshape(LAT_ROWS, FNET_WIDTH), ctx], axis=0)


def _even_out_kernel(x_ref, mod_ref, four_ref, yf_ref, yb_ref, xs_ref, z_ref, dsk_ref, gn_ref, wof_ref, wos_ref,
                     o_ref):
    y = yf_ref[...] + yb_ref[...] + dsk_ref[...] * xs_ref[...]
    gated = y * _silu(z_ref[...])
    out = jnp.dot(four_ref[...].astype(BF16), wof_ref[...], preferred_element_type=F32)
    for g in range(SSD_GROUPS):
        gs = slice(g * SSD_GROUP_W, (g + 1) * SSD_GROUP_W)
        v = gated[:, gs]
        normed = v * lax.rsqrt(jnp.mean(v * v, axis=-1, keepdims=True) + EPS) * gn_ref[:, gs]
        out = out + jnp.dot(normed.astype(BF16), wos_ref[gs, :], preferred_element_type=F32)
    gate = mod_ref[0][:, 2 * D_MODEL:3 * D_MODEL]
    o_ref[...] = x_ref[...] + gate * out


def even_out(x_all, mod, four, yf, yb, u, z, d_skip, ssd_norm, w_out):
    rows = x_all.shape[0]
    tm = ROW_TILE
    dsk = jnp.repeat(d_skip[0] + d_skip[1], SSD_HEAD_DIM)[None, :]
    wof = w_out[:FNET_WIDTH].astype(BF16)
    wos = w_out[FNET_WIDTH:].astype(BF16)
    row = lambda n: pl.BlockSpec((tm, n), lambda i: (i, 0))
    return pl.pallas_call(
        _even_out_kernel,
        grid=(rows // tm,),
        in_specs=[row(D_MODEL),
                  pl.BlockSpec((1, 1, N_MOD * D_MODEL), lambda i: (_mod_row(i, tm), 0, 0)),
                  row(FNET_WIDTH), row(SSD_INNER), row(SSD_INNER), row(SSD_INNER), row(SSD_INNER),
                  _full((1, SSD_INNER)), _full((1, SSD_INNER)), _full(wof.shape), _full(wos.shape)],
        out_specs=row(D_MODEL),
        out_shape=jax.ShapeDtypeStruct((rows, D_MODEL), F32),
        compiler_params=_params("arbitrary"),
        name="even_out",
    )(x_all, mod, four, yf, yb, u, z, dsk, ssd_norm[None, :], wof, wos)


BF16_ROWS = 2 * SUBLANES
NOT_TOP = 64.0


def _erf_gelu(z):
    return 0.5 * z * (1.0 + lax.erf(z * (1.0 / math.sqrt(2.0))))


def _top_ranks(score, k, want_rank=True):
    work, vals = score, []
    rank = jnp.full(score.shape, NOT_TOP, F32) if want_rank else None
    for i in range(k):
        m = jnp.max(work, axis=0, keepdims=True)
        hit = work == m
        vals.append(m)
        if want_rank:
            rank = jnp.where(hit, float(i), rank)
        work = jnp.where(hit, NEG_INF, work)
    return jnp.concatenate(vals, axis=0), rank


def _pair_candidates(v1, v2, k):
    assert k == 2 * SUBLANES, "the row grouping below is laid out for k = 16"
    t = v1.shape[1]
    row8 = lax.broadcasted_iota(jnp.int32, (SUBLANES, t), 0)
    rowf = row8.astype(F32)
    sums = [v1[0:1] + v2[0:8], v1[0:1] + v2[8:16], v1[1:2] + v2[0:8], v1[8:16] + v2[0:1]]
    order = [rowf, rowf + 8.0, rowf + float(k), (rowf + 8.0) * float(k)]
    for i in range(2, 8):
        sums.append(jnp.where(row8 < k // (i + 1), v1[i:i + 1] + v2[0:8], NEG_INF))
        order.append(rowf + float(i * k))
    return jnp.concatenate(sums, axis=0), jnp.concatenate(order, axis=0)


def _select_by_value(s1, s2, k):
    v1, _ = _top_ranks(s1, k, want_rank=False)
    v2, r2 = _top_ranks(s2, k)
    cand, _ = _pair_candidates(v1, v2, k)
    tau = _top_ranks(cand, k, want_rank=False)[0][k - 1:k, :]
    chosen = cand >= tau
    zsum = jnp.sum(jnp.where(chosen, jnp.exp(cand - (v1[0:1] + v2[0:1])), 0.0), axis=0, keepdims=True)
    in_top = s1 >= v1[k - 1:k]
    a1 = jnp.where(in_top, s1, NEG_INF)
    lim = jnp.zeros_like(s1)
    for j in range(k // 2):
        lim = lim + jnp.where(a1 + v2[j:j + 1] >= tau, 1.0, 0.0)
    best = jnp.zeros_like(tau)
    for j in range(k // 2, k):
        best = best + jnp.where(v1[0:1] + v2[j:j + 1] >= tau, 1.0, 0.0)
    lim = lim + jnp.where(s1 == v1[0:1], best, 0.0)
    count = lambda mask: jnp.sum(jnp.where(mask, 1.0, 0.0), axis=0, keepdims=True)
    most = jnp.maximum(jnp.maximum(count(in_top), count(r2 < k)), count(chosen))
    return lim, jnp.exp(s1 - v1[0:1]) / zsum, r2, jnp.exp(s2 - v2[0:1]), most


def _top_ranks_ordered(score, order, k):
    work, vals = score, []
    rank = jnp.full(score.shape, NOT_TOP, F32)
    for i in range(k):
        m = jnp.max(work, axis=0, keepdims=True)
        first = jnp.min(jnp.where(work == m, order, float(2 ** 20)), axis=0, keepdims=True)
        taken = order == first
        vals.append(m)
        rank = jnp.where(taken, float(i), rank)
        work = jnp.where(taken, NEG_INF, work)
    return jnp.concatenate(vals, axis=0), rank


def _select_by_order(s1, s2, k):
    key_index = lax.broadcasted_iota(jnp.int32, s1.shape, 0).astype(F32)
    v1, r1 = _top_ranks_ordered(s1, key_index, k)
    v2, r2 = _top_ranks_ordered(s2, key_index, k)
    cand, position = _pair_candidates(v1, v2, k)
    chosen = _top_ranks_ordered(cand, position, k)[1] < k
    zsum = jnp.sum(jnp.where(chosen, jnp.exp(cand - (v1[0:1] + v2[0:1])), 0.0), axis=0, keepdims=True)
    picks = jnp.where(chosen, 1.0, 0.0)
    group = lambda g: jnp.sum(picks[g * SUBLANES:(g + 1) * SUBLANES], axis=0, keepdims=True)
    per_rank = [group(0) + group(1), group(2)] + [group(g) for g in range(4, 10)]
    per_rank = jnp.concatenate(per_rank + [picks[3 * SUBLANES:4 * SUBLANES]], axis=0)
    lim = jnp.zeros_like(s1)
    for i in range(k):
        lim = lim + jnp.where(r1 == float(i), per_rank[i:i + 1], 0.0)
    return lim, jnp.exp(s1 - v1[0:1]) / zsum, r2, jnp.exp(s2 - v2[0:1])


def _peer_kernel(x_ref, mod_ref, g_ref, wq_ref, keys_ref, u0_ref, *rest, tm, chunk, per_step):
    u_refs = rest[:per_step]
    (vt_ref, o_ref, ht_ref, q_ref, lim_ref, w1_ref, rank2_ref, w2_ref, acc_ref, za_ref, zb_ref) = rest[per_step:]
    c = pl.program_id(1)
    k = PEER_TOPK
    packed = (PEER_KEYS // BF16_ROWS, BF16_ROWS, tm)
    tsel = PEER_SELECT_TOKENS

    @pl.when(c == 0)
    def _prologue():
        h2 = _modulated_norm(x_ref[...], g_ref[...], mod_ref[0], 3, 4)
        ht = h2.T.astype(BF16)
        ht_ref[...] = ht
        q = jnp.dot(wq_ref[...], ht, preferred_element_type=F32)
        q_ref[...] = q.reshape(2 * PEER_HEADS, PEER_QDIM // 2, tm)
        acc_ref[...] = jnp.zeros_like(acc_ref)
        za_ref[...] = jnp.dot(u0_ref[...], ht, preferred_element_type=F32)
        def head_body(h, carry):
            def scores(part):
                ts = slice(part * tsel, (part + 1) * tsel)
                s1 = jnp.dot(keys_ref[2 * h], q_ref[2 * h, :, ts].astype(BF16), preferred_element_type=F32)
                s2 = jnp.dot(keys_ref[2 * h + 1], q_ref[2 * h + 1, :, ts].astype(BF16), preferred_element_type=F32)
                return ts, s1, s2

            def store(ts, lim, w1, r2, w2):
                small = (PEER_KEYS // BF16_ROWS, BF16_ROWS, tsel)
                lim_ref[h, :, ts] = lim
                w1_ref[h, :, ts] = w1
                rank2_ref[h, :, :, ts] = r2.reshape(small).astype(BF16)
                w2_ref[h, :, :, ts] = w2.reshape(small).astype(BF16)

            most = None
            for part in range(tm // tsel):
                ts, s1, s2 = scores(part)
                lim, w1, r2, w2, count = _select_by_value(s1, s2, k)
                store(ts, lim, w1, r2, w2)
                most = count if most is None else jnp.maximum(most, count)

            @pl.when(jnp.max(most) > k)
            def _ties():
                for part in range(tm // tsel):
                    ts, s1, s2 = scores(part)
                    store(ts, *_select_by_order(s1, s2, k))

            return carry

        lax.fori_loop(0, PEER_HEADS, head_body, 0)

    per_key = PEER_KEYS // BF16_ROWS
    keys_per_sub = PEER_SUB // PEER_KEYS
    step_keys = per_step * chunk // PEER_KEYS
    key0 = pl.multiple_of(c * step_keys, step_keys)
    step_lim = [lim_ref[h, pl.ds(key0, step_keys), :] for h in range(PEER_HEADS)]
    step_w1 = [w1_ref[h, pl.ds(key0, step_keys), :] for h in range(PEER_HEADS)]

    def gated_values(z_ref, which):
        total = None
        for j in range(chunk // PEER_SUB):
            rows = slice(j * PEER_SUB, (j + 1) * PEER_SUB)
            act = _erf_gelu(z_ref[rows, :]).reshape(PEER_SUB // BF16_ROWS, BF16_ROWS, tm).astype(BF16)
            parts = []
            for jj in range(keys_per_sub):
                e1 = which * (chunk // PEER_KEYS) + j * keys_per_sub + jj
                gate = jnp.zeros(packed, BF16)
                for h in range(PEER_HEADS):
                    lim = jnp.broadcast_to(step_lim[h][e1:e1 + 1, :], (BF16_ROWS, tm)).astype(BF16)
                    w1 = jnp.broadcast_to(step_w1[h][e1:e1 + 1, :], (BF16_ROWS, tm)).astype(BF16)
                    picked = jnp.where(rank2_ref[h] < lim[None], w2_ref[h], jnp.zeros(packed, BF16))
                    gate = gate + picked * w1[None]
                parts.append(act[jj * per_key:(jj + 1) * per_key] * gate)
            a = jnp.concatenate(parts, axis=0).reshape(PEER_SUB, tm)
            cols = slice(which * chunk + j * PEER_SUB, which * chunk + (j + 1) * PEER_SUB)
            d = jnp.dot(vt_ref[:, cols], a, preferred_element_type=F32)
            total = d if total is None else total + d
        return total

    bufs = (za_ref, zb_ref)
    total = None
    for i in range(per_step):
        bufs[(i + 1) % 2][...] = jnp.dot(u_refs[i][...], ht_ref[...], preferred_element_type=F32)
        d = gated_values(bufs[i % 2], i)
        total = d if total is None else total + d
    acc_ref[...] += total

    @pl.when(c == pl.num_programs(1) - 1)
    def _epilogue():
        gate2 = mod_ref[0][:, 5 * D_MODEL:6 * D_MODEL]
        o_ref[...] = x_ref[...] + gate2 * acc_ref[...].T


def peer_layer(x_rows, mod, norm2, peer_wq, peer_keys, peer_u, peer_v):
    rows = x_rows.shape[0]
    tm, chunk = PEER_TILE, PEER_CHUNK
    wq_t = peer_wq.T.astype(BF16)
    keys = peer_keys.reshape(2 * PEER_HEADS, PEER_KEYS, PEER_QDIM // 2).astype(BF16)
    u = peer_u.astype(BF16)
    vt = peer_v.astype(BF16).T
    head_f32 = pltpu.VMEM((PEER_HEADS, PEER_KEYS, tm), F32)
    head_bf16 = pltpu.VMEM((PEER_HEADS, PEER_KEYS // BF16_ROWS, BF16_ROWS, tm), BF16)
    n_chunks = PEER_EXPERTS // chunk
    per_step = PEER_CHUNKS_PER_STEP
    ahead = lambda k: pl.BlockSpec((chunk, D_MODEL), lambda i, c: (jnp.minimum(per_step * c + k, n_chunks - 1), 0))
    return pl.pallas_call(
        functools.partial(_peer_kernel, tm=tm, chunk=chunk, per_step=per_step),
        grid=(rows // tm, n_chunks // per_step),
        in_specs=[pl.BlockSpec((tm, D_MODEL), lambda i, c: (i, 0)),
                  pl.BlockSpec((1, 1, N_MOD * D_MODEL), lambda i, c: (_mod_row(i, tm), 0, 0)),
                  _full((1, D_MODEL)), _full(wq_t.shape), _full(keys.shape),
                  pl.BlockSpec((chunk, D_MODEL), lambda i, c: (0, 0))]
                 + [ahead(k) for k in range(1, per_step + 1)]
                 + [pl.BlockSpec((D_MODEL, per_step * chunk), lambda i, c: (0, c))],
        out_specs=pl.BlockSpec((tm, D_MODEL), lambda i, c: (i, 0)),
        out_shape=jax.ShapeDtypeStruct((rows, D_MODEL), F32),
        scratch_shapes=[pltpu.VMEM((D_MODEL, tm), BF16),
                        pltpu.VMEM((2 * PEER_HEADS, PEER_QDIM // 2, tm), F32),
                        head_f32, head_f32, head_bf16, head_bf16,
                        pltpu.VMEM((D_MODEL, tm), F32),
                        pltpu.VMEM((chunk, tm), F32), pltpu.VMEM((chunk, tm), F32)],
        compiler_params=_params("arbitrary", "arbitrary"),
        name="peer",
    )(x_rows, mod, norm2[None, :], wq_t, keys, u, *([u] * per_step), vt)


def _rope_tables(rows):
    l = jnp.arange(SEQ)
    per_axis = MLA_ROPE // 2
    inv = ROPE_THETA ** (-jnp.arange(0, per_axis, 2, dtype=F32) / per_axis)
    ang = jnp.concatenate([(l // GRID_W)[:, None] * inv, (l % GRID_W)[:, None] * inv], axis=-1)
    cos, sin = jnp.cos(ang), jnp.sin(ang)
    one = jnp.ones((SEQ, MLA_NOPE), F32)
    tail = jnp.zeros((SEQ, HEAD_PAD - MLA_QK), F32)
    zero64 = jnp.zeros((SEQ, MLA_NOPE), F32)
    cos_t = jnp.concatenate([one, cos, cos, tail], axis=1)
    sin_t = jnp.concatenate([zero64, -sin, sin, tail], axis=1)
    n_ctx = rows - LAT_ROWS
    ident = jnp.concatenate([jnp.ones((n_ctx, MLA_QK), F32), jnp.zeros((n_ctx, HEAD_PAD - MLA_QK), F32)], axis=1)
    zeros = jnp.zeros((n_ctx, HEAD_PAD), F32)
    tile = lambda t, c: jnp.concatenate([t] * BATCH + [c], axis=0)
    return tile(cos_t, ident), tile(sin_t, zeros)


def _rope_swap_matrix():
    per_axis = MLA_ROPE // 2
    swap = np.zeros((HEAD_PAD, HEAD_PAD), np.float32)
    for j in range(per_axis):
        swap[MLA_NOPE + per_axis + j, MLA_NOPE + j] = 1.0
        swap[MLA_NOPE + j, MLA_NOPE + per_axis + j] = 1.0
    return swap


def _head_norm_rope(t, gain, cos, sin, ones, swap):
    ms = jnp.dot((t * t).astype(BF16), ones, preferred_element_type=F32) * (1.0 / MLA_QK)
    t = t * lax.rsqrt(ms + EPS) * gain
    return t * cos + jnp.dot(t.astype(BF16), swap, preferred_element_type=F32) * sin


def _mla_in_kernel(x_ref, mod_ref, g_ref, wdq_ref, wdkv_ref, wpe_ref, qan_ref, kvan_ref, wuq_ref, wuk_ref, wuv_ref,
                   qn_ref, kn_ref, ones_ref, swap_ref, cos_ref, sin_ref, q_ref, k_ref, v_ref):
    h = _modulated_norm(x_ref[...], g_ref[...], mod_ref[0], 0, 1).astype(BF16)
    dq = jnp.dot(h, wdq_ref[...], preferred_element_type=F32)
    dkv = jnp.dot(h, wdkv_ref[...], preferred_element_type=F32)
    kpe = jnp.dot(h, wpe_ref[...], preferred_element_type=F32)
    qa = dq * lax.rsqrt(jnp.mean(dq * dq, axis=-1, keepdims=True) + EPS) * qan_ref[...]
    kva = (dkv * lax.rsqrt(jnp.mean(dkv * dkv, axis=-1, keepdims=True) + EPS) * kvan_ref[...]).astype(BF16)
    q = jnp.dot(qa.astype(BF16), wuq_ref[...], preferred_element_type=F32)
    kn = jnp.dot(kva, wuk_ref[...], preferred_element_type=F32)
    v_ref[...] = jnp.dot(kva, wuv_ref[...], preferred_element_type=F32).astype(BF16)
    cos, sin, ones, swap = cos_ref[...], sin_ref[...], ones_ref[...], swap_ref[...]
    scale = MLA_QK ** -0.5 * math.log2(math.e)
    for hd in range(MLA_HEADS):
        hs = slice(hd * HEAD_PAD, (hd + 1) * HEAD_PAD)
        q_ref[:, hs] = (_head_norm_rope(q[:, hs], qn_ref[...], cos, sin, ones, swap) * scale).astype(BF16)
        k_ref[:, hs] = _head_norm_rope(kn[:, hs] + kpe, kn_ref[...], cos, sin, ones, swap).astype(BF16)


def mla_in(x_all, mod, p):
    rows = x_all.shape[0]
    tm = ROW_TILE
    w = p["w_dqkv"]
    wdq = w[:, :MLA_Q_LORA].astype(BF16)
    wdkv = w[:, MLA_Q_LORA:MLA_Q_LORA + MLA_KV_LORA].astype(BF16)
    wpe = jnp.pad(w[:, MLA_Q_LORA + MLA_KV_LORA:], ((0, 0), (MLA_NOPE, HEAD_PAD - MLA_QK))).astype(BF16)
    wuq = jnp.pad(p["w_uq"].reshape(MLA_Q_LORA, MLA_HEADS, MLA_QK), ((0, 0), (0, 0), (0, HEAD_PAD - MLA_QK)))
    wuq = wuq.reshape(MLA_Q_LORA, MLA_HEADS * HEAD_PAD).astype(BF16)
    wukv = p["w_ukv"].reshape(MLA_KV_LORA, MLA_HEADS, MLA_NOPE + MLA_V)
    wuk = jnp.pad(wukv[:, :, :MLA_NOPE], ((0, 0), (0, 0), (0, HEAD_PAD - MLA_NOPE)))
    wuk = wuk.reshape(MLA_KV_LORA, MLA_HEADS * HEAD_PAD).astype(BF16)
    wuv = wukv[:, :, MLA_NOPE:].reshape(MLA_KV_LORA, MLA_HEADS * MLA_V).astype(BF16)
    padg = lambda g: jnp.pad(g, (0, HEAD_PAD - MLA_QK))[None, :]
    cos, sin = _rope_tables(rows)
    row = lambda n: pl.BlockSpec((tm, n), lambda i: (i, 0))
    weights = [wdq, wdkv, wpe, p["q_a_norm"][None, :], p["kv_a_norm"][None, :], wuq, wuk, wuv,
               padg(p["q_norm"]), padg(p["k_norm"]),
               jnp.ones((HEAD_PAD, HEAD_PAD), BF16), jnp.asarray(_rope_swap_matrix(), BF16)]
    return pl.pallas_call(
        _mla_in_kernel,
        grid=(rows // tm,),
        in_specs=[row(D_MODEL),
                  pl.BlockSpec((1, 1, N_MOD * D_MODEL), lambda i: (_mod_row(i, tm), 0, 0)),
                  _full((1, D_MODEL))] + [_full(a.shape) for a in weights] + [row(HEAD_PAD)] * 2,
        out_specs=[row(MLA_HEADS * HEAD_PAD), row(MLA_HEADS * HEAD_PAD), row(MLA_HEADS * MLA_V)],
        out_shape=[jax.ShapeDtypeStruct((rows, MLA_HEADS * HEAD_PAD), BF16),
                   jax.ShapeDtypeStruct((rows, MLA_HEADS * HEAD_PAD), BF16),
                   jax.ShapeDtypeStruct((rows, MLA_HEADS * MLA_V), BF16)],
        compiler_params=_params("arbitrary"),
        name="mla_in",
    )(x_all, mod, p["norm1"][None, :], *weights, cos, sin)


ATTN_SAFE_SHIFT = 50.0


def _attn_kernel(q_ref, kc_ref, kl_ref, vc_ref, vl_ref, o_ref, knorm_ref, *, tq, tk):
    heads = (slice(0, HEAD_PAD), slice(HEAD_PAD, 2 * HEAD_PAD))
    qs = [q_ref[:, hs] for hs in heads]
    n_lat = SEQ // tk

    @pl.when(pl.program_id(2) == 0)
    def _key_norms():
        def sq_max(k2, hh):
            kf = k2[:, heads[hh]].astype(F32)
            return jnp.max(jnp.sum(kf * kf, axis=-1, keepdims=True), axis=0, keepdims=True)

        for hh in range(2):
            best = sq_max(kc_ref[...], hh)
            best = lax.fori_loop(
                0, n_lat, lambda j, b: jnp.maximum(b, sq_max(kl_ref[pl.ds(pl.multiple_of(j * tk, tk), tk), :], hh)), best)
            knorm_ref[hh] = jnp.broadcast_to(jnp.sqrt(best), (SUBLANES, LANES))

    def values(v, hh):
        lane = lax.broadcasted_iota(jnp.int32, v.shape, 1)
        own = (lane < MLA_V) if hh == 0 else (lane >= MLA_V)
        return jnp.where(own, v, jnp.ones_like(v))

    def scores(hh, k2):
        return lax.dot_general(qs[hh], k2[:, heads[hh]], (((1,), (1,)), ((), ())), preferred_element_type=F32)

    def finish(accs):
        lane = lax.broadcasted_iota(jnp.int32, (tq, 2 * MLA_V), 1)
        outs = [acc / pltpu.roll(acc, MLA_V, 1) for acc in accs]
        o_ref[...] = jnp.where(lane < MLA_V, outs[0], outs[1]).astype(o_ref.dtype)

    def lat_chunk(j):
        off = pl.multiple_of(j * tk, tk)
        return kl_ref[pl.ds(off, tk), :], vl_ref[pl.ds(off, tk), :]

    bounds = []
    for hh in range(2):
        qf = qs[hh].astype(F32)
        qn = jnp.sqrt(jnp.sum(qf * qf, axis=-1, keepdims=True))
        bounds.append(qn * knorm_ref[hh, 0:1, 0:1] * 1.001 + 1e-3)
    widest = jnp.max(jnp.maximum(bounds[0], bounds[1]))

    def fixed_shift():
        def step(k2, v, accs):
            return tuple(accs[hh] + jnp.dot(jnp.exp2(scores(hh, k2) - bounds[hh]).astype(BF16), values(v, hh),
                                            preferred_element_type=F32) for hh in range(2))

        zero = jnp.zeros((tq, 2 * MLA_V), F32)
        accs = step(kc_ref[...], vc_ref[...], (zero, zero))
        finish(lax.fori_loop(0, n_lat, lambda j, a: step(*lat_chunk(j), a), accs, unroll=4))

    def running_max():
        def step(k2, v, carry):
            new = []
            for hh in range(2):
                m, acc = carry[hh]
                s = scores(hh, k2)
                m_new = jnp.maximum(m, jnp.max(s, axis=-1, keepdims=True))
                p = jnp.exp2(s - m_new).astype(BF16)
                acc = jnp.exp2(m - m_new) * acc + jnp.dot(p, values(v, hh), preferred_element_type=F32)
                new.append((m_new, acc))
            return tuple(new)

        init = tuple((jnp.full((tq, 1), NEG_INF, F32), jnp.zeros((tq, 2 * MLA_V), F32)) for _ in range(2))
        carry = step(kc_ref[...], vc_ref[...], init)
        carry = lax.fori_loop(0, n_lat, lambda j, c: step(*lat_chunk(j), c), carry)
        finish([acc for _, acc in carry])

    lax.cond(widest <= ATTN_SAFE_SHIFT, fixed_shift, running_max)


def attention(q, k, v):
    tq, tk = ATTN_TQ, ATTN_TK
    nq = SEQ // tq
    ctx0 = LAT_ROWS // CTX_LEN
    return pl.pallas_call(
        functools.partial(_attn_kernel, tq=tq, tk=tk),
        grid=(BATCH, MLA_HEADS // 2, nq),
        in_specs=[pl.BlockSpec((tq, 2 * HEAD_PAD), lambda b, h, i: (b * nq + i, h)),
                  pl.BlockSpec((CTX_LEN, 2 * HEAD_PAD), lambda b, h, i: (ctx0 + b, h)),
                  pl.BlockSpec((SEQ, 2 * HEAD_PAD), lambda b, h, i: (b, h)),
                  pl.BlockSpec((CTX_LEN, 2 * MLA_V), lambda b, h, i: (ctx0 + b, h)),
                  pl.BlockSpec((SEQ, 2 * MLA_V), lambda b, h, i: (b, h))],
        out_specs=pl.BlockSpec((tq, 2 * MLA_V), lambda b, h, i: (b * nq + i, h)),
        out_shape=jax.ShapeDtypeStruct((LAT_ROWS, MLA_HEADS * MLA_V), BF16),
        scratch_shapes=[pltpu.VMEM((2, SUBLANES, LANES), F32)],
        compiler_params=_params("arbitrary", "arbitrary", "arbitrary"),
        name="attention",
    )(q, k, k, v, v)


def _attn_out_kernel(x_ref, mod_ref, o_ref_in, wo_ref, out_ref):
    gate = mod_ref[0][:, 2 * D_MODEL:3 * D_MODEL]
    out_ref[...] = x_ref[...] + gate * jnp.dot(o_ref_in[...], wo_ref[...], preferred_element_type=F32)


def attn_out(x_all, mod, o, w_o):
    tm = ROW_TILE
    row = lambda n: pl.BlockSpec((tm, n), lambda i: (i, 0))
    return pl.pallas_call(
        _attn_out_kernel,
        grid=(LAT_ROWS // tm,),
        in_specs=[row(D_MODEL), pl.BlockSpec((1, 1, N_MOD * D_MODEL), lambda i: (_mod_row(i, tm), 0, 0)),
                  row(MLA_HEADS * MLA_V), _full(w_o.shape)],
        out_specs=row(D_MODEL),
        out_shape=jax.ShapeDtypeStruct((LAT_ROWS, D_MODEL), F32),
        compiler_params=_params("arbitrary"),
        name="attn_out",
    )(x_all, mod, o, w_o.astype(BF16))


def kernel(x, c, ctx, c_ctx, l0_w_ada, l0_b_ada, l0_norm1, l0_w_in, l0_conv_w, l0_conv_b, l0_dt_bias, l0_a_log, l0_d_skip, l0_ssd_norm, l0_w_out, l0_norm2, l0_peer_wq, l0_peer_keys, l0_peer_u, l0_peer_v, l1_w_ada, l1_b_ada, l1_norm1, l1_w_dqkv, l1_q_a_norm, l1_kv_a_norm, l1_w_uq, l1_w_ukv, l1_q_norm, l1_k_norm, l1_w_o, l1_norm2, l1_peer_wq, l1_peer_keys, l1_peer_u, l1_peer_v):
    x_all = jnp.concatenate([x.reshape(LAT_ROWS, D_MODEL), ctx.reshape(BATCH * CTX_LEN, D_MODEL)], axis=0)

    mod0 = ada_table(c, c_ctx, l0_w_ada, l0_b_ada)
    f, z, xbc, dt = even_in(x_all, mod0, l0_norm1, l0_w_in)
    u = conv_silu(xbc, l0_conv_w, l0_conv_b)
    yf, yb = ssd_scan(u, dt, l0_dt_bias, l0_a_log)
    four = fourier_mix(f)
    x_all = even_out(x_all, mod0, four, yf, yb, u, z, l0_d_skip, l0_ssd_norm, l0_w_out)
    x_all = peer_layer(x_all, mod0, l0_norm2, l0_peer_wq, l0_peer_keys, l0_peer_u, l0_peer_v)

    mod1 = ada_table(c, c_ctx, l1_w_ada, l1_b_ada)
    p1 = dict(norm1=l1_norm1, w_dqkv=l1_w_dqkv, q_a_norm=l1_q_a_norm, kv_a_norm=l1_kv_a_norm, w_uq=l1_w_uq,
              w_ukv=l1_w_ukv, q_norm=l1_q_norm, k_norm=l1_k_norm)
    q, k, v = mla_in(x_all, mod1, p1)
    o = attention(q, k, v)
    x_lat = attn_out(x_all, mod1, o, l1_w_o)
    x_lat = peer_layer(x_lat, mod1, l1_norm2, l1_peer_wq, l1_peer_keys, l1_peer_u, l1_peer_v)
    return x_lat.reshape(BATCH, SEQ, D_MODEL)
```

```python
import functools
import math

import numpy as np
import jax
import jax.numpy as jnp
from jax import lax
from jax.experimental import pallas as pl
from jax.experimental.pallas import tpu as pltpu

D_MODEL = 1024
BATCH = 2
SEQ = 8192
GRID_W = 64
CTX_LEN = 256
EPS = 1e-6
N_MOD = 6
LAT_ROWS = BATCH * SEQ
ALL_ROWS = LAT_ROWS + BATCH * CTX_LEN

FNET_GROUPS = 4
FNET_GROUP_DIM = 128
FNET_WIDTH = FNET_GROUPS * FNET_GROUP_DIM
FFT_N1 = 64
FFT_N2 = 128

SSD_HEADS = 16
SSD_HEAD_DIM = 64
SSD_INNER = SSD_HEADS * SSD_HEAD_DIM
SSD_GROUPS = 4
SSD_STATE = 128
SSD_CONV = 5
SSD_CHUNK = 128
SSD_CONV_DIM = SSD_INNER + 2 * SSD_GROUPS * SSD_STATE
SSD_GROUP_W = SSD_INNER // SSD_GROUPS

MLA_HEADS = 16
MLA_NOPE = 64
MLA_ROPE = 32
MLA_QK = MLA_NOPE + MLA_ROPE
MLA_V = 64
MLA_Q_LORA = 384
MLA_KV_LORA = 256
ROPE_THETA = 10000.0
HEAD_PAD = 128

PEER_HEADS = 8
PEER_KEYS = 128
PEER_EXPERTS = PEER_KEYS * PEER_KEYS
PEER_QDIM = 256
PEER_TOPK = 16

LANES = 128
SUBLANES = 8
VMEM_LIMIT = 56 * 1024 * 1024

ROW_TILE = 512
CONV_TILE = 256
PEER_TILE = 512
PEER_CHUNK = 512
PEER_CHUNKS_PER_STEP = 2
PEER_SUB = 256
PEER_SELECT_TOKENS = 128
ATTN_TQ = 1024
ATTN_TK = 512

F32 = jnp.float32
BF16 = jnp.bfloat16
HIGHEST = lax.Precision.HIGHEST
NEG_INF = float("-inf")


def _params(*sem):
    return pltpu.CompilerParams(dimension_semantics=sem, vmem_limit_bytes=VMEM_LIMIT)


def _mod_row(i, tile):
    return jnp.minimum((i * tile) // SEQ, BATCH)


def _full(shape):
    return pl.BlockSpec(shape, lambda *_: (0,) * len(shape))


def _silu(x):
    return x * (1.0 / (1.0 + jnp.exp(-x)))


def _modulated_norm(x, gain, mod, k_shift, k_scale):
    shift = mod[:, k_shift * D_MODEL:(k_shift + 1) * D_MODEL]
    scale = mod[:, k_scale * D_MODEL:(k_scale + 1) * D_MODEL]
    ms = jnp.mean(x * x, axis=-1, keepdims=True)
    return x * lax.rsqrt(ms + EPS) * gain * (1.0 + scale) + shift


def _bdot(a, b):
    return jnp.dot(a.astype(BF16), b.astype(BF16), preferred_element_type=F32)


def _hdot(a, b):
    return jnp.dot(a, b, precision=HIGHEST, preferred_element_type=F32)


def _bf16_terms(x, n):
    terms = []
    for _ in range(n):
        t = x.astype(BF16)
        terms.append(t)
        x = x - t.astype(F32)
    return terms


def _dot_exact_rhs(a, b01, terms=3):
    return sum(jnp.dot(t, b01, preferred_element_type=F32) for t in _bf16_terms(a, terms))


def _dot_exact_lhs(a01, b):
    return sum(jnp.dot(a01, t, preferred_element_type=F32) for t in _bf16_terms(b, 3))


def _dot3(a, b):
    a_hi, a_lo = _bf16_terms(a, 2)
    b_hi, b_lo = _bf16_terms(b, 2)
    dot = lambda x, y: jnp.dot(x, y, preferred_element_type=F32)
    return dot(a_hi, b_hi) + (dot(a_hi, b_lo) + dot(a_lo, b_hi))


def _ada_kernel(c_ref, w_ref, b_ref, o_ref):
    o_ref[...] = _hdot(_silu(c_ref[...]), w_ref[...]) + b_ref[...]


def ada_table(c, c_ctx, w_ada, b_ada):
    cond = jnp.concatenate([c, c_ctx[None, :], jnp.zeros((SUBLANES - BATCH - 1, D_MODEL), F32)], axis=0)
    tn = 512
    out = pl.pallas_call(
        _ada_kernel,
        grid=(N_MOD * D_MODEL // tn,),
        in_specs=[_full((SUBLANES, D_MODEL)),
                  pl.BlockSpec((D_MODEL, tn), lambda j: (0, j)),
                  pl.BlockSpec((1, tn), lambda j: (0, j))],
        out_specs=pl.BlockSpec((SUBLANES, tn), lambda j: (0, j)),
        out_shape=jax.ShapeDtypeStruct((SUBLANES, N_MOD * D_MODEL), F32),
        compiler_params=_params("arbitrary"),
        name="ada_table",
    )(cond, w_ada, b_ada[None, :])
    return out[:BATCH + 1].reshape(BATCH + 1, 1, N_MOD * D_MODEL)


def _even_in_kernel(x_ref, mod_ref, g_ref, wf_ref, wz_ref, wx_ref, wd_ref, f_ref, z_ref, xbc_ref, dt_ref):
    h = _modulated_norm(x_ref[...], g_ref[...], mod_ref[0], 0, 1).astype(BF16)
    f_ref[...] = jnp.dot(h, wf_ref[...], preferred_element_type=F32)
    z_ref[...] = jnp.dot(h, wz_ref[...], preferred_element_type=F32)
    xbc_ref[...] = jnp.dot(h, wx_ref[...], preferred_element_type=F32)
    dt_ref[...] = jnp.dot(h, wd_ref[...], preferred_element_type=F32)


def even_in(x_all, mod, norm1, w_in):
    rows = x_all.shape[0]
    o1, o2, o3 = FNET_WIDTH, FNET_WIDTH + SSD_INNER, FNET_WIDTH + SSD_INNER + SSD_CONV_DIM
    wf = w_in[:, :o1].astype(BF16)
    wz = w_in[:, o1:o2].astype(BF16)
    wx = w_in[:, o2:o3].astype(BF16)
    wd = jnp.pad(w_in[:, o3:], ((0, 0), (0, LANES - 2 * SSD_HEADS))).astype(BF16)
    tm = ROW_TILE
    row = lambda n: pl.BlockSpec((tm, n), lambda i: (i, 0))
    return pl.pallas_call(
        _even_in_kernel,
        grid=(rows // tm,),
        in_specs=[row(D_MODEL),
                  pl.BlockSpec((1, 1, N_MOD * D_MODEL), lambda i: (_mod_row(i, tm), 0, 0)),
                  _full((1, D_MODEL)),
                  _full(wf.shape), _full(wz.shape), _full(wx.shape), _full(wd.shape)],
        out_specs=[row(FNET_WIDTH), row(SSD_INNER), row(SSD_CONV_DIM), row(LANES)],
        out_shape=[jax.ShapeDtypeStruct((rows, FNET_WIDTH), F32),
                   jax.ShapeDtypeStruct((rows, SSD_INNER), F32),
                   jax.ShapeDtypeStruct((rows, SSD_CONV_DIM), F32),
                   jax.ShapeDtypeStruct((rows, LANES), F32)],
        compiler_params=_params("arbitrary"),
        name="even_in",
    )(x_all, mod, norm1[None, :], wf, wz, wx, wd)


def _conv_kernel(x_ref, prev_ref, next_ref, w_ref, b_ref, o_ref, ext_ref, *, tm):
    row0 = pl.program_id(0) * tm
    in_lat = row0 < LAT_ROWS
    first = jnp.where(in_lat, row0 % SEQ == 0, (row0 - LAT_ROWS) % CTX_LEN == 0)
    last = jnp.where(in_lat, (row0 + tm) % SEQ == 0, (row0 + tm - LAT_ROWS) % CTX_LEN == 0)
    ext_ref[0:SUBLANES, :] = prev_ref[...] * jnp.where(first, 0.0, 1.0)
    ext_ref[SUBLANES:SUBLANES + tm, :] = x_ref[...]
    ext_ref[SUBLANES + tm:2 * SUBLANES + tm, :] = next_ref[...] * jnp.where(last, 0.0, 1.0)
    pad = SSD_CONV // 2
    acc = b_ref[...] + w_ref[0:1, :] * ext_ref[pl.ds(SUBLANES - pad, tm), :]
    for k in range(1, SSD_CONV):
        acc = acc + w_ref[k:k + 1, :] * ext_ref[pl.ds(SUBLANES - pad + k, tm), :]
    o_ref[...] = _silu(acc)


def conv_silu(xbc, conv_w, conv_b):
    rows, ch = xbc.shape
    tm, tc = CONV_TILE, ch
    assert CTX_LEN % tm == 0 and SEQ % tm == 0
    per = tm // SUBLANES
    n_small = rows // SUBLANES
    w = jnp.pad(conv_w, ((0, SUBLANES - SSD_CONV), (0, 0)))
    return pl.pallas_call(
        functools.partial(_conv_kernel, tm=tm),
        grid=(rows // tm, ch // tc),
        in_specs=[pl.BlockSpec((tm, tc), lambda i, j: (i, j)),
                  pl.BlockSpec((SUBLANES, tc), lambda i, j: (jnp.maximum(i * per - 1, 0), j)),
                  pl.BlockSpec((SUBLANES, tc), lambda i, j: (jnp.minimum((i + 1) * per, n_small - 1), j)),
                  pl.BlockSpec((SUBLANES, tc), lambda i, j: (0, j)),
                  pl.BlockSpec((1, tc), lambda i, j: (0, j))],
        out_specs=pl.BlockSpec((tm, tc), lambda i, j: (i, j)),
        out_shape=jax.ShapeDtypeStruct((rows, ch), F32),
        scratch_shapes=[pltpu.VMEM((tm + 2 * SUBLANES, tc), F32)],
        compiler_params=_params("arbitrary", "arbitrary"),
        name="conv_silu",
    )(xbc, xbc, xbc, w, conv_b[None, :])


def _softplus(x):
    return jnp.maximum(x, 0.0) + jnp.log1p(jnp.exp(-jnp.abs(x)))


def _ssd_direction(u_ref, dt_ref, dtb_ref, alog_ref, exp_ref, state_ref, y_ref, direction):
    cl = SSD_CHUNK
    xs = u_ref[:, :SSD_INNER]
    dtv = _softplus(dt_ref[...] + dtb_ref[...])
    dta = dtv * (-jnp.exp(alog_ref[...]))
    r = lax.broadcasted_iota(jnp.int32, (cl, cl), 0)
    c = lax.broadcasted_iota(jnp.int32, (cl, cl), 1)
    tri = (r >= c) if direction == 0 else (r <= c)
    cs = _dot_exact_lhs(tri.astype(F32).astype(BF16), dta)
    cs_t = cs.T
    edge = cs[cl - 1:cl, :] if direction == 0 else cs[0:1, :]
    to_end = jnp.exp(edge - cs)
    from_start = jnp.exp(cs)
    spread = _dot_exact_rhs(jnp.concatenate([dtv, to_end, from_start], axis=0), exp_ref[direction], terms=2)
    dt_x, te_x, fs_x = spread[:cl], spread[cl:2 * cl], spread[2 * cl:]
    chunk_decay = fs_x[cl - 1:cl, :] if direction == 0 else fs_x[0:1, :]
    xd = xs * dt_x
    xd_b = xd.astype(BF16)
    xte_b = (xd * te_x).astype(BF16)
    lane = lax.broadcasted_iota(jnp.int32, (cl, LANES), 1)
    low = lane < SSD_HEAD_DIM
    for g in range(SSD_GROUPS):
        bg = u_ref[:, SSD_INNER + g * SSD_STATE:SSD_INNER + (g + 1) * SSD_STATE]
        cg = u_ref[:, SSD_INNER + (SSD_GROUPS + g) * SSD_STATE:SSD_INNER + (SSD_GROUPS + g + 1) * SSD_STATE]
        bg_b, cg_b = bg.astype(BF16), cg.astype(BF16)
        cb = lax.dot_general(cg_b, bg_b, (((1,), (1,)), ((), ())), preferred_element_type=F32)
        gs = slice(g * SSD_GROUP_W, (g + 1) * SSD_GROUP_W)
        state = state_ref[g]
        y_off = jnp.dot(cg_b, state.astype(BF16), preferred_element_type=F32) * fs_x[:, gs]
        new_state = jnp.dot(bg.T.astype(BF16), xte_b[:, gs], preferred_element_type=F32)
        state_ref[g] = state * chunk_decay[:, gs] + new_state
        for pair in range(2):
            blk = xd_b[:, g * SSD_GROUP_W + pair * LANES:g * SSD_GROUP_W + (pair + 1) * LANES]
            y_pair = None
            for sub in range(2):
                col = direction * SSD_HEADS + g * 4 + pair * 2 + sub
                seg = cs[:, col:col + 1] - cs_t[col:col + 1, :]
                decay = jnp.exp(jnp.where(tri, seg, NEG_INF))
                m = (cb * decay).astype(BF16)
                half = jnp.where(low if sub == 0 else jnp.logical_not(low), blk, jnp.zeros_like(blk))
                part = jnp.dot(m, half, preferred_element_type=F32)
                y_pair = part if y_pair is None else y_pair + part
            lo = g * SSD_GROUP_W + pair * LANES
            y_ref[:, lo:lo + LANES] = y_pair + y_off[:, pair * LANES:(pair + 1) * LANES]


def _ssd_kernel(uf_ref, ub_ref, dtf_ref, dtb_in_ref, bias_ref, alog_ref, exp_ref, yf_ref, yb_ref, sf_ref, sb_ref):
    @pl.when(pl.program_id(1) == 0)
    def _():
        sf_ref[...] = jnp.zeros_like(sf_ref)
        sb_ref[...] = jnp.zeros_like(sb_ref)

    _ssd_direction(uf_ref, dtf_ref, bias_ref, alog_ref, exp_ref, sf_ref, yf_ref, 0)
    _ssd_direction(ub_ref, dtb_in_ref, bias_ref, alog_ref, exp_ref, sb_ref, yb_ref, 1)


def ssd_scan(u, dt, dt_bias, a_log):
    rows = u.shape[0]
    cl = SSD_CHUNK
    lat_chunks, ctx_chunks = SEQ // cl, CTX_LEN // cl
    steps = ctx_chunks + lat_chunks
    ctx0 = LAT_ROWS // cl

    def fwd_chunk(b, t):
        return jnp.where(t < ctx_chunks, ctx0 + b * ctx_chunks + t, b * lat_chunks + t - ctx_chunks)

    def bwd_chunk(b, t):
        return jnp.where(t < ctx_chunks, ctx0 + b * ctx_chunks + (ctx_chunks - 1 - t),
                         b * lat_chunks + (lat_chunks - 1) - (t - ctx_chunks))

    pad = LANES - 2 * SSD_HEADS
    bias = jnp.pad(dt_bias.reshape(1, -1), ((0, 0), (0, pad)))
    alog = jnp.pad(a_log.reshape(1, -1), ((0, 0), (0, pad)))
    expand = np.zeros((2, LANES, SSD_INNER), np.float32)
    for d in range(2):
        for h in range(SSD_HEADS):
            expand[d, d * SSD_HEADS + h, h * SSD_HEAD_DIM:(h + 1) * SSD_HEAD_DIM] = 1.0
    spec = lambda n, fn: pl.BlockSpec((cl, n), lambda b, t: (fn(b, t), 0))
    return pl.pallas_call(
        _ssd_kernel,
        grid=(BATCH, steps),
        in_specs=[spec(SSD_CONV_DIM, fwd_chunk), spec(SSD_CONV_DIM, bwd_chunk),
                  spec(LANES, fwd_chunk), spec(LANES, bwd_chunk),
                  _full((1, LANES)), _full((1, LANES)), _full((2, LANES, SSD_INNER))],
        out_specs=[spec(SSD_INNER, fwd_chunk), spec(SSD_INNER, bwd_chunk)],
        out_shape=[jax.ShapeDtypeStruct((rows, SSD_INNER), F32)] * 2,
        scratch_shapes=[pltpu.VMEM((SSD_GROUPS, SSD_STATE, SSD_GROUP_W), F32)] * 2,
        compiler_params=_params("arbitrary", "arbitrary"),
        name="ssd_scan",
    )(u, u, dt, dt, bias, alog, jnp.asarray(expand, BF16))


def _dft_cos_sin(n):
    k = np.arange(n)
    ang = 2.0 * np.pi * ((k[:, None] * k[None, :]) % n) / n
    return np.cos(ang), np.sin(ang)


def _fft1_kernel(x_ref, m_ref, o_ref):
    o_ref[0] = _dot3(m_ref[...], x_ref[...])


def _fft2_kernel(ar_ref, ai_ref, twr_ref, twi_ref, m2_ref, mc_ref, o_ref):
    ar, ai = ar_ref[0, 0], ai_ref[0, 0]
    twr = jnp.concatenate([twr_ref[0]] * FNET_GROUPS, axis=1)
    twi = jnp.concatenate([twi_ref[0]] * FNET_GROUPS, axis=1)
    p = ar * twr - ai * twi
    q = ar * twi + ai * twr
    uv = _dot3(m2_ref[...], jnp.concatenate([p, q], axis=0))
    n2 = FFT_N2
    for g in range(FNET_GROUPS):
        gs = slice(g * FNET_GROUP_DIM, (g + 1) * FNET_GROUP_DIM)
        o_ref[0, :, gs] = _dot3(jnp.concatenate([uv[:n2, gs], uv[n2:, gs]], axis=1), mc_ref[...])


def _fft_ctx_kernel(x_ref, mc_ref, mp_ref, o_ref):
    x = x_ref[...]
    for g in range(FNET_GROUPS):
        gs = slice(g * FNET_GROUP_DIM, (g + 1) * FNET_GROUP_DIM)
        cs = _dot3(x[:, gs], mc_ref[...])
        stacked = jnp.concatenate([cs[:, :FNET_GROUP_DIM], cs[:, FNET_GROUP_DIM:]], axis=0)
        o_ref[:, gs] = _dot3(mp_ref[...], stacked)


def fourier_mix(f_all):
    rows = f_all.shape[0]
    n1, n2, gd = FFT_N1, FFT_N2, FNET_GROUP_DIM
    row_w = n2 * FNET_WIDTH
    c1, s1 = _dft_cos_sin(n1)
    m1 = jnp.asarray(np.concatenate([c1, -s1], axis=0), F32)
    tn = 8192
    stage1 = pl.pallas_call(
        _fft1_kernel,
        grid=(BATCH, row_w // tn),
        in_specs=[pl.BlockSpec((n1, tn), lambda b, j: (b, j)), _full((2 * n1, n1))],
        out_specs=pl.BlockSpec((1, 2 * n1, tn), lambda b, j: (b, 0, j)),
        out_shape=jax.ShapeDtypeStruct((BATCH, 2 * n1, row_w), F32),
        compiler_params=_params("arbitrary", "arbitrary"),
        name="fft_stage1",
    )(f_all.reshape(rows // n2, row_w), m1)

    k1 = np.arange(n1)[:, None]
    l2 = np.arange(n2)[None, :]
    tw = 2.0 * np.pi * (k1 * l2) / SEQ
    twr = jnp.asarray(np.repeat(np.cos(tw)[:, :, None], gd, axis=2), F32)
    twi = jnp.asarray(np.repeat(-np.sin(tw)[:, :, None], gd, axis=2), F32)
    c2, s2 = _dft_cos_sin(n2)
    m2 = jnp.asarray(np.block([[c2, s2], [-s2, c2]]), F32)
    cc, sc = _dft_cos_sin(gd)
    mc = jnp.asarray(np.concatenate([cc, sc], axis=0) / math.sqrt(SEQ * gd), F32)
    a4 = stage1.reshape(BATCH, 2 * n1, n2, FNET_WIDTH)
    lat = pl.pallas_call(
        _fft2_kernel,
        grid=(BATCH, n1),
        in_specs=[pl.BlockSpec((1, 1, n2, FNET_WIDTH), lambda b, k: (b, k, 0, 0)),
                  pl.BlockSpec((1, 1, n2, FNET_WIDTH), lambda b, k: (b, n1 + k, 0, 0)),
                  pl.BlockSpec((1, n2, gd), lambda b, k: (k, 0, 0)),
                  pl.BlockSpec((1, n2, gd), lambda b, k: (k, 0, 0)),
                  _full((2 * n2, 2 * n2)), _full((2 * gd, gd))],
        out_specs=pl.BlockSpec((1, n2, FNET_WIDTH), lambda b, k: (b, 0, k)),
        out_shape=jax.ShapeDtypeStruct((BATCH, n2, n1 * FNET_WIDTH), F32),
        compiler_params=_params("arbitrary", "arbitrary"),
        name="fft_stage2",
    )(a4, a4, twr, twi, m2, mc)

    cp, sp = _dft_cos_sin(CTX_LEN)
    mp = jnp.asarray(np.concatenate([cp, -sp], axis=1) / math.sqrt(CTX_LEN * gd), F32)
    mcc = jnp.asarray(np.concatenate([cc, sc], axis=1), F32)
    ctx0 = LAT_ROWS // CTX_LEN
    ctx = pl.pallas_call(
        _fft_ctx_kernel,
        grid=(BATCH,),
        in_specs=[pl.BlockSpec((CTX_LEN, FNET_WIDTH), lambda b: (ctx0 + b, 0)),
                  _full((gd, 2 * gd)), _full((CTX_LEN, 2 * CTX_LEN))],
        out_specs=pl.BlockSpec((CTX_LEN, FNET_WIDTH), lambda b: (b, 0)),
        out_shape=jax.ShapeDtypeStruct((BATCH * CTX_LEN, FNET_WIDTH), F32),
        compiler_params=_params("arbitrary"),
        name="fft_ctx",
    )(f_all, mcc, mp)
    return jnp.concatenate([lat.reshape(LAT_ROWS, FNET_WIDTH), ctx], axis=0)


def _even_out_kernel(x_ref, mod_ref, four_ref, yf_ref, yb_ref, xs_ref, z_ref, dsk_ref, gn_ref, wof_ref, wos_ref,
                     o_ref):
    y = yf_ref[...] + yb_ref[...] + dsk_ref[...] * xs_ref[...]
    gated = y * _silu(z_ref[...])
    out = jnp.dot(four_ref[...].astype(BF16), wof_ref[...], preferred_element_type=F32)
    for g in range(SSD_GROUPS):
        gs = slice(g * SSD_GROUP_W, (g + 1) * SSD_GROUP_W)
        v = gated[:, gs]
        normed = v * lax.rsqrt(jnp.mean(v * v, axis=-1, keepdims=True) + EPS) * gn_ref[:, gs]
        out = out + jnp.dot(normed.astype(BF16), wos_ref[gs, :], preferred_element_type=F32)
    gate = mod_ref[0][:, 2 * D_MODEL:3 * D_MODEL]
    o_ref[...] = x_ref[...] + gate * out


def even_out(x_all, mod, four, yf, yb, u, z, d_skip, ssd_norm, w_out):
    rows = x_all.shape[0]
    tm = ROW_TILE
    dsk = jnp.repeat(d_skip[0] + d_skip[1], SSD_HEAD_DIM)[None, :]
    wof = w_out[:FNET_WIDTH].astype(BF16)
    wos = w_out[FNET_WIDTH:].astype(BF16)
    row = lambda n: pl.BlockSpec((tm, n), lambda i: (i, 0))
    return pl.pallas_call(
        _even_out_kernel,
        grid=(rows // tm,),
        in_specs=[row(D_MODEL),
                  pl.BlockSpec((1, 1, N_MOD * D_MODEL), lambda i: (_mod_row(i, tm), 0, 0)),
                  row(FNET_WIDTH), row(SSD_INNER), row(SSD_INNER), row(SSD_INNER), row(SSD_INNER),
                  _full((1, SSD_INNER)), _full((1, SSD_INNER)), _full(wof.shape), _full(wos.shape)],
        out_specs=row(D_MODEL),
        out_shape=jax.ShapeDtypeStruct((rows, D_MODEL), F32),
        compiler_params=_params("arbitrary"),
        name="even_out",
    )(x_all, mod, four, yf, yb, u, z, dsk, ssd_norm[None, :], wof, wos)


BF16_ROWS = 2 * SUBLANES
NOT_TOP = 64.0


def _erf_gelu(z):
    return 0.5 * z * (1.0 + lax.erf(z * (1.0 / math.sqrt(2.0))))


def _top_ranks(score, k, want_rank=True):
    work, vals = score, []
    rank = jnp.full(score.shape, NOT_TOP, F32) if want_rank else None
    for i in range(k):
        m = jnp.max(work, axis=0, keepdims=True)
        hit = work == m
        vals.append(m)
        if want_rank:
            rank = jnp.where(hit, float(i), rank)
        work = jnp.where(hit, NEG_INF, work)
    return jnp.concatenate(vals, axis=0), rank


def _pair_candidates(v1, v2, k):
    assert k == 2 * SUBLANES, "the row grouping below is laid out for k = 16"
    t = v1.shape[1]
    row8 = lax.broadcasted_iota(jnp.int32, (SUBLANES, t), 0)
    rowf = row8.astype(F32)
    sums = [v1[0:1] + v2[0:8], v1[0:1] + v2[8:16], v1[1:2] + v2[0:8], v1[8:16] + v2[0:1]]
    order = [rowf, rowf + 8.0, rowf + float(k), (rowf + 8.0) * float(k)]
    for i in range(2, 8):
        sums.append(jnp.where(row8 < k // (i + 1), v1[i:i + 1] + v2[0:8], NEG_INF))
        order.append(rowf + float(i * k))
    return jnp.concatenate(sums, axis=0), jnp.concatenate(order, axis=0)


def _select_by_value(s1, s2, k):
    v1, _ = _top_ranks(s1, k, want_rank=False)
    v2, r2 = _top_ranks(s2, k)
    cand, _ = _pair_candidates(v1, v2, k)
    tau = _top_ranks(cand, k, want_rank=False)[0][k - 1:k, :]
    chosen = cand >= tau
    zsum = jnp.sum(jnp.where(chosen, jnp.exp(cand - (v1[0:1] + v2[0:1])), 0.0), axis=0, keepdims=True)
    in_top = s1 >= v1[k - 1:k]
    a1 = jnp.where(in_top, s1, NEG_INF)
    lim = jnp.zeros_like(s1)
    for j in range(k // 2):
        lim = lim + jnp.where(a1 + v2[j:j + 1] >= tau, 1.0, 0.0)
    best = jnp.zeros_like(tau)
    for j in range(k // 2, k):
        best = best + jnp.where(v1[0:1] + v2[j:j + 1] >= tau, 1.0, 0.0)
    lim = lim + jnp.where(s1 == v1[0:1], best, 0.0)
    count = lambda mask: jnp.sum(jnp.where(mask, 1.0, 0.0), axis=0, keepdims=True)
    most = jnp.maximum(jnp.maximum(count(in_top), count(r2 < k)), count(chosen))
    return lim, jnp.exp(s1 - v1[0:1]) / zsum, r2, jnp.exp(s2 - v2[0:1]), most


def _top_ranks_ordered(score, order, k):
    work, vals = score, []
    rank = jnp.full(score.shape, NOT_TOP, F32)
    for i in range(k):
        m = jnp.max(work, axis=0, keepdims=True)
        first = jnp.min(jnp.where(work == m, order, float(2 ** 20)), axis=0, keepdims=True)
        taken = order == first
        vals.append(m)
        rank = jnp.where(taken, float(i), rank)
        work = jnp.where(taken, NEG_INF, work)
    return jnp.concatenate(vals, axis=0), rank


def _select_by_order(s1, s2, k):
    key_index = lax.broadcasted_iota(jnp.int32, s1.shape, 0).astype(F32)
    v1, r1 = _top_ranks_ordered(s1, key_index, k)
    v2, r2 = _top_ranks_ordered(s2, key_index, k)
    cand, position = _pair_candidates(v1, v2, k)
    chosen = _top_ranks_ordered(cand, position, k)[1] < k
    zsum = jnp.sum(jnp.where(chosen, jnp.exp(cand - (v1[0:1] + v2[0:1])), 0.0), axis=0, keepdims=True)
    picks = jnp.where(chosen, 1.0, 0.0)
    group = lambda g: jnp.sum(picks[g * SUBLANES:(g + 1) * SUBLANES], axis=0, keepdims=True)
    per_rank = [group(0) + group(1), group(2)] + [group(g) for g in range(4, 10)]
    per_rank = jnp.concatenate(per_rank + [picks[3 * SUBLANES:4 * SUBLANES]], axis=0)
    lim = jnp.zeros_like(s1)
    for i in range(k):
        lim = lim + jnp.where(r1 == float(i), per_rank[i:i + 1], 0.0)
    return lim, jnp.exp(s1 - v1[0:1]) / zsum, r2, jnp.exp(s2 - v2[0:1])


def _peer_kernel(x_ref, mod_ref, g_ref, wq_ref, keys_ref, u0_ref, *rest, tm, chunk, per_step):
    u_refs = rest[:per_step]
    (vt_ref, o_ref, ht_ref, q_ref, lim_ref, w1_ref, rank2_ref, w2_ref, acc_ref, za_ref, zb_ref) = rest[per_step:]
    c = pl.program_id(1)
    k = PEER_TOPK
    packed = (PEER_KEYS // BF16_ROWS, BF16_ROWS, tm)
    tsel = PEER_SELECT_TOKENS

    @pl.when(c == 0)
    def _prologue():
        h2 = _modulated_norm(x_ref[...], g_ref[...], mod_ref[0], 3, 4)
        ht = h2.T.astype(BF16)
        ht_ref[...] = ht
        q = jnp.dot(wq_ref[...], ht, preferred_element_type=F32)
        q_ref[...] = q.reshape(2 * PEER_HEADS, PEER_QDIM // 2, tm)
        acc_ref[...] = jnp.zeros_like(acc_ref)
        za_ref[...] = jnp.dot(u0_ref[...], ht, preferred_element_type=F32)
        def head_body(h, carry):
            def scores(part):
                ts = slice(part * tsel, (part + 1) * tsel)
                s1 = jnp.dot(keys_ref[2 * h], q_ref[2 * h, :, ts].astype(BF16), preferred_element_type=F32)
                s2 = jnp.dot(keys_ref[2 * h + 1], q_ref[2 * h + 1, :, ts].astype(BF16), preferred_element_type=F32)
                return ts, s1, s2

            def store(ts, lim, w1, r2, w2):
                small = (PEER_KEYS // BF16_ROWS, BF16_ROWS, tsel)
                lim_ref[h, :, ts] = lim
                w1_ref[h, :, ts] = w1
                rank2_ref[h, :, :, ts] = r2.reshape(small).astype(BF16)
                w2_ref[h, :, :, ts] = w2.reshape(small).astype(BF16)

            most = None
            for part in range(tm // tsel):
                ts, s1, s2 = scores(part)
                lim, w1, r2, w2, count = _select_by_value(s1, s2, k)
                store(ts, lim, w1, r2, w2)
                most = count if most is None else jnp.maximum(most, count)

            @pl.when(jnp.max(most) > k)
            def _ties():
                for part in range(tm // tsel):
                    ts, s1, s2 = scores(part)
                    store(ts, *_select_by_order(s1, s2, k))

            return carry

        lax.fori_loop(0, PEER_HEADS, head_body, 0)

    per_key = PEER_KEYS // BF16_ROWS
    keys_per_sub = PEER_SUB // PEER_KEYS
    step_keys = per_step * chunk // PEER_KEYS
    key0 = pl.multiple_of(c * step_keys, step_keys)
    step_lim = [lim_ref[h, pl.ds(key0, step_keys), :] for h in range(PEER_HEADS)]
    step_w1 = [w1_ref[h, pl.ds(key0, step_keys), :] for h in range(PEER_HEADS)]

    def gated_values(z_ref, which):
        total = None
        for j in range(chunk // PEER_SUB):
            rows = slice(j * PEER_SUB, (j + 1) * PEER_SUB)
            act = _erf_gelu(z_ref[rows, :]).reshape(PEER_SUB // BF16_ROWS, BF16_ROWS, tm).astype(BF16)
            parts = []
            for jj in range(keys_per_sub):
                e1 = which * (chunk // PEER_KEYS) + j * keys_per_sub + jj
                gate = jnp.zeros(packed, BF16)
                for h in range(PEER_HEADS):
                    lim = jnp.broadcast_to(step_lim[h][e1:e1 + 1, :], (BF16_ROWS, tm)).astype(BF16)
                    w1 = jnp.broadcast_to(step_w1[h][e1:e1 + 1, :], (BF16_ROWS, tm)).astype(BF16)
                    picked = jnp.where(rank2_ref[h] < lim[None], w2_ref[h], jnp.zeros(packed, BF16))
                    gate = gate + picked * w1[None]
                parts.append(act[jj * per_key:(jj + 1) * per_key] * gate)
            a = jnp.concatenate(parts, axis=0).reshape(PEER_SUB, tm)
            cols = slice(which * chunk + j * PEER_SUB, which * chunk + (j + 1) * PEER_SUB)
            d = jnp.dot(vt_ref[:, cols], a, preferred_element_type=F32)
            total = d if total is None else total + d
        return total

    bufs = (za_ref, zb_ref)
    total = None
    for i in range(per_step):
        bufs[(i + 1) % 2][...] = jnp.dot(u_refs[i][...], ht_ref[...], preferred_element_type=F32)
        d = gated_values(bufs[i % 2], i)
        total = d if total is None else total + d
    acc_ref[...] += total

    @pl.when(c == pl.num_programs(1) - 1)
    def _epilogue():
        gate2 = mod_ref[0][:, 5 * D_MODEL:6 * D_MODEL]
        o_ref[...] = x_ref[...] + gate2 * acc_ref[...].T


def peer_layer(x_rows, mod, norm2, peer_wq, peer_keys, peer_u, peer_v):
    rows = x_rows.shape[0]
    tm, chunk = PEER_TILE, PEER_CHUNK
    wq_t = peer_wq.T.astype(BF16)
    keys = peer_keys.reshape(2 * PEER_HEADS, PEER_KEYS, PEER_QDIM // 2).astype(BF16)
    u = peer_u.astype(BF16)
    vt = peer_v.astype(BF16).T
    head_f32 = pltpu.VMEM((PEER_HEADS, PEER_KEYS, tm), F32)
    head_bf16 = pltpu.VMEM((PEER_HEADS, PEER_KEYS // BF16_ROWS, BF16_ROWS, tm), BF16)
    n_chunks = PEER_EXPERTS // chunk
    per_step = PEER_CHUNKS_PER_STEP
    ahead = lambda k: pl.BlockSpec((chunk, D_MODEL), lambda i, c: (jnp.minimum(per_step * c + k, n_chunks - 1), 0))
    return pl.pallas_call(
        functools.partial(_peer_kernel, tm=tm, chunk=chunk, per_step=per_step),
        grid=(rows // tm, n_chunks // per_step),
        in_specs=[pl.BlockSpec((tm, D_MODEL), lambda i, c: (i, 0)),
                  pl.BlockSpec((1, 1, N_MOD * D_MODEL), lambda i, c: (_mod_row(i, tm), 0, 0)),
                  _full((1, D_MODEL)), _full(wq_t.shape), _full(keys.shape),
                  pl.BlockSpec((chunk, D_MODEL), lambda i, c: (0, 0))]
                 + [ahead(k) for k in range(1, per_step + 1)]
                 + [pl.BlockSpec((D_MODEL, per_step * chunk), lambda i, c: (0, c))],
        out_specs=pl.BlockSpec((tm, D_MODEL), lambda i, c: (i, 0)),
        out_shape=jax.ShapeDtypeStruct((rows, D_MODEL), F32),
        scratch_shapes=[pltpu.VMEM((D_MODEL, tm), BF16),
                        pltpu.VMEM((2 * PEER_HEADS, PEER_QDIM // 2, tm), F32),
                        head_f32, head_f32, head_bf16, head_bf16,
                        pltpu.VMEM((D_MODEL, tm), F32),
                        pltpu.VMEM((chunk, tm), F32), pltpu.VMEM((chunk, tm), F32)],
        compiler_params=_params("arbitrary", "arbitrary"),
        name="peer",
    )(x_rows, mod, norm2[None, :], wq_t, keys, u, *([u] * per_step), vt)


def _rope_tables(rows):
    l = jnp.arange(SEQ)
    per_axis = MLA_ROPE // 2
    inv = ROPE_THETA ** (-jnp.arange(0, per_axis, 2, dtype=F32) / per_axis)
    ang = jnp.concatenate([(l // GRID_W)[:, None] * inv, (l % GRID_W)[:, None] * inv], axis=-1)
    cos, sin = jnp.cos(ang), jnp.sin(ang)
    one = jnp.ones((SEQ, MLA_NOPE), F32)
    tail = jnp.zeros((SEQ, HEAD_PAD - MLA_QK), F32)
    zero64 = jnp.zeros((SEQ, MLA_NOPE), F32)
    cos_t = jnp.concatenate([one, cos, cos, tail], axis=1)
    sin_t = jnp.concatenate([zero64, -sin, sin, tail], axis=1)
    n_ctx = rows - LAT_ROWS
    ident = jnp.concatenate([jnp.ones((n_ctx, MLA_QK), F32), jnp.zeros((n_ctx, HEAD_PAD - MLA_QK), F32)], axis=1)
    zeros = jnp.zeros((n_ctx, HEAD_PAD), F32)
    tile = lambda t, c: jnp.concatenate([t] * BATCH + [c], axis=0)
    return tile(cos_t, ident), tile(sin_t, zeros)


def _rope_swap_matrix():
    per_axis = MLA_ROPE // 2
    swap = np.zeros((HEAD_PAD, HEAD_PAD), np.float32)
    for j in range(per_axis):
        swap[MLA_NOPE + per_axis + j, MLA_NOPE + j] = 1.0
        swap[MLA_NOPE + j, MLA_NOPE + per_axis + j] = 1.0
    return swap


def _head_norm_rope(t, gain, cos, sin, ones, swap):
    ms = jnp.dot((t * t).astype(BF16), ones, preferred_element_type=F32) * (1.0 / MLA_QK)
    t = t * lax.rsqrt(ms + EPS) * gain
    return t * cos + jnp.dot(t.astype(BF16), swap, preferred_element_type=F32) * sin


def _mla_in_kernel(x_ref, mod_ref, g_ref, wdq_ref, wdkv_ref, wpe_ref, qan_ref, kvan_ref, wuq_ref, wuk_ref, wuv_ref,
                   qn_ref, kn_ref, ones_ref, swap_ref, cos_ref, sin_ref, q_ref, k_ref, v_ref):
    h = _modulated_norm(x_ref[...], g_ref[...], mod_ref[0], 0, 1).astype(BF16)
    dq = jnp.dot(h, wdq_ref[...], preferred_element_type=F32)
    dkv = jnp.dot(h, wdkv_ref[...], preferred_element_type=F32)
    kpe = jnp.dot(h, wpe_ref[...], preferred_element_type=F32)
    qa = dq * lax.rsqrt(jnp.mean(dq * dq, axis=-1, keepdims=True) + EPS) * qan_ref[...]
    kva = (dkv * lax.rsqrt(jnp.mean(dkv * dkv, axis=-1, keepdims=True) + EPS) * kvan_ref[...]).astype(BF16)
    q = jnp.dot(qa.astype(BF16), wuq_ref[...], preferred_element_type=F32)
    kn = jnp.dot(kva, wuk_ref[...], preferred_element_type=F32)
    v_ref[...] = jnp.dot(kva, wuv_ref[...], preferred_element_type=F32).astype(BF16)
    cos, sin, ones, swap = cos_ref[...], sin_ref[...], ones_ref[...], swap_ref[...]
    scale = MLA_QK ** -0.5 * math.log2(math.e)
    for hd in range(MLA_HEADS):
        hs = slice(hd * HEAD_PAD, (hd + 1) * HEAD_PAD)
        q_ref[:, hs] = (_head_norm_rope(q[:, hs], qn_ref[...], cos, sin, ones, swap) * scale).astype(BF16)
        k_ref[:, hs] = _head_norm_rope(kn[:, hs] + kpe, kn_ref[...], cos, sin, ones, swap).astype(BF16)


def mla_in(x_all, mod, p):
    rows = x_all.shape[0]
    tm = ROW_TILE
    w = p["w_dqkv"]
    wdq = w[:, :MLA_Q_LORA].astype(BF16)
    wdkv = w[:, MLA_Q_LORA:MLA_Q_LORA + MLA_KV_LORA].astype(BF16)
    wpe = jnp.pad(w[:, MLA_Q_LORA + MLA_KV_LORA:], ((0, 0), (MLA_NOPE, HEAD_PAD - MLA_QK))).astype(BF16)
    wuq = jnp.pad(p["w_uq"].reshape(MLA_Q_LORA, MLA_HEADS, MLA_QK), ((0, 0), (0, 0), (0, HEAD_PAD - MLA_QK)))
    wuq = wuq.reshape(MLA_Q_LORA, MLA_HEADS * HEAD_PAD).astype(BF16)
    wukv = p["w_ukv"].reshape(MLA_KV_LORA, MLA_HEADS, MLA_NOPE + MLA_V)
    wuk = jnp.pad(wukv[:, :, :MLA_NOPE], ((0, 0), (0, 0), (0, HEAD_PAD - MLA_NOPE)))
    wuk = wuk.reshape(MLA_KV_LORA, MLA_HEADS * HEAD_PAD).astype(BF16)
    wuv = wukv[:, :, MLA_NOPE:].reshape(MLA_KV_LORA, MLA_HEADS * MLA_V).astype(BF16)
    padg = lambda g: jnp.pad(g, (0, HEAD_PAD - MLA_QK))[None, :]
    cos, sin = _rope_tables(rows)
    row = lambda n: pl.BlockSpec((tm, n), lambda i: (i, 0))
    weights = [wdq, wdkv, wpe, p["q_a_norm"][None, :], p["kv_a_norm"][None, :], wuq, wuk, wuv,
               padg(p["q_norm"]), padg(p["k_norm"]),
               jnp.ones((HEAD_PAD, HEAD_PAD), BF16), jnp.asarray(_rope_swap_matrix(), BF16)]
    return pl.pallas_call(
        _mla_in_kernel,
        grid=(rows // tm,),
        in_specs=[row(D_MODEL),
                  pl.BlockSpec((1, 1, N_MOD * D_MODEL), lambda i: (_mod_row(i, tm), 0, 0)),
                  _full((1, D_MODEL))] + [_full(a.shape) for a in weights] + [row(HEAD_PAD)] * 2,
        out_specs=[row(MLA_HEADS * HEAD_PAD), row(MLA_HEADS * HEAD_PAD), row(MLA_HEADS * MLA_V)],
        out_shape=[jax.ShapeDtypeStruct((rows, MLA_HEADS * HEAD_PAD), BF16),
                   jax.ShapeDtypeStruct((rows, MLA_HEADS * HEAD_PAD), BF16),
                   jax.ShapeDtypeStruct((rows, MLA_HEADS * MLA_V), BF16)],
        compiler_params=_params("arbitrary"),
        name="mla_in",
    )(x_all, mod, p["norm1"][None, :], *weights, cos, sin)


ATTN_SAFE_SHIFT = 50.0


def _attn_kernel(q_ref, kc_ref, kl_ref, vc_ref, vl_ref, o_ref, knorm_ref, *, tq, tk):
    heads = (slice(0, HEAD_PAD), slice(HEAD_PAD, 2 * HEAD_PAD))
    qs = [q_ref[:, hs] for hs in heads]
    n_lat = SEQ // tk

    @pl.when(pl.program_id(2) == 0)
    def _key_norms():
        def sq_max(k2, hh):
            kf = k2[:, heads[hh]].astype(F32)
            return jnp.max(jnp.sum(kf * kf, axis=-1, keepdims=True), axis=0, keepdims=True)

        for hh in range(2):
            best = sq_max(kc_ref[...], hh)
            best = lax.fori_loop(
                0, n_lat, lambda j, b: jnp.maximum(b, sq_max(kl_ref[pl.ds(pl.multiple_of(j * tk, tk), tk), :], hh)), best)
            knorm_ref[hh] = jnp.broadcast_to(jnp.sqrt(best), (SUBLANES, LANES))

    def values(v, hh):
        lane = lax.broadcasted_iota(jnp.int32, v.shape, 1)
        own = (lane < MLA_V) if hh == 0 else (lane >= MLA_V)
        return jnp.where(own, v, jnp.ones_like(v))

    def scores(hh, k2):
        return lax.dot_general(qs[hh], k2[:, heads[hh]], (((1,), (1,)), ((), ())), preferred_element_type=F32)

    def finish(accs):
        lane = lax.broadcasted_iota(jnp.int32, (tq, 2 * MLA_V), 1)
        outs = [acc / pltpu.roll(acc, MLA_V, 1) for acc in accs]
        o_ref[...] = jnp.where(lane < MLA_V, outs[0], outs[1]).astype(o_ref.dtype)

    def lat_chunk(j):
        off = pl.multiple_of(j * tk, tk)
        return kl_ref[pl.ds(off, tk), :], vl_ref[pl.ds(off, tk), :]

    bounds = []
    for hh in range(2):
        qf = qs[hh].astype(F32)
        qn = jnp.sqrt(jnp.sum(qf * qf, axis=-1, keepdims=True))
        bounds.append(qn * knorm_ref[hh, 0:1, 0:1] * 1.001 + 1e-3)
    widest = jnp.max(jnp.maximum(bounds[0], bounds[1]))

    def fixed_shift():
        def step(k2, v, accs):
            return tuple(accs[hh] + jnp.dot(jnp.exp2(scores(hh, k2) - bounds[hh]).astype(BF16), values(v, hh),
                                            preferred_element_type=F32) for hh in range(2))

        zero = jnp.zeros((tq, 2 * MLA_V), F32)
        accs = step(kc_ref[...], vc_ref[...], (zero, zero))
        finish(lax.fori_loop(0, n_lat, lambda j, a: step(*lat_chunk(j), a), accs, unroll=4))

    def running_max():
        def step(k2, v, carry):
            new = []
            for hh in range(2):
                m, acc = carry[hh]
                s = scores(hh, k2)
                m_new = jnp.maximum(m, jnp.max(s, axis=-1, keepdims=True))
                p = jnp.exp2(s - m_new).astype(BF16)
                acc = jnp.exp2(m - m_new) * acc + jnp.dot(p, values(v, hh), preferred_element_type=F32)
                new.append((m_new, acc))
            return tuple(new)

        init = tuple((jnp.full((tq, 1), NEG_INF, F32), jnp.zeros((tq, 2 * MLA_V), F32)) for _ in range(2))
        carry = step(kc_ref[...], vc_ref[...], init)
        carry = lax.fori_loop(0, n_lat, lambda j, c: step(*lat_chunk(j), c), carry)
        finish([acc for _, acc in carry])

    lax.cond(widest <= ATTN_SAFE_SHIFT, fixed_shift, running_max)


def attention(q, k, v):
    tq, tk = ATTN_TQ, ATTN_TK
    nq = SEQ // tq
    ctx0 = LAT_ROWS // CTX_LEN
    return pl.pallas_call(
        functools.partial(_attn_kernel, tq=tq, tk=tk),
        grid=(BATCH, MLA_HEADS // 2, nq),
        in_specs=[pl.BlockSpec((tq, 2 * HEAD_PAD), lambda b, h, i: (b * nq + i, h)),
                  pl.BlockSpec((CTX_LEN, 2 * HEAD_PAD), lambda b, h, i: (ctx0 + b, h)),
                  pl.BlockSpec((SEQ, 2 * HEAD_PAD), lambda b, h, i: (b, h)),
                  pl.BlockSpec((CTX_LEN, 2 * MLA_V), lambda b, h, i: (ctx0 + b, h)),
                  pl.BlockSpec((SEQ, 2 * MLA_V), lambda b, h, i: (b, h))],
        out_specs=pl.BlockSpec((tq, 2 * MLA_V), lambda b, h, i: (b * nq + i, h)),
        out_shape=jax.ShapeDtypeStruct((LAT_ROWS, MLA_HEADS * MLA_V), BF16),
        scratch_shapes=[pltpu.VMEM((2, SUBLANES, LANES), F32)],
        compiler_params=_params("arbitrary", "arbitrary", "arbitrary"),
        name="attention",
    )(q, k, k, v, v)


def _attn_out_kernel(x_ref, mod_ref, o_ref_in, wo_ref, out_ref):
    gate = mod_ref[0][:, 2 * D_MODEL:3 * D_MODEL]
    out_ref[...] = x_ref[...] + gate * jnp.dot(o_ref_in[...], wo_ref[...], preferred_element_type=F32)


def attn_out(x_all, mod, o, w_o):
    tm = ROW_TILE
    row = lambda n: pl.BlockSpec((tm, n), lambda i: (i, 0))
    return pl.pallas_call(
        _attn_out_kernel,
        grid=(LAT_ROWS // tm,),
        in_specs=[row(D_MODEL), pl.BlockSpec((1, 1, N_MOD * D_MODEL), lambda i: (_mod_row(i, tm), 0, 0)),
                  row(MLA_HEADS * MLA_V), _full(w_o.shape)],
        out_specs=row(D_MODEL),
        out_shape=jax.ShapeDtypeStruct((LAT_ROWS, D_MODEL), F32),
        compiler_params=_params("arbitrary"),
        name="attn_out",
    )(x_all, mod, o, w_o.astype(BF16))


def kernel(x, c, ctx, c_ctx, l0_w_ada, l0_b_ada, l0_norm1, l0_w_in, l0_conv_w, l0_conv_b, l0_dt_bias, l0_a_log, l0_d_skip, l0_ssd_norm, l0_w_out, l0_norm2, l0_peer_wq, l0_peer_keys, l0_peer_u, l0_peer_v, l1_w_ada, l1_b_ada, l1_norm1, l1_w_dqkv, l1_q_a_norm, l1_kv_a_norm, l1_w_uq, l1_w_ukv, l1_q_norm, l1_k_norm, l1_w_o, l1_norm2, l1_peer_wq, l1_peer_keys, l1_peer_u, l1_peer_v):
    x_all = jnp.concatenate([x.reshape(LAT_ROWS, D_MODEL), ctx.reshape(BATCH * CTX_LEN, D_MODEL)], axis=0)

    mod0 = ada_table(c, c_ctx, l0_w_ada, l0_b_ada)
    f, z, xbc, dt = even_in(x_all, mod0, l0_norm1, l0_w_in)
    u = conv_silu(xbc, l0_conv_w, l0_conv_b)
    yf, yb = ssd_scan(u, dt, l0_dt_bias, l0_a_log)
    four = fourier_mix(f)
    x_all = even_out(x_all, mod0, four, yf, yb, u, z, l0_d_skip, l0_ssd_norm, l0_w_out)
    x_all = peer_layer(x_all, mod0, l0_norm2, l0_peer_wq, l0_peer_keys, l0_peer_u, l0_peer_v)

    mod1 = ada_table(c, c_ctx, l1_w_ada, l1_b_ada)
    p1 = dict(norm1=l1_norm1, w_dqkv=l1_w_dqkv, q_a_norm=l1_q_a_norm, kv_a_norm=l1_kv_a_norm, w_uq=l1_w_uq,
              w_ukv=l1_w_ukv, q_norm=l1_q_norm, k_norm=l1_k_norm)
    q, k, v = mla_in(x_all, mod1, p1)
    o = attention(q, k, v)
    x_lat = attn_out(x_all, mod1, o, l1_w_o)
    x_lat = peer_layer(x_lat, mod1, l1_norm2, l1_peer_wq, l1_peer_keys, l1_peer_u, l1_peer_v)
    return x_lat.reshape(BATCH, SEQ, D_MODEL)
```

```python
import functools
import math

import numpy as np
import jax
import jax.numpy as jnp
from jax import lax
from jax.experimental import pallas as pl
from jax.experimental.pallas import tpu as pltpu

D_MODEL = 1024
BATCH = 2
SEQ = 8192
GRID_W = 64
CTX_LEN = 256
EPS = 1e-6
N_MOD = 6
LAT_ROWS = BATCH * SEQ
ALL_ROWS = LAT_ROWS + BATCH * CTX_LEN

FNET_GROUPS = 4
FNET_GROUP_DIM = 128
FNET_WIDTH = FNET_GROUPS * FNET_GROUP_DIM
FFT_N1 = 64
FFT_N2 = 128

SSD_HEADS = 16
SSD_HEAD_DIM = 64
SSD_INNER = SSD_HEADS * SSD_HEAD_DIM
SSD_GROUPS = 4
SSD_STATE = 128
SSD_CONV = 5
SSD_CHUNK = 128
SSD_CONV_DIM = SSD_INNER + 2 * SSD_GROUPS * SSD_STATE
SSD_GROUP_W = SSD_INNER // SSD_GROUPS

MLA_HEADS = 16
MLA_NOPE = 64
MLA_ROPE = 32
MLA_QK = MLA_NOPE + MLA_ROPE
MLA_V = 64
MLA_Q_LORA = 384
MLA_KV_LORA = 256
ROPE_THETA = 10000.0
HEAD_PAD = 128

PEER_HEADS = 8
PEER_KEYS = 128
PEER_EXPERTS = PEER_KEYS * PEER_KEYS
PEER_QDIM = 256
PEER_TOPK = 16

LANES = 128
SUBLANES = 8
VMEM_LIMIT = 56 * 1024 * 1024

ROW_TILE = 512
CONV_TILE = 256
PEER_TILE = 512
PEER_CHUNK = 512
PEER_CHUNKS_PER_STEP = 2
PEER_SUB = 256
PEER_SELECT_TOKENS = 128
ATTN_TQ = 1024
ATTN_TK = 512

F32 = jnp.float32
BF16 = jnp.bfloat16
HIGHEST = lax.Precision.HIGHEST
NEG_INF = float("-inf")


def _params(*sem):
    return pltpu.CompilerParams(dimension_semantics=sem, vmem_limit_bytes=VMEM_LIMIT)


def _mod_row(i, tile):
    return jnp.minimum((i * tile) // SEQ, BATCH)


def _full(shape):
    return pl.BlockSpec(shape, lambda *_: (0,) * len(shape))


def _silu(x):
    return x * (1.0 / (1.0 + jnp.exp(-x)))


def _modulated_norm(x, gain, mod, k_shift, k_scale):
    shift = mod[:, k_shift * D_MODEL:(k_shift + 1) * D_MODEL]
    scale = mod[:, k_scale * D_MODEL:(k_scale + 1) * D_MODEL]
    ms = jnp.mean(x * x, axis=-1, keepdims=True)
    return x * lax.rsqrt(ms + EPS) * gain * (1.0 + scale) + shift


def _bdot(a, b):
    return jnp.dot(a.astype(BF16), b.astype(BF16), preferred_element_type=F32)


def _hdot(a, b):
    return jnp.dot(a, b, precision=HIGHEST, preferred_element_type=F32)


def _bf16_terms(x, n):
    terms = []
    for _ in range(n):
        t = x.astype(BF16)
        terms.append(t)
        x = x - t.astype(F32)
    return terms


def _dot_exact_rhs(a, b01, terms=3):
    return sum(jnp.dot(t, b01, preferred_element_type=F32) for t in _bf16_terms(a, terms))


def _dot_exact_lhs(a01, b):
    return sum(jnp.dot(a01, t, preferred_element_type=F32) for t in _bf16_terms(b, 3))


def _dot3(a, b):
    a_hi, a_lo = _bf16_terms(a, 2)
    b_hi, b_lo = _bf16_terms(b, 2)
    dot = lambda x, y: jnp.dot(x, y, preferred_element_type=F32)
    return dot(a_hi, b_hi) + (dot(a_hi, b_lo) + dot(a_lo, b_hi))


def _ada_kernel(c_ref, w_ref, b_ref, o_ref):
    o_ref[...] = _hdot(_silu(c_ref[...]), w_ref[...]) + b_ref[...]


def ada_table(c, c_ctx, w_ada, b_ada):
    cond = jnp.concatenate([c, c_ctx[None, :], jnp.zeros((SUBLANES - BATCH - 1, D_MODEL), F32)], axis=0)
    tn = 512
    out = pl.pallas_call(
        _ada_kernel,
        grid=(N_MOD * D_MODEL // tn,),
        in_specs=[_full((SUBLANES, D_MODEL)),
                  pl.BlockSpec((D_MODEL, tn), lambda j: (0, j)),
                  pl.BlockSpec((1, tn), lambda j: (0, j))],
        out_specs=pl.BlockSpec((SUBLANES, tn), lambda j: (0, j)),
        out_shape=jax.ShapeDtypeStruct((SUBLANES, N_MOD * D_MODEL), F32),
        compiler_params=_params("arbitrary"),
        name="ada_table",
    )(cond, w_ada, b_ada[None, :])
    return out[:BATCH + 1].reshape(BATCH + 1, 1, N_MOD * D_MODEL)


def _even_in_kernel(x_ref, mod_ref, g_ref, wf_ref, wz_ref, wx_ref, wd_ref, f_ref, z_ref, xbc_ref, dt_ref):
    h = _modulated_norm(x_ref[...], g_ref[...], mod_ref[0], 0, 1).astype(BF16)
    f_ref[...] = jnp.dot(h, wf_ref[...], preferred_element_type=F32)
    z_ref[...] = jnp.dot(h, wz_ref[...], preferred_element_type=F32)
    xbc_ref[...] = jnp.dot(h, wx_ref[...], preferred_element_type=F32)
    dt_ref[...] = jnp.dot(h, wd_ref[...], preferred_element_type=F32)


def even_in(x_all, mod, norm1, w_in):
    rows = x_all.shape[0]
    o1, o2, o3 = FNET_WIDTH, FNET_WIDTH + SSD_INNER, FNET_WIDTH + SSD_INNER + SSD_CONV_DIM
    wf = w_in[:, :o1].astype(BF16)
    wz = w_in[:, o1:o2].astype(BF16)
    wx = w_in[:, o2:o3].astype(BF16)
    wd = jnp.pad(w_in[:, o3:], ((0, 0), (0, LANES - 2 * SSD_HEADS))).astype(BF16)
    tm = ROW_TILE
    row = lambda n: pl.BlockSpec((tm, n), lambda i: (i, 0))
    return pl.pallas_call(
        _even_in_kernel,
        grid=(rows // tm,),
        in_specs=[row(D_MODEL),
                  pl.BlockSpec((1, 1, N_MOD * D_MODEL), lambda i: (_mod_row(i, tm), 0, 0)),
                  _full((1, D_MODEL)),
                  _full(wf.shape), _full(wz.shape), _full(wx.shape), _full(wd.shape)],
        out_specs=[row(FNET_WIDTH), row(SSD_INNER), row(SSD_CONV_DIM), row(LANES)],
        out_shape=[jax.ShapeDtypeStruct((rows, FNET_WIDTH), F32),
                   jax.ShapeDtypeStruct((rows, SSD_INNER), F32),
                   jax.ShapeDtypeStruct((rows, SSD_CONV_DIM), F32),
                   jax.ShapeDtypeStruct((rows, LANES), F32)],
        compiler_params=_params("arbitrary"),
        name="even_in",
    )(x_all, mod, norm1[None, :], wf, wz, wx, wd)


def _conv_kernel(x_ref, prev_ref, next_ref, w_ref, b_ref, o_ref, ext_ref, *, tm):
    row0 = pl.program_id(0) * tm
    in_lat = row0 < LAT_ROWS
    first = jnp.where(in_lat, row0 % SEQ == 0, (row0 - LAT_ROWS) % CTX_LEN == 0)
    last = jnp.where(in_lat, (row0 + tm) % SEQ == 0, (row0 + tm - LAT_ROWS) % CTX_LEN == 0)
    ext_ref[0:SUBLANES, :] = prev_ref[...] * jnp.where(first, 0.0, 1.0)
    ext_ref[SUBLANES:SUBLANES + tm, :] = x_ref[...]
    ext_ref[SUBLANES + tm:2 * SUBLANES + tm, :] = next_ref[...] * jnp.where(last, 0.0, 1.0)
    pad = SSD_CONV // 2
    acc = b_ref[...] + w_ref[0:1, :] * ext_ref[pl.ds(SUBLANES - pad, tm), :]
    for k in range(1, SSD_CONV):
        acc = acc + w_ref[k:k + 1, :] * ext_ref[pl.ds(SUBLANES - pad + k, tm), :]
    o_ref[...] = _silu(acc)


def conv_silu(xbc, conv_w, conv_b):
    rows, ch = xbc.shape
    tm, tc = CONV_TILE, ch
    assert CTX_LEN % tm == 0 and SEQ % tm == 0
    per = tm // SUBLANES
    n_small = rows // SUBLANES
    w = jnp.pad(conv_w, ((0, SUBLANES - SSD_CONV), (0, 0)))
    return pl.pallas_call(
        functools.partial(_conv_kernel, tm=tm),
        grid=(rows // tm, ch // tc),
        in_specs=[pl.BlockSpec((tm, tc), lambda i, j: (i, j)),
                  pl.BlockSpec((SUBLANES, tc), lambda i, j: (jnp.maximum(i * per - 1, 0), j)),
                  pl.BlockSpec((SUBLANES, tc), lambda i, j: (jnp.minimum((i + 1) * per, n_small - 1), j)),
                  pl.BlockSpec((SUBLANES, tc), lambda i, j: (0, j)),
                  pl.BlockSpec((1, tc), lambda i, j: (0, j))],
        out_specs=pl.BlockSpec((tm, tc), lambda i, j: (i, j)),
        out_shape=jax.ShapeDtypeStruct((rows, ch), F32),
        scratch_shapes=[pltpu.VMEM((tm + 2 * SUBLANES, tc), F32)],
        compiler_params=_params("arbitrary", "arbitrary"),
        name="conv_silu",
    )(xbc, xbc, xbc, w, conv_b[None, :])


def _softplus(x):
    return jnp.maximum(x, 0.0) + jnp.log1p(jnp.exp(-jnp.abs(x)))


def _ssd_direction(u_ref, dt_ref, dtb_ref, alog_ref, exp_ref, state_ref, y_ref, direction):
    cl = SSD_CHUNK
    xs = u_ref[:, :SSD_INNER]
    dtv = _softplus(dt_ref[...] + dtb_ref[...])
    dta = dtv * (-jnp.exp(alog_ref[...]))
    r = lax.broadcasted_iota(jnp.int32, (cl, cl), 0)
    c = lax.broadcasted_iota(jnp.int32, (cl, cl), 1)
    tri = (r >= c) if direction == 0 else (r <= c)
    cs = _dot_exact_lhs(tri.astype(F32).astype(BF16), dta)
    cs_t = cs.T
    edge = cs[cl - 1:cl, :] if direction == 0 else cs[0:1, :]
    to_end = jnp.exp(edge - cs)
    from_start = jnp.exp(cs)
    spread = _dot_exact_rhs(jnp.concatenate([dtv, to_end, from_start], axis=0), exp_ref[direction], terms=2)
    dt_x, te_x, fs_x = spread[:cl], spread[cl:2 * cl], spread[2 * cl:]
    chunk_decay = fs_x[cl - 1:cl, :] if direction == 0 else fs_x[0:1, :]
    xd = xs * dt_x
    xd_b = xd.astype(BF16)
    xte_b = (xd * te_x).astype(BF16)
    lane = lax.broadcasted_iota(jnp.int32, (cl, LANES), 1)
    low = lane < SSD_HEAD_DIM
    for g in range(SSD_GROUPS):
        bg = u_ref[:, SSD_INNER + g * SSD_STATE:SSD_INNER + (g + 1) * SSD_STATE]
        cg = u_ref[:, SSD_INNER + (SSD_GROUPS + g) * SSD_STATE:SSD_INNER + (SSD_GROUPS + g + 1) * SSD_STATE]
        bg_b, cg_b = bg.astype(BF16), cg.astype(BF16)
        cb = lax.dot_general(cg_b, bg_b, (((1,), (1,)), ((), ())), preferred_element_type=F32)
        gs = slice(g * SSD_GROUP_W, (g + 1) * SSD_GROUP_W)
        state = state_ref[g]
        y_off = jnp.dot(cg_b, state.astype(BF16), preferred_element_type=F32) * fs_x[:, gs]
        new_state = jnp.dot(bg.T.astype(BF16), xte_b[:, gs], preferred_element_type=F32)
        state_ref[g] = state * chunk_decay[:, gs] + new_state
        for pair in range(2):
            blk = xd_b[:, g * SSD_GROUP_W + pair * LANES:g * SSD_GROUP_W + (pair + 1) * LANES]
            y_pair = None
            for sub in range(2):
                col = direction * SSD_HEADS + g * 4 + pair * 2 + sub
                seg = cs[:, col:col + 1] - cs_t[col:col + 1, :]
                decay = jnp.exp(jnp.where(tri, seg, NEG_INF))
                m = (cb * decay).astype(BF16)
                half = jnp.where(low if sub == 0 else jnp.logical_not(low), blk, jnp.zeros_like(blk))
                part = jnp.dot(m, half, preferred_element_type=F32)
                y_pair = part if y_pair is None else y_pair + part
            lo = g * SSD_GROUP_W + pair * LANES
            y_ref[:, lo:lo + LANES] = y_pair + y_off[:, pair * LANES:(pair + 1) * LANES]


def _ssd_kernel(uf_ref, ub_ref, dtf_ref, dtb_in_ref, bias_ref, alog_ref, exp_ref, yf_ref, yb_ref, sf_ref, sb_ref):
    @pl.when(pl.program_id(1) == 0)
    def _():
        sf_ref[...] = jnp.zeros_like(sf_ref)
        sb_ref[...] = jnp.zeros_like(sb_ref)

    _ssd_direction(uf_ref, dtf_ref, bias_ref, alog_ref, exp_ref, sf_ref, yf_ref, 0)
    _ssd_direction(ub_ref, dtb_in_ref, bias_ref, alog_ref, exp_ref, sb_ref, yb_ref, 1)


def ssd_scan(u, dt, dt_bias, a_log):
    rows = u.shape[0]
    cl = SSD_CHUNK
    lat_chunks, ctx_chunks = SEQ // cl, CTX_LEN // cl
    steps = ctx_chunks + lat_chunks
    ctx0 = LAT_ROWS // cl

    def fwd_chunk(b, t):
        return jnp.where(t < ctx_chunks, ctx0 + b * ctx_chunks + t, b * lat_chunks + t - ctx_chunks)

    def bwd_chunk(b, t):
        return jnp.where(t < ctx_chunks, ctx0 + b * ctx_chunks + (ctx_chunks - 1 - t),
                         b * lat_chunks + (lat_chunks - 1) - (t - ctx_chunks))

    pad = LANES - 2 * SSD_HEADS
    bias = jnp.pad(dt_bias.reshape(1, -1), ((0, 0), (0, pad)))
    alog = jnp.pad(a_log.reshape(1, -1), ((0, 0), (0, pad)))
    expand = np.zeros((2, LANES, SSD_INNER), np.float32)
    for d in range(2):
        for h in range(SSD_HEADS):
            expand[d, d * SSD_HEADS + h, h * SSD_HEAD_DIM:(h + 1) * SSD_HEAD_DIM] = 1.0
    spec = lambda n, fn: pl.BlockSpec((cl, n), lambda b, t: (fn(b, t), 0))
    return pl.pallas_call(
        _ssd_kernel,
        grid=(BATCH, steps),
        in_specs=[spec(SSD_CONV_DIM, fwd_chunk), spec(SSD_CONV_DIM, bwd_chunk),
                  spec(LANES, fwd_chunk), spec(LANES, bwd_chunk),
                  _full((1, LANES)), _full((1, LANES)), _full((2, LANES, SSD_INNER))],
        out_specs=[spec(SSD_INNER, fwd_chunk), spec(SSD_INNER, bwd_chunk)],
        out_shape=[jax.ShapeDtypeStruct((rows, SSD_INNER), F32)] * 2,
        scratch_shapes=[pltpu.VMEM((SSD_GROUPS, SSD_STATE, SSD_GROUP_W), F32)] * 2,
        compiler_params=_params("arbitrary", "arbitrary"),
        name="ssd_scan",
    )(u, u, dt, dt, bias, alog, jnp.asarray(expand, BF16))


def _dft_cos_sin(n):
    k = np.arange(n)
    ang = 2.0 * np.pi * ((k[:, None] * k[None, :]) % n) / n
    return np.cos(ang), np.sin(ang)


def _fft1_kernel(x_ref, m_ref, o_ref):
    o_ref[0] = _dot3(m_ref[...], x_ref[...])


def _fft2_kernel(ar_ref, ai_ref, twr_ref, twi_ref, m2_ref, mc_ref, o_ref):
    ar, ai = ar_ref[0, 0], ai_ref[0, 0]
    twr = jnp.concatenate([twr_ref[0]] * FNET_GROUPS, axis=1)
    twi = jnp.concatenate([twi_ref[0]] * FNET_GROUPS, axis=1)
    p = ar * twr - ai * twi
    q = ar * twi + ai * twr
    uv = _dot3(m2_ref[...], jnp.concatenate([p, q], axis=0))
    n2 = FFT_N2
    for g in range(FNET_GROUPS):
        gs = slice(g * FNET_GROUP_DIM, (g + 1) * FNET_GROUP_DIM)
        o_ref[0, :, gs] = _dot3(jnp.concatenate([uv[:n2, gs], uv[n2:, gs]], axis=1), mc_ref[...])


def _fft_ctx_kernel(x_ref, mc_ref, mp_ref, o_ref):
    x = x_ref[...]
    for g in range(FNET_GROUPS):
        gs = slice(g * FNET_GROUP_DIM, (g + 1) * FNET_GROUP_DIM)
        cs = _dot3(x[:, gs], mc_ref[...])
        stacked = jnp.concatenate([cs[:, :FNET_GROUP_DIM], cs[:, FNET_GROUP_DIM:]], axis=0)
        o_ref[:, gs] = _dot3(mp_ref[...], stacked)


def fourier_mix(f_all):
    rows = f_all.shape[0]
    n1, n2, gd = FFT_N1, FFT_N2, FNET_GROUP_DIM
    row_w = n2 * FNET_WIDTH
    c1, s1 = _dft_cos_sin(n1)
    m1 = jnp.asarray(np.concatenate([c1, -s1], axis=0), F32)
    tn = 8192
    stage1 = pl.pallas_call(
        _fft1_kernel,
        grid=(BATCH, row_w // tn),
        in_specs=[pl.BlockSpec((n1, tn), lambda b, j: (b, j)), _full((2 * n1, n1))],
        out_specs=pl.BlockSpec((1, 2 * n1, tn), lambda b, j: (b, 0, j)),
        out_shape=jax.ShapeDtypeStruct((BATCH, 2 * n1, row_w), F32),
        compiler_params=_params("arbitrary", "arbitrary"),
        name="fft_stage1",
    )(f_all.reshape(rows // n2, row_w), m1)

    k1 = np.arange(n1)[:, None]
    l2 = np.arange(n2)[None, :]
    tw = 2.0 * np.pi * (k1 * l2) / SEQ
    twr = jnp.asarray(np.repeat(np.cos(tw)[:, :, None], gd, axis=2), F32)
    twi = jnp.asarray(np.repeat(-np.sin(tw)[:, :, None], gd, axis=2), F32)
    c2, s2 = _dft_cos_sin(n2)
    m2 = jnp.asarray(np.block([[c2, s2], [-s2, c2]]), F32)
    cc, sc = _dft_cos_sin(gd)
    mc = jnp.asarray(np.concatenate([cc, sc], axis=0) / math.sqrt(SEQ * gd), F32)
    a4 = stage1.reshape(BATCH, 2 * n1, n2, FNET_WIDTH)
    lat = pl.pallas_call(
        _fft2_kernel,
        grid=(BATCH, n1),
        in_specs=[pl.BlockSpec((1, 1, n2, FNET_WIDTH), lambda b, k: (b, k, 0, 0)),
                  pl.BlockSpec((1, 1, n2, FNET_WIDTH), lambda b, k: (b, n1 + k, 0, 0)),
                  pl.BlockSpec((1, n2, gd), lambda b, k: (k, 0, 0)),
                  pl.BlockSpec((1, n2, gd), lambda b, k: (k, 0, 0)),
                  _full((2 * n2, 2 * n2)), _full((2 * gd, gd))],
        out_specs=pl.BlockSpec((1, n2, FNET_WIDTH), lambda b, k: (b, 0, k)),
        out_shape=jax.ShapeDtypeStruct((BATCH, n2, n1 * FNET_WIDTH), F32),
        compiler_params=_params("arbitrary", "arbitrary"),
        name="fft_stage2",
    )(a4, a4, twr, twi, m2, mc)

    cp, sp = _dft_cos_sin(CTX_LEN)
    mp = jnp.asarray(np.concatenate([cp, -sp], axis=1) / math.sqrt(CTX_LEN * gd), F32)
    mcc = jnp.asarray(np.concatenate([cc, sc], axis=1), F32)
    ctx0 = LAT_ROWS // CTX_LEN
    ctx = pl.pallas_call(
        _fft_ctx_kernel,
        grid=(BATCH,),
        in_specs=[pl.BlockSpec((CTX_LEN, FNET_WIDTH), lambda b: (ctx0 + b, 0)),
                  _full((gd, 2 * gd)), _full((CTX_LEN, 2 * CTX_LEN))],
        out_specs=pl.BlockSpec((CTX_LEN, FNET_WIDTH), lambda b: (b, 0)),
        out_shape=jax.ShapeDtypeStruct((BATCH * CTX_LEN, FNET_WIDTH), F32),
        compiler_params=_params("arbitrary"),
        name="fft_ctx",
    )(f_all, mcc, mp)
    return jnp.concatenate([lat.reshape(LAT_ROWS, FNET_WIDTH), ctx], axis=0)


def _even_out_kernel(x_ref, mod_ref, four_ref, yf_ref, yb_ref, xs_ref, z_ref, dsk_ref, gn_ref, wof_ref, wos_ref,
                     o_ref):
    y = yf_ref[...] + yb_ref[...] + dsk_ref[...] * xs_ref[...]
    gated = y * _silu(z_ref[...])
    out = jnp.dot(four_ref[...].astype(BF16), wof_ref[...], preferred_element_type=F32)
    for g in range(SSD_GROUPS):
        gs = slice(g * SSD_GROUP_W, (g + 1) * SSD_GROUP_W)
        v = gated[:, gs]
        normed = v * lax.rsqrt(jnp.mean(v * v, axis=-1, keepdims=True) + EPS) * gn_ref[:, gs]
        out = out + jnp.dot(normed.astype(BF16), wos_ref[gs, :], preferred_element_type=F32)
    gate = mod_ref[0][:, 2 * D_MODEL:3 * D_MODEL]
    o_ref[...] = x_ref[...] + gate * out


def even_out(x_all, mod, four, yf, yb, u, z, d_skip, ssd_norm, w_out):
    rows = x_all.shape[0]
    tm = ROW_TILE
    dsk = jnp.repeat(d_skip[0] + d_skip[1], SSD_HEAD_DIM)[None, :]
    wof = w_out[:FNET_WIDTH].astype(BF16)
    wos = w_out[FNET_WIDTH:].astype(BF16)
    row = lambda n: pl.BlockSpec((tm, n), lambda i: (i, 0))
    return pl.pallas_call(
        _even_out_kernel,
        grid=(rows // tm,),
        in_specs=[row(D_MODEL),
                  pl.BlockSpec((1, 1, N_MOD * D_MODEL), lambda i: (_mod_row(i, tm), 0, 0)),
                  row(FNET_WIDTH), row(SSD_INNER), row(SSD_INNER), row(SSD_INNER), row(SSD_INNER),
                  _full((1, SSD_INNER)), _full((1, SSD_INNER)), _full(wof.shape), _full(wos.shape)],
        out_specs=row(D_MODEL),
        out_shape=jax.ShapeDtypeStruct((rows, D_MODEL), F32),
        compiler_params=_params("arbitrary"),
        name="even_out",
    )(x_all, mod, four, yf, yb, u, z, dsk, ssd_norm[None, :], wof, wos)


BF16_ROWS = 2 * SUBLANES
NOT_TOP = 64.0


def _erf_gelu(z):
    return 0.5 * z * (1.0 + lax.erf(z * (1.0 / math.sqrt(2.0))))


def _top_ranks(score, k, want_rank=True):
    work, vals = score, []
    rank = jnp.full(score.shape, NOT_TOP, F32) if want_rank else None
    for i in range(k):
        m = jnp.max(work, axis=0, keepdims=True)
        hit = work == m
        vals.append(m)
        if want_rank:
            rank = jnp.where(hit, float(i), rank)
        work = jnp.where(hit, NEG_INF, work)
    return jnp.concatenate(vals, axis=0), rank


def _merge_exchange_pairs(n):
    t = max(1, math.ceil(math.log2(n)))
    p, pairs = 2 ** (t - 1), []
    while p > 0:
        q, r, d = 2 ** (t - 1), 0, p
        while d > 0:
            pairs += [(i, i + d) for i in range(n - d) if i & p == r]
            d, q, r = q - p, q // 2, p
        p //= 2
    return pairs


def _top_values_sorted(score, k):
    groups = [score[r * SUBLANES:(r + 1) * SUBLANES] for r in range(score.shape[0] // SUBLANES)]
    for a, b in _merge_exchange_pairs(len(groups)):
        groups[a], groups[b] = jnp.maximum(groups[a], groups[b]), jnp.minimum(groups[a], groups[b])
    vals = []
    for i in range(k):
        m = jnp.max(groups[0], axis=0, keepdims=True)
        vals.append(m)
        needed = k - 1 - i
        if needed == 0:
            break
        hit = groups[0] == m
        shifted = [jnp.where(hit, groups[d + 1], groups[d]) for d in range(len(groups) - 1)]
        if len(groups) <= needed:
            shifted.append(jnp.where(hit, NEG_INF, groups[-1]))
        groups = shifted[:needed]
    return jnp.concatenate(vals, axis=0)


def _rank_among(values, top, k):
    assert k & (k - 1) == 0
    bits, count, stride = [], jnp.zeros_like(values), k // 2
    while stride >= 1:
        cands = [top[p + stride - 1:p + stride] for p in range(0, k, 2 * stride)]
        for b in reversed(bits):
            cands = [jnp.where(b, cands[2 * i + 1], cands[2 * i]) for i in range(len(cands) // 2)]
        bit = cands[0] > values
        count = count + jnp.where(bit, float(stride), 0.0)
        bits.append(bit)
        stride //= 2
    return jnp.where(values >= top[k - 1:k], count, NOT_TOP)


def _pair_candidates(v1, v2, k):
    assert k == 2 * SUBLANES, "the row grouping below is laid out for k = 16"
    t = v1.shape[1]
    row8 = lax.broadcasted_iota(jnp.int32, (SUBLANES, t), 0)
    rowf = row8.astype(F32)
    sums = [v1[0:1] + v2[0:8], v1[0:1] + v2[8:16], v1[1:2] + v2[0:8], v1[8:16] + v2[0:1]]
    order = [rowf, rowf + 8.0, rowf + float(k), (rowf + 8.0) * float(k)]
    for i in range(2, 8):
        sums.append(jnp.where(row8 < k // (i + 1), v1[i:i + 1] + v2[0:8], NEG_INF))
        order.append(rowf + float(i * k))
    return jnp.concatenate(sums, axis=0), jnp.concatenate(order, axis=0)


def _select_by_value(s1, s2, k):
    v1 = _top_values_sorted(s1, k)
    v2 = _top_values_sorted(s2, k)
    r2 = _rank_among(s2, v2, k)
    cand, _ = _pair_candidates(v1, v2, k)
    tau = _top_values_sorted(cand, k)[k - 1:k, :]
    chosen = cand >= tau
    zsum = jnp.sum(jnp.where(chosen, jnp.exp(cand - (v1[0:1] + v2[0:1])), 0.0), axis=0, keepdims=True)
    in_top = s1 >= v1[k - 1:k]
    a1 = jnp.where(in_top, s1, NEG_INF)
    lim = jnp.zeros_like(s1)
    for j in range(k // 2):
        lim = lim + jnp.where(a1 + v2[j:j + 1] >= tau, 1.0, 0.0)
    best = jnp.zeros_like(tau)
    for j in range(k // 2, k):
        best = best + jnp.where(v1[0:1] + v2[j:j + 1] >= tau, 1.0, 0.0)
    lim = lim + jnp.where(s1 == v1[0:1], best, 0.0)
    count = lambda mask: jnp.sum(jnp.where(mask, 1.0, 0.0), axis=0, keepdims=True)
    most = jnp.maximum(jnp.maximum(count(in_top), count(r2 < k)), count(chosen))
    return lim, jnp.exp(s1 - v1[0:1]) / zsum, r2, jnp.exp(s2 - v2[0:1]), most


def _top_ranks_ordered(score, order, k):
    work, vals = score, []
    rank = jnp.full(score.shape, NOT_TOP, F32)
    for i in range(k):
        m = jnp.max(work, axis=0, keepdims=True)
        first = jnp.min(jnp.where(work == m, order, float(2 ** 20)), axis=0, keepdims=True)
        taken = order == first
        vals.append(m)
        rank = jnp.where(taken, float(i), rank)
        work = jnp.where(taken, NEG_INF, work)
    return jnp.concatenate(vals, axis=0), rank


def _select_by_order(s1, s2, k):
    key_index = lax.broadcasted_iota(jnp.int32, s1.shape, 0).astype(F32)
    v1, r1 = _top_ranks_ordered(s1, key_index, k)
    v2, r2 = _top_ranks_ordered(s2, key_index, k)
    cand, position = _pair_candidates(v1, v2, k)
    chosen = _top_ranks_ordered(cand, position, k)[1] < k
    zsum = jnp.sum(jnp.where(chosen, jnp.exp(cand - (v1[0:1] + v2[0:1])), 0.0), axis=0, keepdims=True)
    picks = jnp.where(chosen, 1.0, 0.0)
    group = lambda g: jnp.sum(picks[g * SUBLANES:(g + 1) * SUBLANES], axis=0, keepdims=True)
    per_rank = [group(0) + group(1), group(2)] + [group(g) for g in range(4, 10)]
    per_rank = jnp.concatenate(per_rank + [picks[3 * SUBLANES:4 * SUBLANES]], axis=0)
    lim = jnp.zeros_like(s1)
    for i in range(k):
        lim = lim + jnp.where(r1 == float(i), per_rank[i:i + 1], 0.0)
    return lim, jnp.exp(s1 - v1[0:1]) / zsum, r2, jnp.exp(s2 - v2[0:1])


def _peer_kernel(x_ref, mod_ref, g_ref, wq_ref, keys_ref, u0_ref, *rest, tm, chunk, per_step):
    u_refs = rest[:per_step]
    (vt_ref, o_ref, ht_ref, q_ref, lim_ref, w1_ref, rank2_ref, w2_ref, acc_ref, za_ref, zb_ref) = rest[per_step:]
    c = pl.program_id(1)
    k = PEER_TOPK
    packed = (PEER_KEYS // BF16_ROWS, BF16_ROWS, tm)
    tsel = PEER_SELECT_TOKENS

    @pl.when(c == 0)
    def _prologue():
        h2 = _modulated_norm(x_ref[...], g_ref[...], mod_ref[0], 3, 4)
        ht = h2.T.astype(BF16)
        ht_ref[...] = ht
        q = jnp.dot(wq_ref[...], ht, preferred_element_type=F32)
        q_ref[...] = q.reshape(2 * PEER_HEADS, PEER_QDIM // 2, tm)
        acc_ref[...] = jnp.zeros_like(acc_ref)
        za_ref[...] = jnp.dot(u0_ref[...], ht, preferred_element_type=F32)
        def head_body(h, carry):
            def scores(part):
                ts = slice(part * tsel, (part + 1) * tsel)
                s1 = jnp.dot(keys_ref[2 * h], q_ref[2 * h, :, ts].astype(BF16), preferred_element_type=F32)
                s2 = jnp.dot(keys_ref[2 * h + 1], q_ref[2 * h + 1, :, ts].astype(BF16), preferred_element_type=F32)
                return ts, s1, s2

            def store(ts, lim, w1, r2, w2):
                small = (PEER_KEYS // BF16_ROWS, BF16_ROWS, tsel)
                lim_ref[h, :, ts] = lim
                w1_ref[h, :, ts] = w1
                rank2_ref[h, :, :, ts] = r2.reshape(small).astype(BF16)
                w2_ref[h, :, :, ts] = w2.reshape(small).astype(BF16)

            most = None
            for part in range(tm // tsel):
                ts, s1, s2 = scores(part)
                lim, w1, r2, w2, count = _select_by_value(s1, s2, k)
                store(ts, lim, w1, r2, w2)
                most = count if most is None else jnp.maximum(most, count)

            @pl.when(jnp.max(most) > k)
            def _ties():
                for part in range(tm // tsel):
                    ts, s1, s2 = scores(part)
                    store(ts, *_select_by_order(s1, s2, k))

            return carry

        lax.fori_loop(0, PEER_HEADS, head_body, 0)

    per_key = PEER_KEYS // BF16_ROWS
    keys_per_sub = PEER_SUB // PEER_KEYS
    step_keys = per_step * chunk // PEER_KEYS
    key0 = pl.multiple_of(c * step_keys, step_keys)
    step_lim = [lim_ref[h, pl.ds(key0, step_keys), :] for h in range(PEER_HEADS)]
    step_w1 = [w1_ref[h, pl.ds(key0, step_keys), :] for h in range(PEER_HEADS)]

    def gated_values(z_ref, which):
        total = None
        for j in range(chunk // PEER_SUB):
            rows = slice(j * PEER_SUB, (j + 1) * PEER_SUB)
            act = _erf_gelu(z_ref[rows, :]).reshape(PEER_SUB // BF16_ROWS, BF16_ROWS, tm).astype(BF16)
            parts = []
            for jj in range(keys_per_sub):
                e1 = which * (chunk // PEER_KEYS) + j * keys_per_sub + jj
                gate = jnp.zeros(packed, BF16)
                for h in range(PEER_HEADS):
                    lim = jnp.broadcast_to(step_lim[h][e1:e1 + 1, :], (BF16_ROWS, tm)).astype(BF16)
                    w1 = jnp.broadcast_to(step_w1[h][e1:e1 + 1, :], (BF16_ROWS, tm)).astype(BF16)
                    picked = jnp.where(rank2_ref[h] < lim[None], w2_ref[h], jnp.zeros(packed, BF16))
                    gate = gate + picked * w1[None]
                parts.append(act[jj * per_key:(jj + 1) * per_key] * gate)
            a = jnp.concatenate(parts, axis=0).reshape(PEER_SUB, tm)
            cols = slice(which * chunk + j * PEER_SUB, which * chunk + (j + 1) * PEER_SUB)
            d = jnp.dot(vt_ref[:, cols], a, preferred_element_type=F32)
            total = d if total is None else total + d
        return total

    bufs = (za_ref, zb_ref)
    total = None
    for i in range(per_step):
        bufs[(i + 1) % 2][...] = jnp.dot(u_refs[i][...], ht_ref[...], preferred_element_type=F32)
        d = gated_values(bufs[i % 2], i)
        total = d if total is None else total + d
    acc_ref[...] += total

    @pl.when(c == pl.num_programs(1) - 1)
    def _epilogue():
        gate2 = mod_ref[0][:, 5 * D_MODEL:6 * D_MODEL]
        o_ref[...] = x_ref[...] + gate2 * acc_ref[...].T


def peer_layer(x_rows, mod, norm2, peer_wq, peer_keys, peer_u, peer_v):
    rows = x_rows.shape[0]
    tm, chunk = PEER_TILE, PEER_CHUNK
    wq_t = peer_wq.T.astype(BF16)
    keys = peer_keys.reshape(2 * PEER_HEADS, PEER_KEYS, PEER_QDIM // 2).astype(BF16)
    u = peer_u.astype(BF16)
    vt = peer_v.astype(BF16).T
    head_f32 = pltpu.VMEM((PEER_HEADS, PEER_KEYS, tm), F32)
    head_bf16 = pltpu.VMEM((PEER_HEADS, PEER_KEYS // BF16_ROWS, BF16_ROWS, tm), BF16)
    n_chunks = PEER_EXPERTS // chunk
    per_step = PEER_CHUNKS_PER_STEP
    ahead = lambda k: pl.BlockSpec((chunk, D_MODEL), lambda i, c: (jnp.minimum(per_step * c + k, n_chunks - 1), 0))
    return pl.pallas_call(
        functools.partial(_peer_kernel, tm=tm, chunk=chunk, per_step=per_step),
        grid=(rows // tm, n_chunks // per_step),
        in_specs=[pl.BlockSpec((tm, D_MODEL), lambda i, c: (i, 0)),
                  pl.BlockSpec((1, 1, N_MOD * D_MODEL), lambda i, c: (_mod_row(i, tm), 0, 0)),
                  _full((1, D_MODEL)), _full(wq_t.shape), _full(keys.shape),
                  pl.BlockSpec((chunk, D_MODEL), lambda i, c: (0, 0))]
                 + [ahead(k) for k in range(1, per_step + 1)]
                 + [pl.BlockSpec((D_MODEL, per_step * chunk), lambda i, c: (0, c))],
        out_specs=pl.BlockSpec((tm, D_MODEL), lambda i, c: (i, 0)),
        out_shape=jax.ShapeDtypeStruct((rows, D_MODEL), F32),
        scratch_shapes=[pltpu.VMEM((D_MODEL, tm), BF16),
                        pltpu.VMEM((2 * PEER_HEADS, PEER_QDIM // 2, tm), F32),
                        head_f32, head_f32, head_bf16, head_bf16,
                        pltpu.VMEM((D_MODEL, tm), F32),
                        pltpu.VMEM((chunk, tm), F32), pltpu.VMEM((chunk, tm), F32)],
        compiler_params=_params("arbitrary", "arbitrary"),
        name="peer",
    )(x_rows, mod, norm2[None, :], wq_t, keys, u, *([u] * per_step), vt)


def _rope_tables(rows):
    l = jnp.arange(SEQ)
    per_axis = MLA_ROPE // 2
    inv = ROPE_THETA ** (-jnp.arange(0, per_axis, 2, dtype=F32) / per_axis)
    ang = jnp.concatenate([(l // GRID_W)[:, None] * inv, (l % GRID_W)[:, None] * inv], axis=-1)
    cos, sin = jnp.cos(ang), jnp.sin(ang)
    one = jnp.ones((SEQ, MLA_NOPE), F32)
    tail = jnp.zeros((SEQ, HEAD_PAD - MLA_QK), F32)
    zero64 = jnp.zeros((SEQ, MLA_NOPE), F32)
    cos_t = jnp.concatenate([one, cos, cos, tail], axis=1)
    sin_t = jnp.concatenate([zero64, -sin, sin, tail], axis=1)
    n_ctx = rows - LAT_ROWS
    ident = jnp.concatenate([jnp.ones((n_ctx, MLA_QK), F32), jnp.zeros((n_ctx, HEAD_PAD - MLA_QK), F32)], axis=1)
    zeros = jnp.zeros((n_ctx, HEAD_PAD), F32)
    tile = lambda t, c: jnp.concatenate([t] * BATCH + [c], axis=0)
    return tile(cos_t, ident), tile(sin_t, zeros)


def _rope_swap_matrix():
    per_axis = MLA_ROPE // 2
    swap = np.zeros((HEAD_PAD, HEAD_PAD), np.float32)
    for j in range(per_axis):
        swap[MLA_NOPE + per_axis + j, MLA_NOPE + j] = 1.0
        swap[MLA_NOPE + j, MLA_NOPE + per_axis + j] = 1.0
    return swap


def _head_norm_rope(t, gain, cos, sin, ones, swap):
    ms = jnp.dot((t * t).astype(BF16), ones, preferred_element_type=F32) * (1.0 / MLA_QK)
    t = t * lax.rsqrt(ms + EPS) * gain
    return t * cos + jnp.dot(t.astype(BF16), swap, preferred_element_type=F32) * sin


def _mla_in_kernel(x_ref, mod_ref, g_ref, wdq_ref, wdkv_ref, wpe_ref, qan_ref, kvan_ref, wuq_ref, wuk_ref, wuv_ref,
                   qn_ref, kn_ref, ones_ref, swap_ref, cos_ref, sin_ref, q_ref, k_ref, v_ref):
    h = _modulated_norm(x_ref[...], g_ref[...], mod_ref[0], 0, 1).astype(BF16)
    dq = jnp.dot(h, wdq_ref[...], preferred_element_type=F32)
    dkv = jnp.dot(h, wdkv_ref[...], preferred_element_type=F32)
    kpe = jnp.dot(h, wpe_ref[...], preferred_element_type=F32)
    qa = dq * lax.rsqrt(jnp.mean(dq * dq, axis=-1, keepdims=True) + EPS) * qan_ref[...]
    kva = (dkv * lax.rsqrt(jnp.mean(dkv * dkv, axis=-1, keepdims=True) + EPS) * kvan_ref[...]).astype(BF16)
    q = jnp.dot(qa.astype(BF16), wuq_ref[...], preferred_element_type=F32)
    kn = jnp.dot(kva, wuk_ref[...], preferred_element_type=F32)
    v_ref[...] = jnp.dot(kva, wuv_ref[...], preferred_element_type=F32).astype(BF16)
    cos, sin, ones, swap = cos_ref[...], sin_ref[...], ones_ref[...], swap_ref[...]
    scale = MLA_QK ** -0.5 * math.log2(math.e)
    for hd in range(MLA_HEADS):
        hs = slice(hd * HEAD_PAD, (hd + 1) * HEAD_PAD)
        q_ref[:, hs] = (_head_norm_rope(q[:, hs], qn_ref[...], cos, sin, ones, swap) * scale).astype(BF16)
        k_ref[:, hs] = _head_norm_rope(kn[:, hs] + kpe, kn_ref[...], cos, sin, ones, swap).astype(BF16)


def mla_in(x_all, mod, p):
    rows = x_all.shape[0]
    tm = ROW_TILE
    w = p["w_dqkv"]
    wdq = w[:, :MLA_Q_LORA].astype(BF16)
    wdkv = w[:, MLA_Q_LORA:MLA_Q_LORA + MLA_KV_LORA].astype(BF16)
    wpe = jnp.pad(w[:, MLA_Q_LORA + MLA_KV_LORA:], ((0, 0), (MLA_NOPE, HEAD_PAD - MLA_QK))).astype(BF16)
    wuq = jnp.pad(p["w_uq"].reshape(MLA_Q_LORA, MLA_HEADS, MLA_QK), ((0, 0), (0, 0), (0, HEAD_PAD - MLA_QK)))
    wuq = wuq.reshape(MLA_Q_LORA, MLA_HEADS * HEAD_PAD).astype(BF16)
    wukv = p["w_ukv"].reshape(MLA_KV_LORA, MLA_HEADS, MLA_NOPE + MLA_V)
    wuk = jnp.pad(wukv[:, :, :MLA_NOPE], ((0, 0), (0, 0), (0, HEAD_PAD - MLA_NOPE)))
    wuk = wuk.reshape(MLA_KV_LORA, MLA_HEADS * HEAD_PAD).astype(BF16)
    wuv = wukv[:, :, MLA_NOPE:].reshape(MLA_KV_LORA, MLA_HEADS * MLA_V).astype(BF16)
    padg = lambda g: jnp.pad(g, (0, HEAD_PAD - MLA_QK))[None, :]
    cos, sin = _rope_tables(rows)
    row = lambda n: pl.BlockSpec((tm, n), lambda i: (i, 0))
    weights = [wdq, wdkv, wpe, p["q_a_norm"][None, :], p["kv_a_norm"][None, :], wuq, wuk, wuv,
               padg(p["q_norm"]), padg(p["k_norm"]),
               jnp.ones((HEAD_PAD, HEAD_PAD), BF16), jnp.asarray(_rope_swap_matrix(), BF16)]
    return pl.pallas_call(
        _mla_in_kernel,
        grid=(rows // tm,),
        in_specs=[row(D_MODEL),
                  pl.BlockSpec((1, 1, N_MOD * D_MODEL), lambda i: (_mod_row(i, tm), 0, 0)),
                  _full((1, D_MODEL))] + [_full(a.shape) for a in weights] + [row(HEAD_PAD)] * 2,
        out_specs=[row(MLA_HEADS * HEAD_PAD), row(MLA_HEADS * HEAD_PAD), row(MLA_HEADS * MLA_V)],
        out_shape=[jax.ShapeDtypeStruct((rows, MLA_HEADS * HEAD_PAD), BF16),
                   jax.ShapeDtypeStruct((rows, MLA_HEADS * HEAD_PAD), BF16),
                   jax.ShapeDtypeStruct((rows, MLA_HEADS * MLA_V), BF16)],
        compiler_params=_params("arbitrary"),
        name="mla_in",
    )(x_all, mod, p["norm1"][None, :], *weights, cos, sin)


ATTN_SAFE_SHIFT = 50.0


def _attn_kernel(q_ref, kc_ref, kl_ref, vc_ref, vl_ref, o_ref, knorm_ref, *, tq, tk):
    heads = (slice(0, HEAD_PAD), slice(HEAD_PAD, 2 * HEAD_PAD))
    qs = [q_ref[:, hs] for hs in heads]
    n_lat = SEQ // tk

    @pl.when(pl.program_id(2) == 0)
    def _key_norms():
        def sq_max(k2, hh):
            kf = k2[:, heads[hh]].astype(F32)
            return jnp.max(jnp.sum(kf * kf, axis=-1, keepdims=True), axis=0, keepdims=True)

        for hh in range(2):
            best = sq_max(kc_ref[...], hh)
            best = lax.fori_loop(
                0, n_lat, lambda j, b: jnp.maximum(b, sq_max(kl_ref[pl.ds(pl.multiple_of(j * tk, tk), tk), :], hh)), best)
            knorm_ref[hh] = jnp.broadcast_to(jnp.sqrt(best), (SUBLANES, LANES))

    def values(v, hh):
        lane = lax.broadcasted_iota(jnp.int32, v.shape, 1)
        own = (lane < MLA_V) if hh == 0 else (lane >= MLA_V)
        return jnp.where(own, v, jnp.ones_like(v))

    def scores(hh, k2):
        return lax.dot_general(qs[hh], k2[:, heads[hh]], (((1,), (1,)), ((), ())), preferred_element_type=F32)

    def finish(accs):
        lane = lax.broadcasted_iota(jnp.int32, (tq, 2 * MLA_V), 1)
        outs = [acc / pltpu.roll(acc, MLA_V, 1) for acc in accs]
        o_ref[...] = jnp.where(lane < MLA_V, outs[0], outs[1]).astype(o_ref.dtype)

    def lat_chunk(j):
        off = pl.multiple_of(j * tk, tk)
        return kl_ref[pl.ds(off, tk), :], vl_ref[pl.ds(off, tk), :]

    bounds = []
    for hh in range(2):
        qf = qs[hh].astype(F32)
        qn = jnp.sqrt(jnp.sum(qf * qf, axis=-1, keepdims=True))
        bounds.append(qn * knorm_ref[hh, 0:1, 0:1] * 1.001 + 1e-3)
    widest = jnp.max(jnp.maximum(bounds[0], bounds[1]))

    def fixed_shift():
        def step(k2, v, accs):
            return tuple(accs[hh] + jnp.dot(jnp.exp2(scores(hh, k2) - bounds[hh]).astype(BF16), values(v, hh),
                                            preferred_element_type=F32) for hh in range(2))

        zero = jnp.zeros((tq, 2 * MLA_V), F32)
        accs = step(kc_ref[...], vc_ref[...], (zero, zero))
        finish(lax.fori_loop(0, n_lat, lambda j, a: step(*lat_chunk(j), a), accs, unroll=8))

    def running_max():
        def step(k2, v, carry):
            new = []
            for hh in range(2):
                m, acc = carry[hh]
                s = scores(hh, k2)
                m_new = jnp.maximum(m, jnp.max(s, axis=-1, keepdims=True))
                p = jnp.exp2(s - m_new).astype(BF16)
                acc = jnp.exp2(m - m_new) * acc + jnp.dot(p, values(v, hh), preferred_element_type=F32)
                new.append((m_new, acc))
            return tuple(new)

        init = tuple((jnp.full((tq, 1), NEG_INF, F32), jnp.zeros((tq, 2 * MLA_V), F32)) for _ in range(2))
        carry = step(kc_ref[...], vc_ref[...], init)
        carry = lax.fori_loop(0, n_lat, lambda j, c: step(*lat_chunk(j), c), carry)
        finish([acc for _, acc in carry])

    lax.cond(widest <= ATTN_SAFE_SHIFT, fixed_shift, running_max)


def attention(q, k, v):
    tq, tk = ATTN_TQ, ATTN_TK
    nq = SEQ // tq
    ctx0 = LAT_ROWS // CTX_LEN
    return pl.pallas_call(
        functools.partial(_attn_kernel, tq=tq, tk=tk),
        grid=(BATCH, MLA_HEADS // 2, nq),
        in_specs=[pl.BlockSpec((tq, 2 * HEAD_PAD), lambda b, h, i: (b * nq + i, h)),
                  pl.BlockSpec((CTX_LEN, 2 * HEAD_PAD), lambda b, h, i: (ctx0 + b, h)),
                  pl.BlockSpec((SEQ, 2 * HEAD_PAD), lambda b, h, i: (b, h)),
                  pl.BlockSpec((CTX_LEN, 2 * MLA_V), lambda b, h, i: (ctx0 + b, h)),
                  pl.BlockSpec((SEQ, 2 * MLA_V), lambda b, h, i: (b, h))],
        out_specs=pl.BlockSpec((tq, 2 * MLA_V), lambda b, h, i: (b * nq + i, h)),
        out_shape=jax.ShapeDtypeStruct((LAT_ROWS, MLA_HEADS * MLA_V), BF16),
        scratch_shapes=[pltpu.VMEM((2, SUBLANES, LANES), F32)],
        compiler_params=_params("arbitrary", "arbitrary", "arbitrary"),
        name="attention",
    )(q, k, k, v, v)


def _attn_out_kernel(x_ref, mod_ref, o_ref_in, wo_ref, out_ref):
    gate = mod_ref[0][:, 2 * D_MODEL:3 * D_MODEL]
    out_ref[...] = x_ref[...] + gate * jnp.dot(o_ref_in[...], wo_ref[...], preferred_element_type=F32)


def attn_out(x_all, mod, o, w_o):
    tm = ROW_TILE
    row = lambda n: pl.BlockSpec((tm, n), lambda i: (i, 0))
    return pl.pallas_call(
        _attn_out_kernel,
        grid=(LAT_ROWS // tm,),
        in_specs=[row(D_MODEL), pl.BlockSpec((1, 1, N_MOD * D_MODEL), lambda i: (_mod_row(i, tm), 0, 0)),
                  row(MLA_HEADS * MLA_V), _full(w_o.shape)],
        out_specs=row(D_MODEL),
        out_shape=jax.ShapeDtypeStruct((LAT_ROWS, D_MODEL), F32),
        compiler_params=_params("arbitrary"),
        name="attn_out",
    )(x_all, mod, o, w_o.astype(BF16))


def kernel(x, c, ctx, c_ctx, l0_w_ada, l0_b_ada, l0_norm1, l0_w_in, l0_conv_w, l0_conv_b, l0_dt_bias, l0_a_log, l0_d_skip, l0_ssd_norm, l0_w_out, l0_norm2, l0_peer_wq, l0_peer_keys, l0_peer_u, l0_peer_v, l1_w_ada, l1_b_ada, l1_norm1, l1_w_dqkv, l1_q_a_norm, l1_kv_a_norm, l1_w_uq, l1_w_ukv, l1_q_norm, l1_k_norm, l1_w_o, l1_norm2, l1_peer_wq, l1_peer_keys, l1_peer_u, l1_peer_v):
    x_all = jnp.concatenate([x.reshape(LAT_ROWS, D_MODEL), ctx.reshape(BATCH * CTX_LEN, D_MODEL)], axis=0)

    mod0 = ada_table(c, c_ctx, l0_w_ada, l0_b_ada)
    f, z, xbc, dt = even_in(x_all, mod0, l0_norm1, l0_w_in)
    u = conv_silu(xbc, l0_conv_w, l0_conv_b)
    yf, yb = ssd_scan(u, dt, l0_dt_bias, l0_a_log)
    four = fourier_mix(f)
    x_all = even_out(x_all, mod0, four, yf, yb, u, z, l0_d_skip, l0_ssd_norm, l0_w_out)
    x_all = peer_layer(x_all, mod0, l0_norm2, l0_peer_wq, l0_peer_keys, l0_peer_u, l0_peer_v)

    mod1 = ada_table(c, c_ctx, l1_w_ada, l1_b_ada)
    p1 = dict(norm1=l1_norm1, w_dqkv=l1_w_dqkv, q_a_norm=l1_q_a_norm, kv_a_norm=l1_kv_a_norm, w_uq=l1_w_uq,
              w_ukv=l1_w_ukv, q_norm=l1_q_norm, k_norm=l1_k_norm)
    q, k, v = mla_in(x_all, mod1, p1)
    o = attention(q, k, v)
    x_lat = attn_out(x_all, mod1, o, l1_w_o)
    x_lat = peer_layer(x_lat, mod1, l1_norm2, l1_peer_wq, l1_peer_keys, l1_peer_u, l1_peer_v)
    return x_lat.reshape(BATCH, SEQ, D_MODEL)
```

```python
import functools
import math

import numpy as np
import jax
import jax.numpy as jnp
from jax import lax
from jax.experimental import pallas as pl
from jax.experimental.pallas import tpu as pltpu

D_MODEL = 1024
BATCH = 2
SEQ = 8192
GRID_W = 64
CTX_LEN = 256
EPS = 1e-6
N_MOD = 6
LAT_ROWS = BATCH * SEQ
ALL_ROWS = LAT_ROWS + BATCH * CTX_LEN

FNET_GROUPS = 4
FNET_GROUP_DIM = 128
FNET_WIDTH = FNET_GROUPS * FNET_GROUP_DIM
FFT_N1 = 64
FFT_N2 = 128

SSD_HEADS = 16
SSD_HEAD_DIM = 64
SSD_INNER = SSD_HEADS * SSD_HEAD_DIM
SSD_GROUPS = 4
SSD_STATE = 128
SSD_CONV = 5
SSD_CHUNK = 128
SSD_CONV_DIM = SSD_INNER + 2 * SSD_GROUPS * SSD_STATE
SSD_GROUP_W = SSD_INNER // SSD_GROUPS

MLA_HEADS = 16
MLA_NOPE = 64
MLA_ROPE = 32
MLA_QK = MLA_NOPE + MLA_ROPE
MLA_V = 64
MLA_Q_LORA = 384
MLA_KV_LORA = 256
ROPE_THETA = 10000.0
HEAD_PAD = 128

PEER_HEADS = 8
PEER_KEYS = 128
PEER_EXPERTS = PEER_KEYS * PEER_KEYS
PEER_QDIM = 256
PEER_TOPK = 16

LANES = 128
SUBLANES = 8
VMEM_LIMIT = 56 * 1024 * 1024

ROW_TILE = 512
CONV_TILE = 256
PEER_TILE = 512
PEER_CHUNK = 512
PEER_CHUNKS_PER_STEP = 2
PEER_SUB = 256
PEER_SELECT_TOKENS = 128
ATTN_TQ = 1024
ATTN_TK = 512
ADA_COLS = 512
FFT1_COLS = 8192

F32 = jnp.float32
BF16 = jnp.bfloat16
HIGHEST = lax.Precision.HIGHEST
NEG_INF = float("-inf")


def _params(*sem):
    return pltpu.CompilerParams(dimension_semantics=sem, vmem_limit_bytes=VMEM_LIMIT)


def _mod_row(i, tile):
    return jnp.minimum((i * tile) // SEQ, BATCH)


def _full(shape):
    return pl.BlockSpec(shape, lambda *_: (0,) * len(shape))


def _silu(x):
    return x * (1.0 / (1.0 + jnp.exp(-x)))


def _modulated_norm(x, gain, mod, k_shift, k_scale):
    shift = mod[:, k_shift * D_MODEL:(k_shift + 1) * D_MODEL]
    scale = mod[:, k_scale * D_MODEL:(k_scale + 1) * D_MODEL]
    ms = jnp.mean(x * x, axis=-1, keepdims=True)
    return x * lax.rsqrt(ms + EPS) * gain * (1.0 + scale) + shift


def _hdot(a, b):
    return jnp.dot(a, b, precision=HIGHEST, preferred_element_type=F32)


def _bf16_terms(x, n):
    terms = []
    for _ in range(n):
        t = x.astype(BF16)
        terms.append(t)
        x = x - t.astype(F32)
    return terms


def _dot_exact_rhs(a, b01, terms=3):
    return sum(jnp.dot(t, b01, preferred_element_type=F32) for t in _bf16_terms(a, terms))


def _dot_exact_lhs(a01, b):
    return sum(jnp.dot(a01, t, preferred_element_type=F32) for t in _bf16_terms(b, 3))


def _dot3(a, b):
    a_hi, a_lo = _bf16_terms(a, 2)
    b_hi, b_lo = _bf16_terms(b, 2)
    dot = lambda x, y: jnp.dot(x, y, preferred_element_type=F32)
    return dot(a_hi, b_hi) + (dot(a_hi, b_lo) + dot(a_lo, b_hi))


def _ada_kernel(c_ref, w_ref, b_ref, o_ref):
    o_ref[...] = _hdot(_silu(c_ref[...]), w_ref[...]) + b_ref[...]


def ada_table(c, c_ctx, w_ada, b_ada):
    cond = jnp.concatenate([c, c_ctx[None, :], jnp.zeros((SUBLANES - BATCH - 1, D_MODEL), F32)], axis=0)
    tn = ADA_COLS
    out = pl.pallas_call(
        _ada_kernel,
        grid=(N_MOD * D_MODEL // tn,),
        in_specs=[_full((SUBLANES, D_MODEL)),
                  pl.BlockSpec((D_MODEL, tn), lambda j: (0, j)),
                  pl.BlockSpec((1, tn), lambda j: (0, j))],
        out_specs=pl.BlockSpec((SUBLANES, tn), lambda j: (0, j)),
        out_shape=jax.ShapeDtypeStruct((SUBLANES, N_MOD * D_MODEL), F32),
        compiler_params=_params("arbitrary"),
        name="ada_table",
    )(cond, w_ada, b_ada[None, :])
    return out[:BATCH + 1].reshape(BATCH + 1, 1, N_MOD * D_MODEL)


def _even_in_kernel(x_ref, mod_ref, g_ref, wf_ref, wz_ref, wx_ref, wd_ref, f_ref, z_ref, xbc_ref, dt_ref):
    h = _modulated_norm(x_ref[...], g_ref[...], mod_ref[0], 0, 1).astype(BF16)
    f_ref[...] = jnp.dot(h, wf_ref[...], preferred_element_type=F32)
    z_ref[...] = jnp.dot(h, wz_ref[...], preferred_element_type=F32)
    xbc_ref[...] = jnp.dot(h, wx_ref[...], preferred_element_type=F32)
    dt_ref[...] = jnp.dot(h, wd_ref[...], preferred_element_type=F32)


def even_in(x_all, mod, norm1, w_in):
    rows = x_all.shape[0]
    o1, o2, o3 = FNET_WIDTH, FNET_WIDTH + SSD_INNER, FNET_WIDTH + SSD_INNER + SSD_CONV_DIM
    wf = w_in[:, :o1].astype(BF16)
    wz = w_in[:, o1:o2].astype(BF16)
    wx = w_in[:, o2:o3].astype(BF16)
    wd = jnp.pad(w_in[:, o3:], ((0, 0), (0, LANES - 2 * SSD_HEADS))).astype(BF16)
    tm = ROW_TILE
    row = lambda n: pl.BlockSpec((tm, n), lambda i: (i, 0))
    return pl.pallas_call(
        _even_in_kernel,
        grid=(rows // tm,),
        in_specs=[row(D_MODEL),
                  pl.BlockSpec((1, 1, N_MOD * D_MODEL), lambda i: (_mod_row(i, tm), 0, 0)),
                  _full((1, D_MODEL)),
                  _full(wf.shape), _full(wz.shape), _full(wx.shape), _full(wd.shape)],
        out_specs=[row(FNET_WIDTH), row(SSD_INNER), row(SSD_CONV_DIM), row(LANES)],
        out_shape=[jax.ShapeDtypeStruct((rows, FNET_WIDTH), F32),
                   jax.ShapeDtypeStruct((rows, SSD_INNER), F32),
                   jax.ShapeDtypeStruct((rows, SSD_CONV_DIM), F32),
                   jax.ShapeDtypeStruct((rows, LANES), F32)],
        compiler_params=_params("arbitrary"),
        name="even_in",
    )(x_all, mod, norm1[None, :], wf, wz, wx, wd)


def _conv_kernel(x_ref, prev_ref, next_ref, w_ref, b_ref, o_ref, ext_ref, *, tm):
    row0 = pl.program_id(0) * tm
    in_lat = row0 < LAT_ROWS
    first = jnp.where(in_lat, row0 % SEQ == 0, (row0 - LAT_ROWS) % CTX_LEN == 0)
    last = jnp.where(in_lat, (row0 + tm) % SEQ == 0, (row0 + tm - LAT_ROWS) % CTX_LEN == 0)
    ext_ref[0:SUBLANES, :] = prev_ref[...] * jnp.where(first, 0.0, 1.0)
    ext_ref[SUBLANES:SUBLANES + tm, :] = x_ref[...]
    ext_ref[SUBLANES + tm:2 * SUBLANES + tm, :] = next_ref[...] * jnp.where(last, 0.0, 1.0)
    pad = SSD_CONV // 2
    acc = b_ref[...] + w_ref[0:1, :] * ext_ref[pl.ds(SUBLANES - pad, tm), :]
    for k in range(1, SSD_CONV):
        acc = acc + w_ref[k:k + 1, :] * ext_ref[pl.ds(SUBLANES - pad + k, tm), :]
    o_ref[...] = _silu(acc)


def conv_silu(xbc, conv_w, conv_b):
    rows, ch = xbc.shape
    tm, tc = CONV_TILE, ch
    assert CTX_LEN % tm == 0 and SEQ % tm == 0
    per = tm // SUBLANES
    n_small = rows // SUBLANES
    w = jnp.pad(conv_w, ((0, SUBLANES - SSD_CONV), (0, 0)))
    return pl.pallas_call(
        functools.partial(_conv_kernel, tm=tm),
        grid=(rows // tm, ch // tc),
        in_specs=[pl.BlockSpec((tm, tc), lambda i, j: (i, j)),
                  pl.BlockSpec((SUBLANES, tc), lambda i, j: (jnp.maximum(i * per - 1, 0), j)),
                  pl.BlockSpec((SUBLANES, tc), lambda i, j: (jnp.minimum((i + 1) * per, n_small - 1), j)),
                  pl.BlockSpec((SUBLANES, tc), lambda i, j: (0, j)),
                  pl.BlockSpec((1, tc), lambda i, j: (0, j))],
        out_specs=pl.BlockSpec((tm, tc), lambda i, j: (i, j)),
        out_shape=jax.ShapeDtypeStruct((rows, ch), F32),
        scratch_shapes=[pltpu.VMEM((tm + 2 * SUBLANES, tc), F32)],
        compiler_params=_params("arbitrary", "arbitrary"),
        name="conv_silu",
    )(xbc, xbc, xbc, w, conv_b[None, :])


def _softplus(x):
    return jnp.maximum(x, 0.0) + jnp.log1p(jnp.exp(-jnp.abs(x)))


def _ssd_direction(u_ref, dt_ref, dtb_ref, alog_ref, exp_ref, state_ref, y_ref, direction):
    cl = SSD_CHUNK
    xs = u_ref[:, :SSD_INNER]
    dtv = _softplus(dt_ref[...] + dtb_ref[...])
    dta = dtv * (-jnp.exp(alog_ref[...]))
    r = lax.broadcasted_iota(jnp.int32, (cl, cl), 0)
    c = lax.broadcasted_iota(jnp.int32, (cl, cl), 1)
    tri = (r >= c) if direction == 0 else (r <= c)
    cs = _dot_exact_lhs(tri.astype(F32).astype(BF16), dta)
    cs_t = cs.T
    edge = cs[cl - 1:cl, :] if direction == 0 else cs[0:1, :]
    to_end = jnp.exp(edge - cs)
    from_start = jnp.exp(cs)
    spread = _dot_exact_rhs(jnp.concatenate([dtv, to_end, from_start], axis=0), exp_ref[direction], terms=2)
    dt_x, te_x, fs_x = spread[:cl], spread[cl:2 * cl], spread[2 * cl:]
    chunk_decay = fs_x[cl - 1:cl, :] if direction == 0 else fs_x[0:1, :]
    xd = xs * dt_x
    xd_b = xd.astype(BF16)
    xte_b = (xd * te_x).astype(BF16)
    lane = lax.broadcasted_iota(jnp.int32, (cl, LANES), 1)
    low = lane < SSD_HEAD_DIM
    for g in range(SSD_GROUPS):
        bg = u_ref[:, SSD_INNER + g * SSD_STATE:SSD_INNER + (g + 1) * SSD_STATE]
        cg = u_ref[:, SSD_INNER + (SSD_GROUPS + g) * SSD_STATE:SSD_INNER + (SSD_GROUPS + g + 1) * SSD_STATE]
        bg_b, cg_b = bg.astype(BF16), cg.astype(BF16)
        cb = lax.dot_general(cg_b, bg_b, (((1,), (1,)), ((), ())), preferred_element_type=F32)
        gs = slice(g * SSD_GROUP_W, (g + 1) * SSD_GROUP_W)
        state = state_ref[g]
        y_off = jnp.dot(cg_b, state.astype(BF16), preferred_element_type=F32) * fs_x[:, gs]
        new_state = jnp.dot(bg.T.astype(BF16), xte_b[:, gs], preferred_element_type=F32)
        state_ref[g] = state * chunk_decay[:, gs] + new_state
        for pair in range(2):
            blk = xd_b[:, g * SSD_GROUP_W + pair * LANES:g * SSD_GROUP_W + (pair + 1) * LANES]
            ms, halves = [], []
            for sub in range(2):
                col = direction * SSD_HEADS + g * 4 + pair * 2 + sub
                seg = cs[:, col:col + 1] - cs_t[col:col + 1, :]
                decay = jnp.exp(jnp.where(tri, seg, NEG_INF))
                ms.append((cb * decay).astype(BF16))
                halves.append(jnp.where(low if sub == 0 else jnp.logical_not(low), blk, jnp.zeros_like(blk)))
            y_pair = jnp.dot(jnp.concatenate(ms, axis=1), jnp.concatenate(halves, axis=0),
                             preferred_element_type=F32)
            lo = g * SSD_GROUP_W + pair * LANES
            y_ref[:, lo:lo + LANES] = y_pair + y_off[:, pair * LANES:(pair + 1) * LANES]


def _ssd_kernel(uf_ref, ub_ref, dtf_ref, dtb_in_ref, bias_ref, alog_ref, exp_ref, yf_ref, yb_ref, sf_ref, sb_ref):
    @pl.when(pl.program_id(1) == 0)
    def _():
        sf_ref[...] = jnp.zeros_like(sf_ref)
        sb_ref[...] = jnp.zeros_like(sb_ref)

    _ssd_direction(uf_ref, dtf_ref, bias_ref, alog_ref, exp_ref, sf_ref, yf_ref, 0)
    _ssd_direction(ub_ref, dtb_in_ref, bias_ref, alog_ref, exp_ref, sb_ref, yb_ref, 1)


def ssd_scan(u, dt, dt_bias, a_log):
    rows = u.shape[0]
    cl = SSD_CHUNK
    lat_chunks, ctx_chunks = SEQ // cl, CTX_LEN // cl
    steps = ctx_chunks + lat_chunks
    ctx0 = LAT_ROWS // cl

    def fwd_chunk(b, t):
        return jnp.where(t < ctx_chunks, ctx0 + b * ctx_chunks + t, b * lat_chunks + t - ctx_chunks)

    def bwd_chunk(b, t):
        return jnp.where(t < ctx_chunks, ctx0 + b * ctx_chunks + (ctx_chunks - 1 - t),
                         b * lat_chunks + (lat_chunks - 1) - (t - ctx_chunks))

    pad = LANES - 2 * SSD_HEADS
    bias = jnp.pad(dt_bias.reshape(1, -1), ((0, 0), (0, pad)))
    alog = jnp.pad(a_log.reshape(1, -1), ((0, 0), (0, pad)))
    expand = np.zeros((2, LANES, SSD_INNER), np.float32)
    for d in range(2):
        for h in range(SSD_HEADS):
            expand[d, d * SSD_HEADS + h, h * SSD_HEAD_DIM:(h + 1) * SSD_HEAD_DIM] = 1.0
    spec = lambda n, fn: pl.BlockSpec((cl, n), lambda b, t: (fn(b, t), 0))
    return pl.pallas_call(
        _ssd_kernel,
        grid=(BATCH, steps),
        in_specs=[spec(SSD_CONV_DIM, fwd_chunk), spec(SSD_CONV_DIM, bwd_chunk),
                  spec(LANES, fwd_chunk), spec(LANES, bwd_chunk),
                  _full((1, LANES)), _full((1, LANES)), _full((2, LANES, SSD_INNER))],
        out_specs=[spec(SSD_INNER, fwd_chunk), spec(SSD_INNER, bwd_chunk)],
        out_shape=[jax.ShapeDtypeStruct((rows, SSD_INNER), F32)] * 2,
        scratch_shapes=[pltpu.VMEM((SSD_GROUPS, SSD_STATE, SSD_GROUP_W), F32)] * 2,
        compiler_params=_params("arbitrary", "arbitrary"),
        name="ssd_scan",
    )(u, u, dt, dt, bias, alog, jnp.asarray(expand, BF16))


def _dft_cos_sin(n):
    k = np.arange(n)
    ang = 2.0 * np.pi * ((k[:, None] * k[None, :]) % n) / n
    return np.cos(ang), np.sin(ang)


def _fft1_kernel(x_ref, m_ref, o_ref):
    o_ref[0] = _dot3(m_ref[...], x_ref[...])


def _fft2_kernel(ar_ref, ai_ref, twr_ref, twi_ref, m2_ref, mc_ref, o_ref):
    ar, ai = ar_ref[0, 0], ai_ref[0, 0]
    twr = jnp.concatenate([twr_ref[0]] * FNET_GROUPS, axis=1)
    twi = jnp.concatenate([twi_ref[0]] * FNET_GROUPS, axis=1)
    p = ar * twr - ai * twi
    q = ar * twi + ai * twr
    uv = _dot3(m2_ref[...], jnp.concatenate([p, q], axis=0))
    n2 = FFT_N2
    for g in range(FNET_GROUPS):
        gs = slice(g * FNET_GROUP_DIM, (g + 1) * FNET_GROUP_DIM)
        o_ref[0, :, gs] = _dot3(jnp.concatenate([uv[:n2, gs], uv[n2:, gs]], axis=1), mc_ref[...])


def _fft_ctx_kernel(x_ref, mc_ref, mp_ref, o_ref):
    x = x_ref[...]
    for g in range(FNET_GROUPS):
        gs = slice(g * FNET_GROUP_DIM, (g + 1) * FNET_GROUP_DIM)
        cs = _dot3(x[:, gs], mc_ref[...])
        stacked = jnp.concatenate([cs[:, :FNET_GROUP_DIM], cs[:, FNET_GROUP_DIM:]], axis=0)
        o_ref[:, gs] = _dot3(mp_ref[...], stacked)


def fourier_mix(f_all):
    rows = f_all.shape[0]
    n1, n2, gd = FFT_N1, FFT_N2, FNET_GROUP_DIM
    row_w = n2 * FNET_WIDTH
    c1, s1 = _dft_cos_sin(n1)
    m1 = jnp.asarray(np.concatenate([c1, -s1], axis=0), F32)
    tn = FFT1_COLS
    stage1 = pl.pallas_call(
        _fft1_kernel,
        grid=(BATCH, row_w // tn),
        in_specs=[pl.BlockSpec((n1, tn), lambda b, j: (b, j)), _full((2 * n1, n1))],
        out_specs=pl.BlockSpec((1, 2 * n1, tn), lambda b, j: (b, 0, j)),
        out_shape=jax.ShapeDtypeStruct((BATCH, 2 * n1, row_w), F32),
        compiler_params=_params("arbitrary", "arbitrary"),
        name="fft_stage1",
    )(f_all.reshape(rows // n2, row_w), m1)

    k1 = np.arange(n1)[:, None]
    l2 = np.arange(n2)[None, :]
    tw = 2.0 * np.pi * (k1 * l2) / SEQ
    twr = jnp.asarray(np.repeat(np.cos(tw)[:, :, None], gd, axis=2), F32)
    twi = jnp.asarray(np.repeat(-np.sin(tw)[:, :, None], gd, axis=2), F32)
    c2, s2 = _dft_cos_sin(n2)
    m2 = jnp.asarray(np.block([[c2, s2], [-s2, c2]]), F32)
    cc, sc = _dft_cos_sin(gd)
    mc = jnp.asarray(np.concatenate([cc, sc], axis=0) / math.sqrt(SEQ * gd), F32)
    a4 = stage1.reshape(BATCH, 2 * n1, n2, FNET_WIDTH)
    lat = pl.pallas_call(
        _fft2_kernel,
        grid=(BATCH, n1),
        in_specs=[pl.BlockSpec((1, 1, n2, FNET_WIDTH), lambda b, k: (b, k, 0, 0)),
                  pl.BlockSpec((1, 1, n2, FNET_WIDTH), lambda b, k: (b, n1 + k, 0, 0)),
                  pl.BlockSpec((1, n2, gd), lambda b, k: (k, 0, 0)),
                  pl.BlockSpec((1, n2, gd), lambda b, k: (k, 0, 0)),
                  _full((2 * n2, 2 * n2)), _full((2 * gd, gd))],
        out_specs=pl.BlockSpec((1, n2, FNET_WIDTH), lambda b, k: (b, 0, k)),
        out_shape=jax.ShapeDtypeStruct((BATCH, n2, n1 * FNET_WIDTH), F32),
        compiler_params=_params("arbitrary", "arbitrary"),
        name="fft_stage2",
    )(a4, a4, twr, twi, m2, mc)

    cp, sp = _dft_cos_sin(CTX_LEN)
    mp = jnp.asarray(np.concatenate([cp, -sp], axis=1) / math.sqrt(CTX_LEN * gd), F32)
    mcc = jnp.asarray(np.concatenate([cc, sc], axis=1), F32)
    ctx0 = LAT_ROWS // CTX_LEN
    ctx = pl.pallas_call(
        _fft_ctx_kernel,
        grid=(BATCH,),
        in_specs=[pl.BlockSpec((CTX_LEN, FNET_WIDTH), lambda b: (ctx0 + b, 0)),
                  _full((gd, 2 * gd)), _full((CTX_LEN, 2 * CTX_LEN))],
        out_specs=pl.BlockSpec((CTX_LEN, FNET_WIDTH), lambda b: (b, 0)),
        out_shape=jax.ShapeDtypeStruct((BATCH * CTX_LEN, FNET_WIDTH), F32),
        compiler_params=_params("arbitrary"),
        name="fft_ctx",
    )(f_all, mcc, mp)
    return jnp.concatenate([lat.reshape(LAT_ROWS, FNET_WIDTH), ctx], axis=0)


def _even_out_kernel(x_ref, mod_ref, four_ref, yf_ref, yb_ref, xs_ref, z_ref, dsk_ref, gn_ref, wof_ref, wos_ref,
                     o_ref):
    y = yf_ref[...] + yb_ref[...] + dsk_ref[...] * xs_ref[...]
    gated = y * _silu(z_ref[...])
    out = jnp.dot(four_ref[...].astype(BF16), wof_ref[...], preferred_element_type=F32)
    for g in range(SSD_GROUPS):
        gs = slice(g * SSD_GROUP_W, (g + 1) * SSD_GROUP_W)
        v = gated[:, gs]
        normed = v * lax.rsqrt(jnp.mean(v * v, axis=-1, keepdims=True) + EPS) * gn_ref[:, gs]
        out = out + jnp.dot(normed.astype(BF16), wos_ref[gs, :], preferred_element_type=F32)
    gate = mod_ref[0][:, 2 * D_MODEL:3 * D_MODEL]
    o_ref[...] = x_ref[...] + gate * out


def even_out(x_all, mod, four, yf, yb, u, z, d_skip, ssd_norm, w_out):
    rows = x_all.shape[0]
    tm = ROW_TILE
    dsk = jnp.repeat(d_skip[0] + d_skip[1], SSD_HEAD_DIM)[None, :]
    wof = w_out[:FNET_WIDTH].astype(BF16)
    wos = w_out[FNET_WIDTH:].astype(BF16)
    row = lambda n: pl.BlockSpec((tm, n), lambda i: (i, 0))
    return pl.pallas_call(
        _even_out_kernel,
        grid=(rows // tm,),
        in_specs=[row(D_MODEL),
                  pl.BlockSpec((1, 1, N_MOD * D_MODEL), lambda i: (_mod_row(i, tm), 0, 0)),
                  row(FNET_WIDTH), row(SSD_INNER), row(SSD_INNER), row(SSD_INNER), row(SSD_INNER),
                  _full((1, SSD_INNER)), _full((1, SSD_INNER)), _full(wof.shape), _full(wos.shape)],
        out_specs=row(D_MODEL),
        out_shape=jax.ShapeDtypeStruct((rows, D_MODEL), F32),
        compiler_params=_params("arbitrary"),
        name="even_out",
    )(x_all, mod, four, yf, yb, u, z, dsk, ssd_norm[None, :], wof, wos)


BF16_ROWS = 2 * SUBLANES
NOT_TOP = 64.0
AFTER_EVERY_ORDER = float(PEER_TOPK * PEER_KEYS)


def _erf_gelu(z):
    return 0.5 * z * (1.0 + lax.erf(z * (1.0 / math.sqrt(2.0))))


def _merge_exchange_pairs(n):
    t = max(1, math.ceil(math.log2(n)))
    p, pairs = 2 ** (t - 1), []
    while p > 0:
        q, r, d = 2 ** (t - 1), 0, p
        while d > 0:
            pairs += [(i, i + d) for i in range(n - d) if i & p == r]
            d, q, r = q - p, q // 2, p
        p //= 2
    return pairs


def _top_values_sorted(score, k):
    groups = [score[r * SUBLANES:(r + 1) * SUBLANES] for r in range(score.shape[0] // SUBLANES)]
    for a, b in _merge_exchange_pairs(len(groups)):
        groups[a], groups[b] = jnp.maximum(groups[a], groups[b]), jnp.minimum(groups[a], groups[b])
    vals = []
    for i in range(k):
        m = jnp.max(groups[0], axis=0, keepdims=True)
        vals.append(m)
        needed = k - 1 - i
        if needed == 0:
            break
        hit = groups[0] == m
        shifted = [jnp.where(hit, groups[d + 1], groups[d]) for d in range(len(groups) - 1)]
        if len(groups) <= needed:
            shifted.append(jnp.where(hit, NEG_INF, groups[-1]))
        groups = shifted[:needed]
    return jnp.concatenate(vals, axis=0)


def _rank_among(values, top, k):
    assert k & (k - 1) == 0
    bits, count, stride = [], jnp.zeros_like(values), k // 2
    while stride >= 1:
        cands = [top[p + stride - 1:p + stride] for p in range(0, k, 2 * stride)]
        for b in reversed(bits):
            cands = [jnp.where(b, cands[2 * i + 1], cands[2 * i]) for i in range(len(cands) // 2)]
        bit = cands[0] > values
        count = count + jnp.where(bit, float(stride), 0.0)
        bits.append(bit)
        stride //= 2
    return jnp.where(values >= top[k - 1:k], count, NOT_TOP)


def _pair_candidates(v1, v2, k):
    assert k == 2 * SUBLANES, "the row grouping below is laid out for k = 16"
    t = v1.shape[1]
    row8 = lax.broadcasted_iota(jnp.int32, (SUBLANES, t), 0)
    rowf = row8.astype(F32)
    sums = [v1[0:1] + v2[0:8], v1[0:1] + v2[8:16], v1[1:2] + v2[0:8], v1[8:16] + v2[0:1]]
    order = [rowf, rowf + 8.0, rowf + float(k), (rowf + 8.0) * float(k)]
    for i in range(2, 8):
        sums.append(jnp.where(row8 < k // (i + 1), v1[i:i + 1] + v2[0:8], NEG_INF))
        order.append(rowf + float(i * k))
    return jnp.concatenate(sums, axis=0), jnp.concatenate(order, axis=0)


def _select_by_value(s1, s2, k):
    v1 = _top_values_sorted(s1, k)
    v2 = _top_values_sorted(s2, k)
    r2 = _rank_among(s2, v2, k)
    cand, _ = _pair_candidates(v1, v2, k)
    tau = _top_values_sorted(cand, k)[k - 1:k, :]
    chosen = cand >= tau
    zsum = jnp.sum(jnp.where(chosen, jnp.exp(cand - (v1[0:1] + v2[0:1])), 0.0), axis=0, keepdims=True)
    in_top = s1 >= v1[k - 1:k]
    a1 = jnp.where(in_top, s1, NEG_INF)
    lim = jnp.zeros_like(s1)
    for j in range(k // 2):
        lim = lim + jnp.where(a1 + v2[j:j + 1] >= tau, 1.0, 0.0)
    best = jnp.zeros_like(tau)
    for j in range(k // 2, k):
        best = best + jnp.where(v1[0:1] + v2[j:j + 1] >= tau, 1.0, 0.0)
    lim = lim + jnp.where(s1 == v1[0:1], best, 0.0)
    count = lambda mask: jnp.sum(jnp.where(mask, 1.0, 0.0), axis=0, keepdims=True)
    most = jnp.maximum(jnp.maximum(count(in_top), count(r2 < k)), count(chosen))
    return lim, jnp.exp(s1 - v1[0:1]) / zsum, r2, jnp.exp(s2 - v2[0:1]), most


def _top_ranks_ordered(score, order, k):
    work, vals = score, []
    rank = jnp.full(score.shape, NOT_TOP, F32)
    for i in range(k):
        m = jnp.max(work, axis=0, keepdims=True)
        first = jnp.min(jnp.where(work == m, order, AFTER_EVERY_ORDER), axis=0, keepdims=True)
        taken = order == first
        vals.append(m)
        rank = jnp.where(taken, float(i), rank)
        work = jnp.where(taken, NEG_INF, work)
    return jnp.concatenate(vals, axis=0), rank


def _select_by_order(s1, s2, k):
    key_index = lax.broadcasted_iota(jnp.int32, s1.shape, 0).astype(F32)
    v1, r1 = _top_ranks_ordered(s1, key_index, k)
    v2, r2 = _top_ranks_ordered(s2, key_index, k)
    cand, position = _pair_candidates(v1, v2, k)
    chosen = _top_ranks_ordered(cand, position, k)[1] < k
    zsum = jnp.sum(jnp.where(chosen, jnp.exp(cand - (v1[0:1] + v2[0:1])), 0.0), axis=0, keepdims=True)
    picks = jnp.where(chosen, 1.0, 0.0)
    group = lambda g: jnp.sum(picks[g * SUBLANES:(g + 1) * SUBLANES], axis=0, keepdims=True)
    per_rank = [group(0) + group(1), group(2)] + [group(g) for g in range(4, 10)]
    per_rank = jnp.concatenate(per_rank + [picks[3 * SUBLANES:4 * SUBLANES]], axis=0)
    lim = jnp.zeros_like(s1)
    for i in range(k):
        lim = lim + jnp.where(r1 == float(i), per_rank[i:i + 1], 0.0)
    return lim, jnp.exp(s1 - v1[0:1]) / zsum, r2, jnp.exp(s2 - v2[0:1])


def _peer_kernel(x_ref, mod_ref, g_ref, wq_ref, keys_ref, u0_ref, *rest, tm, chunk, per_step):
    u_refs = rest[:per_step]
    (vt_ref, o_ref, ht_ref, q_ref, lim_ref, w1_ref, rank2_ref, w2_ref, acc_ref, za_ref, zb_ref) = rest[per_step:]
    c = pl.program_id(1)
    k = PEER_TOPK
    packed = (PEER_KEYS // BF16_ROWS, BF16_ROWS, tm)
    tsel = PEER_SELECT_TOKENS

    @pl.when(c == 0)
    def _prologue():
        h2 = _modulated_norm(x_ref[...], g_ref[...], mod_ref[0], 3, 4)
        ht = h2.T.astype(BF16)
        ht_ref[...] = ht
        q = jnp.dot(wq_ref[...], ht, preferred_element_type=F32)
        q_ref[...] = q.reshape(2 * PEER_HEADS, PEER_QDIM // 2, tm)
        acc_ref[...] = jnp.zeros_like(acc_ref)
        za_ref[...] = jnp.dot(u0_ref[...], ht, preferred_element_type=F32)
        def head_body(h, carry):
            def scores(part):
                ts = slice(part * tsel, (part + 1) * tsel)
                s1 = jnp.dot(keys_ref[2 * h], q_ref[2 * h, :, ts].astype(BF16), preferred_element_type=F32)
                s2 = jnp.dot(keys_ref[2 * h + 1], q_ref[2 * h + 1, :, ts].astype(BF16), preferred_element_type=F32)
                return ts, s1, s2

            def store(ts, lim, w1, r2, w2):
                small = (PEER_KEYS // BF16_ROWS, BF16_ROWS, tsel)
                lim_ref[h, :, ts] = lim
                w1_ref[h, :, ts] = w1
                rank2_ref[h, :, :, ts] = r2.reshape(small).astype(BF16)
                w2_ref[h, :, :, ts] = w2.reshape(small).astype(BF16)

            most = None
            for part in range(tm // tsel):
                ts, s1, s2 = scores(part)
                lim, w1, r2, w2, count = _select_by_value(s1, s2, k)
                store(ts, lim, w1, r2, w2)
                most = count if most is None else jnp.maximum(most, count)

            @pl.when(jnp.max(most) > k)
            def _ties():
                for part in range(tm // tsel):
                    ts, s1, s2 = scores(part)
                    store(ts, *_select_by_order(s1, s2, k))

            return carry

        lax.fori_loop(0, PEER_HEADS, head_body, 0)

    per_key = PEER_KEYS // BF16_ROWS
    keys_per_sub = PEER_SUB // PEER_KEYS
    step_keys = per_step * chunk // PEER_KEYS
    key0 = pl.multiple_of(c * step_keys, step_keys)
    step_lim = [lim_ref[h, pl.ds(key0, step_keys), :] for h in range(PEER_HEADS)]
    step_w1 = [w1_ref[h, pl.ds(key0, step_keys), :] for h in range(PEER_HEADS)]

    def gated_values(z_ref, which):
        total = None
        for j in range(chunk // PEER_SUB):
            rows = slice(j * PEER_SUB, (j + 1) * PEER_SUB)
            act = _erf_gelu(z_ref[rows, :]).reshape(PEER_SUB // BF16_ROWS, BF16_ROWS, tm).astype(BF16)
            parts = []
            for jj in range(keys_per_sub):
                e1 = which * (chunk // PEER_KEYS) + j * keys_per_sub + jj
                gate = jnp.zeros(packed, BF16)
                for h in range(PEER_HEADS):
                    lim = jnp.broadcast_to(step_lim[h][e1:e1 + 1, :], (BF16_ROWS, tm)).astype(BF16)
                    w1 = jnp.broadcast_to(step_w1[h][e1:e1 + 1, :], (BF16_ROWS, tm)).astype(BF16)
                    picked = jnp.where(rank2_ref[h] < lim[None], w2_ref[h], jnp.zeros(packed, BF16))
                    gate = gate + picked * w1[None]
                parts.append(act[jj * per_key:(jj + 1) * per_key] * gate)
            a = jnp.concatenate(parts, axis=0).reshape(PEER_SUB, tm)
            cols = slice(which * chunk + j * PEER_SUB, which * chunk + (j + 1) * PEER_SUB)
            d = jnp.dot(vt_ref[:, cols], a, preferred_element_type=F32)
            total = d if total is None else total + d
        return total

    bufs = (za_ref, zb_ref)
    total = None
    for i in range(per_step):
        bufs[(i + 1) % 2][...] = jnp.dot(u_refs[i][...], ht_ref[...], preferred_element_type=F32)
        d = gated_values(bufs[i % 2], i)
        total = d if total is None else total + d
    acc_ref[...] += total

    @pl.when(c == pl.num_programs(1) - 1)
    def _epilogue():
        gate2 = mod_ref[0][:, 5 * D_MODEL:6 * D_MODEL]
        o_ref[...] = x_ref[...] + gate2 * acc_ref[...].T


def peer_layer(x_rows, mod, norm2, peer_wq, peer_keys, peer_u, peer_v):
    rows = x_rows.shape[0]
    tm, chunk = PEER_TILE, PEER_CHUNK
    wq_t = peer_wq.T.astype(BF16)
    keys = peer_keys.reshape(2 * PEER_HEADS, PEER_KEYS, PEER_QDIM // 2).astype(BF16)
    u = peer_u.astype(BF16)
    vt = peer_v.astype(BF16).T
    head_f32 = pltpu.VMEM((PEER_HEADS, PEER_KEYS, tm), F32)
    head_bf16 = pltpu.VMEM((PEER_HEADS, PEER_KEYS // BF16_ROWS, BF16_ROWS, tm), BF16)
    n_chunks = PEER_EXPERTS // chunk
    per_step = PEER_CHUNKS_PER_STEP
    ahead = lambda k: pl.BlockSpec((chunk, D_MODEL), lambda i, c: (jnp.minimum(per_step * c + k, n_chunks - 1), 0))
    return pl.pallas_call(
        functools.partial(_peer_kernel, tm=tm, chunk=chunk, per_step=per_step),
        grid=(rows // tm, n_chunks // per_step),
        in_specs=[pl.BlockSpec((tm, D_MODEL), lambda i, c: (i, 0)),
                  pl.BlockSpec((1, 1, N_MOD * D_MODEL), lambda i, c: (_mod_row(i, tm), 0, 0)),
                  _full((1, D_MODEL)), _full(wq_t.shape), _full(keys.shape),
                  pl.BlockSpec((chunk, D_MODEL), lambda i, c: (0, 0))]
                 + [ahead(k) for k in range(1, per_step + 1)]
                 + [pl.BlockSpec((D_MODEL, per_step * chunk), lambda i, c: (0, c))],
        out_specs=pl.BlockSpec((tm, D_MODEL), lambda i, c: (i, 0)),
        out_shape=jax.ShapeDtypeStruct((rows, D_MODEL), F32),
        scratch_shapes=[pltpu.VMEM((D_MODEL, tm), BF16),
                        pltpu.VMEM((2 * PEER_HEADS, PEER_QDIM // 2, tm), F32),
                        head_f32, head_f32, head_bf16, head_bf16,
                        pltpu.VMEM((D_MODEL, tm), F32),
                        pltpu.VMEM((chunk, tm), F32), pltpu.VMEM((chunk, tm), F32)],
        compiler_params=_params("arbitrary", "arbitrary"),
        name="peer",
    )(x_rows, mod, norm2[None, :], wq_t, keys, u, *([u] * per_step), vt)


def _rope_tables(rows):
    l = jnp.arange(SEQ)
    per_axis = MLA_ROPE // 2
    inv = ROPE_THETA ** (-jnp.arange(0, per_axis, 2, dtype=F32) / per_axis)
    ang = jnp.concatenate([(l // GRID_W)[:, None] * inv, (l % GRID_W)[:, None] * inv], axis=-1)
    cos, sin = jnp.cos(ang), jnp.sin(ang)
    one = jnp.ones((SEQ, MLA_NOPE), F32)
    tail = jnp.zeros((SEQ, HEAD_PAD - MLA_QK), F32)
    zero64 = jnp.zeros((SEQ, MLA_NOPE), F32)
    cos_t = jnp.concatenate([one, cos, cos, tail], axis=1)
    sin_t = jnp.concatenate([zero64, -sin, sin, tail], axis=1)
    n_ctx = rows - LAT_ROWS
    ident = jnp.concatenate([jnp.ones((n_ctx, MLA_QK), F32), jnp.zeros((n_ctx, HEAD_PAD - MLA_QK), F32)], axis=1)
    zeros = jnp.zeros((n_ctx, HEAD_PAD), F32)
    tile = lambda t, c: jnp.concatenate([t] * BATCH + [c], axis=0)
    return tile(cos_t, ident), tile(sin_t, zeros)


def _rope_swap_matrix():
    per_axis = MLA_ROPE // 2
    swap = np.zeros((HEAD_PAD, HEAD_PAD), np.float32)
    for j in range(per_axis):
        swap[MLA_NOPE + per_axis + j, MLA_NOPE + j] = 1.0
        swap[MLA_NOPE + j, MLA_NOPE + per_axis + j] = 1.0
    return swap


def _head_norm_rope(t, gain, cos, sin, ones, swap):
    ms = jnp.dot((t * t).astype(BF16), ones, preferred_element_type=F32) * (1.0 / MLA_QK)
    t = t * lax.rsqrt(ms + EPS) * gain
    return t * cos + jnp.dot(t.astype(BF16), swap, preferred_element_type=F32) * sin


def _mla_in_kernel(x_ref, mod_ref, g_ref, wdq_ref, wdkv_ref, wpe_ref, qan_ref, kvan_ref, wuq_ref, wuk_ref, wuv_ref,
                   qn_ref, kn_ref, ones_ref, swap_ref, cos_ref, sin_ref, q_ref, k_ref, v_ref):
    h = _modulated_norm(x_ref[...], g_ref[...], mod_ref[0], 0, 1).astype(BF16)
    dq = jnp.dot(h, wdq_ref[...], preferred_element_type=F32)
    dkv = jnp.dot(h, wdkv_ref[...], preferred_element_type=F32)
    kpe = jnp.dot(h, wpe_ref[...], preferred_element_type=F32)
    qa = dq * lax.rsqrt(jnp.mean(dq * dq, axis=-1, keepdims=True) + EPS) * qan_ref[...]
    kva = (dkv * lax.rsqrt(jnp.mean(dkv * dkv, axis=-1, keepdims=True) + EPS) * kvan_ref[...]).astype(BF16)
    q = jnp.dot(qa.astype(BF16), wuq_ref[...], preferred_element_type=F32)
    kn = jnp.dot(kva, wuk_ref[...], preferred_element_type=F32)
    v_ref[...] = jnp.dot(kva, wuv_ref[...], preferred_element_type=F32).astype(BF16)
    cos, sin, ones, swap = cos_ref[...], sin_ref[...], ones_ref[...], swap_ref[...]
    scale = MLA_QK ** -0.5 * math.log2(math.e)
    for hd in range(MLA_HEADS):
        hs = slice(hd * HEAD_PAD, (hd + 1) * HEAD_PAD)
        q_ref[:, hs] = (_head_norm_rope(q[:, hs], qn_ref[...], cos, sin, ones, swap) * scale).astype(BF16)
        k_ref[:, hs] = _head_norm_rope(kn[:, hs] + kpe, kn_ref[...], cos, sin, ones, swap).astype(BF16)


def mla_in(x_all, mod, p):
    rows = x_all.shape[0]
    tm = ROW_TILE
    w = p["w_dqkv"]
    wdq = w[:, :MLA_Q_LORA].astype(BF16)
    wdkv = w[:, MLA_Q_LORA:MLA_Q_LORA + MLA_KV_LORA].astype(BF16)
    wpe = jnp.pad(w[:, MLA_Q_LORA + MLA_KV_LORA:], ((0, 0), (MLA_NOPE, HEAD_PAD - MLA_QK))).astype(BF16)
    wuq = jnp.pad(p["w_uq"].reshape(MLA_Q_LORA, MLA_HEADS, MLA_QK), ((0, 0), (0, 0), (0, HEAD_PAD - MLA_QK)))
    wuq = wuq.reshape(MLA_Q_LORA, MLA_HEADS * HEAD_PAD).astype(BF16)
    wukv = p["w_ukv"].reshape(MLA_KV_LORA, MLA_HEADS, MLA_NOPE + MLA_V)
    wuk = jnp.pad(wukv[:, :, :MLA_NOPE], ((0, 0), (0, 0), (0, HEAD_PAD - MLA_NOPE)))
    wuk = wuk.reshape(MLA_KV_LORA, MLA_HEADS * HEAD_PAD).astype(BF16)
    wuv = wukv[:, :, MLA_NOPE:].reshape(MLA_KV_LORA, MLA_HEADS * MLA_V).astype(BF16)
    padg = lambda g: jnp.pad(g, (0, HEAD_PAD - MLA_QK))[None, :]
    cos, sin = _rope_tables(rows)
    row = lambda n: pl.BlockSpec((tm, n), lambda i: (i, 0))
    weights = [wdq, wdkv, wpe, p["q_a_norm"][None, :], p["kv_a_norm"][None, :], wuq, wuk, wuv,
               padg(p["q_norm"]), padg(p["k_norm"]),
               jnp.ones((HEAD_PAD, HEAD_PAD), BF16), jnp.asarray(_rope_swap_matrix(), BF16)]
    return pl.pallas_call(
        _mla_in_kernel,
        grid=(rows // tm,),
        in_specs=[row(D_MODEL),
                  pl.BlockSpec((1, 1, N_MOD * D_MODEL), lambda i: (_mod_row(i, tm), 0, 0)),
                  _full((1, D_MODEL))] + [_full(a.shape) for a in weights] + [row(HEAD_PAD)] * 2,
        out_specs=[row(MLA_HEADS * HEAD_PAD), row(MLA_HEADS * HEAD_PAD), row(MLA_HEADS * MLA_V)],
        out_shape=[jax.ShapeDtypeStruct((rows, MLA_HEADS * HEAD_PAD), BF16),
                   jax.ShapeDtypeStruct((rows, MLA_HEADS * HEAD_PAD), BF16),
                   jax.ShapeDtypeStruct((rows, MLA_HEADS * MLA_V), BF16)],
        compiler_params=_params("arbitrary"),
        name="mla_in",
    )(x_all, mod, p["norm1"][None, :], *weights, cos, sin)


ATTN_SAFE_SHIFT = 50.0
ATTN_BOUND_SLACK = 1.001
ATTN_BOUND_MARGIN = 1e-3


def _attn_kernel(q_ref, kc_ref, kl_ref, vc_ref, vl_ref, o_ref, knorm_ref, *, tq, tk):
    heads = (slice(0, HEAD_PAD), slice(HEAD_PAD, 2 * HEAD_PAD))
    qs = [q_ref[:, hs] for hs in heads]
    n_lat = SEQ // tk

    @pl.when(pl.program_id(2) == 0)
    def _key_norms():
        def sq_max(k2, hh):
            kf = k2[:, heads[hh]].astype(F32)
            return jnp.max(jnp.sum(kf * kf, axis=-1, keepdims=True), axis=0, keepdims=True)

        for hh in range(2):
            best = sq_max(kc_ref[...], hh)
            best = lax.fori_loop(
                0, n_lat, lambda j, b: jnp.maximum(b, sq_max(kl_ref[pl.ds(pl.multiple_of(j * tk, tk), tk), :], hh)), best)
            knorm_ref[hh] = jnp.broadcast_to(jnp.sqrt(best), (SUBLANES, LANES))

    def values(v, hh):
        lane = lax.broadcasted_iota(jnp.int32, v.shape, 1)
        own = (lane < MLA_V) if hh == 0 else (lane >= MLA_V)
        return jnp.where(own, v, jnp.ones_like(v))

    def scores(hh, k2):
        return lax.dot_general(qs[hh], k2[:, heads[hh]], (((1,), (1,)), ((), ())), preferred_element_type=F32)

    def finish(accs):
        lane = lax.broadcasted_iota(jnp.int32, (tq, 2 * MLA_V), 1)
        outs = [acc / pltpu.roll(acc, MLA_V, 1) for acc in accs]
        o_ref[...] = jnp.where(lane < MLA_V, outs[0], outs[1]).astype(o_ref.dtype)

    def lat_chunk(j):
        off = pl.multiple_of(j * tk, tk)
        return kl_ref[pl.ds(off, tk), :], vl_ref[pl.ds(off, tk), :]

    bounds = []
    for hh in range(2):
        qf = qs[hh].astype(F32)
        qn = jnp.sqrt(jnp.sum(qf * qf, axis=-1, keepdims=True))
        bounds.append(qn * knorm_ref[hh, 0:1, 0:1] * ATTN_BOUND_SLACK + ATTN_BOUND_MARGIN)
    widest = jnp.max(jnp.maximum(bounds[0], bounds[1]))

    def fixed_shift():
        def step(k2, v, accs):
            return tuple(accs[hh] + jnp.dot(jnp.exp2(scores(hh, k2) - bounds[hh]).astype(BF16), values(v, hh),
                                            preferred_element_type=F32) for hh in range(2))

        zero = jnp.zeros((tq, 2 * MLA_V), F32)
        accs = step(kc_ref[...], vc_ref[...], (zero, zero))
        finish(lax.fori_loop(0, n_lat, lambda j, a: step(*lat_chunk(j), a), accs, unroll=8))

    def running_max():
        def step(k2, v, carry):
            new = []
            for hh in range(2):
                m, acc = carry[hh]
                s = scores(hh, k2)
                m_new = jnp.maximum(m, jnp.max(s, axis=-1, keepdims=True))
                p = jnp.exp2(s - m_new).astype(BF16)
                acc = jnp.exp2(m - m_new) * acc + jnp.dot(p, values(v, hh), preferred_element_type=F32)
                new.append((m_new, acc))
            return tuple(new)

        init = tuple((jnp.full((tq, 1), NEG_INF, F32), jnp.zeros((tq, 2 * MLA_V), F32)) for _ in range(2))
        carry = step(kc_ref[...], vc_ref[...], init)
        carry = lax.fori_loop(0, n_lat, lambda j, c: step(*lat_chunk(j), c), carry)
        finish([acc for _, acc in carry])

    lax.cond(widest <= ATTN_SAFE_SHIFT, fixed_shift, running_max)


def attention(q, k, v):
    tq, tk = ATTN_TQ, ATTN_TK
    nq = SEQ // tq
    ctx0 = LAT_ROWS // CTX_LEN
    return pl.pallas_call(
        functools.partial(_attn_kernel, tq=tq, tk=tk),
        grid=(BATCH, MLA_HEADS // 2, nq),
        in_specs=[pl.BlockSpec((tq, 2 * HEAD_PAD), lambda b, h, i: (b * nq + i, h)),
                  pl.BlockSpec((CTX_LEN, 2 * HEAD_PAD), lambda b, h, i: (ctx0 + b, h)),
                  pl.BlockSpec((SEQ, 2 * HEAD_PAD), lambda b, h, i: (b, h)),
                  pl.BlockSpec((CTX_LEN, 2 * MLA_V), lambda b, h, i: (ctx0 + b, h)),
                  pl.BlockSpec((SEQ, 2 * MLA_V), lambda b, h, i: (b, h))],
        out_specs=pl.BlockSpec((tq, 2 * MLA_V), lambda b, h, i: (b * nq + i, h)),
        out_shape=jax.ShapeDtypeStruct((LAT_ROWS, MLA_HEADS * MLA_V), BF16),
        scratch_shapes=[pltpu.VMEM((2, SUBLANES, LANES), F32)],
        compiler_params=_params("arbitrary", "arbitrary", "arbitrary"),
        name="attention",
    )(q, k, k, v, v)


def _attn_out_kernel(x_ref, mod_ref, o_ref_in, wo_ref, out_ref):
    gate = mod_ref[0][:, 2 * D_MODEL:3 * D_MODEL]
    out_ref[...] = x_ref[...] + gate * jnp.dot(o_ref_in[...], wo_ref[...], preferred_element_type=F32)


def attn_out(x_all, mod, o, w_o):
    tm = ROW_TILE
    row = lambda n: pl.BlockSpec((tm, n), lambda i: (i, 0))
    return pl.pallas_call(
        _attn_out_kernel,
        grid=(LAT_ROWS // tm,),
        in_specs=[row(D_MODEL), pl.BlockSpec((1, 1, N_MOD * D_MODEL), lambda i: (_mod_row(i, tm), 0, 0)),
                  row(MLA_HEADS * MLA_V), _full(w_o.shape)],
        out_specs=row(D_MODEL),
        out_shape=jax.ShapeDtypeStruct((LAT_ROWS, D_MODEL), F32),
        compiler_params=_params("arbitrary"),
        name="attn_out",
    )(x_all, mod, o, w_o.astype(BF16))


def kernel(x, c, ctx, c_ctx, l0_w_ada, l0_b_ada, l0_norm1, l0_w_in, l0_conv_w, l0_conv_b, l0_dt_bias, l0_a_log, l0_d_skip, l0_ssd_norm, l0_w_out, l0_norm2, l0_peer_wq, l0_peer_keys, l0_peer_u, l0_peer_v, l1_w_ada, l1_b_ada, l1_norm1, l1_w_dqkv, l1_q_a_norm, l1_kv_a_norm, l1_w_uq, l1_w_ukv, l1_q_norm, l1_k_norm, l1_w_o, l1_norm2, l1_peer_wq, l1_peer_keys, l1_peer_u, l1_peer_v):
    x_all = jnp.concatenate([x.reshape(LAT_ROWS, D_MODEL), ctx.reshape(BATCH * CTX_LEN, D_MODEL)], axis=0)

    mod0 = ada_table(c, c_ctx, l0_w_ada, l0_b_ada)
    f, z, xbc, dt = even_in(x_all, mod0, l0_norm1, l0_w_in)
    u = conv_silu(xbc, l0_conv_w, l0_conv_b)
    yf, yb = ssd_scan(u, dt, l0_dt_bias, l0_a_log)
    four = fourier_mix(f)
    x_all = even_out(x_all, mod0, four, yf, yb, u, z, l0_d_skip, l0_ssd_norm, l0_w_out)
    x_all = peer_layer(x_all, mod0, l0_norm2, l0_peer_wq, l0_peer_keys, l0_peer_u, l0_peer_v)

    mod1 = ada_table(c, c_ctx, l1_w_ada, l1_b_ada)
    p1 = dict(norm1=l1_norm1, w_dqkv=l1_w_dqkv, q_a_norm=l1_q_a_norm, kv_a_norm=l1_kv_a_norm, w_uq=l1_w_uq,
              w_ukv=l1_w_ukv, q_norm=l1_q_norm, k_norm=l1_k_norm)
    q, k, v = mla_in(x_all, mod1, p1)
    o = attention(q, k, v)
    x_lat = attn_out(x_all, mod1, o, l1_w_o)
    x_lat = peer_layer(x_lat, mod1, l1_norm2, l1_peer_wq, l1_peer_keys, l1_peer_u, l1_peer_v)
    return x_lat.reshape(BATCH, SEQ, D_MODEL)
```

```python
import functools
import math

import numpy as np
import jax
import jax.numpy as jnp
from jax import lax
from jax.experimental import pallas as pl
from jax.experimental.pallas import tpu as pltpu

D_MODEL = 1024
BATCH = 2
SEQ = 8192
GRID_W = 64
CTX_LEN = 256
EPS = 1e-6
N_MOD = 6
LAT_ROWS = BATCH * SEQ
ALL_ROWS = LAT_ROWS + BATCH * CTX_LEN

FNET_GROUPS = 4
FNET_GROUP_DIM = 128
FNET_WIDTH = FNET_GROUPS * FNET_GROUP_DIM
FFT_N1 = 64
FFT_N2 = 128

SSD_HEADS = 16
SSD_HEAD_DIM = 64
SSD_INNER = SSD_HEADS * SSD_HEAD_DIM
SSD_GROUPS = 4
SSD_STATE = 128
SSD_CONV = 5
SSD_CHUNK = 128
SSD_CONV_DIM = SSD_INNER + 2 * SSD_GROUPS * SSD_STATE
SSD_GROUP_W = SSD_INNER // SSD_GROUPS

MLA_HEADS = 16
MLA_NOPE = 64
MLA_ROPE = 32
MLA_QK = MLA_NOPE + MLA_ROPE
MLA_V = 64
MLA_Q_LORA = 384
MLA_KV_LORA = 256
ROPE_THETA = 10000.0
HEAD_PAD = 128

PEER_HEADS = 8
PEER_KEYS = 128
PEER_EXPERTS = PEER_KEYS * PEER_KEYS
PEER_QDIM = 256
PEER_TOPK = 16

LANES = 128
SUBLANES = 8
VMEM_LIMIT = 56 * 1024 * 1024

ROW_TILE = 512
CONV_TILE = 256
PEER_TILE = 512
PEER_CHUNK = 512
PEER_CHUNKS_PER_STEP = 2
PEER_SUB = 256
PEER_SELECT_TOKENS = 128
ATTN_TQ = 1024
ATTN_TK = 512
ADA_COLS = 512
FFT1_COLS = 8192

F32 = jnp.float32
BF16 = jnp.bfloat16
HIGHEST = lax.Precision.HIGHEST
NEG_INF = float("-inf")


def _params(*sem):
    return pltpu.CompilerParams(dimension_semantics=sem, vmem_limit_bytes=VMEM_LIMIT)


def _mod_row(i, tile):
    return jnp.minimum((i * tile) // SEQ, BATCH)


def _full(shape):
    return pl.BlockSpec(shape, lambda *_: (0,) * len(shape))


def _silu(x):
    return x * (1.0 / (1.0 + jnp.exp(-x)))


def _modulated_norm(x, gain, mod, k_shift, k_scale):
    shift = mod[:, k_shift * D_MODEL:(k_shift + 1) * D_MODEL]
    scale = mod[:, k_scale * D_MODEL:(k_scale + 1) * D_MODEL]
    ms = jnp.mean(x * x, axis=-1, keepdims=True)
    return x * lax.rsqrt(ms + EPS) * gain * (1.0 + scale) + shift


def _hdot(a, b):
    return jnp.dot(a, b, precision=HIGHEST, preferred_element_type=F32)


def _bf16_terms(x, n):
    terms = []
    for _ in range(n):
        t = x.astype(BF16)
        terms.append(t)
        x = x - t.astype(F32)
    return terms


def _dot_exact_rhs(a, b01, terms=3):
    return sum(jnp.dot(t, b01, preferred_element_type=F32) for t in _bf16_terms(a, terms))


def _dot_exact_lhs(a01, b):
    return sum(jnp.dot(a01, t, preferred_element_type=F32) for t in _bf16_terms(b, 3))


def _dot3(a, b):
    a_hi, a_lo = _bf16_terms(a, 2)
    b_hi, b_lo = _bf16_terms(b, 2)
    dot = lambda x, y: jnp.dot(x, y, preferred_element_type=F32)
    return dot(a_hi, b_hi) + (dot(a_hi, b_lo) + dot(a_lo, b_hi))


def _ada_kernel(c_ref, w_ref, b_ref, o_ref):
    o_ref[...] = _hdot(_silu(c_ref[...]), w_ref[...]) + b_ref[...]


def ada_table(c, c_ctx, w_ada, b_ada):
    cond = jnp.concatenate([c, c_ctx[None, :], jnp.zeros((SUBLANES - BATCH - 1, D_MODEL), F32)], axis=0)
    tn = ADA_COLS
    out = pl.pallas_call(
        _ada_kernel,
        grid=(N_MOD * D_MODEL // tn,),
        in_specs=[_full((SUBLANES, D_MODEL)),
                  pl.BlockSpec((D_MODEL, tn), lambda j: (0, j)),
                  pl.BlockSpec((1, tn), lambda j: (0, j))],
        out_specs=pl.BlockSpec((SUBLANES, tn), lambda j: (0, j)),
        out_shape=jax.ShapeDtypeStruct((SUBLANES, N_MOD * D_MODEL), F32),
        compiler_params=_params("arbitrary"),
        name="ada_table",
    )(cond, w_ada, b_ada[None, :])
    return out[:BATCH + 1].reshape(BATCH + 1, 1, N_MOD * D_MODEL)


def _even_in_kernel(x_ref, mod_ref, g_ref, wf_ref, wz_ref, wx_ref, wd_ref, f_ref, z_ref, xbc_ref, dt_ref):
    h = _modulated_norm(x_ref[...], g_ref[...], mod_ref[0], 0, 1).astype(BF16)
    f_ref[...] = jnp.dot(h, wf_ref[...], preferred_element_type=F32)
    z_ref[...] = jnp.dot(h, wz_ref[...], preferred_element_type=F32)
    xbc_ref[...] = jnp.dot(h, wx_ref[...], preferred_element_type=F32)
    dt_ref[...] = jnp.dot(h, wd_ref[...], preferred_element_type=F32)


def even_in(x_all, mod, norm1, w_in):
    rows = x_all.shape[0]
    o1, o2, o3 = FNET_WIDTH, FNET_WIDTH + SSD_INNER, FNET_WIDTH + SSD_INNER + SSD_CONV_DIM
    wf = w_in[:, :o1].astype(BF16)
    wz = w_in[:, o1:o2].astype(BF16)
    wx = w_in[:, o2:o3].astype(BF16)
    wd = jnp.pad(w_in[:, o3:], ((0, 0), (0, LANES - 2 * SSD_HEADS))).astype(BF16)
    tm = ROW_TILE
    row = lambda n: pl.BlockSpec((tm, n), lambda i: (i, 0))
    return pl.pallas_call(
        _even_in_kernel,
        grid=(rows // tm,),
        in_specs=[row(D_MODEL),
                  pl.BlockSpec((1, 1, N_MOD * D_MODEL), lambda i: (_mod_row(i, tm), 0, 0)),
                  _full((1, D_MODEL)),
                  _full(wf.shape), _full(wz.shape), _full(wx.shape), _full(wd.shape)],
        out_specs=[row(FNET_WIDTH), row(SSD_INNER), row(SSD_CONV_DIM), row(LANES)],
        out_shape=[jax.ShapeDtypeStruct((rows, FNET_WIDTH), F32),
                   jax.ShapeDtypeStruct((rows, SSD_INNER), F32),
                   jax.ShapeDtypeStruct((rows, SSD_CONV_DIM), F32),
                   jax.ShapeDtypeStruct((rows, LANES), F32)],
        compiler_params=_params("arbitrary"),
        name="even_in",
    )(x_all, mod, norm1[None, :], wf, wz, wx, wd)


def _conv_kernel(x_ref, prev_ref, next_ref, w_ref, b_ref, o_ref, ext_ref, *, tm):
    row0 = pl.program_id(0) * tm
    in_lat = row0 < LAT_ROWS
    first = jnp.where(in_lat, row0 % SEQ == 0, (row0 - LAT_ROWS) % CTX_LEN == 0)
    last = jnp.where(in_lat, (row0 + tm) % SEQ == 0, (row0 + tm - LAT_ROWS) % CTX_LEN == 0)
    ext_ref[0:SUBLANES, :] = prev_ref[...] * jnp.where(first, 0.0, 1.0)
    ext_ref[SUBLANES:SUBLANES + tm, :] = x_ref[...]
    ext_ref[SUBLANES + tm:2 * SUBLANES + tm, :] = next_ref[...] * jnp.where(last, 0.0, 1.0)
    pad = SSD_CONV // 2
    acc = b_ref[...] + w_ref[0:1, :] * ext_ref[pl.ds(SUBLANES - pad, tm), :]
    for k in range(1, SSD_CONV):
        acc = acc + w_ref[k:k + 1, :] * ext_ref[pl.ds(SUBLANES - pad + k, tm), :]
    o_ref[...] = _silu(acc)


def conv_silu(xbc, conv_w, conv_b):
    rows, ch = xbc.shape
    tm, tc = CONV_TILE, ch
    assert CTX_LEN % tm == 0 and SEQ % tm == 0
    per = tm // SUBLANES
    n_small = rows // SUBLANES
    w = jnp.pad(conv_w, ((0, SUBLANES - SSD_CONV), (0, 0)))
    return pl.pallas_call(
        functools.partial(_conv_kernel, tm=tm),
        grid=(rows // tm, ch // tc),
        in_specs=[pl.BlockSpec((tm, tc), lambda i, j: (i, j)),
                  pl.BlockSpec((SUBLANES, tc), lambda i, j: (jnp.maximum(i * per - 1, 0), j)),
                  pl.BlockSpec((SUBLANES, tc), lambda i, j: (jnp.minimum((i + 1) * per, n_small - 1), j)),
                  pl.BlockSpec((SUBLANES, tc), lambda i, j: (0, j)),
                  pl.BlockSpec((1, tc), lambda i, j: (0, j))],
        out_specs=pl.BlockSpec((tm, tc), lambda i, j: (i, j)),
        out_shape=jax.ShapeDtypeStruct((rows, ch), F32),
        scratch_shapes=[pltpu.VMEM((tm + 2 * SUBLANES, tc), F32)],
        compiler_params=_params("arbitrary", "arbitrary"),
        name="conv_silu",
    )(xbc, xbc, xbc, w, conv_b[None, :])


def _softplus(x):
    return jnp.maximum(x, 0.0) + jnp.log1p(jnp.exp(-jnp.abs(x)))


def _ssd_direction(u_ref, dt_ref, dtb_ref, alog_ref, exp_ref, state_ref, y_ref, direction):
    cl = SSD_CHUNK
    xs = u_ref[:, :SSD_INNER]
    dtv = _softplus(dt_ref[...] + dtb_ref[...])
    dta = dtv * (-jnp.exp(alog_ref[...]))
    r = lax.broadcasted_iota(jnp.int32, (cl, cl), 0)
    c = lax.broadcasted_iota(jnp.int32, (cl, cl), 1)
    tri = (r >= c) if direction == 0 else (r <= c)
    cs = _dot_exact_lhs(tri.astype(F32).astype(BF16), dta)
    cs_t = cs.T
    edge = cs[cl - 1:cl, :] if direction == 0 else cs[0:1, :]
    to_end = jnp.exp(edge - cs)
    from_start = jnp.exp(cs)
    spread = _dot_exact_rhs(jnp.concatenate([dtv, to_end, from_start], axis=0), exp_ref[direction], terms=2)
    dt_x, te_x, fs_x = spread[:cl], spread[cl:2 * cl], spread[2 * cl:]
    chunk_decay = fs_x[cl - 1:cl, :] if direction == 0 else fs_x[0:1, :]
    xd = xs * dt_x
    xd_b = xd.astype(BF16)
    xte_b = (xd * te_x).astype(BF16)
    lane = lax.broadcasted_iota(jnp.int32, (cl, LANES), 1)
    low = lane < SSD_HEAD_DIM
    for g in range(SSD_GROUPS):
        bg = u_ref[:, SSD_INNER + g * SSD_STATE:SSD_INNER + (g + 1) * SSD_STATE]
        cg = u_ref[:, SSD_INNER + (SSD_GROUPS + g) * SSD_STATE:SSD_INNER + (SSD_GROUPS + g + 1) * SSD_STATE]
        bg_b, cg_b = bg.astype(BF16), cg.astype(BF16)
        cb = lax.dot_general(cg_b, bg_b, (((1,), (1,)), ((), ())), preferred_element_type=F32)
        gs = slice(g * SSD_GROUP_W, (g + 1) * SSD_GROUP_W)
        state = state_ref[g]
        y_off = jnp.dot(cg_b, state.astype(BF16), preferred_element_type=F32) * fs_x[:, gs]
        new_state = jnp.dot(bg.T.astype(BF16), xte_b[:, gs], preferred_element_type=F32)
        state_ref[g] = state * chunk_decay[:, gs] + new_state
        for pair in range(2):
            blk = xd_b[:, g * SSD_GROUP_W + pair * LANES:g * SSD_GROUP_W + (pair + 1) * LANES]
            ms, halves = [], []
            for sub in range(2):
                col = direction * SSD_HEADS + g * 4 + pair * 2 + sub
                seg = cs[:, col:col + 1] - cs_t[col:col + 1, :]
                decay = jnp.exp(jnp.where(tri, seg, NEG_INF))
                ms.append((cb * decay).astype(BF16))
                halves.append(jnp.where(low if sub == 0 else jnp.logical_not(low), blk, jnp.zeros_like(blk)))
            y_pair = jnp.dot(jnp.concatenate(ms, axis=1), jnp.concatenate(halves, axis=0),
                             preferred_element_type=F32)
            lo = g * SSD_GROUP_W + pair * LANES
            y_ref[:, lo:lo + LANES] = y_pair + y_off[:, pair * LANES:(pair + 1) * LANES]


def _ssd_kernel(uf_ref, ub_ref, dtf_ref, dtb_in_ref, bias_ref, alog_ref, exp_ref, yf_ref, yb_ref, sf_ref, sb_ref):
    @pl.when(pl.program_id(1) == 0)
    def _():
        sf_ref[...] = jnp.zeros_like(sf_ref)
        sb_ref[...] = jnp.zeros_like(sb_ref)

    _ssd_direction(uf_ref, dtf_ref, bias_ref, alog_ref, exp_ref, sf_ref, yf_ref, 0)
    _ssd_direction(ub_ref, dtb_in_ref, bias_ref, alog_ref, exp_ref, sb_ref, yb_ref, 1)


def ssd_scan(u, dt, dt_bias, a_log):
    rows = u.shape[0]
    cl = SSD_CHUNK
    lat_chunks, ctx_chunks = SEQ // cl, CTX_LEN // cl
    steps = ctx_chunks + lat_chunks
    ctx0 = LAT_ROWS // cl

    def fwd_chunk(b, t):
        return jnp.where(t < ctx_chunks, ctx0 + b * ctx_chunks + t, b * lat_chunks + t - ctx_chunks)

    def bwd_chunk(b, t):
        return jnp.where(t < ctx_chunks, ctx0 + b * ctx_chunks + (ctx_chunks - 1 - t),
                         b * lat_chunks + (lat_chunks - 1) - (t - ctx_chunks))

    pad = LANES - 2 * SSD_HEADS
    bias = jnp.pad(dt_bias.reshape(1, -1), ((0, 0), (0, pad)))
    alog = jnp.pad(a_log.reshape(1, -1), ((0, 0), (0, pad)))
    expand = np.zeros((2, LANES, SSD_INNER), np.float32)
    for d in range(2):
        for h in range(SSD_HEADS):
            expand[d, d * SSD_HEADS + h, h * SSD_HEAD_DIM:(h + 1) * SSD_HEAD_DIM] = 1.0
    spec = lambda n, fn: pl.BlockSpec((cl, n), lambda b, t: (fn(b, t), 0))
    return pl.pallas_call(
        _ssd_kernel,
        grid=(BATCH, steps),
        in_specs=[spec(SSD_CONV_DIM, fwd_chunk), spec(SSD_CONV_DIM, bwd_chunk),
                  spec(LANES, fwd_chunk), spec(LANES, bwd_chunk),
                  _full((1, LANES)), _full((1, LANES)), _full((2, LANES, SSD_INNER))],
        out_specs=[spec(SSD_INNER, fwd_chunk), spec(SSD_INNER, bwd_chunk)],
        out_shape=[jax.ShapeDtypeStruct((rows, SSD_INNER), F32)] * 2,
        scratch_shapes=[pltpu.VMEM((SSD_GROUPS, SSD_STATE, SSD_GROUP_W), F32)] * 2,
        compiler_params=_params("arbitrary", "arbitrary"),
        name="ssd_scan",
    )(u, u, dt, dt, bias, alog, jnp.asarray(expand, BF16))


def _dft_cos_sin(n):
    k = np.arange(n)
    ang = 2.0 * np.pi * ((k[:, None] * k[None, :]) % n) / n
    return np.cos(ang), np.sin(ang)


def _fft1_kernel(x_ref, m_ref, o_ref):
    o_ref[0] = _dot3(m_ref[...], x_ref[...])


def _fft2_kernel(ar_ref, ai_ref, twr_ref, twi_ref, m2_ref, mc_ref, o_ref):
    ar, ai = ar_ref[0, 0], ai_ref[0, 0]
    twr = jnp.concatenate([twr_ref[0]] * FNET_GROUPS, axis=1)
    twi = jnp.concatenate([twi_ref[0]] * FNET_GROUPS, axis=1)
    p = ar * twr - ai * twi
    q = ar * twi + ai * twr
    uv = _dot3(m2_ref[...], jnp.concatenate([p, q], axis=0))
    n2 = FFT_N2
    for g in range(FNET_GROUPS):
        gs = slice(g * FNET_GROUP_DIM, (g + 1) * FNET_GROUP_DIM)
        o_ref[0, :, gs] = _dot3(jnp.concatenate([uv[:n2, gs], uv[n2:, gs]], axis=1), mc_ref[...])


def _fft_ctx_kernel(x_ref, mc_ref, mp_ref, o_ref):
    x = x_ref[...]
    for g in range(FNET_GROUPS):
        gs = slice(g * FNET_GROUP_DIM, (g + 1) * FNET_GROUP_DIM)
        cs = _dot3(x[:, gs], mc_ref[...])
        stacked = jnp.concatenate([cs[:, :FNET_GROUP_DIM], cs[:, FNET_GROUP_DIM:]], axis=0)
        o_ref[:, gs] = _dot3(mp_ref[...], stacked)


def fourier_mix(f_all):
    rows = f_all.shape[0]
    n1, n2, gd = FFT_N1, FFT_N2, FNET_GROUP_DIM
    row_w = n2 * FNET_WIDTH
    c1, s1 = _dft_cos_sin(n1)
    m1 = jnp.asarray(np.concatenate([c1, -s1], axis=0), F32)
    tn = FFT1_COLS
    stage1 = pl.pallas_call(
        _fft1_kernel,
        grid=(BATCH, row_w // tn),
        in_specs=[pl.BlockSpec((n1, tn), lambda b, j: (b, j)), _full((2 * n1, n1))],
        out_specs=pl.BlockSpec((1, 2 * n1, tn), lambda b, j: (b, 0, j)),
        out_shape=jax.ShapeDtypeStruct((BATCH, 2 * n1, row_w), F32),
        compiler_params=_params("arbitrary", "arbitrary"),
        name="fft_stage1",
    )(f_all.reshape(rows // n2, row_w), m1)

    k1 = np.arange(n1)[:, None]
    l2 = np.arange(n2)[None, :]
    tw = 2.0 * np.pi * (k1 * l2) / SEQ
    twr = jnp.asarray(np.repeat(np.cos(tw)[:, :, None], gd, axis=2), F32)
    twi = jnp.asarray(np.repeat(-np.sin(tw)[:, :, None], gd, axis=2), F32)
    c2, s2 = _dft_cos_sin(n2)
    m2 = jnp.asarray(np.block([[c2, s2], [-s2, c2]]), F32)
    cc, sc = _dft_cos_sin(gd)
    mc = jnp.asarray(np.concatenate([cc, sc], axis=0) / math.sqrt(SEQ * gd), F32)
    a4 = stage1.reshape(BATCH, 2 * n1, n2, FNET_WIDTH)
    lat = pl.pallas_call(
        _fft2_kernel,
        grid=(BATCH, n1),
        in_specs=[pl.BlockSpec((1, 1, n2, FNET_WIDTH), lambda b, k: (b, k, 0, 0)),
                  pl.BlockSpec((1, 1, n2, FNET_WIDTH), lambda b, k: (b, n1 + k, 0, 0)),
                  pl.BlockSpec((1, n2, gd), lambda b, k: (k, 0, 0)),
                  pl.BlockSpec((1, n2, gd), lambda b, k: (k, 0, 0)),
                  _full((2 * n2, 2 * n2)), _full((2 * gd, gd))],
        out_specs=pl.BlockSpec((1, n2, FNET_WIDTH), lambda b, k: (b, 0, k)),
        out_shape=jax.ShapeDtypeStruct((BATCH, n2, n1 * FNET_WIDTH), F32),
        compiler_params=_params("arbitrary", "arbitrary"),
        name="fft_stage2",
    )(a4, a4, twr, twi, m2, mc)

    cp, sp = _dft_cos_sin(CTX_LEN)
    mp = jnp.asarray(np.concatenate([cp, -sp], axis=1) / math.sqrt(CTX_LEN * gd), F32)
    mcc = jnp.asarray(np.concatenate([cc, sc], axis=1), F32)
    ctx0 = LAT_ROWS // CTX_LEN
    ctx = pl.pallas_call(
        _fft_ctx_kernel,
        grid=(BATCH,),
        in_specs=[pl.BlockSpec((CTX_LEN, FNET_WIDTH), lambda b: (ctx0 + b, 0)),
                  _full((gd, 2 * gd)), _full((CTX_LEN, 2 * CTX_LEN))],
        out_specs=pl.BlockSpec((CTX_LEN, FNET_WIDTH), lambda b: (b, 0)),
        out_shape=jax.ShapeDtypeStruct((BATCH * CTX_LEN, FNET_WIDTH), F32),
        compiler_params=_params("arbitrary"),
        name="fft_ctx",
    )(f_all, mcc, mp)
    return jnp.concatenate([lat.reshape(LAT_ROWS, FNET_WIDTH), ctx], axis=0)


def _even_out_kernel(x_ref, mod_ref, four_ref, yf_ref, yb_ref, xs_ref, z_ref, dsk_ref, gn_ref, wof_ref, wos_ref,
                     o_ref):
    y = yf_ref[...] + yb_ref[...] + dsk_ref[...] * xs_ref[...]
    gated = y * _silu(z_ref[...])
    out = jnp.dot(four_ref[...].astype(BF16), wof_ref[...], preferred_element_type=F32)
    for g in range(SSD_GROUPS):
        gs = slice(g * SSD_GROUP_W, (g + 1) * SSD_GROUP_W)
        v = gated[:, gs]
        normed = v * lax.rsqrt(jnp.mean(v * v, axis=-1, keepdims=True) + EPS) * gn_ref[:, gs]
        out = out + jnp.dot(normed.astype(BF16), wos_ref[gs, :], preferred_element_type=F32)
    gate = mod_ref[0][:, 2 * D_MODEL:3 * D_MODEL]
    o_ref[...] = x_ref[...] + gate * out


def even_out(x_all, mod, four, yf, yb, u, z, d_skip, ssd_norm, w_out):
    rows = x_all.shape[0]
    tm = ROW_TILE
    dsk = jnp.repeat(d_skip[0] + d_skip[1], SSD_HEAD_DIM)[None, :]
    wof = w_out[:FNET_WIDTH].astype(BF16)
    wos = w_out[FNET_WIDTH:].astype(BF16)
    row = lambda n: pl.BlockSpec((tm, n), lambda i: (i, 0))
    return pl.pallas_call(
        _even_out_kernel,
        grid=(rows // tm,),
        in_specs=[row(D_MODEL),
                  pl.BlockSpec((1, 1, N_MOD * D_MODEL), lambda i: (_mod_row(i, tm), 0, 0)),
                  row(FNET_WIDTH), row(SSD_INNER), row(SSD_INNER), row(SSD_INNER), row(SSD_INNER),
                  _full((1, SSD_INNER)), _full((1, SSD_INNER)), _full(wof.shape), _full(wos.shape)],
        out_specs=row(D_MODEL),
        out_shape=jax.ShapeDtypeStruct((rows, D_MODEL), F32),
        compiler_params=_params("arbitrary"),
        name="even_out",
    )(x_all, mod, four, yf, yb, u, z, dsk, ssd_norm[None, :], wof, wos)


BF16_ROWS = 2 * SUBLANES
NOT_TOP = 64.0
AFTER_EVERY_ORDER = float(PEER_TOPK * PEER_KEYS)


def _erf_gelu(z):
    return 0.5 * z * (1.0 + lax.erf(z * (1.0 / math.sqrt(2.0))))


def _merge_exchange_pairs(n):
    t = max(1, math.ceil(math.log2(n)))
    p, pairs = 2 ** (t - 1), []
    while p > 0:
        q, r, d = 2 ** (t - 1), 0, p
        while d > 0:
            pairs += [(i, i + d) for i in range(n - d) if i & p == r]
            d, q, r = q - p, q // 2, p
        p //= 2
    return pairs


def _top_values_sorted(score, k):
    groups = [score[r * SUBLANES:(r + 1) * SUBLANES] for r in range(score.shape[0] // SUBLANES)]
    for a, b in _merge_exchange_pairs(len(groups)):
        groups[a], groups[b] = jnp.maximum(groups[a], groups[b]), jnp.minimum(groups[a], groups[b])
    vals = []
    for i in range(k):
        m = jnp.max(groups[0], axis=0, keepdims=True)
        vals.append(m)
        needed = k - 1 - i
        if needed == 0:
            break
        hit = groups[0] == m
        shifted = [jnp.where(hit, groups[d + 1], groups[d]) for d in range(len(groups) - 1)]
        if len(groups) <= needed:
            shifted.append(jnp.where(hit, NEG_INF, groups[-1]))
        groups = shifted[:needed]
    return jnp.concatenate(vals, axis=0)


def _rank_among(values, top, k):
    assert k & (k - 1) == 0
    bits, count, stride = [], jnp.zeros_like(values), k // 2
    while stride >= 1:
        cands = [top[p + stride - 1:p + stride] for p in range(0, k, 2 * stride)]
        for b in reversed(bits):
            cands = [jnp.where(b, cands[2 * i + 1], cands[2 * i]) for i in range(len(cands) // 2)]
        bit = cands[0] > values
        count = count + jnp.where(bit, float(stride), 0.0)
        bits.append(bit)
        stride //= 2
    return jnp.where(values >= top[k - 1:k], count, NOT_TOP)


def _pair_candidates(v1, v2, k):
    assert k == 2 * SUBLANES, "the row grouping below is laid out for k = 16"
    t = v1.shape[1]
    row8 = lax.broadcasted_iota(jnp.int32, (SUBLANES, t), 0)
    rowf = row8.astype(F32)
    sums = [v1[0:1] + v2[0:8], v1[0:1] + v2[8:16], v1[1:2] + v2[0:8], v1[8:16] + v2[0:1]]
    order = [rowf, rowf + 8.0, rowf + float(k), (rowf + 8.0) * float(k)]
    for i in range(2, 8):
        sums.append(jnp.where(row8 < k // (i + 1), v1[i:i + 1] + v2[0:8], NEG_INF))
        order.append(rowf + float(i * k))
    return jnp.concatenate(sums, axis=0), jnp.concatenate(order, axis=0)


def _select_by_value(s1, s2, k):
    v1 = _top_values_sorted(s1, k)
    v2 = _top_values_sorted(s2, k)
    r2 = _rank_among(s2, v2, k)
    cand, _ = _pair_candidates(v1, v2, k)
    tau = _top_values_sorted(cand, k)[k - 1:k, :]
    chosen = cand >= tau
    zsum = jnp.sum(jnp.where(chosen, jnp.exp(cand - (v1[0:1] + v2[0:1])), 0.0), axis=0, keepdims=True)
    in_top = s1 >= v1[k - 1:k]
    a1 = jnp.where(in_top, s1, NEG_INF)
    lim = jnp.zeros_like(s1)
    for j in range(k // 2):
        lim = lim + jnp.where(a1 + v2[j:j + 1] >= tau, 1.0, 0.0)
    best = jnp.zeros_like(tau)
    for j in range(k // 2, k):
        best = best + jnp.where(v1[0:1] + v2[j:j + 1] >= tau, 1.0, 0.0)
    lim = lim + jnp.where(s1 == v1[0:1], best, 0.0)
    count = lambda mask: jnp.sum(jnp.where(mask, 1.0, 0.0), axis=0, keepdims=True)
    most = jnp.maximum(jnp.maximum(count(in_top), count(r2 < k)), count(chosen))
    return lim, jnp.exp(s1 - v1[0:1]) / zsum, r2, jnp.exp(s2 - v2[0:1]), most


def _top_ranks_ordered(score, order, k):
    work, vals = score, []
    rank = jnp.full(score.shape, NOT_TOP, F32)
    for i in range(k):
        m = jnp.max(work, axis=0, keepdims=True)
        first = jnp.min(jnp.where(work == m, order, AFTER_EVERY_ORDER), axis=0, keepdims=True)
        taken = order == first
        vals.append(m)
        rank = jnp.where(taken, float(i), rank)
        work = jnp.where(taken, NEG_INF, work)
    return jnp.concatenate(vals, axis=0), rank


def _select_by_order(s1, s2, k):
    key_index = lax.broadcasted_iota(jnp.int32, s1.shape, 0).astype(F32)
    v1, r1 = _top_ranks_ordered(s1, key_index, k)
    v2, r2 = _top_ranks_ordered(s2, key_index, k)
    cand, position = _pair_candidates(v1, v2, k)
    chosen = _top_ranks_ordered(cand, position, k)[1] < k
    zsum = jnp.sum(jnp.where(chosen, jnp.exp(cand - (v1[0:1] + v2[0:1])), 0.0), axis=0, keepdims=True)
    picks = jnp.where(chosen, 1.0, 0.0)
    group = lambda g: jnp.sum(picks[g * SUBLANES:(g + 1) * SUBLANES], axis=0, keepdims=True)
    per_rank = [group(0) + group(1), group(2)] + [group(g) for g in range(4, 10)]
    per_rank = jnp.concatenate(per_rank + [picks[3 * SUBLANES:4 * SUBLANES]], axis=0)
    lim = jnp.zeros_like(s1)
    for i in range(k):
        lim = lim + jnp.where(r1 == float(i), per_rank[i:i + 1], 0.0)
    return lim, jnp.exp(s1 - v1[0:1]) / zsum, r2, jnp.exp(s2 - v2[0:1])


def _peer_kernel(x_ref, mod_ref, g_ref, wq_ref, keys_ref, u0_ref, *rest, tm, chunk, per_step):
    u_refs = rest[:per_step]
    (vt_ref, o_ref, ht_ref, q_ref, lim_ref, w1_ref, rank2_ref, w2_ref, acc_ref, za_ref, zb_ref) = rest[per_step:]
    c = pl.program_id(1)
    k = PEER_TOPK
    packed = (PEER_KEYS // BF16_ROWS, BF16_ROWS, tm)
    tsel = PEER_SELECT_TOKENS

    @pl.when(c == 0)
    def _prologue():
        h2 = _modulated_norm(x_ref[...], g_ref[...], mod_ref[0], 3, 4)
        ht = h2.T.astype(BF16)
        ht_ref[...] = ht
        q = jnp.dot(wq_ref[...], ht, preferred_element_type=F32)
        q_ref[...] = q.reshape(2 * PEER_HEADS, PEER_QDIM // 2, tm)
        acc_ref[...] = jnp.zeros_like(acc_ref)
        za_ref[...] = jnp.dot(u0_ref[...], ht, preferred_element_type=F32)
        def head_body(h, carry):
            def scores(part):
                ts = slice(part * tsel, (part + 1) * tsel)
                s1 = jnp.dot(keys_ref[2 * h], q_ref[2 * h, :, ts].astype(BF16), preferred_element_type=F32)
                s2 = jnp.dot(keys_ref[2 * h + 1], q_ref[2 * h + 1, :, ts].astype(BF16), preferred_element_type=F32)
                return ts, s1, s2

            def store(ts, lim, w1, r2, w2):
                small = (PEER_KEYS // BF16_ROWS, BF16_ROWS, tsel)
                lim_ref[h, :, ts] = lim
                w1_ref[h, :, ts] = w1
                rank2_ref[h, :, :, ts] = r2.reshape(small).astype(BF16)
                w2_ref[h, :, :, ts] = w2.reshape(small).astype(BF16)

            counts = []
            for part in range(tm // tsel):
                ts, s1, s2 = scores(part)
                lim, w1, r2, w2, count = _select_by_value(s1, s2, k)
                store(ts, lim, w1, r2, w2)
                counts.append(count)

            @pl.when(jnp.max(functools.reduce(jnp.maximum, counts)) > k)
            def _ties():
                for part, count in enumerate(counts):
                    @pl.when(jnp.max(count) > k)
                    def _redo():
                        ts, s1, s2 = scores(part)
                        store(ts, *_select_by_order(s1, s2, k))

            return carry

        lax.fori_loop(0, PEER_HEADS, head_body, 0)

    per_key = PEER_KEYS // BF16_ROWS
    keys_per_sub = PEER_SUB // PEER_KEYS
    step_keys = per_step * chunk // PEER_KEYS
    key0 = pl.multiple_of(c * step_keys, step_keys)
    step_lim = [lim_ref[h, pl.ds(key0, step_keys), :] for h in range(PEER_HEADS)]
    step_w1 = [w1_ref[h, pl.ds(key0, step_keys), :] for h in range(PEER_HEADS)]

    def gated_values(z_ref, which):
        total = None
        for j in range(chunk // PEER_SUB):
            rows = slice(j * PEER_SUB, (j + 1) * PEER_SUB)
            act = _erf_gelu(z_ref[rows, :]).reshape(PEER_SUB // BF16_ROWS, BF16_ROWS, tm).astype(BF16)
            parts = []
            for jj in range(keys_per_sub):
                e1 = which * (chunk // PEER_KEYS) + j * keys_per_sub + jj
                gate = jnp.zeros(packed, BF16)
                for h in range(PEER_HEADS):
                    lim = jnp.broadcast_to(step_lim[h][e1:e1 + 1, :], (BF16_ROWS, tm)).astype(BF16)
                    w1 = jnp.broadcast_to(step_w1[h][e1:e1 + 1, :], (BF16_ROWS, tm)).astype(BF16)
                    picked = jnp.where(rank2_ref[h] < lim[None], w2_ref[h], jnp.zeros(packed, BF16))
                    gate = gate + picked * w1[None]
                parts.append(act[jj * per_key:(jj + 1) * per_key] * gate)
            a = jnp.concatenate(parts, axis=0).reshape(PEER_SUB, tm)
            cols = slice(which * chunk + j * PEER_SUB, which * chunk + (j + 1) * PEER_SUB)
            d = jnp.dot(vt_ref[:, cols], a, preferred_element_type=F32)
            total = d if total is None else total + d
        return total

    bufs = (za_ref, zb_ref)
    total = None
    for i in range(per_step):
        bufs[(i + 1) % 2][...] = jnp.dot(u_refs[i][...], ht_ref[...], preferred_element_type=F32)
        d = gated_values(bufs[i % 2], i)
        total = d if total is None else total + d
    acc_ref[...] += total

    @pl.when(c == pl.num_programs(1) - 1)
    def _epilogue():
        gate2 = mod_ref[0][:, 5 * D_MODEL:6 * D_MODEL]
        o_ref[...] = x_ref[...] + gate2 * acc_ref[...].T


def peer_layer(x_rows, mod, norm2, peer_wq, peer_keys, peer_u, peer_v):
    rows = x_rows.shape[0]
    tm, chunk = PEER_TILE, PEER_CHUNK
    wq_t = peer_wq.T.astype(BF16)
    keys = peer_keys.reshape(2 * PEER_HEADS, PEER_KEYS, PEER_QDIM // 2).astype(BF16)
    u = peer_u.astype(BF16)
    vt = peer_v.astype(BF16).T
    head_f32 = pltpu.VMEM((PEER_HEADS, PEER_KEYS, tm), F32)
    head_bf16 = pltpu.VMEM((PEER_HEADS, PEER_KEYS // BF16_ROWS, BF16_ROWS, tm), BF16)
    n_chunks = PEER_EXPERTS // chunk
    per_step = PEER_CHUNKS_PER_STEP
    ahead = lambda k: pl.BlockSpec((chunk, D_MODEL), lambda i, c: (jnp.minimum(per_step * c + k, n_chunks - 1), 0))
    return pl.pallas_call(
        functools.partial(_peer_kernel, tm=tm, chunk=chunk, per_step=per_step),
        grid=(rows // tm, n_chunks // per_step),
        in_specs=[pl.BlockSpec((tm, D_MODEL), lambda i, c: (i, 0)),
                  pl.BlockSpec((1, 1, N_MOD * D_MODEL), lambda i, c: (_mod_row(i, tm), 0, 0)),
                  _full((1, D_MODEL)), _full(wq_t.shape), _full(keys.shape),
                  pl.BlockSpec((chunk, D_MODEL), lambda i, c: (0, 0))]
                 + [ahead(k) for k in range(1, per_step + 1)]
                 + [pl.BlockSpec((D_MODEL, per_step * chunk), lambda i, c: (0, c))],
        out_specs=pl.BlockSpec((tm, D_MODEL), lambda i, c: (i, 0)),
        out_shape=jax.ShapeDtypeStruct((rows, D_MODEL), F32),
        scratch_shapes=[pltpu.VMEM((D_MODEL, tm), BF16),
                        pltpu.VMEM((2 * PEER_HEADS, PEER_QDIM // 2, tm), F32),
                        head_f32, head_f32, head_bf16, head_bf16,
                        pltpu.VMEM((D_MODEL, tm), F32),
                        pltpu.VMEM((chunk, tm), F32), pltpu.VMEM((chunk, tm), F32)],
        compiler_params=_params("arbitrary", "arbitrary"),
        name="peer",
    )(x_rows, mod, norm2[None, :], wq_t, keys, u, *([u] * per_step), vt)


def _rope_tables(rows):
    l = jnp.arange(SEQ)
    per_axis = MLA_ROPE // 2
    inv = ROPE_THETA ** (-jnp.arange(0, per_axis, 2, dtype=F32) / per_axis)
    ang = jnp.concatenate([(l // GRID_W)[:, None] * inv, (l % GRID_W)[:, None] * inv], axis=-1)
    cos, sin = jnp.cos(ang), jnp.sin(ang)
    one = jnp.ones((SEQ, MLA_NOPE), F32)
    tail = jnp.zeros((SEQ, HEAD_PAD - MLA_QK), F32)
    zero64 = jnp.zeros((SEQ, MLA_NOPE), F32)
    cos_t = jnp.concatenate([one, cos, cos, tail], axis=1)
    sin_t = jnp.concatenate([zero64, -sin, sin, tail], axis=1)
    n_ctx = rows - LAT_ROWS
    ident = jnp.concatenate([jnp.ones((n_ctx, MLA_QK), F32), jnp.zeros((n_ctx, HEAD_PAD - MLA_QK), F32)], axis=1)
    zeros = jnp.zeros((n_ctx, HEAD_PAD), F32)
    tile = lambda t, c: jnp.concatenate([t] * BATCH + [c], axis=0)
    return tile(cos_t, ident), tile(sin_t, zeros)


def _rope_swap_matrix():
    per_axis = MLA_ROPE // 2
    swap = np.zeros((HEAD_PAD, HEAD_PAD), np.float32)
    for j in range(per_axis):
        swap[MLA_NOPE + per_axis + j, MLA_NOPE + j] = 1.0
        swap[MLA_NOPE + j, MLA_NOPE + per_axis + j] = 1.0
    return swap


def _head_norm_rope(t, gain, cos, sin, ones, swap):
    ms = jnp.dot((t * t).astype(BF16), ones, preferred_element_type=F32) * (1.0 / MLA_QK)
    t = t * lax.rsqrt(ms + EPS) * gain
    return t * cos + jnp.dot(t.astype(BF16), swap, preferred_element_type=F32) * sin


def _mla_in_kernel(x_ref, mod_ref, g_ref, wdq_ref, wdkv_ref, wpe_ref, qan_ref, kvan_ref, wuq_ref, wuk_ref, wuv_ref,
                   qn_ref, kn_ref, ones_ref, swap_ref, cos_ref, sin_ref, q_ref, k_ref, v_ref):
    h = _modulated_norm(x_ref[...], g_ref[...], mod_ref[0], 0, 1).astype(BF16)
    dq = jnp.dot(h, wdq_ref[...], preferred_element_type=F32)
    dkv = jnp.dot(h, wdkv_ref[...], preferred_element_type=F32)
    kpe = jnp.dot(h, wpe_ref[...], preferred_element_type=F32)
    qa = dq * lax.rsqrt(jnp.mean(dq * dq, axis=-1, keepdims=True) + EPS) * qan_ref[...]
    kva = (dkv * lax.rsqrt(jnp.mean(dkv * dkv, axis=-1, keepdims=True) + EPS) * kvan_ref[...]).astype(BF16)
    q = jnp.dot(qa.astype(BF16), wuq_ref[...], preferred_element_type=F32)
    kn = jnp.dot(kva, wuk_ref[...], preferred_element_type=F32)
    v_ref[...] = jnp.dot(kva, wuv_ref[...], preferred_element_type=F32).astype(BF16)
    cos, sin, ones, swap = cos_ref[...], sin_ref[...], ones_ref[...], swap_ref[...]
    scale = MLA_QK ** -0.5 * math.log2(math.e)
    for hd in range(MLA_HEADS):
        hs = slice(hd * HEAD_PAD, (hd + 1) * HEAD_PAD)
        q_ref[:, hs] = (_head_norm_rope(q[:, hs], qn_ref[...], cos, sin, ones, swap) * scale).astype(BF16)
        k_ref[:, hs] = _head_norm_rope(kn[:, hs] + kpe, kn_ref[...], cos, sin, ones, swap).astype(BF16)


def mla_in(x_all, mod, p):
    rows = x_all.shape[0]
    tm = ROW_TILE
    w = p["w_dqkv"]
    wdq = w[:, :MLA_Q_LORA].astype(BF16)
    wdkv = w[:, MLA_Q_LORA:MLA_Q_LORA + MLA_KV_LORA].astype(BF16)
    wpe = jnp.pad(w[:, MLA_Q_LORA + MLA_KV_LORA:], ((0, 0), (MLA_NOPE, HEAD_PAD - MLA_QK))).astype(BF16)
    wuq = jnp.pad(p["w_uq"].reshape(MLA_Q_LORA, MLA_HEADS, MLA_QK), ((0, 0), (0, 0), (0, HEAD_PAD - MLA_QK)))
    wuq = wuq.reshape(MLA_Q_LORA, MLA_HEADS * HEAD_PAD).astype(BF16)
    wukv = p["w_ukv"].reshape(MLA_KV_LORA, MLA_HEADS, MLA_NOPE + MLA_V)
    wuk = jnp.pad(wukv[:, :, :MLA_NOPE], ((0, 0), (0, 0), (0, HEAD_PAD - MLA_NOPE)))
    wuk = wuk.reshape(MLA_KV_LORA, MLA_HEADS * HEAD_PAD).astype(BF16)
    wuv = wukv[:, :, MLA_NOPE:].reshape(MLA_KV_LORA, MLA_HEADS * MLA_V).astype(BF16)
    padg = lambda g: jnp.pad(g, (0, HEAD_PAD - MLA_QK))[None, :]
    cos, sin = _rope_tables(rows)
    row = lambda n: pl.BlockSpec((tm, n), lambda i: (i, 0))
    weights = [wdq, wdkv, wpe, p["q_a_norm"][None, :], p["kv_a_norm"][None, :], wuq, wuk, wuv,
               padg(p["q_norm"]), padg(p["k_norm"]),
               jnp.ones((HEAD_PAD, HEAD_PAD), BF16), jnp.asarray(_rope_swap_matrix(), BF16)]
    return pl.pallas_call(
        _mla_in_kernel,
        grid=(rows // tm,),
        in_specs=[row(D_MODEL),
                  pl.BlockSpec((1, 1, N_MOD * D_MODEL), lambda i: (_mod_row(i, tm), 0, 0)),
                  _full((1, D_MODEL))] + [_full(a.shape) for a in weights] + [row(HEAD_PAD)] * 2,
        out_specs=[row(MLA_HEADS * HEAD_PAD), row(MLA_HEADS * HEAD_PAD), row(MLA_HEADS * MLA_V)],
        out_shape=[jax.ShapeDtypeStruct((rows, MLA_HEADS * HEAD_PAD), BF16),
                   jax.ShapeDtypeStruct((rows, MLA_HEADS * HEAD_PAD), BF16),
                   jax.ShapeDtypeStruct((rows, MLA_HEADS * MLA_V), BF16)],
        compiler_params=_params("arbitrary"),
        name="mla_in",
    )(x_all, mod, p["norm1"][None, :], *weights, cos, sin)


ATTN_SAFE_SHIFT = 50.0
ATTN_BOUND_SLACK = 1.001
ATTN_BOUND_MARGIN = 1e-3


def _attn_kernel(q_ref, kc_ref, kl_ref, vc_ref, vl_ref, o_ref, knorm_ref, *, tq, tk):
    heads = (slice(0, HEAD_PAD), slice(HEAD_PAD, 2 * HEAD_PAD))
    qs = [q_ref[:, hs] for hs in heads]
    n_lat = SEQ // tk

    @pl.when(pl.program_id(2) == 0)
    def _key_norms():
        def sq_max(k2, hh):
            kf = k2[:, heads[hh]].astype(F32)
            return jnp.max(jnp.sum(kf * kf, axis=-1, keepdims=True), axis=0, keepdims=True)

        for hh in range(2):
            best = sq_max(kc_ref[...], hh)
            best = lax.fori_loop(
                0, n_lat, lambda j, b: jnp.maximum(b, sq_max(kl_ref[pl.ds(pl.multiple_of(j * tk, tk), tk), :], hh)), best)
            knorm_ref[hh] = jnp.broadcast_to(jnp.sqrt(best), (SUBLANES, LANES))

    def values(v, hh):
        lane = lax.broadcasted_iota(jnp.int32, v.shape, 1)
        own = (lane < MLA_V) if hh == 0 else (lane >= MLA_V)
        return jnp.where(own, v, jnp.ones_like(v))

    def scores(hh, k2):
        return lax.dot_general(qs[hh], k2[:, heads[hh]], (((1,), (1,)), ((), ())), preferred_element_type=F32)

    def finish(accs):
        lane = lax.broadcasted_iota(jnp.int32, (tq, 2 * MLA_V), 1)
        outs = [acc / pltpu.roll(acc, MLA_V, 1) for acc in accs]
        o_ref[...] = jnp.where(lane < MLA_V, outs[0], outs[1]).astype(o_ref.dtype)

    def lat_chunk(j):
        off = pl.multiple_of(j * tk, tk)
        return kl_ref[pl.ds(off, tk), :], vl_ref[pl.ds(off, tk), :]

    bounds = []
    for hh in range(2):
        qf = qs[hh].astype(F32)
        qn = jnp.sqrt(jnp.sum(qf * qf, axis=-1, keepdims=True))
        bounds.append(qn * knorm_ref[hh, 0:1, 0:1] * ATTN_BOUND_SLACK + ATTN_BOUND_MARGIN)
    widest = jnp.max(jnp.maximum(bounds[0], bounds[1]))

    def fixed_shift():
        def step(k2, v, accs):
            return tuple(accs[hh] + jnp.dot(jnp.exp2(scores(hh, k2) - bounds[hh]).astype(BF16), values(v, hh),
                                            preferred_element_type=F32) for hh in range(2))

        zero = jnp.zeros((tq, 2 * MLA_V), F32)
        accs = step(kc_ref[...], vc_ref[...], (zero, zero))
        finish(lax.fori_loop(0, n_lat, lambda j, a: step(*lat_chunk(j), a), accs, unroll=8))

    def running_max():
        def step(k2, v, carry):
            new = []
            for hh in range(2):
                m, acc = carry[hh]
                s = scores(hh, k2)
                m_new = jnp.maximum(m, jnp.max(s, axis=-1, keepdims=True))
                p = jnp.exp2(s - m_new).astype(BF16)
                acc = jnp.exp2(m - m_new) * acc + jnp.dot(p, values(v, hh), preferred_element_type=F32)
                new.append((m_new, acc))
            return tuple(new)

        init = tuple((jnp.full((tq, 1), NEG_INF, F32), jnp.zeros((tq, 2 * MLA_V), F32)) for _ in range(2))
        carry = step(kc_ref[...], vc_ref[...], init)
        carry = lax.fori_loop(0, n_lat, lambda j, c: step(*lat_chunk(j), c), carry)
        finish([acc for _, acc in carry])

    lax.cond(widest <= ATTN_SAFE_SHIFT, fixed_shift, running_max)


def attention(q, k, v):
    tq, tk = ATTN_TQ, ATTN_TK
    nq = SEQ // tq
    ctx0 = LAT_ROWS // CTX_LEN
    return pl.pallas_call(
        functools.partial(_attn_kernel, tq=tq, tk=tk),
        grid=(BATCH, MLA_HEADS // 2, nq),
        in_specs=[pl.BlockSpec((tq, 2 * HEAD_PAD), lambda b, h, i: (b * nq + i, h)),
                  pl.BlockSpec((CTX_LEN, 2 * HEAD_PAD), lambda b, h, i: (ctx0 + b, h)),
                  pl.BlockSpec((SEQ, 2 * HEAD_PAD), lambda b, h, i: (b, h)),
                  pl.BlockSpec((CTX_LEN, 2 * MLA_V), lambda b, h, i: (ctx0 + b, h)),
                  pl.BlockSpec((SEQ, 2 * MLA_V), lambda b, h, i: (b, h))],
        out_specs=pl.BlockSpec((tq, 2 * MLA_V), lambda b, h, i: (b * nq + i, h)),
        out_shape=jax.ShapeDtypeStruct((LAT_ROWS, MLA_HEADS * MLA_V), BF16),
        scratch_shapes=[pltpu.VMEM((2, SUBLANES, LANES), F32)],
        compiler_params=_params("arbitrary", "arbitrary", "arbitrary"),
        name="attention",
    )(q, k, k, v, v)


def _attn_out_kernel(x_ref, mod_ref, o_ref_in, wo_ref, out_ref):
    gate = mod_ref[0][:, 2 * D_MODEL:3 * D_MODEL]
    out_ref[...] = x_ref[...] + gate * jnp.dot(o_ref_in[...], wo_ref[...], preferred_element_type=F32)


def attn_out(x_all, mod, o, w_o):
    tm = ROW_TILE
    row = lambda n: pl.BlockSpec((tm, n), lambda i: (i, 0))
    return pl.pallas_call(
        _attn_out_kernel,
        grid=(LAT_ROWS // tm,),
        in_specs=[row(D_MODEL), pl.BlockSpec((1, 1, N_MOD * D_MODEL), lambda i: (_mod_row(i, tm), 0, 0)),
                  row(MLA_HEADS * MLA_V), _full(w_o.shape)],
        out_specs=row(D_MODEL),
        out_shape=jax.ShapeDtypeStruct((LAT_ROWS, D_MODEL), F32),
        compiler_params=_params("arbitrary"),
        name="attn_out",
    )(x_all, mod, o, w_o.astype(BF16))


def kernel(x, c, ctx, c_ctx, l0_w_ada, l0_b_ada, l0_norm1, l0_w_in, l0_conv_w, l0_conv_b, l0_dt_bias, l0_a_log, l0_d_skip, l0_ssd_norm, l0_w_out, l0_norm2, l0_peer_wq, l0_peer_keys, l0_peer_u, l0_peer_v, l1_w_ada, l1_b_ada, l1_norm1, l1_w_dqkv, l1_q_a_norm, l1_kv_a_norm, l1_w_uq, l1_w_ukv, l1_q_norm, l1_k_norm, l1_w_o, l1_norm2, l1_peer_wq, l1_peer_keys, l1_peer_u, l1_peer_v):
    x_all = jnp.concatenate([x.reshape(LAT_ROWS, D_MODEL), ctx.reshape(BATCH * CTX_LEN, D_MODEL)], axis=0)

    mod0 = ada_table(c, c_ctx, l0_w_ada, l0_b_ada)
    f, z, xbc, dt = even_in(x_all, mod0, l0_norm1, l0_w_in)
    u = conv_silu(xbc, l0_conv_w, l0_conv_b)
    yf, yb = ssd_scan(u, dt, l0_dt_bias, l0_a_log)
    four = fourier_mix(f)
    x_all = even_out(x_all, mod0, four, yf, yb, u, z, l0_d_skip, l0_ssd_norm, l0_w_out)
    x_all = peer_layer(x_all, mod0, l0_norm2, l0_peer_wq, l0_peer_keys, l0_peer_u, l0_peer_v)

    mod1 = ada_table(c, c_ctx, l1_w_ada, l1_b_ada)
    p1 = dict(norm1=l1_norm1, w_dqkv=l1_w_dqkv, q_a_norm=l1_q_a_norm, kv_a_norm=l1_kv_a_norm, w_uq=l1_w_uq,
              w_ukv=l1_w_ukv, q_norm=l1_q_norm, k_norm=l1_k_norm)
    q, k, v = mla_in(x_all, mod1, p1)
    o = attention(q, k, v)
    x_lat = attn_out(x_all, mod1, o, l1_w_o)
    x_lat = peer_layer(x_lat, mod1, l1_norm2, l1_peer_wq, l1_peer_keys, l1_peer_u, l1_peer_v)
    return x_lat.reshape(BATCH, SEQ, D_MODEL)
```

```python
import functools
import math

import numpy as np
import jax
import jax.numpy as jnp
from jax import lax
from jax.experimental import pallas as pl
from jax.experimental.pallas import tpu as pltpu

D_MODEL = 1024
BATCH = 2
SEQ = 8192
GRID_W = 64
CTX_LEN = 256
EPS = 1e-6
N_MOD = 6
LAT_ROWS = BATCH * SEQ
ALL_ROWS = LAT_ROWS + BATCH * CTX_LEN

FNET_GROUPS = 4
FNET_GROUP_DIM = 128
FNET_WIDTH = FNET_GROUPS * FNET_GROUP_DIM
FFT_N1 = 64
FFT_N2 = 128

SSD_HEADS = 16
SSD_HEAD_DIM = 64
SSD_INNER = SSD_HEADS * SSD_HEAD_DIM
SSD_GROUPS = 4
SSD_STATE = 128
SSD_CONV = 5
SSD_CHUNK = 128
SSD_CONV_DIM = SSD_INNER + 2 * SSD_GROUPS * SSD_STATE
SSD_GROUP_W = SSD_INNER // SSD_GROUPS

MLA_HEADS = 16
MLA_NOPE = 64
MLA_ROPE = 32
MLA_QK = MLA_NOPE + MLA_ROPE
MLA_V = 64
MLA_Q_LORA = 384
MLA_KV_LORA = 256
ROPE_THETA = 10000.0
HEAD_PAD = 128

PEER_HEADS = 8
PEER_KEYS = 128
PEER_EXPERTS = PEER_KEYS * PEER_KEYS
PEER_QDIM = 256
PEER_TOPK = 16

LANES = 128
SUBLANES = 8
VMEM_LIMIT = 56 * 1024 * 1024

ROW_TILE = 512
CONV_TILE = 256
PEER_TILE = 512
PEER_CHUNK = 512
PEER_CHUNKS_PER_STEP = 2
PEER_SUB = 256
PEER_SELECT_TOKENS = 128
ATTN_TQ = 1024
ATTN_TK = 512
ADA_COLS = 512
FFT1_COLS = 8192

F32 = jnp.float32
BF16 = jnp.bfloat16
HIGHEST = lax.Precision.HIGHEST
NEG_INF = float("-inf")


def _params(*sem):
    return pltpu.CompilerParams(dimension_semantics=sem, vmem_limit_bytes=VMEM_LIMIT)


def _mod_row(i, tile):
    return jnp.minimum((i * tile) // SEQ, BATCH)


def _full(shape):
    return pl.BlockSpec(shape, lambda *_: (0,) * len(shape))


def _silu(x):
    return x * (1.0 / (1.0 + jnp.exp(-x)))


def _modulated_norm(x, gain, mod, k_shift, k_scale):
    shift = mod[:, k_shift * D_MODEL:(k_shift + 1) * D_MODEL]
    scale = mod[:, k_scale * D_MODEL:(k_scale + 1) * D_MODEL]
    ms = jnp.mean(x * x, axis=-1, keepdims=True)
    return x * lax.rsqrt(ms + EPS) * gain * (1.0 + scale) + shift


def _hdot(a, b):
    return jnp.dot(a, b, precision=HIGHEST, preferred_element_type=F32)


def _bf16_terms(x, n):
    terms = []
    for _ in range(n):
        t = x.astype(BF16)
        terms.append(t)
        x = x - t.astype(F32)
    return terms


def _dot_exact_rhs(a, b01, terms=3):
    return sum(jnp.dot(t, b01, preferred_element_type=F32) for t in _bf16_terms(a, terms))


def _dot_exact_lhs(a01, b):
    return sum(jnp.dot(a01, t, preferred_element_type=F32) for t in _bf16_terms(b, 3))


def _dot3(a, b):
    a_hi, a_lo = _bf16_terms(a, 2)
    b_hi, b_lo = _bf16_terms(b, 2)
    dot = lambda x, y: jnp.dot(x, y, preferred_element_type=F32)
    return dot(a_hi, b_hi) + (dot(a_hi, b_lo) + dot(a_lo, b_hi))


def _ada_kernel(c_ref, w_ref, b_ref, o_ref):
    o_ref[...] = _hdot(_silu(c_ref[...]), w_ref[...]) + b_ref[...]


def ada_table(c, c_ctx, w_ada, b_ada):
    cond = jnp.concatenate([c, c_ctx[None, :], jnp.zeros((SUBLANES - BATCH - 1, D_MODEL), F32)], axis=0)
    tn = ADA_COLS
    out = pl.pallas_call(
        _ada_kernel,
        grid=(N_MOD * D_MODEL // tn,),
        in_specs=[_full((SUBLANES, D_MODEL)),
                  pl.BlockSpec((D_MODEL, tn), lambda j: (0, j)),
                  pl.BlockSpec((1, tn), lambda j: (0, j))],
        out_specs=pl.BlockSpec((SUBLANES, tn), lambda j: (0, j)),
        out_shape=jax.ShapeDtypeStruct((SUBLANES, N_MOD * D_MODEL), F32),
        compiler_params=_params("arbitrary"),
        name="ada_table",
    )(cond, w_ada, b_ada[None, :])
    return out[:BATCH + 1].reshape(BATCH + 1, 1, N_MOD * D_MODEL)


def _even_in_kernel(x_ref, mod_ref, g_ref, wf_ref, wz_ref, wx_ref, wd_ref, f_ref, z_ref, xbc_ref, dt_ref):
    h = _modulated_norm(x_ref[...], g_ref[...], mod_ref[0], 0, 1).astype(BF16)
    f_ref[...] = jnp.dot(h, wf_ref[...], preferred_element_type=F32)
    z_ref[...] = jnp.dot(h, wz_ref[...], preferred_element_type=F32)
    xbc_ref[...] = jnp.dot(h, wx_ref[...], preferred_element_type=F32)
    dt_ref[...] = jnp.dot(h, wd_ref[...], preferred_element_type=F32)


def even_in(x_all, mod, norm1, w_in):
    rows = x_all.shape[0]
    o1, o2, o3 = FNET_WIDTH, FNET_WIDTH + SSD_INNER, FNET_WIDTH + SSD_INNER + SSD_CONV_DIM
    wf = w_in[:, :o1].astype(BF16)
    wz = w_in[:, o1:o2].astype(BF16)
    wx = w_in[:, o2:o3].astype(BF16)
    wd = jnp.pad(w_in[:, o3:], ((0, 0), (0, LANES - 2 * SSD_HEADS))).astype(BF16)
    tm = ROW_TILE
    row = lambda n: pl.BlockSpec((tm, n), lambda i: (i, 0))
    return pl.pallas_call(
        _even_in_kernel,
        grid=(rows // tm,),
        in_specs=[row(D_MODEL),
                  pl.BlockSpec((1, 1, N_MOD * D_MODEL), lambda i: (_mod_row(i, tm), 0, 0)),
                  _full((1, D_MODEL)),
                  _full(wf.shape), _full(wz.shape), _full(wx.shape), _full(wd.shape)],
        out_specs=[row(FNET_WIDTH), row(SSD_INNER), row(SSD_CONV_DIM), row(LANES)],
        out_shape=[jax.ShapeDtypeStruct((rows, FNET_WIDTH), F32),
                   jax.ShapeDtypeStruct((rows, SSD_INNER), F32),
                   jax.ShapeDtypeStruct((rows, SSD_CONV_DIM), F32),
                   jax.ShapeDtypeStruct((rows, LANES), F32)],
        compiler_params=_params("arbitrary"),
        name="even_in",
    )(x_all, mod, norm1[None, :], wf, wz, wx, wd)


def _conv_kernel(x_ref, prev_ref, next_ref, w_ref, b_ref, o_ref, ext_ref, *, tm):
    row0 = pl.program_id(0) * tm
    in_lat = row0 < LAT_ROWS
    first = jnp.where(in_lat, row0 % SEQ == 0, (row0 - LAT_ROWS) % CTX_LEN == 0)
    last = jnp.where(in_lat, (row0 + tm) % SEQ == 0, (row0 + tm - LAT_ROWS) % CTX_LEN == 0)
    ext_ref[0:SUBLANES, :] = prev_ref[...] * jnp.where(first, 0.0, 1.0)
    ext_ref[SUBLANES:SUBLANES + tm, :] = x_ref[...]
    ext_ref[SUBLANES + tm:2 * SUBLANES + tm, :] = next_ref[...] * jnp.where(last, 0.0, 1.0)
    pad = SSD_CONV // 2
    acc = b_ref[...] + w_ref[0:1, :] * ext_ref[pl.ds(SUBLANES - pad, tm), :]
    for k in range(1, SSD_CONV):
        acc = acc + w_ref[k:k + 1, :] * ext_ref[pl.ds(SUBLANES - pad + k, tm), :]
    o_ref[...] = _silu(acc)


def conv_silu(xbc, conv_w, conv_b):
    rows, ch = xbc.shape
    tm, tc = CONV_TILE, ch
    assert CTX_LEN % tm == 0 and SEQ % tm == 0
    per = tm // SUBLANES
    n_small = rows // SUBLANES
    w = jnp.pad(conv_w, ((0, SUBLANES - SSD_CONV), (0, 0)))
    return pl.pallas_call(
        functools.partial(_conv_kernel, tm=tm),
        grid=(rows // tm, ch // tc),
        in_specs=[pl.BlockSpec((tm, tc), lambda i, j: (i, j)),
                  pl.BlockSpec((SUBLANES, tc), lambda i, j: (jnp.maximum(i * per - 1, 0), j)),
                  pl.BlockSpec((SUBLANES, tc), lambda i, j: (jnp.minimum((i + 1) * per, n_small - 1), j)),
                  pl.BlockSpec((SUBLANES, tc), lambda i, j: (0, j)),
                  pl.BlockSpec((1, tc), lambda i, j: (0, j))],
        out_specs=pl.BlockSpec((tm, tc), lambda i, j: (i, j)),
        out_shape=jax.ShapeDtypeStruct((rows, ch), F32),
        scratch_shapes=[pltpu.VMEM((tm + 2 * SUBLANES, tc), F32)],
        compiler_params=_params("arbitrary", "arbitrary"),
        name="conv_silu",
    )(xbc, xbc, xbc, w, conv_b[None, :])


def _softplus(x):
    return jnp.maximum(x, 0.0) + jnp.log1p(jnp.exp(-jnp.abs(x)))


def _ssd_direction(u_ref, dt_ref, dtb_ref, alog_ref, exp_ref, state_ref, y_ref, direction):
    cl = SSD_CHUNK
    xs = u_ref[:, :SSD_INNER]
    dtv = _softplus(dt_ref[...] + dtb_ref[...])
    dta = dtv * (-jnp.exp(alog_ref[...]))
    r = lax.broadcasted_iota(jnp.int32, (cl, cl), 0)
    c = lax.broadcasted_iota(jnp.int32, (cl, cl), 1)
    tri = (r >= c) if direction == 0 else (r <= c)
    cs = _dot_exact_lhs(tri.astype(F32).astype(BF16), dta)
    cs_t = cs.T
    edge = cs[cl - 1:cl, :] if direction == 0 else cs[0:1, :]
    to_end = jnp.exp(edge - cs)
    from_start = jnp.exp(cs)
    spread = _dot_exact_rhs(jnp.concatenate([dtv, to_end, from_start], axis=0), exp_ref[direction], terms=2)
    dt_x, te_x, fs_x = spread[:cl], spread[cl:2 * cl], spread[2 * cl:]
    chunk_decay = fs_x[cl - 1:cl, :] if direction == 0 else fs_x[0:1, :]
    xd = xs * dt_x
    xd_b = xd.astype(BF16)
    xte_b = (xd * te_x).astype(BF16)
    lane = lax.broadcasted_iota(jnp.int32, (cl, LANES), 1)
    low = lane < SSD_HEAD_DIM
    for g in range(SSD_GROUPS):
        bg = u_ref[:, SSD_INNER + g * SSD_STATE:SSD_INNER + (g + 1) * SSD_STATE]
        cg = u_ref[:, SSD_INNER + (SSD_GROUPS + g) * SSD_STATE:SSD_INNER + (SSD_GROUPS + g + 1) * SSD_STATE]
        bg_b, cg_b = bg.astype(BF16), cg.astype(BF16)
        cb = lax.dot_general(cg_b, bg_b, (((1,), (1,)), ((), ())), preferred_element_type=F32)
        gs = slice(g * SSD_GROUP_W, (g + 1) * SSD_GROUP_W)
        state = state_ref[g]
        y_off = jnp.dot(cg_b, state.astype(BF16), preferred_element_type=F32) * fs_x[:, gs]
        new_state = jnp.dot(bg.T.astype(BF16), xte_b[:, gs], preferred_element_type=F32)
        state_ref[g] = state * chunk_decay[:, gs] + new_state
        for pair in range(2):
            blk = xd_b[:, g * SSD_GROUP_W + pair * LANES:g * SSD_GROUP_W + (pair + 1) * LANES]
            ms, halves = [], []
            for sub in range(2):
                col = direction * SSD_HEADS + g * 4 + pair * 2 + sub
                seg = cs[:, col:col + 1] - cs_t[col:col + 1, :]
                decay = jnp.exp(jnp.where(tri, seg, NEG_INF))
                ms.append((cb * decay).astype(BF16))
                halves.append(jnp.where(low if sub == 0 else jnp.logical_not(low), blk, jnp.zeros_like(blk)))
            y_pair = jnp.dot(jnp.concatenate(ms, axis=1), jnp.concatenate(halves, axis=0),
                             preferred_element_type=F32)
            lo = g * SSD_GROUP_W + pair * LANES
            y_ref[:, lo:lo + LANES] = y_pair + y_off[:, pair * LANES:(pair + 1) * LANES]


def _ssd_kernel(uf_ref, ub_ref, dtf_ref, dtb_in_ref, bias_ref, alog_ref, exp_ref, yf_ref, yb_ref, sf_ref, sb_ref):
    @pl.when(pl.program_id(1) == 0)
    def _():
        sf_ref[...] = jnp.zeros_like(sf_ref)
        sb_ref[...] = jnp.zeros_like(sb_ref)

    _ssd_direction(uf_ref, dtf_ref, bias_ref, alog_ref, exp_ref, sf_ref, yf_ref, 0)
    _ssd_direction(ub_ref, dtb_in_ref, bias_ref, alog_ref, exp_ref, sb_ref, yb_ref, 1)


def ssd_scan(u, dt, dt_bias, a_log):
    rows = u.shape[0]
    cl = SSD_CHUNK
    lat_chunks, ctx_chunks = SEQ // cl, CTX_LEN // cl
    steps = ctx_chunks + lat_chunks
    ctx0 = LAT_ROWS // cl

    def fwd_chunk(b, t):
        return jnp.where(t < ctx_chunks, ctx0 + b * ctx_chunks + t, b * lat_chunks + t - ctx_chunks)

    def bwd_chunk(b, t):
        return jnp.where(t < ctx_chunks, ctx0 + b * ctx_chunks + (ctx_chunks - 1 - t),
                         b * lat_chunks + (lat_chunks - 1) - (t - ctx_chunks))

    pad = LANES - 2 * SSD_HEADS
    bias = jnp.pad(dt_bias.reshape(1, -1), ((0, 0), (0, pad)))
    alog = jnp.pad(a_log.reshape(1, -1), ((0, 0), (0, pad)))
    expand = np.zeros((2, LANES, SSD_INNER), np.float32)
    for d in range(2):
        for h in range(SSD_HEADS):
            expand[d, d * SSD_HEADS + h, h * SSD_HEAD_DIM:(h + 1) * SSD_HEAD_DIM] = 1.0
    spec = lambda n, fn: pl.BlockSpec((cl, n), lambda b, t: (fn(b, t), 0))
    return pl.pallas_call(
        _ssd_kernel,
        grid=(BATCH, steps),
        in_specs=[spec(SSD_CONV_DIM, fwd_chunk), spec(SSD_CONV_DIM, bwd_chunk),
                  spec(LANES, fwd_chunk), spec(LANES, bwd_chunk),
                  _full((1, LANES)), _full((1, LANES)), _full((2, LANES, SSD_INNER))],
        out_specs=[spec(SSD_INNER, fwd_chunk), spec(SSD_INNER, bwd_chunk)],
        out_shape=[jax.ShapeDtypeStruct((rows, SSD_INNER), F32)] * 2,
        scratch_shapes=[pltpu.VMEM((SSD_GROUPS, SSD_STATE, SSD_GROUP_W), F32)] * 2,
        compiler_params=_params("arbitrary", "arbitrary"),
        name="ssd_scan",
    )(u, u, dt, dt, bias, alog, jnp.asarray(expand, BF16))


def _dft_cos_sin(n):
    k = np.arange(n)
    ang = 2.0 * np.pi * ((k[:, None] * k[None, :]) % n) / n
    return np.cos(ang), np.sin(ang)


def _fft1_kernel(x_ref, m_ref, o_ref):
    o_ref[0] = _dot3(m_ref[...], x_ref[...])


def _fft2_kernel(ar_ref, ai_ref, twr_ref, twi_ref, m2_ref, mc_ref, o_ref):
    n2 = FFT_N2
    for j in range(SUBLANES):
        ar, ai = ar_ref[0, j], ai_ref[0, j]
        twr = jnp.concatenate([twr_ref[j]] * FNET_GROUPS, axis=1)
        twi = jnp.concatenate([twi_ref[j]] * FNET_GROUPS, axis=1)
        p = ar * twr - ai * twi
        q = ar * twi + ai * twr
        uv = _dot3(m2_ref[...], jnp.concatenate([p, q], axis=0))
        groups = []
        for g in range(FNET_GROUPS):
            gs = slice(g * FNET_GROUP_DIM, (g + 1) * FNET_GROUP_DIM)
            groups.append(_dot3(jnp.concatenate([uv[:n2, gs], uv[n2:, gs]], axis=1), mc_ref[...]))
        o_ref[0, :, j, :] = jnp.concatenate(groups, axis=1)


def _fft_ctx_kernel(x_ref, mc_ref, mp_ref, o_ref):
    x = x_ref[...]
    for g in range(FNET_GROUPS):
        gs = slice(g * FNET_GROUP_DIM, (g + 1) * FNET_GROUP_DIM)
        cs = _dot3(x[:, gs], mc_ref[...])
        stacked = jnp.concatenate([cs[:, :FNET_GROUP_DIM], cs[:, FNET_GROUP_DIM:]], axis=0)
        o_ref[:, gs] = _dot3(mp_ref[...], stacked)


def fourier_mix(f_all):
    rows = f_all.shape[0]
    n1, n2, gd = FFT_N1, FFT_N2, FNET_GROUP_DIM
    row_w = n2 * FNET_WIDTH
    c1, s1 = _dft_cos_sin(n1)
    m1 = jnp.asarray(np.concatenate([c1, -s1], axis=0), F32)
    tn = FFT1_COLS
    stage1 = pl.pallas_call(
        _fft1_kernel,
        grid=(BATCH, row_w // tn),
        in_specs=[pl.BlockSpec((n1, tn), lambda b, j: (b, j)), _full((2 * n1, n1))],
        out_specs=pl.BlockSpec((1, 2 * n1, tn), lambda b, j: (b, 0, j)),
        out_shape=jax.ShapeDtypeStruct((BATCH, 2 * n1, row_w), F32),
        compiler_params=_params("arbitrary", "arbitrary"),
        name="fft_stage1",
    )(f_all.reshape(rows // n2, row_w), m1)

    k1 = np.arange(n1)[:, None]
    l2 = np.arange(n2)[None, :]
    tw = 2.0 * np.pi * (k1 * l2) / SEQ
    twr = jnp.asarray(np.repeat(np.cos(tw)[:, :, None], gd, axis=2), F32)
    twi = jnp.asarray(np.repeat(-np.sin(tw)[:, :, None], gd, axis=2), F32)
    c2, s2 = _dft_cos_sin(n2)
    m2 = jnp.asarray(np.block([[c2, s2], [-s2, c2]]), F32)
    cc, sc = _dft_cos_sin(gd)
    mc = jnp.asarray(np.concatenate([cc, sc], axis=0) / math.sqrt(SEQ * gd), F32)
    a4 = stage1.reshape(BATCH, 2 * n1, n2, FNET_WIDTH)
    sub = SUBLANES
    lat = pl.pallas_call(
        _fft2_kernel,
        grid=(BATCH, n1 // sub),
        in_specs=[pl.BlockSpec((1, sub, n2, FNET_WIDTH), lambda b, k: (b, k, 0, 0)),
                  pl.BlockSpec((1, sub, n2, FNET_WIDTH), lambda b, k: (b, n1 // sub + k, 0, 0)),
                  pl.BlockSpec((sub, n2, gd), lambda b, k: (k, 0, 0)),
                  pl.BlockSpec((sub, n2, gd), lambda b, k: (k, 0, 0)),
                  _full((2 * n2, 2 * n2)), _full((2 * gd, gd))],
        out_specs=pl.BlockSpec((1, n2, sub, FNET_WIDTH), lambda b, k: (b, 0, k, 0)),
        out_shape=jax.ShapeDtypeStruct((BATCH, n2, n1, FNET_WIDTH), F32),
        compiler_params=_params("arbitrary", "arbitrary"),
        name="fft_stage2",
    )(a4, a4, twr, twi, m2, mc)

    cp, sp = _dft_cos_sin(CTX_LEN)
    mp = jnp.asarray(np.concatenate([cp, -sp], axis=1) / math.sqrt(CTX_LEN * gd), F32)
    mcc = jnp.asarray(np.concatenate([cc, sc], axis=1), F32)
    ctx0 = LAT_ROWS // CTX_LEN
    ctx = pl.pallas_call(
        _fft_ctx_kernel,
        grid=(BATCH,),
        in_specs=[pl.BlockSpec((CTX_LEN, FNET_WIDTH), lambda b: (ctx0 + b, 0)),
                  _full((gd, 2 * gd)), _full((CTX_LEN, 2 * CTX_LEN))],
        out_specs=pl.BlockSpec((CTX_LEN, FNET_WIDTH), lambda b: (b, 0)),
        out_shape=jax.ShapeDtypeStruct((BATCH * CTX_LEN, FNET_WIDTH), F32),
        compiler_params=_params("arbitrary"),
        name="fft_ctx",
    )(f_all, mcc, mp)
    return jnp.concatenate([lat.reshape(LAT_ROWS, FNET_WIDTH), ctx], axis=0)


def _even_out_kernel(x_ref, mod_ref, four_ref, yf_ref, yb_ref, xs_ref, z_ref, dsk_ref, gn_ref, wof_ref, wos_ref,
                     o_ref):
    y = yf_ref[...] + yb_ref[...] + dsk_ref[...] * xs_ref[...]
    gated = y * _silu(z_ref[...])
    out = jnp.dot(four_ref[...].astype(BF16), wof_ref[...], preferred_element_type=F32)
    for g in range(SSD_GROUPS):
        gs = slice(g * SSD_GROUP_W, (g + 1) * SSD_GROUP_W)
        v = gated[:, gs]
        normed = v * lax.rsqrt(jnp.mean(v * v, axis=-1, keepdims=True) + EPS) * gn_ref[:, gs]
        out = out + jnp.dot(normed.astype(BF16), wos_ref[gs, :], preferred_element_type=F32)
    gate = mod_ref[0][:, 2 * D_MODEL:3 * D_MODEL]
    o_ref[...] = x_ref[...] + gate * out


def even_out(x_all, mod, four, yf, yb, u, z, d_skip, ssd_norm, w_out):
    rows = x_all.shape[0]
    tm = ROW_TILE
    dsk = jnp.repeat(d_skip[0] + d_skip[1], SSD_HEAD_DIM)[None, :]
    wof = w_out[:FNET_WIDTH].astype(BF16)
    wos = w_out[FNET_WIDTH:].astype(BF16)
    row = lambda n: pl.BlockSpec((tm, n), lambda i: (i, 0))
    return pl.pallas_call(
        _even_out_kernel,
        grid=(rows // tm,),
        in_specs=[row(D_MODEL),
                  pl.BlockSpec((1, 1, N_MOD * D_MODEL), lambda i: (_mod_row(i, tm), 0, 0)),
                  row(FNET_WIDTH), row(SSD_INNER), row(SSD_INNER), row(SSD_INNER), row(SSD_INNER),
                  _full((1, SSD_INNER)), _full((1, SSD_INNER)), _full(wof.shape), _full(wos.shape)],
        out_specs=row(D_MODEL),
        out_shape=jax.ShapeDtypeStruct((rows, D_MODEL), F32),
        compiler_params=_params("arbitrary"),
        name="even_out",
    )(x_all, mod, four, yf, yb, u, z, dsk, ssd_norm[None, :], wof, wos)


BF16_ROWS = 2 * SUBLANES
NOT_TOP = 64.0
AFTER_EVERY_ORDER = float(PEER_TOPK * PEER_KEYS)


def _erf_gelu(z):
    return 0.5 * z * (1.0 + lax.erf(z * (1.0 / math.sqrt(2.0))))


def _merge_exchange_pairs(n):
    t = max(1, math.ceil(math.log2(n)))
    p, pairs = 2 ** (t - 1), []
    while p > 0:
        q, r, d = 2 ** (t - 1), 0, p
        while d > 0:
            pairs += [(i, i + d) for i in range(n - d) if i & p == r]
            d, q, r = q - p, q // 2, p
        p //= 2
    return pairs


def _top_values_sorted(score, k):
    groups = [score[r * SUBLANES:(r + 1) * SUBLANES] for r in range(score.shape[0] // SUBLANES)]
    for a, b in _merge_exchange_pairs(len(groups)):
        groups[a], groups[b] = jnp.maximum(groups[a], groups[b]), jnp.minimum(groups[a], groups[b])
    vals = []
    for i in range(k):
        m = jnp.max(groups[0], axis=0, keepdims=True)
        vals.append(m)
        needed = k - 1 - i
        if needed == 0:
            break
        hit = groups[0] == m
        shifted = [jnp.where(hit, groups[d + 1], groups[d]) for d in range(len(groups) - 1)]
        if len(groups) <= needed:
            shifted.append(jnp.where(hit, NEG_INF, groups[-1]))
        groups = shifted[:needed]
    return jnp.concatenate(vals, axis=0)


def _rank_among(values, top, k):
    assert k & (k - 1) == 0
    bits, count, stride = [], jnp.zeros_like(values), k // 2
    while stride >= 1:
        cands = [top[p + stride - 1:p + stride] for p in range(0, k, 2 * stride)]
        for b in reversed(bits):
            cands = [jnp.where(b, cands[2 * i + 1], cands[2 * i]) for i in range(len(cands) // 2)]
        bit = cands[0] > values
        count = count + jnp.where(bit, float(stride), 0.0)
        bits.append(bit)
        stride //= 2
    return jnp.where(values >= top[k - 1:k], count, NOT_TOP)


def _pair_candidates(v1, v2, k):
    assert k == 2 * SUBLANES, "the row grouping below is laid out for k = 16"
    t = v1.shape[1]
    row8 = lax.broadcasted_iota(jnp.int32, (SUBLANES, t), 0)
    rowf = row8.astype(F32)
    sums = [v1[0:1] + v2[0:8], v1[0:1] + v2[8:16], v1[1:2] + v2[0:8], v1[8:16] + v2[0:1]]
    order = [rowf, rowf + 8.0, rowf + float(k), (rowf + 8.0) * float(k)]
    for i in range(2, 8):
        sums.append(jnp.where(row8 < k // (i + 1), v1[i:i + 1] + v2[0:8], NEG_INF))
        order.append(rowf + float(i * k))
    return jnp.concatenate(sums, axis=0), jnp.concatenate(order, axis=0)


def _select_by_value(s1, s2, k):
    v1 = _top_values_sorted(s1, k)
    v2 = _top_values_sorted(s2, k)
    r2 = _rank_among(s2, v2, k)
    cand, _ = _pair_candidates(v1, v2, k)
    tau = _top_values_sorted(cand, k)[k - 1:k, :]
    chosen = cand >= tau
    zsum = jnp.sum(jnp.where(chosen, jnp.exp(cand - (v1[0:1] + v2[0:1])), 0.0), axis=0, keepdims=True)
    in_top = s1 >= v1[k - 1:k]
    a1 = jnp.where(in_top, s1, NEG_INF)
    lim = jnp.zeros_like(s1)
    for j in range(k // 2):
        lim = lim + jnp.where(a1 + v2[j:j + 1] >= tau, 1.0, 0.0)
    best = jnp.zeros_like(tau)
    for j in range(k // 2, k):
        best = best + jnp.where(v1[0:1] + v2[j:j + 1] >= tau, 1.0, 0.0)
    lim = lim + jnp.where(s1 == v1[0:1], best, 0.0)
    count = lambda mask: jnp.sum(jnp.where(mask, 1.0, 0.0), axis=0, keepdims=True)
    most = jnp.maximum(jnp.maximum(count(in_top), count(r2 < k)), count(chosen))
    return lim, jnp.exp(s1 - v1[0:1]) / zsum, r2, jnp.exp(s2 - v2[0:1]), most


def _top_ranks_ordered(score, order, k):
    work, vals = score, []
    rank = jnp.full(score.shape, NOT_TOP, F32)
    for i in range(k):
        m = jnp.max(work, axis=0, keepdims=True)
        first = jnp.min(jnp.where(work == m, order, AFTER_EVERY_ORDER), axis=0, keepdims=True)
        taken = order == first
        vals.append(m)
        rank = jnp.where(taken, float(i), rank)
        work = jnp.where(taken, NEG_INF, work)
    return jnp.concatenate(vals, axis=0), rank


def _select_by_order(s1, s2, k):
    key_index = lax.broadcasted_iota(jnp.int32, s1.shape, 0).astype(F32)
    v1, r1 = _top_ranks_ordered(s1, key_index, k)
    v2, r2 = _top_ranks_ordered(s2, key_index, k)
    cand, position = _pair_candidates(v1, v2, k)
    chosen = _top_ranks_ordered(cand, position, k)[1] < k
    zsum = jnp.sum(jnp.where(chosen, jnp.exp(cand - (v1[0:1] + v2[0:1])), 0.0), axis=0, keepdims=True)
    picks = jnp.where(chosen, 1.0, 0.0)
    group = lambda g: jnp.sum(picks[g * SUBLANES:(g + 1) * SUBLANES], axis=0, keepdims=True)
    per_rank = [group(0) + group(1), group(2)] + [group(g) for g in range(4, 10)]
    per_rank = jnp.concatenate(per_rank + [picks[3 * SUBLANES:4 * SUBLANES]], axis=0)
    lim = jnp.zeros_like(s1)
    for i in range(k):
        lim = lim + jnp.where(r1 == float(i), per_rank[i:i + 1], 0.0)
    return lim, jnp.exp(s1 - v1[0:1]) / zsum, r2, jnp.exp(s2 - v2[0:1])


def _peer_kernel(x_ref, mod_ref, g_ref, wq_ref, keys_ref, u0_ref, *rest, tm, chunk, per_step):
    u_refs = rest[:per_step]
    (vt_ref, o_ref, ht_ref, q_ref, lim_ref, w1_ref, rank2_ref, w2_ref, acc_ref, za_ref, zb_ref) = rest[per_step:]
    c = pl.program_id(1)
    k = PEER_TOPK
    packed = (PEER_KEYS // BF16_ROWS, BF16_ROWS, tm)
    tsel = PEER_SELECT_TOKENS

    @pl.when(c == 0)
    def _prologue():
        h2 = _modulated_norm(x_ref[...], g_ref[...], mod_ref[0], 3, 4)
        ht = h2.T.astype(BF16)
        ht_ref[...] = ht
        q = jnp.dot(wq_ref[...], ht, preferred_element_type=F32)
        q_ref[...] = q.reshape(2 * PEER_HEADS, PEER_QDIM // 2, tm)
        acc_ref[...] = jnp.zeros_like(acc_ref)
        za_ref[...] = jnp.dot(u0_ref[...], ht, preferred_element_type=F32)
        def head_body(h, carry):
            def scores(part):
                ts = slice(part * tsel, (part + 1) * tsel)
                s1 = jnp.dot(keys_ref[2 * h], q_ref[2 * h, :, ts].astype(BF16), preferred_element_type=F32)
                s2 = jnp.dot(keys_ref[2 * h + 1], q_ref[2 * h + 1, :, ts].astype(BF16), preferred_element_type=F32)
                return ts, s1, s2

            def store(ts, lim, w1, r2, w2):
                small = (PEER_KEYS // BF16_ROWS, BF16_ROWS, tsel)
                lim_ref[h, :, ts] = lim
                w1_ref[h, :, ts] = w1
                rank2_ref[h, :, :, ts] = r2.reshape(small).astype(BF16)
                w2_ref[h, :, :, ts] = w2.reshape(small).astype(BF16)

            counts = []
            for part in range(tm // tsel):
                ts, s1, s2 = scores(part)
                lim, w1, r2, w2, count = _select_by_value(s1, s2, k)
                store(ts, lim, w1, r2, w2)
                counts.append(count)

            @pl.when(jnp.max(functools.reduce(jnp.maximum, counts)) > k)
            def _ties():
                for part, count in enumerate(counts):
                    @pl.when(jnp.max(count) > k)
                    def _redo():
                        ts, s1, s2 = scores(part)
                        store(ts, *_select_by_order(s1, s2, k))

            return carry

        lax.fori_loop(0, PEER_HEADS, head_body, 0)

    per_key = PEER_KEYS // BF16_ROWS
    keys_per_sub = PEER_SUB // PEER_KEYS
    step_keys = per_step * chunk // PEER_KEYS
    key0 = pl.multiple_of(c * step_keys, step_keys)
    step_lim = [lim_ref[h, pl.ds(key0, step_keys), :] for h in range(PEER_HEADS)]
    step_w1 = [w1_ref[h, pl.ds(key0, step_keys), :] for h in range(PEER_HEADS)]

    def gated_values(z_ref, which):
        total = None
        for j in range(chunk // PEER_SUB):
            rows = slice(j * PEER_SUB, (j + 1) * PEER_SUB)
            act = _erf_gelu(z_ref[rows, :]).reshape(PEER_SUB // BF16_ROWS, BF16_ROWS, tm).astype(BF16)
            parts = []
            for jj in range(keys_per_sub):
                e1 = which * (chunk // PEER_KEYS) + j * keys_per_sub + jj
                gate = jnp.zeros(packed, BF16)
                for h in range(PEER_HEADS):
                    lim = jnp.broadcast_to(step_lim[h][e1:e1 + 1, :], (BF16_ROWS, tm)).astype(BF16)
                    w1 = jnp.broadcast_to(step_w1[h][e1:e1 + 1, :], (BF16_ROWS, tm)).astype(BF16)
                    picked = jnp.where(rank2_ref[h] < lim[None], w2_ref[h], jnp.zeros(packed, BF16))
                    gate = gate + picked * w1[None]
                parts.append(act[jj * per_key:(jj + 1) * per_key] * gate)
            a = jnp.concatenate(parts, axis=0).reshape(PEER_SUB, tm)
            cols = slice(which * chunk + j * PEER_SUB, which * chunk + (j + 1) * PEER_SUB)
            d = jnp.dot(vt_ref[:, cols], a, preferred_element_type=F32)
            total = d if total is None else total + d
        return total

    bufs = (za_ref, zb_ref)
    total = None
    for i in range(per_step):
        bufs[(i + 1) % 2][...] = jnp.dot(u_refs[i][...], ht_ref[...], preferred_element_type=F32)
        d = gated_values(bufs[i % 2], i)
        total = d if total is None else total + d
    acc_ref[...] += total

    @pl.when(c == pl.num_programs(1) - 1)
    def _epilogue():
        gate2 = mod_ref[0][:, 5 * D_MODEL:6 * D_MODEL]
        o_ref[...] = x_ref[...] + gate2 * acc_ref[...].T


def peer_layer(x_rows, mod, norm2, peer_wq, peer_keys, peer_u, peer_v):
    rows = x_rows.shape[0]
    tm, chunk = PEER_TILE, PEER_CHUNK
    wq_t = peer_wq.T.astype(BF16)
    keys = peer_keys.reshape(2 * PEER_HEADS, PEER_KEYS, PEER_QDIM // 2).astype(BF16)
    u = peer_u.astype(BF16)
    vt = peer_v.astype(BF16).T
    head_f32 = pltpu.VMEM((PEER_HEADS, PEER_KEYS, tm), F32)
    head_bf16 = pltpu.VMEM((PEER_HEADS, PEER_KEYS // BF16_ROWS, BF16_ROWS, tm), BF16)
    n_chunks = PEER_EXPERTS // chunk
    per_step = PEER_CHUNKS_PER_STEP
    ahead = lambda k: pl.BlockSpec((chunk, D_MODEL), lambda i, c: (jnp.minimum(per_step * c + k, n_chunks - 1), 0))
    return pl.pallas_call(
        functools.partial(_peer_kernel, tm=tm, chunk=chunk, per_step=per_step),
        grid=(rows // tm, n_chunks // per_step),
        in_specs=[pl.BlockSpec((tm, D_MODEL), lambda i, c: (i, 0)),
                  pl.BlockSpec((1, 1, N_MOD * D_MODEL), lambda i, c: (_mod_row(i, tm), 0, 0)),
                  _full((1, D_MODEL)), _full(wq_t.shape), _full(keys.shape),
                  pl.BlockSpec((chunk, D_MODEL), lambda i, c: (0, 0))]
                 + [ahead(k) for k in range(1, per_step + 1)]
                 + [pl.BlockSpec((D_MODEL, per_step * chunk), lambda i, c: (0, c))],
        out_specs=pl.BlockSpec((tm, D_MODEL), lambda i, c: (i, 0)),
        out_shape=jax.ShapeDtypeStruct((rows, D_MODEL), F32),
        scratch_shapes=[pltpu.VMEM((D_MODEL, tm), BF16),
                        pltpu.VMEM((2 * PEER_HEADS, PEER_QDIM // 2, tm), F32),
                        head_f32, head_f32, head_bf16, head_bf16,
                        pltpu.VMEM((D_MODEL, tm), F32),
                        pltpu.VMEM((chunk, tm), F32), pltpu.VMEM((chunk, tm), F32)],
        compiler_params=_params("arbitrary", "arbitrary"),
        name="peer",
    )(x_rows, mod, norm2[None, :], wq_t, keys, u, *([u] * per_step), vt)


def _rope_tables(rows):
    l = jnp.arange(SEQ)
    per_axis = MLA_ROPE // 2
    inv = ROPE_THETA ** (-jnp.arange(0, per_axis, 2, dtype=F32) / per_axis)
    ang = jnp.concatenate([(l // GRID_W)[:, None] * inv, (l % GRID_W)[:, None] * inv], axis=-1)
    cos, sin = jnp.cos(ang), jnp.sin(ang)
    one = jnp.ones((SEQ, MLA_NOPE), F32)
    tail = jnp.zeros((SEQ, HEAD_PAD - MLA_QK), F32)
    zero64 = jnp.zeros((SEQ, MLA_NOPE), F32)
    cos_t = jnp.concatenate([one, cos, cos, tail], axis=1)
    sin_t = jnp.concatenate([zero64, -sin, sin, tail], axis=1)
    n_ctx = rows - LAT_ROWS
    ident = jnp.concatenate([jnp.ones((n_ctx, MLA_QK), F32), jnp.zeros((n_ctx, HEAD_PAD - MLA_QK), F32)], axis=1)
    zeros = jnp.zeros((n_ctx, HEAD_PAD), F32)
    tile = lambda t, c: jnp.concatenate([t] * BATCH + [c], axis=0)
    return tile(cos_t, ident), tile(sin_t, zeros)


def _rope_swap_matrix():
    per_axis = MLA_ROPE // 2
    swap = np.zeros((HEAD_PAD, HEAD_PAD), np.float32)
    for j in range(per_axis):
        swap[MLA_NOPE + per_axis + j, MLA_NOPE + j] = 1.0
        swap[MLA_NOPE + j, MLA_NOPE + per_axis + j] = 1.0
    return swap


def _head_norm_rope(t, gain, cos, sin, ones, swap):
    ms = jnp.dot((t * t).astype(BF16), ones, preferred_element_type=F32) * (1.0 / MLA_QK)
    t = t * lax.rsqrt(ms + EPS) * gain
    return t * cos + jnp.dot(t.astype(BF16), swap, preferred_element_type=F32) * sin


def _mla_in_kernel(x_ref, mod_ref, g_ref, wdq_ref, wdkv_ref, wpe_ref, qan_ref, kvan_ref, wuq_ref, wuk_ref, wuv_ref,
                   qn_ref, kn_ref, ones_ref, swap_ref, cos_ref, sin_ref, q_ref, k_ref, v_ref):
    h = _modulated_norm(x_ref[...], g_ref[...], mod_ref[0], 0, 1).astype(BF16)
    dq = jnp.dot(h, wdq_ref[...], preferred_element_type=F32)
    dkv = jnp.dot(h, wdkv_ref[...], preferred_element_type=F32)
    kpe = jnp.dot(h, wpe_ref[...], preferred_element_type=F32)
    qa = dq * lax.rsqrt(jnp.mean(dq * dq, axis=-1, keepdims=True) + EPS) * qan_ref[...]
    kva = (dkv * lax.rsqrt(jnp.mean(dkv * dkv, axis=-1, keepdims=True) + EPS) * kvan_ref[...]).astype(BF16)
    q = jnp.dot(qa.astype(BF16), wuq_ref[...], preferred_element_type=F32)
    kn = jnp.dot(kva, wuk_ref[...], preferred_element_type=F32)
    v_ref[...] = jnp.dot(kva, wuv_ref[...], preferred_element_type=F32).astype(BF16)
    cos, sin, ones, swap = cos_ref[...], sin_ref[...], ones_ref[...], swap_ref[...]
    scale = MLA_QK ** -0.5 * math.log2(math.e)
    for hd in range(MLA_HEADS):
        hs = slice(hd * HEAD_PAD, (hd + 1) * HEAD_PAD)
        q_ref[:, hs] = (_head_norm_rope(q[:, hs], qn_ref[...], cos, sin, ones, swap) * scale).astype(BF16)
        k_ref[:, hs] = _head_norm_rope(kn[:, hs] + kpe, kn_ref[...], cos, sin, ones, swap).astype(BF16)


def mla_in(x_all, mod, p):
    rows = x_all.shape[0]
    tm = ROW_TILE
    w = p["w_dqkv"]
    wdq = w[:, :MLA_Q_LORA].astype(BF16)
    wdkv = w[:, MLA_Q_LORA:MLA_Q_LORA + MLA_KV_LORA].astype(BF16)
    wpe = jnp.pad(w[:, MLA_Q_LORA + MLA_KV_LORA:], ((0, 0), (MLA_NOPE, HEAD_PAD - MLA_QK))).astype(BF16)
    wuq = jnp.pad(p["w_uq"].reshape(MLA_Q_LORA, MLA_HEADS, MLA_QK), ((0, 0), (0, 0), (0, HEAD_PAD - MLA_QK)))
    wuq = wuq.reshape(MLA_Q_LORA, MLA_HEADS * HEAD_PAD).astype(BF16)
    wukv = p["w_ukv"].reshape(MLA_KV_LORA, MLA_HEADS, MLA_NOPE + MLA_V)
    wuk = jnp.pad(wukv[:, :, :MLA_NOPE], ((0, 0), (0, 0), (0, HEAD_PAD - MLA_NOPE)))
    wuk = wuk.reshape(MLA_KV_LORA, MLA_HEADS * HEAD_PAD).astype(BF16)
    wuv = wukv[:, :, MLA_NOPE:].reshape(MLA_KV_LORA, MLA_HEADS * MLA_V).astype(BF16)
    padg = lambda g: jnp.pad(g, (0, HEAD_PAD - MLA_QK))[None, :]
    cos, sin = _rope_tables(rows)
    row = lambda n: pl.BlockSpec((tm, n), lambda i: (i, 0))
    weights = [wdq, wdkv, wpe, p["q_a_norm"][None, :], p["kv_a_norm"][None, :], wuq, wuk, wuv,
               padg(p["q_norm"]), padg(p["k_norm"]),
               jnp.ones((HEAD_PAD, HEAD_PAD), BF16), jnp.asarray(_rope_swap_matrix(), BF16)]
    return pl.pallas_call(
        _mla_in_kernel,
        grid=(rows // tm,),
        in_specs=[row(D_MODEL),
                  pl.BlockSpec((1, 1, N_MOD * D_MODEL), lambda i: (_mod_row(i, tm), 0, 0)),
                  _full((1, D_MODEL))] + [_full(a.shape) for a in weights] + [row(HEAD_PAD)] * 2,
        out_specs=[row(MLA_HEADS * HEAD_PAD), row(MLA_HEADS * HEAD_PAD), row(MLA_HEADS * MLA_V)],
        out_shape=[jax.ShapeDtypeStruct((rows, MLA_HEADS * HEAD_PAD), BF16),
                   jax.ShapeDtypeStruct((rows, MLA_HEADS * HEAD_PAD), BF16),
                   jax.ShapeDtypeStruct((rows, MLA_HEADS * MLA_V), BF16)],
        compiler_params=_params("arbitrary"),
        name="mla_in",
    )(x_all, mod, p["norm1"][None, :], *weights, cos, sin)


ATTN_SAFE_SHIFT = 50.0
ATTN_BOUND_SLACK = 1.001
ATTN_BOUND_MARGIN = 1e-3


def _attn_kernel(q_ref, kc_ref, kl_ref, vc_ref, vl_ref, o_ref, knorm_ref, *, tq, tk):
    heads = (slice(0, HEAD_PAD), slice(HEAD_PAD, 2 * HEAD_PAD))
    qs = [q_ref[:, hs] for hs in heads]
    n_lat = SEQ // tk

    @pl.when(pl.program_id(2) == 0)
    def _key_norms():
        def sq_max(k2, hh):
            kf = k2[:, heads[hh]].astype(F32)
            return jnp.max(jnp.sum(kf * kf, axis=-1, keepdims=True), axis=0, keepdims=True)

        for hh in range(2):
            best = sq_max(kc_ref[...], hh)
            best = lax.fori_loop(
                0, n_lat, lambda j, b: jnp.maximum(b, sq_max(kl_ref[pl.ds(pl.multiple_of(j * tk, tk), tk), :], hh)), best)
            knorm_ref[hh] = jnp.broadcast_to(jnp.sqrt(best), (SUBLANES, LANES))

    def values(v, hh):
        lane = lax.broadcasted_iota(jnp.int32, v.shape, 1)
        own = (lane < MLA_V) if hh == 0 else (lane >= MLA_V)
        return jnp.where(own, v, jnp.ones_like(v))

    def scores(hh, k2):
        return lax.dot_general(qs[hh], k2[:, heads[hh]], (((1,), (1,)), ((), ())), preferred_element_type=F32)

    def finish(accs):
        lane = lax.broadcasted_iota(jnp.int32, (tq, 2 * MLA_V), 1)
        outs = [acc / pltpu.roll(acc, MLA_V, 1) for acc in accs]
        o_ref[...] = jnp.where(lane < MLA_V, outs[0], outs[1]).astype(o_ref.dtype)

    def lat_chunk(j):
        off = pl.multiple_of(j * tk, tk)
        return kl_ref[pl.ds(off, tk), :], vl_ref[pl.ds(off, tk), :]

    bounds = []
    for hh in range(2):
        qf = qs[hh].astype(F32)
        qn = jnp.sqrt(jnp.sum(qf * qf, axis=-1, keepdims=True))
        bounds.append(qn * knorm_ref[hh, 0:1, 0:1] * ATTN_BOUND_SLACK + ATTN_BOUND_MARGIN)
    widest = jnp.max(jnp.maximum(bounds[0], bounds[1]))

    def fixed_shift():
        def step(k2, v, accs):
            return tuple(accs[hh] + jnp.dot(jnp.exp2(scores(hh, k2) - bounds[hh]).astype(BF16), values(v, hh),
                                            preferred_element_type=F32) for hh in range(2))

        zero = jnp.zeros((tq, 2 * MLA_V), F32)
        accs = step(kc_ref[...], vc_ref[...], (zero, zero))
        finish(lax.fori_loop(0, n_lat, lambda j, a: step(*lat_chunk(j), a), accs, unroll=8))

    def running_max():
        def step(k2, v, carry):
            new = []
            for hh in range(2):
                m, acc = carry[hh]
                s = scores(hh, k2)
                m_new = jnp.maximum(m, jnp.max(s, axis=-1, keepdims=True))
                p = jnp.exp2(s - m_new).astype(BF16)
                acc = jnp.exp2(m - m_new) * acc + jnp.dot(p, values(v, hh), preferred_element_type=F32)
                new.append((m_new, acc))
            return tuple(new)

        init = tuple((jnp.full((tq, 1), NEG_INF, F32), jnp.zeros((tq, 2 * MLA_V), F32)) for _ in range(2))
        carry = step(kc_ref[...], vc_ref[...], init)
        carry = lax.fori_loop(0, n_lat, lambda j, c: step(*lat_chunk(j), c), carry)
        finish([acc for _, acc in carry])

    lax.cond(widest <= ATTN_SAFE_SHIFT, fixed_shift, running_max)


def attention(q, k, v):
    tq, tk = ATTN_TQ, ATTN_TK
    nq = SEQ // tq
    ctx0 = LAT_ROWS // CTX_LEN
    return pl.pallas_call(
        functools.partial(_attn_kernel, tq=tq, tk=tk),
        grid=(BATCH, MLA_HEADS // 2, nq),
        in_specs=[pl.BlockSpec((tq, 2 * HEAD_PAD), lambda b, h, i: (b * nq + i, h)),
                  pl.BlockSpec((CTX_LEN, 2 * HEAD_PAD), lambda b, h, i: (ctx0 + b, h)),
                  pl.BlockSpec((SEQ, 2 * HEAD_PAD), lambda b, h, i: (b, h)),
                  pl.BlockSpec((CTX_LEN, 2 * MLA_V), lambda b, h, i: (ctx0 + b, h)),
                  pl.BlockSpec((SEQ, 2 * MLA_V), lambda b, h, i: (b, h))],
        out_specs=pl.BlockSpec((tq, 2 * MLA_V), lambda b, h, i: (b * nq + i, h)),
        out_shape=jax.ShapeDtypeStruct((LAT_ROWS, MLA_HEADS * MLA_V), BF16),
        scratch_shapes=[pltpu.VMEM((2, SUBLANES, LANES), F32)],
        compiler_params=_params("arbitrary", "arbitrary", "arbitrary"),
        name="attention",
    )(q, k, k, v, v)


def _attn_out_kernel(x_ref, mod_ref, o_ref_in, wo_ref, out_ref):
    gate = mod_ref[0][:, 2 * D_MODEL:3 * D_MODEL]
    out_ref[...] = x_ref[...] + gate * jnp.dot(o_ref_in[...], wo_ref[...], preferred_element_type=F32)


def attn_out(x_all, mod, o, w_o):
    tm = ROW_TILE
    row = lambda n: pl.BlockSpec((tm, n), lambda i: (i, 0))
    return pl.pallas_call(
        _attn_out_kernel,
        grid=(LAT_ROWS // tm,),
        in_specs=[row(D_MODEL), pl.BlockSpec((1, 1, N_MOD * D_MODEL), lambda i: (_mod_row(i, tm), 0, 0)),
                  row(MLA_HEADS * MLA_V), _full(w_o.shape)],
        out_specs=row(D_MODEL),
        out_shape=jax.ShapeDtypeStruct((LAT_ROWS, D_MODEL), F32),
        compiler_params=_params("arbitrary"),
        name="attn_out",
    )(x_all, mod, o, w_o.astype(BF16))


def kernel(x, c, ctx, c_ctx, l0_w_ada, l0_b_ada, l0_norm1, l0_w_in, l0_conv_w, l0_conv_b, l0_dt_bias, l0_a_log, l0_d_skip, l0_ssd_norm, l0_w_out, l0_norm2, l0_peer_wq, l0_peer_keys, l0_peer_u, l0_peer_v, l1_w_ada, l1_b_ada, l1_norm1, l1_w_dqkv, l1_q_a_norm, l1_kv_a_norm, l1_w_uq, l1_w_ukv, l1_q_norm, l1_k_norm, l1_w_o, l1_norm2, l1_peer_wq, l1_peer_keys, l1_peer_u, l1_peer_v):
    x_all = jnp.concatenate([x.reshape(LAT_ROWS, D_MODEL), ctx.reshape(BATCH * CTX_LEN, D_MODEL)], axis=0)

    mod0 = ada_table(c, c_ctx, l0_w_ada, l0_b_ada)
    f, z, xbc, dt = even_in(x_all, mod0, l0_norm1, l0_w_in)
    u = conv_silu(xbc, l0_conv_w, l0_conv_b)
    yf, yb = ssd_scan(u, dt, l0_dt_bias, l0_a_log)
    four = fourier_mix(f)
    x_all = even_out(x_all, mod0, four, yf, yb, u, z, l0_d_skip, l0_ssd_norm, l0_w_out)
    x_all = peer_layer(x_all, mod0, l0_norm2, l0_peer_wq, l0_peer_keys, l0_peer_u, l0_peer_v)

    mod1 = ada_table(c, c_ctx, l1_w_ada, l1_b_ada)
    p1 = dict(norm1=l1_norm1, w_dqkv=l1_w_dqkv, q_a_norm=l1_q_a_norm, kv_a_norm=l1_kv_a_norm, w_uq=l1_w_uq,
              w_ukv=l1_w_ukv, q_norm=l1_q_norm, k_norm=l1_k_norm)
    q, k, v = mla_in(x_all, mod1, p1)
    o = attention(q, k, v)
    x_lat = attn_out(x_all, mod1, o, l1_w_o)
    x_lat = peer_layer(x_lat, mod1, l1_norm2, l1_peer_wq, l1_peer_keys, l1_peer_u, l1_peer_v)
    return x_lat.reshape(BATCH, SEQ, D_MODEL)
```

```python
import functools
import math

import numpy as np
import jax
import jax.numpy as jnp
from jax import lax
from jax.experimental import pallas as pl
from jax.experimental.pallas import tpu as pltpu

D_MODEL = 1024
BATCH = 2
SEQ = 8192
GRID_W = 64
CTX_LEN = 256
EPS = 1e-6
N_MOD = 6
LAT_ROWS = BATCH * SEQ
ALL_ROWS = LAT_ROWS + BATCH * CTX_LEN

FNET_GROUPS = 4
FNET_GROUP_DIM = 128
FNET_WIDTH = FNET_GROUPS * FNET_GROUP_DIM
FFT_N1 = 64
FFT_N2 = 128

SSD_HEADS = 16
SSD_HEAD_DIM = 64
SSD_INNER = SSD_HEADS * SSD_HEAD_DIM
SSD_GROUPS = 4
SSD_STATE = 128
SSD_CONV = 5
SSD_CHUNK = 128
SSD_CONV_DIM = SSD_INNER + 2 * SSD_GROUPS * SSD_STATE
SSD_GROUP_W = SSD_INNER // SSD_GROUPS

MLA_HEADS = 16
MLA_NOPE = 64
MLA_ROPE = 32
MLA_QK = MLA_NOPE + MLA_ROPE
MLA_V = 64
MLA_Q_LORA = 384
MLA_KV_LORA = 256
ROPE_THETA = 10000.0
HEAD_PAD = 128

PEER_HEADS = 8
PEER_KEYS = 128
PEER_EXPERTS = PEER_KEYS * PEER_KEYS
PEER_QDIM = 256
PEER_TOPK = 16

LANES = 128
SUBLANES = 8
VMEM_LIMIT = 56 * 1024 * 1024

ROW_TILE = 512
CONV_TILE = 256
PEER_TILE = 512
PEER_CHUNK = 512
PEER_CHUNKS_PER_STEP = 2
PEER_SUB = 256
PEER_SELECT_TOKENS = 128
ATTN_TQ = 1024
ATTN_TK = 512
ADA_COLS = 512
FFT1_COLS = 8192

F32 = jnp.float32
BF16 = jnp.bfloat16
HIGHEST = lax.Precision.HIGHEST
NEG_INF = float("-inf")


def _params(*sem):
    return pltpu.CompilerParams(dimension_semantics=sem, vmem_limit_bytes=VMEM_LIMIT)


def _mod_row(i, tile):
    return jnp.minimum((i * tile) // SEQ, BATCH)


def _full(shape):
    return pl.BlockSpec(shape, lambda *_: (0,) * len(shape))


def _silu(x):
    return x * (1.0 / (1.0 + jnp.exp(-x)))


def _modulated_norm(x, gain, mod, k_shift, k_scale):
    shift = mod[:, k_shift * D_MODEL:(k_shift + 1) * D_MODEL]
    scale = mod[:, k_scale * D_MODEL:(k_scale + 1) * D_MODEL]
    ms = jnp.mean(x * x, axis=-1, keepdims=True)
    return x * lax.rsqrt(ms + EPS) * gain * (1.0 + scale) + shift


def _hdot(a, b):
    return jnp.dot(a, b, precision=HIGHEST, preferred_element_type=F32)


def _bf16_terms(x, n):
    terms = []
    for _ in range(n):
        t = x.astype(BF16)
        terms.append(t)
        x = x - t.astype(F32)
    return terms


def _dot_exact_rhs(a, b01, terms=3):
    return sum(jnp.dot(t, b01, preferred_element_type=F32) for t in _bf16_terms(a, terms))


def _dot_exact_lhs(a01, b):
    return sum(jnp.dot(a01, t, preferred_element_type=F32) for t in _bf16_terms(b, 3))


def _dot3(a, b):
    a_hi, a_lo = _bf16_terms(a, 2)
    b_hi, b_lo = _bf16_terms(b, 2)
    dot = lambda x, y: jnp.dot(x, y, preferred_element_type=F32)
    return dot(a_hi, b_hi) + (dot(a_hi, b_lo) + dot(a_lo, b_hi))


def _ada_kernel(c_ref, w_ref, b_ref, o_ref):
    o_ref[...] = _hdot(_silu(c_ref[...]), w_ref[...]) + b_ref[...]


def ada_table(c, c_ctx, w_ada, b_ada):
    cond = jnp.concatenate([c, c_ctx[None, :], jnp.zeros((SUBLANES - BATCH - 1, D_MODEL), F32)], axis=0)
    tn = ADA_COLS
    out = pl.pallas_call(
        _ada_kernel,
        grid=(N_MOD * D_MODEL // tn,),
        in_specs=[_full((SUBLANES, D_MODEL)),
                  pl.BlockSpec((D_MODEL, tn), lambda j: (0, j)),
                  pl.BlockSpec((1, tn), lambda j: (0, j))],
        out_specs=pl.BlockSpec((SUBLANES, tn), lambda j: (0, j)),
        out_shape=jax.ShapeDtypeStruct((SUBLANES, N_MOD * D_MODEL), F32),
        compiler_params=_params("arbitrary"),
        name="ada_table",
    )(cond, w_ada, b_ada[None, :])
    return out[:BATCH + 1].reshape(BATCH + 1, 1, N_MOD * D_MODEL)


LAT_TILES = LAT_ROWS // ROW_TILE


def _lat_ctx_specs(width):
    assert BATCH * CTX_LEN == ROW_TILE, "the context rows are exactly one row tile"
    return [pl.BlockSpec((ROW_TILE, width), lambda i: (jnp.minimum(i, LAT_TILES - 1), 0)),
            pl.BlockSpec((ROW_TILE, width), lambda i: (0, 0))]


def _lat_or_ctx(lat_ref, ctx_ref):
    return jnp.where(pl.program_id(0) < LAT_TILES, lat_ref[...], ctx_ref[...])


def _even_in_kernel(xl_ref, xc_ref, mod_ref, g_ref, wf_ref, wz_ref, wx_ref, wd_ref, f_ref, z_ref, xbc_ref, dt_ref):
    h = _modulated_norm(_lat_or_ctx(xl_ref, xc_ref), g_ref[...], mod_ref[0], 0, 1).astype(BF16)
    f_ref[...] = jnp.dot(h, wf_ref[...], preferred_element_type=F32)
    z_ref[...] = jnp.dot(h, wz_ref[...], preferred_element_type=F32)
    xbc_ref[...] = jnp.dot(h, wx_ref[...], preferred_element_type=F32)
    dt_ref[...] = jnp.dot(h, wd_ref[...], preferred_element_type=F32)


def even_in(x_lat, x_ctx, mod, norm1, w_in):
    rows = x_lat.shape[0] + x_ctx.shape[0]
    o1, o2, o3 = FNET_WIDTH, FNET_WIDTH + SSD_INNER, FNET_WIDTH + SSD_INNER + SSD_CONV_DIM
    wf = w_in[:, :o1].astype(BF16)
    wz = w_in[:, o1:o2].astype(BF16)
    wx = w_in[:, o2:o3].astype(BF16)
    wd = jnp.pad(w_in[:, o3:], ((0, 0), (0, LANES - 2 * SSD_HEADS))).astype(BF16)
    tm = ROW_TILE
    row = lambda n: pl.BlockSpec((tm, n), lambda i: (i, 0))
    return pl.pallas_call(
        _even_in_kernel,
        grid=(rows // tm,),
        in_specs=_lat_ctx_specs(D_MODEL) + [
                  pl.BlockSpec((1, 1, N_MOD * D_MODEL), lambda i: (_mod_row(i, tm), 0, 0)),
                  _full((1, D_MODEL)),
                  _full(wf.shape), _full(wz.shape), _full(wx.shape), _full(wd.shape)],
        out_specs=[row(FNET_WIDTH), row(SSD_INNER), row(SSD_CONV_DIM), row(LANES)],
        out_shape=[jax.ShapeDtypeStruct((rows, FNET_WIDTH), F32),
                   jax.ShapeDtypeStruct((rows, SSD_INNER), F32),
                   jax.ShapeDtypeStruct((rows, SSD_CONV_DIM), F32),
                   jax.ShapeDtypeStruct((rows, LANES), F32)],
        compiler_params=_params("arbitrary"),
        name="even_in",
    )(x_lat, x_ctx, mod, norm1[None, :], wf, wz, wx, wd)


def _conv_kernel(x_ref, prev_ref, next_ref, w_ref, b_ref, o_ref, ext_ref, *, tm):
    row0 = pl.program_id(0) * tm
    in_lat = row0 < LAT_ROWS
    first = jnp.where(in_lat, row0 % SEQ == 0, (row0 - LAT_ROWS) % CTX_LEN == 0)
    last = jnp.where(in_lat, (row0 + tm) % SEQ == 0, (row0 + tm - LAT_ROWS) % CTX_LEN == 0)
    ext_ref[0:SUBLANES, :] = prev_ref[...] * jnp.where(first, 0.0, 1.0)
    ext_ref[SUBLANES:SUBLANES + tm, :] = x_ref[...]
    ext_ref[SUBLANES + tm:2 * SUBLANES + tm, :] = next_ref[...] * jnp.where(last, 0.0, 1.0)
    pad = SSD_CONV // 2
    acc = b_ref[...] + w_ref[0:1, :] * ext_ref[pl.ds(SUBLANES - pad, tm), :]
    for k in range(1, SSD_CONV):
        acc = acc + w_ref[k:k + 1, :] * ext_ref[pl.ds(SUBLANES - pad + k, tm), :]
    o_ref[...] = _silu(acc)


def conv_silu(xbc, conv_w, conv_b):
    rows, ch = xbc.shape
    tm, tc = CONV_TILE, ch
    assert CTX_LEN % tm == 0 and SEQ % tm == 0
    per = tm // SUBLANES
    n_small = rows // SUBLANES
    w = jnp.pad(conv_w, ((0, SUBLANES - SSD_CONV), (0, 0)))
    return pl.pallas_call(
        functools.partial(_conv_kernel, tm=tm),
        grid=(rows // tm, ch // tc),
        in_specs=[pl.BlockSpec((tm, tc), lambda i, j: (i, j)),
                  pl.BlockSpec((SUBLANES, tc), lambda i, j: (jnp.maximum(i * per - 1, 0), j)),
                  pl.BlockSpec((SUBLANES, tc), lambda i, j: (jnp.minimum((i + 1) * per, n_small - 1), j)),
                  pl.BlockSpec((SUBLANES, tc), lambda i, j: (0, j)),
                  pl.BlockSpec((1, tc), lambda i, j: (0, j))],
        out_specs=pl.BlockSpec((tm, tc), lambda i, j: (i, j)),
        out_shape=jax.ShapeDtypeStruct((rows, ch), F32),
        scratch_shapes=[pltpu.VMEM((tm + 2 * SUBLANES, tc), F32)],
        compiler_params=_params("arbitrary", "arbitrary"),
        name="conv_silu",
    )(xbc, xbc, xbc, w, conv_b[None, :])


def _softplus(x):
    return jnp.maximum(x, 0.0) + jnp.log1p(jnp.exp(-jnp.abs(x)))


def _ssd_direction(u_ref, dt_ref, dtb_ref, alog_ref, exp_ref, state_ref, y_ref, direction):
    cl = SSD_CHUNK
    xs = u_ref[:, :SSD_INNER]
    dtv = _softplus(dt_ref[...] + dtb_ref[...])
    dta = dtv * (-jnp.exp(alog_ref[...]))
    r = lax.broadcasted_iota(jnp.int32, (cl, cl), 0)
    c = lax.broadcasted_iota(jnp.int32, (cl, cl), 1)
    tri = (r >= c) if direction == 0 else (r <= c)
    cs = _dot_exact_lhs(tri.astype(F32).astype(BF16), dta)
    cs_t = cs.T
    edge = cs[cl - 1:cl, :] if direction == 0 else cs[0:1, :]
    to_end = jnp.exp(edge - cs)
    from_start = jnp.exp(cs)
    spread = _dot_exact_rhs(jnp.concatenate([dtv, to_end, from_start], axis=0), exp_ref[direction], terms=2)
    dt_x, te_x, fs_x = spread[:cl], spread[cl:2 * cl], spread[2 * cl:]
    chunk_decay = fs_x[cl - 1:cl, :] if direction == 0 else fs_x[0:1, :]
    xd = xs * dt_x
    xd_b = xd.astype(BF16)
    xte_b = (xd * te_x).astype(BF16)
    lane = lax.broadcasted_iota(jnp.int32, (cl, LANES), 1)
    low = lane < SSD_HEAD_DIM
    for g in range(SSD_GROUPS):
        bg = u_ref[:, SSD_INNER + g * SSD_STATE:SSD_INNER + (g + 1) * SSD_STATE]
        cg = u_ref[:, SSD_INNER + (SSD_GROUPS + g) * SSD_STATE:SSD_INNER + (SSD_GROUPS + g + 1) * SSD_STATE]
        bg_b, cg_b = bg.astype(BF16), cg.astype(BF16)
        cb = lax.dot_general(cg_b, bg_b, (((1,), (1,)), ((), ())), preferred_element_type=F32)
        gs = slice(g * SSD_GROUP_W, (g + 1) * SSD_GROUP_W)
        state = state_ref[g]
        y_off = jnp.dot(cg_b, state.astype(BF16), preferred_element_type=F32) * fs_x[:, gs]
        new_state = jnp.dot(bg.T.astype(BF16), xte_b[:, gs], preferred_element_type=F32)
        state_ref[g] = state * chunk_decay[:, gs] + new_state
        for pair in range(2):
            blk = xd_b[:, g * SSD_GROUP_W + pair * LANES:g * SSD_GROUP_W + (pair + 1) * LANES]
            ms, halves = [], []
            for sub in range(2):
                col = direction * SSD_HEADS + g * 4 + pair * 2 + sub
                seg = cs[:, col:col + 1] - cs_t[col:col + 1, :]
                decay = jnp.exp(jnp.where(tri, seg, NEG_INF))
                ms.append((cb * decay).astype(BF16))
                halves.append(jnp.where(low if sub == 0 else jnp.logical_not(low), blk, jnp.zeros_like(blk)))
            y_pair = jnp.dot(jnp.concatenate(ms, axis=1), jnp.concatenate(halves, axis=0),
                             preferred_element_type=F32)
            lo = g * SSD_GROUP_W + pair * LANES
            y_ref[:, lo:lo + LANES] = y_pair + y_off[:, pair * LANES:(pair + 1) * LANES]


def _ssd_kernel(uf_ref, ub_ref, dtf_ref, dtb_in_ref, bias_ref, alog_ref, exp_ref, yf_ref, yb_ref, sf_ref, sb_ref):
    @pl.when(pl.program_id(1) == 0)
    def _():
        sf_ref[...] = jnp.zeros_like(sf_ref)
        sb_ref[...] = jnp.zeros_like(sb_ref)

    _ssd_direction(uf_ref, dtf_ref, bias_ref, alog_ref, exp_ref, sf_ref, yf_ref, 0)
    _ssd_direction(ub_ref, dtb_in_ref, bias_ref, alog_ref, exp_ref, sb_ref, yb_ref, 1)


def ssd_scan(u, dt, dt_bias, a_log):
    rows = u.shape[0]
    cl = SSD_CHUNK
    lat_chunks, ctx_chunks = SEQ // cl, CTX_LEN // cl
    steps = ctx_chunks + lat_chunks
    ctx0 = LAT_ROWS // cl

    def fwd_chunk(b, t):
        return jnp.where(t < ctx_chunks, ctx0 + b * ctx_chunks + t, b * lat_chunks + t - ctx_chunks)

    def bwd_chunk(b, t):
        return jnp.where(t < ctx_chunks, ctx0 + b * ctx_chunks + (ctx_chunks - 1 - t),
                         b * lat_chunks + (lat_chunks - 1) - (t - ctx_chunks))

    pad = LANES - 2 * SSD_HEADS
    bias = jnp.pad(dt_bias.reshape(1, -1), ((0, 0), (0, pad)))
    alog = jnp.pad(a_log.reshape(1, -1), ((0, 0), (0, pad)))
    expand = np.zeros((2, LANES, SSD_INNER), np.float32)
    for d in range(2):
        for h in range(SSD_HEADS):
            expand[d, d * SSD_HEADS + h, h * SSD_HEAD_DIM:(h + 1) * SSD_HEAD_DIM] = 1.0
    spec = lambda n, fn: pl.BlockSpec((cl, n), lambda b, t: (fn(b, t), 0))
    return pl.pallas_call(
        _ssd_kernel,
        grid=(BATCH, steps),
        in_specs=[spec(SSD_CONV_DIM, fwd_chunk), spec(SSD_CONV_DIM, bwd_chunk),
                  spec(LANES, fwd_chunk), spec(LANES, bwd_chunk),
                  _full((1, LANES)), _full((1, LANES)), _full((2, LANES, SSD_INNER))],
        out_specs=[spec(SSD_INNER, fwd_chunk), spec(SSD_INNER, bwd_chunk)],
        out_shape=[jax.ShapeDtypeStruct((rows, SSD_INNER), F32)] * 2,
        scratch_shapes=[pltpu.VMEM((SSD_GROUPS, SSD_STATE, SSD_GROUP_W), F32)] * 2,
        compiler_params=_params("arbitrary", "arbitrary"),
        name="ssd_scan",
    )(u, u, dt, dt, bias, alog, jnp.asarray(expand, BF16))


def _dft_cos_sin(n):
    k = np.arange(n)
    ang = 2.0 * np.pi * ((k[:, None] * k[None, :]) % n) / n
    return np.cos(ang), np.sin(ang)


def _fft1_kernel(x_ref, m_ref, o_ref):
    o_ref[0] = _dot3(m_ref[...], x_ref[...])


def _fft2_kernel(ar_ref, ai_ref, twr_ref, twi_ref, m2_ref, mc_ref, o_ref):
    n2 = FFT_N2
    for j in range(SUBLANES):
        ar, ai = ar_ref[0, j], ai_ref[0, j]
        twr = jnp.concatenate([twr_ref[j]] * FNET_GROUPS, axis=1)
        twi = jnp.concatenate([twi_ref[j]] * FNET_GROUPS, axis=1)
        p = ar * twr - ai * twi
        q = ar * twi + ai * twr
        uv = _dot3(m2_ref[...], jnp.concatenate([p, q], axis=0))
        groups = []
        for g in range(FNET_GROUPS):
            gs = slice(g * FNET_GROUP_DIM, (g + 1) * FNET_GROUP_DIM)
            groups.append(_dot3(jnp.concatenate([uv[:n2, gs], uv[n2:, gs]], axis=1), mc_ref[...]))
        o_ref[0, :, j, :] = jnp.concatenate(groups, axis=1)


def _fft_ctx_kernel(x_ref, mc_ref, mp_ref, o_ref):
    x = x_ref[...]
    for g in range(FNET_GROUPS):
        gs = slice(g * FNET_GROUP_DIM, (g + 1) * FNET_GROUP_DIM)
        cs = _dot3(x[:, gs], mc_ref[...])
        stacked = jnp.concatenate([cs[:, :FNET_GROUP_DIM], cs[:, FNET_GROUP_DIM:]], axis=0)
        o_ref[:, gs] = _dot3(mp_ref[...], stacked)


def fourier_mix(f_all):
    rows = f_all.shape[0]
    n1, n2, gd = FFT_N1, FFT_N2, FNET_GROUP_DIM
    row_w = n2 * FNET_WIDTH
    c1, s1 = _dft_cos_sin(n1)
    m1 = jnp.asarray(np.concatenate([c1, -s1], axis=0), F32)
    tn = FFT1_COLS
    stage1 = pl.pallas_call(
        _fft1_kernel,
        grid=(BATCH, row_w // tn),
        in_specs=[pl.BlockSpec((n1, tn), lambda b, j: (b, j)), _full((2 * n1, n1))],
        out_specs=pl.BlockSpec((1, 2 * n1, tn), lambda b, j: (b, 0, j)),
        out_shape=jax.ShapeDtypeStruct((BATCH, 2 * n1, row_w), F32),
        compiler_params=_params("arbitrary", "arbitrary"),
        name="fft_stage1",
    )(f_all.reshape(rows // n2, row_w), m1)

    k1 = np.arange(n1)[:, None]
    l2 = np.arange(n2)[None, :]
    tw = 2.0 * np.pi * (k1 * l2) / SEQ
    twr = jnp.asarray(np.repeat(np.cos(tw)[:, :, None], gd, axis=2), F32)
    twi = jnp.asarray(np.repeat(-np.sin(tw)[:, :, None], gd, axis=2), F32)
    c2, s2 = _dft_cos_sin(n2)
    m2 = jnp.asarray(np.block([[c2, s2], [-s2, c2]]), F32)
    cc, sc = _dft_cos_sin(gd)
    mc = jnp.asarray(np.concatenate([cc, sc], axis=0) / math.sqrt(SEQ * gd), F32)
    a4 = stage1.reshape(BATCH, 2 * n1, n2, FNET_WIDTH)
    sub = SUBLANES
    lat = pl.pallas_call(
        _fft2_kernel,
        grid=(BATCH, n1 // sub),
        in_specs=[pl.BlockSpec((1, sub, n2, FNET_WIDTH), lambda b, k: (b, k, 0, 0)),
                  pl.BlockSpec((1, sub, n2, FNET_WIDTH), lambda b, k: (b, n1 // sub + k, 0, 0)),
                  pl.BlockSpec((sub, n2, gd), lambda b, k: (k, 0, 0)),
                  pl.BlockSpec((sub, n2, gd), lambda b, k: (k, 0, 0)),
                  _full((2 * n2, 2 * n2)), _full((2 * gd, gd))],
        out_specs=pl.BlockSpec((1, n2, sub, FNET_WIDTH), lambda b, k: (b, 0, k, 0)),
        out_shape=jax.ShapeDtypeStruct((BATCH, n2, n1, FNET_WIDTH), F32),
        compiler_params=_params("arbitrary", "arbitrary"),
        name="fft_stage2",
    )(a4, a4, twr, twi, m2, mc)

    cp, sp = _dft_cos_sin(CTX_LEN)
    mp = jnp.asarray(np.concatenate([cp, -sp], axis=1) / math.sqrt(CTX_LEN * gd), F32)
    mcc = jnp.asarray(np.concatenate([cc, sc], axis=1), F32)
    ctx0 = LAT_ROWS // CTX_LEN
    ctx = pl.pallas_call(
        _fft_ctx_kernel,
        grid=(BATCH,),
        in_specs=[pl.BlockSpec((CTX_LEN, FNET_WIDTH), lambda b: (ctx0 + b, 0)),
                  _full((gd, 2 * gd)), _full((CTX_LEN, 2 * CTX_LEN))],
        out_specs=pl.BlockSpec((CTX_LEN, FNET_WIDTH), lambda b: (b, 0)),
        out_shape=jax.ShapeDtypeStruct((BATCH * CTX_LEN, FNET_WIDTH), F32),
        compiler_params=_params("arbitrary"),
        name="fft_ctx",
    )(f_all, mcc, mp)
    return lat.reshape(LAT_ROWS, FNET_WIDTH), ctx


def _even_out_kernel(xl_ref, xc_ref, fl_ref, fc_ref, mod_ref, yf_ref, yb_ref, xs_ref, z_ref, dsk_ref, gn_ref,
                     wof_ref, wos_ref, o_ref):
    y = yf_ref[...] + yb_ref[...] + dsk_ref[...] * xs_ref[...]
    gated = y * _silu(z_ref[...])
    out = jnp.dot(_lat_or_ctx(fl_ref, fc_ref).astype(BF16), wof_ref[...], preferred_element_type=F32)
    for g in range(SSD_GROUPS):
        gs = slice(g * SSD_GROUP_W, (g + 1) * SSD_GROUP_W)
        v = gated[:, gs]
        normed = v * lax.rsqrt(jnp.mean(v * v, axis=-1, keepdims=True) + EPS) * gn_ref[:, gs]
        out = out + jnp.dot(normed.astype(BF16), wos_ref[gs, :], preferred_element_type=F32)
    gate = mod_ref[0][:, 2 * D_MODEL:3 * D_MODEL]
    o_ref[...] = _lat_or_ctx(xl_ref, xc_ref) + gate * out


def even_out(x_lat, x_ctx, four_lat, four_ctx, mod, yf, yb, u, z, d_skip, ssd_norm, w_out):
    rows = x_lat.shape[0] + x_ctx.shape[0]
    tm = ROW_TILE
    dsk = jnp.repeat(d_skip[0] + d_skip[1], SSD_HEAD_DIM)[None, :]
    wof = w_out[:FNET_WIDTH].astype(BF16)
    wos = w_out[FNET_WIDTH:].astype(BF16)
    row = lambda n: pl.BlockSpec((tm, n), lambda i: (i, 0))
    return pl.pallas_call(
        _even_out_kernel,
        grid=(rows // tm,),
        in_specs=_lat_ctx_specs(D_MODEL) + _lat_ctx_specs(FNET_WIDTH) + [
                  pl.BlockSpec((1, 1, N_MOD * D_MODEL), lambda i: (_mod_row(i, tm), 0, 0)),
                  row(SSD_INNER), row(SSD_INNER), row(SSD_INNER), row(SSD_INNER),
                  _full((1, SSD_INNER)), _full((1, SSD_INNER)), _full(wof.shape), _full(wos.shape)],
        out_specs=row(D_MODEL),
        out_shape=jax.ShapeDtypeStruct((rows, D_MODEL), F32),
        compiler_params=_params("arbitrary"),
        name="even_out",
    )(x_lat, x_ctx, four_lat, four_ctx, mod, yf, yb, u, z, dsk, ssd_norm[None, :], wof, wos)


BF16_ROWS = 2 * SUBLANES
NOT_TOP = 64.0
AFTER_EVERY_ORDER = float(PEER_TOPK * PEER_KEYS)


def _erf_gelu(z):
    return 0.5 * z * (1.0 + lax.erf(z * (1.0 / math.sqrt(2.0))))


def _merge_exchange_pairs(n):
    t = max(1, math.ceil(math.log2(n)))
    p, pairs = 2 ** (t - 1), []
    while p > 0:
        q, r, d = 2 ** (t - 1), 0, p
        while d > 0:
            pairs += [(i, i + d) for i in range(n - d) if i & p == r]
            d, q, r = q - p, q // 2, p
        p //= 2
    return pairs


def _top_values_sorted(score, k):
    groups = [score[r * SUBLANES:(r + 1) * SUBLANES] for r in range(score.shape[0] // SUBLANES)]
    for a, b in _merge_exchange_pairs(len(groups)):
        groups[a], groups[b] = jnp.maximum(groups[a], groups[b]), jnp.minimum(groups[a], groups[b])
    vals = []
    for i in range(k):
        m = jnp.max(groups[0], axis=0, keepdims=True)
        vals.append(m)
        needed = k - 1 - i
        if needed == 0:
            break
        hit = groups[0] == m
        shifted = [jnp.where(hit, groups[d + 1], groups[d]) for d in range(len(groups) - 1)]
        if len(groups) <= needed:
            shifted.append(jnp.where(hit, NEG_INF, groups[-1]))
        groups = shifted[:needed]
    return jnp.concatenate(vals, axis=0)


def _rank_among(values, top, k):
    assert k & (k - 1) == 0
    bits, count, stride = [], jnp.zeros_like(values), k // 2
    while stride >= 1:
        cands = [top[p + stride - 1:p + stride] for p in range(0, k, 2 * stride)]
        for b in reversed(bits):
            cands = [jnp.where(b, cands[2 * i + 1], cands[2 * i]) for i in range(len(cands) // 2)]
        bit = cands[0] > values
        count = count + jnp.where(bit, float(stride), 0.0)
        bits.append(bit)
        stride //= 2
    return jnp.where(values >= top[k - 1:k], count, NOT_TOP)


def _pair_candidates(v1, v2, k):
    assert k == 2 * SUBLANES, "the row grouping below is laid out for k = 16"
    t = v1.shape[1]
    row8 = lax.broadcasted_iota(jnp.int32, (SUBLANES, t), 0)
    rowf = row8.astype(F32)
    sums = [v1[0:1] + v2[0:8], v1[0:1] + v2[8:16], v1[1:2] + v2[0:8], v1[8:16] + v2[0:1]]
    order = [rowf, rowf + 8.0, rowf + float(k), (rowf + 8.0) * float(k)]
    for i in range(2, 8):
        sums.append(jnp.where(row8 < k // (i + 1), v1[i:i + 1] + v2[0:8], NEG_INF))
        order.append(rowf + float(i * k))
    return jnp.concatenate(sums, axis=0), jnp.concatenate(order, axis=0)


def _select_by_value(s1, s2, k):
    v1 = _top_values_sorted(s1, k)
    v2 = _top_values_sorted(s2, k)
    r2 = _rank_among(s2, v2, k)
    cand, _ = _pair_candidates(v1, v2, k)
    tau = _top_values_sorted(cand, k)[k - 1:k, :]
    chosen = cand >= tau
    zsum = jnp.sum(jnp.where(chosen, jnp.exp(cand - (v1[0:1] + v2[0:1])), 0.0), axis=0, keepdims=True)
    in_top = s1 >= v1[k - 1:k]
    a1 = jnp.where(in_top, s1, NEG_INF)
    lim = jnp.zeros_like(s1)
    for j in range(k // 2):
        lim = lim + jnp.where(a1 + v2[j:j + 1] >= tau, 1.0, 0.0)
    best = jnp.zeros_like(tau)
    for j in range(k // 2, k):
        best = best + jnp.where(v1[0:1] + v2[j:j + 1] >= tau, 1.0, 0.0)
    lim = lim + jnp.where(s1 == v1[0:1], best, 0.0)
    count = lambda mask: jnp.sum(jnp.where(mask, 1.0, 0.0), axis=0, keepdims=True)
    most = jnp.maximum(jnp.maximum(count(in_top), count(r2 < k)), count(chosen))
    return lim, jnp.exp(s1 - v1[0:1]) / zsum, r2, jnp.exp(s2 - v2[0:1]), most


def _top_ranks_ordered(score, order, k):
    work, vals = score, []
    rank = jnp.full(score.shape, NOT_TOP, F32)
    for i in range(k):
        m = jnp.max(work, axis=0, keepdims=True)
        first = jnp.min(jnp.where(work == m, order, AFTER_EVERY_ORDER), axis=0, keepdims=True)
        taken = order == first
        vals.append(m)
        rank = jnp.where(taken, float(i), rank)
        work = jnp.where(taken, NEG_INF, work)
    return jnp.concatenate(vals, axis=0), rank


def _select_by_order(s1, s2, k):
    key_index = lax.broadcasted_iota(jnp.int32, s1.shape, 0).astype(F32)
    v1, r1 = _top_ranks_ordered(s1, key_index, k)
    v2, r2 = _top_ranks_ordered(s2, key_index, k)
    cand, position = _pair_candidates(v1, v2, k)
    chosen = _top_ranks_ordered(cand, position, k)[1] < k
    zsum = jnp.sum(jnp.where(chosen, jnp.exp(cand - (v1[0:1] + v2[0:1])), 0.0), axis=0, keepdims=True)
    picks = jnp.where(chosen, 1.0, 0.0)
    group = lambda g: jnp.sum(picks[g * SUBLANES:(g + 1) * SUBLANES], axis=0, keepdims=True)
    per_rank = [group(0) + group(1), group(2)] + [group(g) for g in range(4, 10)]
    per_rank = jnp.concatenate(per_rank + [picks[3 * SUBLANES:4 * SUBLANES]], axis=0)
    lim = jnp.zeros_like(s1)
    for i in range(k):
        lim = lim + jnp.where(r1 == float(i), per_rank[i:i + 1], 0.0)
    return lim, jnp.exp(s1 - v1[0:1]) / zsum, r2, jnp.exp(s2 - v2[0:1])


def _peer_kernel(x_ref, mod_ref, g_ref, wq_ref, keys_ref, u0_ref, *rest, tm, chunk, per_step):
    u_refs = rest[:per_step]
    (vt_ref, o_ref, ht_ref, q_ref, lim_ref, w1_ref, rank2_ref, w2_ref, acc_ref, za_ref, zb_ref) = rest[per_step:]
    c = pl.program_id(1)
    k = PEER_TOPK
    packed = (PEER_KEYS // BF16_ROWS, BF16_ROWS, tm)
    tsel = PEER_SELECT_TOKENS

    @pl.when(c == 0)
    def _prologue():
        h2 = _modulated_norm(x_ref[...], g_ref[...], mod_ref[0], 3, 4)
        ht = h2.T.astype(BF16)
        ht_ref[...] = ht
        q = jnp.dot(wq_ref[...], ht, preferred_element_type=F32)
        q_ref[...] = q.reshape(2 * PEER_HEADS, PEER_QDIM // 2, tm)
        acc_ref[...] = jnp.zeros_like(acc_ref)
        za_ref[...] = jnp.dot(u0_ref[...], ht, preferred_element_type=F32)
        def head_body(h, carry):
            def scores(part):
                ts = slice(part * tsel, (part + 1) * tsel)
                s1 = jnp.dot(keys_ref[2 * h], q_ref[2 * h, :, ts].astype(BF16), preferred_element_type=F32)
                s2 = jnp.dot(keys_ref[2 * h + 1], q_ref[2 * h + 1, :, ts].astype(BF16), preferred_element_type=F32)
                return ts, s1, s2

            def store(ts, lim, w1, r2, w2):
                small = (PEER_KEYS // BF16_ROWS, BF16_ROWS, tsel)
                lim_ref[h, :, ts] = lim
                w1_ref[h, :, ts] = w1
                rank2_ref[h, :, :, ts] = r2.reshape(small).astype(BF16)
                w2_ref[h, :, :, ts] = w2.reshape(small).astype(BF16)

            counts = []
            for part in range(tm // tsel):
                ts, s1, s2 = scores(part)
                lim, w1, r2, w2, count = _select_by_value(s1, s2, k)
                store(ts, lim, w1, r2, w2)
                counts.append(count)

            @pl.when(jnp.max(functools.reduce(jnp.maximum, counts)) > k)
            def _ties():
                for part, count in enumerate(counts):
                    @pl.when(jnp.max(count) > k)
                    def _redo():
                        ts, s1, s2 = scores(part)
                        store(ts, *_select_by_order(s1, s2, k))

            return carry

        lax.fori_loop(0, PEER_HEADS, head_body, 0)

    per_key = PEER_KEYS // BF16_ROWS
    keys_per_sub = PEER_SUB // PEER_KEYS
    step_keys = per_step * chunk // PEER_KEYS
    key0 = pl.multiple_of(c * step_keys, step_keys)
    step_lim = [lim_ref[h, pl.ds(key0, step_keys), :] for h in range(PEER_HEADS)]
    step_w1 = [w1_ref[h, pl.ds(key0, step_keys), :] for h in range(PEER_HEADS)]

    def gated_values(z_ref, which):
        total = None
        for j in range(chunk // PEER_SUB):
            rows = slice(j * PEER_SUB, (j + 1) * PEER_SUB)
            act = _erf_gelu(z_ref[rows, :]).reshape(PEER_SUB // BF16_ROWS, BF16_ROWS, tm).astype(BF16)
            parts = []
            for jj in range(keys_per_sub):
                e1 = which * (chunk // PEER_KEYS) + j * keys_per_sub + jj
                gate = jnp.zeros(packed, BF16)
                for h in range(PEER_HEADS):
                    lim = jnp.broadcast_to(step_lim[h][e1:e1 + 1, :], (BF16_ROWS, tm)).astype(BF16)
                    w1 = jnp.broadcast_to(step_w1[h][e1:e1 + 1, :], (BF16_ROWS, tm)).astype(BF16)
                    picked = jnp.where(rank2_ref[h] < lim[None], w2_ref[h], jnp.zeros(packed, BF16))
                    gate = gate + picked * w1[None]
                parts.append(act[jj * per_key:(jj + 1) * per_key] * gate)
            a = jnp.concatenate(parts, axis=0).reshape(PEER_SUB, tm)
            cols = slice(which * chunk + j * PEER_SUB, which * chunk + (j + 1) * PEER_SUB)
            d = jnp.dot(vt_ref[:, cols], a, preferred_element_type=F32)
            total = d if total is None else total + d
        return total

    bufs = (za_ref, zb_ref)
    total = None
    for i in range(per_step):
        bufs[(i + 1) % 2][...] = jnp.dot(u_refs[i][...], ht_ref[...], preferred_element_type=F32)
        d = gated_values(bufs[i % 2], i)
        total = d if total is None else total + d
    acc_ref[...] += total

    @pl.when(c == pl.num_programs(1) - 1)
    def _epilogue():
        gate2 = mod_ref[0][:, 5 * D_MODEL:6 * D_MODEL]
        o_ref[...] = x_ref[...] + gate2 * acc_ref[...].T


def peer_layer(x_rows, mod, norm2, peer_wq, peer_keys, peer_u, peer_v):
    rows = x_rows.shape[0]
    tm, chunk = PEER_TILE, PEER_CHUNK
    wq_t = peer_wq.T.astype(BF16)
    keys = peer_keys.reshape(2 * PEER_HEADS, PEER_KEYS, PEER_QDIM // 2).astype(BF16)
    u = peer_u.astype(BF16)
    vt = peer_v.astype(BF16).T
    head_f32 = pltpu.VMEM((PEER_HEADS, PEER_KEYS, tm), F32)
    head_bf16 = pltpu.VMEM((PEER_HEADS, PEER_KEYS // BF16_ROWS, BF16_ROWS, tm), BF16)
    n_chunks = PEER_EXPERTS // chunk
    per_step = PEER_CHUNKS_PER_STEP
    ahead = lambda k: pl.BlockSpec((chunk, D_MODEL), lambda i, c: (jnp.minimum(per_step * c + k, n_chunks - 1), 0))
    return pl.pallas_call(
        functools.partial(_peer_kernel, tm=tm, chunk=chunk, per_step=per_step),
        grid=(rows // tm, n_chunks // per_step),
        in_specs=[pl.BlockSpec((tm, D_MODEL), lambda i, c: (i, 0)),
                  pl.BlockSpec((1, 1, N_MOD * D_MODEL), lambda i, c: (_mod_row(i, tm), 0, 0)),
                  _full((1, D_MODEL)), _full(wq_t.shape), _full(keys.shape),
                  pl.BlockSpec((chunk, D_MODEL), lambda i, c: (0, 0))]
                 + [ahead(k) for k in range(1, per_step + 1)]
                 + [pl.BlockSpec((D_MODEL, per_step * chunk), lambda i, c: (0, c))],
        out_specs=pl.BlockSpec((tm, D_MODEL), lambda i, c: (i, 0)),
        out_shape=jax.ShapeDtypeStruct((rows, D_MODEL), F32),
        scratch_shapes=[pltpu.VMEM((D_MODEL, tm), BF16),
                        pltpu.VMEM((2 * PEER_HEADS, PEER_QDIM // 2, tm), F32),
                        head_f32, head_f32, head_bf16, head_bf16,
                        pltpu.VMEM((D_MODEL, tm), F32),
                        pltpu.VMEM((chunk, tm), F32), pltpu.VMEM((chunk, tm), F32)],
        compiler_params=_params("arbitrary", "arbitrary"),
        name="peer",
    )(x_rows, mod, norm2[None, :], wq_t, keys, u, *([u] * per_step), vt)


def _rope_tables(rows):
    l = jnp.arange(SEQ)
    per_axis = MLA_ROPE // 2
    inv = ROPE_THETA ** (-jnp.arange(0, per_axis, 2, dtype=F32) / per_axis)
    ang = jnp.concatenate([(l // GRID_W)[:, None] * inv, (l % GRID_W)[:, None] * inv], axis=-1)
    cos, sin = jnp.cos(ang), jnp.sin(ang)
    one = jnp.ones((SEQ, MLA_NOPE), F32)
    tail = jnp.zeros((SEQ, HEAD_PAD - MLA_QK), F32)
    zero64 = jnp.zeros((SEQ, MLA_NOPE), F32)
    cos_t = jnp.concatenate([one, cos, cos, tail], axis=1)
    sin_t = jnp.concatenate([zero64, -sin, sin, tail], axis=1)
    n_ctx = rows - LAT_ROWS
    ident = jnp.concatenate([jnp.ones((n_ctx, MLA_QK), F32), jnp.zeros((n_ctx, HEAD_PAD - MLA_QK), F32)], axis=1)
    zeros = jnp.zeros((n_ctx, HEAD_PAD), F32)
    tile = lambda t, c: jnp.concatenate([t] * BATCH + [c], axis=0)
    return tile(cos_t, ident), tile(sin_t, zeros)


def _rope_swap_matrix():
    per_axis = MLA_ROPE // 2
    swap = np.zeros((HEAD_PAD, HEAD_PAD), np.float32)
    for j in range(per_axis):
        swap[MLA_NOPE + per_axis + j, MLA_NOPE + j] = 1.0
        swap[MLA_NOPE + j, MLA_NOPE + per_axis + j] = 1.0
    return swap


def _head_norm_rope(t, gain, cos, sin, ones, swap):
    ms = jnp.dot((t * t).astype(BF16), ones, preferred_element_type=F32) * (1.0 / MLA_QK)
    t = t * lax.rsqrt(ms + EPS) * gain
    return t * cos + jnp.dot(t.astype(BF16), swap, preferred_element_type=F32) * sin


def _mla_in_kernel(x_ref, mod_ref, g_ref, wdq_ref, wdkv_ref, wpe_ref, qan_ref, kvan_ref, wuq_ref, wuk_ref, wuv_ref,
                   qn_ref, kn_ref, ones_ref, swap_ref, cos_ref, sin_ref, q_ref, k_ref, v_ref):
    h = _modulated_norm(x_ref[...], g_ref[...], mod_ref[0], 0, 1).astype(BF16)
    dq = jnp.dot(h, wdq_ref[...], preferred_element_type=F32)
    dkv = jnp.dot(h, wdkv_ref[...], preferred_element_type=F32)
    kpe = jnp.dot(h, wpe_ref[...], preferred_element_type=F32)
    qa = dq * lax.rsqrt(jnp.mean(dq * dq, axis=-1, keepdims=True) + EPS) * qan_ref[...]
    kva = (dkv * lax.rsqrt(jnp.mean(dkv * dkv, axis=-1, keepdims=True) + EPS) * kvan_ref[...]).astype(BF16)
    q = jnp.dot(qa.astype(BF16), wuq_ref[...], preferred_element_type=F32)
    kn = jnp.dot(kva, wuk_ref[...], preferred_element_type=F32)
    v_ref[...] = jnp.dot(kva, wuv_ref[...], preferred_element_type=F32).astype(BF16)
    cos, sin, ones, swap = cos_ref[...], sin_ref[...], ones_ref[...], swap_ref[...]
    scale = MLA_QK ** -0.5 * math.log2(math.e)
    for hd in range(MLA_HEADS):
        hs = slice(hd * HEAD_PAD, (hd + 1) * HEAD_PAD)
        q_ref[:, hs] = (_head_norm_rope(q[:, hs], qn_ref[...], cos, sin, ones, swap) * scale).astype(BF16)
        k_ref[:, hs] = _head_norm_rope(kn[:, hs] + kpe, kn_ref[...], cos, sin, ones, swap).astype(BF16)


def mla_in(x_all, mod, p):
    rows = x_all.shape[0]
    tm = ROW_TILE
    w = p["w_dqkv"]
    wdq = w[:, :MLA_Q_LORA].astype(BF16)
    wdkv = w[:, MLA_Q_LORA:MLA_Q_LORA + MLA_KV_LORA].astype(BF16)
    wpe = jnp.pad(w[:, MLA_Q_LORA + MLA_KV_LORA:], ((0, 0), (MLA_NOPE, HEAD_PAD - MLA_QK))).astype(BF16)
    wuq = jnp.pad(p["w_uq"].reshape(MLA_Q_LORA, MLA_HEADS, MLA_QK), ((0, 0), (0, 0), (0, HEAD_PAD - MLA_QK)))
    wuq = wuq.reshape(MLA_Q_LORA, MLA_HEADS * HEAD_PAD).astype(BF16)
    wukv = p["w_ukv"].reshape(MLA_KV_LORA, MLA_HEADS, MLA_NOPE + MLA_V)
    wuk = jnp.pad(wukv[:, :, :MLA_NOPE], ((0, 0), (0, 0), (0, HEAD_PAD - MLA_NOPE)))
    wuk = wuk.reshape(MLA_KV_LORA, MLA_HEADS * HEAD_PAD).astype(BF16)
    wuv = wukv[:, :, MLA_NOPE:].reshape(MLA_KV_LORA, MLA_HEADS * MLA_V).astype(BF16)
    padg = lambda g: jnp.pad(g, (0, HEAD_PAD - MLA_QK))[None, :]
    cos, sin = _rope_tables(rows)
    row = lambda n: pl.BlockSpec((tm, n), lambda i: (i, 0))
    weights = [wdq, wdkv, wpe, p["q_a_norm"][None, :], p["kv_a_norm"][None, :], wuq, wuk, wuv,
               padg(p["q_norm"]), padg(p["k_norm"]),
               jnp.ones((HEAD_PAD, HEAD_PAD), BF16), jnp.asarray(_rope_swap_matrix(), BF16)]
    return pl.pallas_call(
        _mla_in_kernel,
        grid=(rows // tm,),
        in_specs=[row(D_MODEL),
                  pl.BlockSpec((1, 1, N_MOD * D_MODEL), lambda i: (_mod_row(i, tm), 0, 0)),
                  _full((1, D_MODEL))] + [_full(a.shape) for a in weights] + [row(HEAD_PAD)] * 2,
        out_specs=[row(MLA_HEADS * HEAD_PAD), row(MLA_HEADS * HEAD_PAD), row(MLA_HEADS * MLA_V)],
        out_shape=[jax.ShapeDtypeStruct((rows, MLA_HEADS * HEAD_PAD), BF16),
                   jax.ShapeDtypeStruct((rows, MLA_HEADS * HEAD_PAD), BF16),
                   jax.ShapeDtypeStruct((rows, MLA_HEADS * MLA_V), BF16)],
        compiler_params=_params("arbitrary"),
        name="mla_in",
    )(x_all, mod, p["norm1"][None, :], *weights, cos, sin)


ATTN_SAFE_SHIFT = 50.0
ATTN_BOUND_SLACK = 1.001
ATTN_BOUND_MARGIN = 1e-3


def _attn_kernel(q_ref, kc_ref, kl_ref, vc_ref, vl_ref, o_ref, knorm_ref, *, tq, tk):
    heads = (slice(0, HEAD_PAD), slice(HEAD_PAD, 2 * HEAD_PAD))
    qs = [q_ref[:, hs] for hs in heads]
    n_lat = SEQ // tk

    @pl.when(pl.program_id(2) == 0)
    def _key_norms():
        def sq_max(k2, hh):
            kf = k2[:, heads[hh]].astype(F32)
            return jnp.max(jnp.sum(kf * kf, axis=-1, keepdims=True), axis=0, keepdims=True)

        for hh in range(2):
            best = sq_max(kc_ref[...], hh)
            best = lax.fori_loop(
                0, n_lat, lambda j, b: jnp.maximum(b, sq_max(kl_ref[pl.ds(pl.multiple_of(j * tk, tk), tk), :], hh)), best)
            knorm_ref[hh] = jnp.broadcast_to(jnp.sqrt(best), (SUBLANES, LANES))

    def values(v, hh):
        lane = lax.broadcasted_iota(jnp.int32, v.shape, 1)
        own = (lane < MLA_V) if hh == 0 else (lane >= MLA_V)
        return jnp.where(own, v, jnp.ones_like(v))

    def scores(hh, k2):
        return lax.dot_general(qs[hh], k2[:, heads[hh]], (((1,), (1,)), ((), ())), preferred_element_type=F32)

    def finish(accs):
        lane = lax.broadcasted_iota(jnp.int32, (tq, 2 * MLA_V), 1)
        outs = [acc / pltpu.roll(acc, MLA_V, 1) for acc in accs]
        o_ref[...] = jnp.where(lane < MLA_V, outs[0], outs[1]).astype(o_ref.dtype)

    def lat_chunk(j):
        off = pl.multiple_of(j * tk, tk)
        return kl_ref[pl.ds(off, tk), :], vl_ref[pl.ds(off, tk), :]

    bounds = []
    for hh in range(2):
        qf = qs[hh].astype(F32)
        qn = jnp.sqrt(jnp.sum(qf * qf, axis=-1, keepdims=True))
        bounds.append(qn * knorm_ref[hh, 0:1, 0:1] * ATTN_BOUND_SLACK + ATTN_BOUND_MARGIN)
    widest = jnp.max(jnp.maximum(bounds[0], bounds[1]))

    def fixed_shift():
        def step(k2, v, accs):
            return tuple(accs[hh] + jnp.dot(jnp.exp2(scores(hh, k2) - bounds[hh]).astype(BF16), values(v, hh),
                                            preferred_element_type=F32) for hh in range(2))

        zero = jnp.zeros((tq, 2 * MLA_V), F32)
        accs = step(kc_ref[...], vc_ref[...], (zero, zero))
        finish(lax.fori_loop(0, n_lat, lambda j, a: step(*lat_chunk(j), a), accs, unroll=8))

    def running_max():
        def step(k2, v, carry):
            new = []
            for hh in range(2):
                m, acc = carry[hh]
                s = scores(hh, k2)
                m_new = jnp.maximum(m, jnp.max(s, axis=-1, keepdims=True))
                p = jnp.exp2(s - m_new).astype(BF16)
                acc = jnp.exp2(m - m_new) * acc + jnp.dot(p, values(v, hh), preferred_element_type=F32)
                new.append((m_new, acc))
            return tuple(new)

        init = tuple((jnp.full((tq, 1), NEG_INF, F32), jnp.zeros((tq, 2 * MLA_V), F32)) for _ in range(2))
        carry = step(kc_ref[...], vc_ref[...], init)
        carry = lax.fori_loop(0, n_lat, lambda j, c: step(*lat_chunk(j), c), carry)
        finish([acc for _, acc in carry])

    lax.cond(widest <= ATTN_SAFE_SHIFT, fixed_shift, running_max)


def attention(q, k, v):
    tq, tk = ATTN_TQ, ATTN_TK
    nq = SEQ // tq
    ctx0 = LAT_ROWS // CTX_LEN
    return pl.pallas_call(
        functools.partial(_attn_kernel, tq=tq, tk=tk),
        grid=(BATCH, MLA_HEADS // 2, nq),
        in_specs=[pl.BlockSpec((tq, 2 * HEAD_PAD), lambda b, h, i: (b * nq + i, h)),
                  pl.BlockSpec((CTX_LEN, 2 * HEAD_PAD), lambda b, h, i: (ctx0 + b, h)),
                  pl.BlockSpec((SEQ, 2 * HEAD_PAD), lambda b, h, i: (b, h)),
                  pl.BlockSpec((CTX_LEN, 2 * MLA_V), lambda b, h, i: (ctx0 + b, h)),
                  pl.BlockSpec((SEQ, 2 * MLA_V), lambda b, h, i: (b, h))],
        out_specs=pl.BlockSpec((tq, 2 * MLA_V), lambda b, h, i: (b * nq + i, h)),
        out_shape=jax.ShapeDtypeStruct((LAT_ROWS, MLA_HEADS * MLA_V), BF16),
        scratch_shapes=[pltpu.VMEM((2, SUBLANES, LANES), F32)],
        compiler_params=_params("arbitrary", "arbitrary", "arbitrary"),
        name="attention",
    )(q, k, k, v, v)


def _attn_out_kernel(x_ref, mod_ref, o_ref_in, wo_ref, out_ref):
    gate = mod_ref[0][:, 2 * D_MODEL:3 * D_MODEL]
    out_ref[...] = x_ref[...] + gate * jnp.dot(o_ref_in[...], wo_ref[...], preferred_element_type=F32)


def attn_out(x_all, mod, o, w_o):
    tm = ROW_TILE
    row = lambda n: pl.BlockSpec((tm, n), lambda i: (i, 0))
    return pl.pallas_call(
        _attn_out_kernel,
        grid=(LAT_ROWS // tm,),
        in_specs=[row(D_MODEL), pl.BlockSpec((1, 1, N_MOD * D_MODEL), lambda i: (_mod_row(i, tm), 0, 0)),
                  row(MLA_HEADS * MLA_V), _full(w_o.shape)],
        out_specs=row(D_MODEL),
        out_shape=jax.ShapeDtypeStruct((LAT_ROWS, D_MODEL), F32),
        compiler_params=_params("arbitrary"),
        name="attn_out",
    )(x_all, mod, o, w_o.astype(BF16))


def kernel(x, c, ctx, c_ctx, l0_w_ada, l0_b_ada, l0_norm1, l0_w_in, l0_conv_w, l0_conv_b, l0_dt_bias, l0_a_log, l0_d_skip, l0_ssd_norm, l0_w_out, l0_norm2, l0_peer_wq, l0_peer_keys, l0_peer_u, l0_peer_v, l1_w_ada, l1_b_ada, l1_norm1, l1_w_dqkv, l1_q_a_norm, l1_kv_a_norm, l1_w_uq, l1_w_ukv, l1_q_norm, l1_k_norm, l1_w_o, l1_norm2, l1_peer_wq, l1_peer_keys, l1_peer_u, l1_peer_v):
    x_lat, x_ctx = x.reshape(LAT_ROWS, D_MODEL), ctx.reshape(BATCH * CTX_LEN, D_MODEL)

    mod0 = ada_table(c, c_ctx, l0_w_ada, l0_b_ada)
    f, z, xbc, dt = even_in(x_lat, x_ctx, mod0, l0_norm1, l0_w_in)
    u = conv_silu(xbc, l0_conv_w, l0_conv_b)
    yf, yb = ssd_scan(u, dt, l0_dt_bias, l0_a_log)
    four_lat, four_ctx = fourier_mix(f)
    x_all = even_out(x_lat, x_ctx, four_lat, four_ctx, mod0, yf, yb, u, z, l0_d_skip, l0_ssd_norm, l0_w_out)
    x_all = peer_layer(x_all, mod0, l0_norm2, l0_peer_wq, l0_peer_keys, l0_peer_u, l0_peer_v)

    mod1 = ada_table(c, c_ctx, l1_w_ada, l1_b_ada)
    p1 = dict(norm1=l1_norm1, w_dqkv=l1_w_dqkv, q_a_norm=l1_q_a_norm, kv_a_norm=l1_kv_a_norm, w_uq=l1_w_uq,
              w_ukv=l1_w_ukv, q_norm=l1_q_norm, k_norm=l1_k_norm)
    q, k, v = mla_in(x_all, mod1, p1)
    o = attention(q, k, v)
    x_lat = attn_out(x_all, mod1, o, l1_w_o)
    x_lat = peer_layer(x_lat, mod1, l1_norm2, l1_peer_wq, l1_peer_keys, l1_peer_u, l1_peer_v)
    return x_lat.reshape(BATCH, SEQ, D_MODEL)
```

```python
import functools
import math

import numpy as np
import jax
import jax.numpy as jnp
from jax import lax
from jax.experimental import pallas as pl
from jax.experimental.pallas import tpu as pltpu

D_MODEL = 1024
BATCH = 2
SEQ = 8192
GRID_W = 64
CTX_LEN = 256
EPS = 1e-6
N_MOD = 6
LAT_ROWS = BATCH * SEQ
ALL_ROWS = LAT_ROWS + BATCH * CTX_LEN

FNET_GROUPS = 4
FNET_GROUP_DIM = 128
FNET_WIDTH = FNET_GROUPS * FNET_GROUP_DIM
FFT_N1 = 64
FFT_N2 = 128

SSD_HEADS = 16
SSD_HEAD_DIM = 64
SSD_INNER = SSD_HEADS * SSD_HEAD_DIM
SSD_GROUPS = 4
SSD_STATE = 128
SSD_CONV = 5
SSD_CHUNK = 128
SSD_CONV_DIM = SSD_INNER + 2 * SSD_GROUPS * SSD_STATE
SSD_GROUP_W = SSD_INNER // SSD_GROUPS

MLA_HEADS = 16
MLA_NOPE = 64
MLA_ROPE = 32
MLA_QK = MLA_NOPE + MLA_ROPE
MLA_V = 64
MLA_Q_LORA = 384
MLA_KV_LORA = 256
ROPE_THETA = 10000.0
HEAD_PAD = 128

PEER_HEADS = 8
PEER_KEYS = 128
PEER_EXPERTS = PEER_KEYS * PEER_KEYS
PEER_QDIM = 256
PEER_TOPK = 16

LANES = 128
SUBLANES = 8
VMEM_LIMIT = 56 * 1024 * 1024

ROW_TILE = 512
CONV_TILE = 256
PEER_TILE = 512
PEER_CHUNK = 512
PEER_CHUNKS_PER_STEP = 2
PEER_SUB = 256
PEER_SELECT_TOKENS = 128
ATTN_TQ = 1024
ATTN_TK = 512
ADA_COLS = 512
FFT1_COLS = 8192

F32 = jnp.float32
BF16 = jnp.bfloat16
HIGHEST = lax.Precision.HIGHEST
NEG_INF = float("-inf")


def _params(*sem):
    return pltpu.CompilerParams(dimension_semantics=sem, vmem_limit_bytes=VMEM_LIMIT)


def _mod_row(i, tile):
    return jnp.minimum((i * tile) // SEQ, BATCH)


def _full(shape):
    return pl.BlockSpec(shape, lambda *_: (0,) * len(shape))


def _silu(x):
    return x * (1.0 / (1.0 + jnp.exp(-x)))


def _modulated_norm(x, gain, mod, k_shift, k_scale):
    shift = mod[:, k_shift * D_MODEL:(k_shift + 1) * D_MODEL]
    scale = mod[:, k_scale * D_MODEL:(k_scale + 1) * D_MODEL]
    ms = jnp.mean(x * x, axis=-1, keepdims=True)
    return x * lax.rsqrt(ms + EPS) * gain * (1.0 + scale) + shift


def _hdot(a, b):
    return jnp.dot(a, b, precision=HIGHEST, preferred_element_type=F32)


def _bf16_terms(x, n):
    terms = []
    for _ in range(n):
        t = x.astype(BF16)
        terms.append(t)
        x = x - t.astype(F32)
    return terms


def _dot_exact_rhs(a, b01, terms=3):
    return sum(jnp.dot(t, b01, preferred_element_type=F32) for t in _bf16_terms(a, terms))


def _dot_exact_lhs(a01, b):
    return sum(jnp.dot(a01, t, preferred_element_type=F32) for t in _bf16_terms(b, 3))


def _dot3(a, b):
    a_hi, a_lo = _bf16_terms(a, 2)
    b_hi, b_lo = _bf16_terms(b, 2)
    dot = lambda x, y: jnp.dot(x, y, preferred_element_type=F32)
    return dot(a_hi, b_hi) + (dot(a_hi, b_lo) + dot(a_lo, b_hi))


def _ada_kernel(c_ref, w_ref, b_ref, o_ref):
    o_ref[...] = _hdot(_silu(c_ref[...]), w_ref[...]) + b_ref[...]


def ada_table(c, c_ctx, w_ada, b_ada):
    cond = jnp.concatenate([c, c_ctx[None, :], jnp.zeros((SUBLANES - BATCH - 1, D_MODEL), F32)], axis=0)
    tn = ADA_COLS
    out = pl.pallas_call(
        _ada_kernel,
        grid=(N_MOD * D_MODEL // tn,),
        in_specs=[_full((SUBLANES, D_MODEL)),
                  pl.BlockSpec((D_MODEL, tn), lambda j: (0, j)),
                  pl.BlockSpec((1, tn), lambda j: (0, j))],
        out_specs=pl.BlockSpec((SUBLANES, tn), lambda j: (0, j)),
        out_shape=jax.ShapeDtypeStruct((SUBLANES, N_MOD * D_MODEL), F32),
        compiler_params=_params("arbitrary"),
        name="ada_table",
    )(cond, w_ada, b_ada[None, :])
    return out[:BATCH + 1].reshape(BATCH + 1, 1, N_MOD * D_MODEL)


LAT_TILES = LAT_ROWS // ROW_TILE


def _lat_ctx_specs(width):
    assert BATCH * CTX_LEN == ROW_TILE, "the context rows are exactly one row tile"
    return [pl.BlockSpec((ROW_TILE, width), lambda i: (jnp.minimum(i, LAT_TILES - 1), 0)),
            pl.BlockSpec((ROW_TILE, width), lambda i: (0, 0))]


def _lat_or_ctx(lat_ref, ctx_ref):
    return jnp.where(pl.program_id(0) < LAT_TILES, lat_ref[...], ctx_ref[...])


def _even_in_kernel(xl_ref, xc_ref, mod_ref, g_ref, wf_ref, wz_ref, wx_ref, wd_ref, f_ref, z_ref, xbc_ref, dt_ref):
    h = _modulated_norm(_lat_or_ctx(xl_ref, xc_ref), g_ref[...], mod_ref[0], 0, 1).astype(BF16)
    f_ref[...] = jnp.dot(h, wf_ref[...], preferred_element_type=F32)
    z_ref[...] = jnp.dot(h, wz_ref[...], preferred_element_type=F32)
    xbc_ref[...] = jnp.dot(h, wx_ref[...], preferred_element_type=F32)
    dt_ref[...] = jnp.dot(h, wd_ref[...], preferred_element_type=F32)


def even_in(x_lat, x_ctx, mod, norm1, w_in):
    rows = x_lat.shape[0] + x_ctx.shape[0]
    o1, o2, o3 = FNET_WIDTH, FNET_WIDTH + SSD_INNER, FNET_WIDTH + SSD_INNER + SSD_CONV_DIM
    wf = w_in[:, :o1].astype(BF16)
    wz = w_in[:, o1:o2].astype(BF16)
    wx = w_in[:, o2:o3].astype(BF16)
    wd = jnp.pad(w_in[:, o3:], ((0, 0), (0, LANES - 2 * SSD_HEADS))).astype(BF16)
    tm = ROW_TILE
    row = lambda n: pl.BlockSpec((tm, n), lambda i: (i, 0))
    return pl.pallas_call(
        _even_in_kernel,
        grid=(rows // tm,),
        in_specs=_lat_ctx_specs(D_MODEL) + [
                  pl.BlockSpec((1, 1, N_MOD * D_MODEL), lambda i: (_mod_row(i, tm), 0, 0)),
                  _full((1, D_MODEL)),
                  _full(wf.shape), _full(wz.shape), _full(wx.shape), _full(wd.shape)],
        out_specs=[row(FNET_WIDTH), row(SSD_INNER), row(SSD_CONV_DIM), row(LANES)],
        out_shape=[jax.ShapeDtypeStruct((rows, FNET_WIDTH), F32),
                   jax.ShapeDtypeStruct((rows, SSD_INNER), F32),
                   jax.ShapeDtypeStruct((rows, SSD_CONV_DIM), F32),
                   jax.ShapeDtypeStruct((rows, LANES), F32)],
        compiler_params=_params("arbitrary"),
        name="even_in",
    )(x_lat, x_ctx, mod, norm1[None, :], wf, wz, wx, wd)


def _conv_kernel(x_ref, prev_ref, next_ref, w_ref, b_ref, o_ref, ext_ref, *, tm):
    row0 = pl.program_id(0) * tm
    in_lat = row0 < LAT_ROWS
    first = jnp.where(in_lat, row0 % SEQ == 0, (row0 - LAT_ROWS) % CTX_LEN == 0)
    last = jnp.where(in_lat, (row0 + tm) % SEQ == 0, (row0 + tm - LAT_ROWS) % CTX_LEN == 0)
    ext_ref[0:SUBLANES, :] = prev_ref[...] * jnp.where(first, 0.0, 1.0)
    ext_ref[SUBLANES:SUBLANES + tm, :] = x_ref[...]
    ext_ref[SUBLANES + tm:2 * SUBLANES + tm, :] = next_ref[...] * jnp.where(last, 0.0, 1.0)
    pad = SSD_CONV // 2
    acc = b_ref[...] + w_ref[0:1, :] * ext_ref[pl.ds(SUBLANES - pad, tm), :]
    for k in range(1, SSD_CONV):
        acc = acc + w_ref[k:k + 1, :] * ext_ref[pl.ds(SUBLANES - pad + k, tm), :]
    o_ref[...] = _silu(acc)


def conv_silu(xbc, conv_w, conv_b):
    rows, ch = xbc.shape
    tm, tc = CONV_TILE, ch
    assert CTX_LEN % tm == 0 and SEQ % tm == 0
    per = tm // SUBLANES
    n_small = rows // SUBLANES
    w = jnp.pad(conv_w, ((0, SUBLANES - SSD_CONV), (0, 0)))
    return pl.pallas_call(
        functools.partial(_conv_kernel, tm=tm),
        grid=(rows // tm, ch // tc),
        in_specs=[pl.BlockSpec((tm, tc), lambda i, j: (i, j)),
                  pl.BlockSpec((SUBLANES, tc), lambda i, j: (jnp.maximum(i * per - 1, 0), j)),
                  pl.BlockSpec((SUBLANES, tc), lambda i, j: (jnp.minimum((i + 1) * per, n_small - 1), j)),
                  pl.BlockSpec((SUBLANES, tc), lambda i, j: (0, j)),
                  pl.BlockSpec((1, tc), lambda i, j: (0, j))],
        out_specs=pl.BlockSpec((tm, tc), lambda i, j: (i, j)),
        out_shape=jax.ShapeDtypeStruct((rows, ch), F32),
        scratch_shapes=[pltpu.VMEM((tm + 2 * SUBLANES, tc), F32)],
        compiler_params=_params("arbitrary", "arbitrary"),
        name="conv_silu",
    )(xbc, xbc, xbc, w, conv_b[None, :])


def _softplus(x):
    return jnp.maximum(x, 0.0) + jnp.log1p(jnp.exp(-jnp.abs(x)))


def _ssd_direction(u_ref, dt_ref, dtb_ref, alog_ref, exp_ref, state_ref, y_ref, direction):
    cl = SSD_CHUNK
    xs = u_ref[:, :SSD_INNER]
    dtv = _softplus(dt_ref[...] + dtb_ref[...])
    dta = dtv * (-jnp.exp(alog_ref[...]))
    r = lax.broadcasted_iota(jnp.int32, (cl, cl), 0)
    c = lax.broadcasted_iota(jnp.int32, (cl, cl), 1)
    tri = (r >= c) if direction == 0 else (r <= c)
    cs = _dot_exact_lhs(tri.astype(F32).astype(BF16), dta)
    cs_t = cs.T
    edge = cs[cl - 1:cl, :] if direction == 0 else cs[0:1, :]
    to_end = jnp.exp(edge - cs)
    from_start = jnp.exp(cs)
    spread = _dot_exact_rhs(jnp.concatenate([dtv, to_end, from_start], axis=0), exp_ref[direction], terms=2)
    dt_x, te_x, fs_x = spread[:cl], spread[cl:2 * cl], spread[2 * cl:]
    chunk_decay = fs_x[cl - 1:cl, :] if direction == 0 else fs_x[0:1, :]
    xd = xs * dt_x
    xd_b = xd.astype(BF16)
    xte_b = (xd * te_x).astype(BF16)
    lane = lax.broadcasted_iota(jnp.int32, (cl, LANES), 1)
    low = lane < SSD_HEAD_DIM
    for g in range(SSD_GROUPS):
        bg = u_ref[:, SSD_INNER + g * SSD_STATE:SSD_INNER + (g + 1) * SSD_STATE]
        cg = u_ref[:, SSD_INNER + (SSD_GROUPS + g) * SSD_STATE:SSD_INNER + (SSD_GROUPS + g + 1) * SSD_STATE]
        bg_b, cg_b = bg.astype(BF16), cg.astype(BF16)
        cb = lax.dot_general(cg_b, bg_b, (((1,), (1,)), ((), ())), preferred_element_type=F32)
        gs = slice(g * SSD_GROUP_W, (g + 1) * SSD_GROUP_W)
        state = state_ref[g]
        y_off = jnp.dot(cg_b, state.astype(BF16), preferred_element_type=F32) * fs_x[:, gs]
        new_state = jnp.dot(bg.T.astype(BF16), xte_b[:, gs], preferred_element_type=F32)
        state_ref[g] = state * chunk_decay[:, gs] + new_state
        for pair in range(2):
            blk = xd_b[:, g * SSD_GROUP_W + pair * LANES:g * SSD_GROUP_W + (pair + 1) * LANES]
            ms, halves = [], []
            for sub in range(2):
                col = direction * SSD_HEADS + g * 4 + pair * 2 + sub
                seg = cs[:, col:col + 1] - cs_t[col:col + 1, :]
                decay = jnp.exp(jnp.where(tri, seg, NEG_INF))
                ms.append((cb * decay).astype(BF16))
                halves.append(jnp.where(low if sub == 0 else jnp.logical_not(low), blk, jnp.zeros_like(blk)))
            y_pair = jnp.dot(jnp.concatenate(ms, axis=1), jnp.concatenate(halves, axis=0),
                             preferred_element_type=F32)
            lo = g * SSD_GROUP_W + pair * LANES
            y_ref[:, lo:lo + LANES] = y_pair + y_off[:, pair * LANES:(pair + 1) * LANES]


def _ssd_kernel(uf_ref, ub_ref, dtf_ref, dtb_in_ref, bias_ref, alog_ref, exp_ref, yf_ref, yb_ref, sf_ref, sb_ref):
    @pl.when(pl.program_id(1) == 0)
    def _():
        sf_ref[...] = jnp.zeros_like(sf_ref)
        sb_ref[...] = jnp.zeros_like(sb_ref)

    _ssd_direction(uf_ref, dtf_ref, bias_ref, alog_ref, exp_ref, sf_ref, yf_ref, 0)
    _ssd_direction(ub_ref, dtb_in_ref, bias_ref, alog_ref, exp_ref, sb_ref, yb_ref, 1)


def ssd_scan(u, dt, dt_bias, a_log):
    rows = u.shape[0]
    cl = SSD_CHUNK
    lat_chunks, ctx_chunks = SEQ // cl, CTX_LEN // cl
    steps = ctx_chunks + lat_chunks
    ctx0 = LAT_ROWS // cl

    def fwd_chunk(b, t):
        return jnp.where(t < ctx_chunks, ctx0 + b * ctx_chunks + t, b * lat_chunks + t - ctx_chunks)

    def bwd_chunk(b, t):
        return jnp.where(t < ctx_chunks, ctx0 + b * ctx_chunks + (ctx_chunks - 1 - t),
                         b * lat_chunks + (lat_chunks - 1) - (t - ctx_chunks))

    pad = LANES - 2 * SSD_HEADS
    bias = jnp.pad(dt_bias.reshape(1, -1), ((0, 0), (0, pad)))
    alog = jnp.pad(a_log.reshape(1, -1), ((0, 0), (0, pad)))
    expand = np.zeros((2, LANES, SSD_INNER), np.float32)
    for d in range(2):
        for h in range(SSD_HEADS):
            expand[d, d * SSD_HEADS + h, h * SSD_HEAD_DIM:(h + 1) * SSD_HEAD_DIM] = 1.0
    spec = lambda n, fn: pl.BlockSpec((cl, n), lambda b, t: (fn(b, t), 0))
    return pl.pallas_call(
        _ssd_kernel,
        grid=(BATCH, steps),
        in_specs=[spec(SSD_CONV_DIM, fwd_chunk), spec(SSD_CONV_DIM, bwd_chunk),
                  spec(LANES, fwd_chunk), spec(LANES, bwd_chunk),
                  _full((1, LANES)), _full((1, LANES)), _full((2, LANES, SSD_INNER))],
        out_specs=[spec(SSD_INNER, fwd_chunk), spec(SSD_INNER, bwd_chunk)],
        out_shape=[jax.ShapeDtypeStruct((rows, SSD_INNER), F32)] * 2,
        scratch_shapes=[pltpu.VMEM((SSD_GROUPS, SSD_STATE, SSD_GROUP_W), F32)] * 2,
        compiler_params=_params("arbitrary", "arbitrary"),
        name="ssd_scan",
    )(u, u, dt, dt, bias, alog, jnp.asarray(expand, BF16))


def _dft_cos_sin(n):
    k = np.arange(n)
    ang = 2.0 * np.pi * ((k[:, None] * k[None, :]) % n) / n
    return np.cos(ang), np.sin(ang)


def _fft1_kernel(x_ref, m_ref, o_ref):
    o_ref[0] = _dot3(m_ref[...], x_ref[...])


def _fft2_kernel(ar_ref, ai_ref, twr_ref, twi_ref, m2_ref, mc_ref, o_ref):
    n2 = FFT_N2
    for j in range(SUBLANES):
        ar, ai = ar_ref[0, j], ai_ref[0, j]
        twr = jnp.concatenate([twr_ref[j]] * FNET_GROUPS, axis=1)
        twi = jnp.concatenate([twi_ref[j]] * FNET_GROUPS, axis=1)
        p = ar * twr - ai * twi
        q = ar * twi + ai * twr
        uv = _dot3(m2_ref[...], jnp.concatenate([p, q], axis=0))
        groups = []
        for g in range(FNET_GROUPS):
            gs = slice(g * FNET_GROUP_DIM, (g + 1) * FNET_GROUP_DIM)
            groups.append(_dot3(jnp.concatenate([uv[:n2, gs], uv[n2:, gs]], axis=1), mc_ref[...]))
        o_ref[0, :, j, :] = jnp.concatenate(groups, axis=1)


def _fft_ctx_kernel(x_ref, mc_ref, mp_ref, o_ref):
    x = x_ref[...]
    for g in range(FNET_GROUPS):
        gs = slice(g * FNET_GROUP_DIM, (g + 1) * FNET_GROUP_DIM)
        cs = _dot3(x[:, gs], mc_ref[...])
        stacked = jnp.concatenate([cs[:, :FNET_GROUP_DIM], cs[:, FNET_GROUP_DIM:]], axis=0)
        o_ref[:, gs] = _dot3(mp_ref[...], stacked)


def fourier_mix(f_all):
    rows = f_all.shape[0]
    n1, n2, gd = FFT_N1, FFT_N2, FNET_GROUP_DIM
    row_w = n2 * FNET_WIDTH
    c1, s1 = _dft_cos_sin(n1)
    m1 = jnp.asarray(np.concatenate([c1, -s1], axis=0), F32)
    tn = FFT1_COLS
    stage1 = pl.pallas_call(
        _fft1_kernel,
        grid=(BATCH, row_w // tn),
        in_specs=[pl.BlockSpec((n1, tn), lambda b, j: (b, j)), _full((2 * n1, n1))],
        out_specs=pl.BlockSpec((1, 2 * n1, tn), lambda b, j: (b, 0, j)),
        out_shape=jax.ShapeDtypeStruct((BATCH, 2 * n1, row_w), F32),
        compiler_params=_params("arbitrary", "arbitrary"),
        name="fft_stage1",
    )(f_all.reshape(rows // n2, row_w), m1)

    k1 = np.arange(n1)[:, None]
    l2 = np.arange(n2)[None, :]
    tw = 2.0 * np.pi * (k1 * l2) / SEQ
    twr = jnp.asarray(np.repeat(np.cos(tw)[:, :, None], gd, axis=2), F32)
    twi = jnp.asarray(np.repeat(-np.sin(tw)[:, :, None], gd, axis=2), F32)
    c2, s2 = _dft_cos_sin(n2)
    m2 = jnp.asarray(np.block([[c2, s2], [-s2, c2]]), F32)
    cc, sc = _dft_cos_sin(gd)
    mc = jnp.asarray(np.concatenate([cc, sc], axis=0) / math.sqrt(SEQ * gd), F32)
    a4 = stage1.reshape(BATCH, 2 * n1, n2, FNET_WIDTH)
    sub = SUBLANES
    lat = pl.pallas_call(
        _fft2_kernel,
        grid=(BATCH, n1 // sub),
        in_specs=[pl.BlockSpec((1, sub, n2, FNET_WIDTH), lambda b, k: (b, k, 0, 0)),
                  pl.BlockSpec((1, sub, n2, FNET_WIDTH), lambda b, k: (b, n1 // sub + k, 0, 0)),
                  pl.BlockSpec((sub, n2, gd), lambda b, k: (k, 0, 0)),
                  pl.BlockSpec((sub, n2, gd), lambda b, k: (k, 0, 0)),
                  _full((2 * n2, 2 * n2)), _full((2 * gd, gd))],
        out_specs=pl.BlockSpec((1, n2, sub, FNET_WIDTH), lambda b, k: (b, 0, k, 0)),
        out_shape=jax.ShapeDtypeStruct((BATCH, n2, n1, FNET_WIDTH), F32),
        compiler_params=_params("arbitrary", "arbitrary"),
        name="fft_stage2",
    )(a4, a4, twr, twi, m2, mc)

    cp, sp = _dft_cos_sin(CTX_LEN)
    mp = jnp.asarray(np.concatenate([cp, -sp], axis=1) / math.sqrt(CTX_LEN * gd), F32)
    mcc = jnp.asarray(np.concatenate([cc, sc], axis=1), F32)
    ctx0 = LAT_ROWS // CTX_LEN
    ctx = pl.pallas_call(
        _fft_ctx_kernel,
        grid=(BATCH,),
        in_specs=[pl.BlockSpec((CTX_LEN, FNET_WIDTH), lambda b: (ctx0 + b, 0)),
                  _full((gd, 2 * gd)), _full((CTX_LEN, 2 * CTX_LEN))],
        out_specs=pl.BlockSpec((CTX_LEN, FNET_WIDTH), lambda b: (b, 0)),
        out_shape=jax.ShapeDtypeStruct((BATCH * CTX_LEN, FNET_WIDTH), F32),
        compiler_params=_params("arbitrary"),
        name="fft_ctx",
    )(f_all, mcc, mp)
    return lat.reshape(LAT_ROWS, FNET_WIDTH), ctx


def _even_out_kernel(xl_ref, xc_ref, fl_ref, fc_ref, mod_ref, yf_ref, yb_ref, xs_ref, z_ref, dsk_ref, gn_ref,
                     wof_ref, wos_ref, o_ref):
    y = yf_ref[...] + yb_ref[...] + dsk_ref[...] * xs_ref[...]
    gated = y * _silu(z_ref[...])
    out = jnp.dot(_lat_or_ctx(fl_ref, fc_ref).astype(BF16), wof_ref[...], preferred_element_type=F32)
    for g in range(SSD_GROUPS):
        gs = slice(g * SSD_GROUP_W, (g + 1) * SSD_GROUP_W)
        v = gated[:, gs]
        normed = v * lax.rsqrt(jnp.mean(v * v, axis=-1, keepdims=True) + EPS) * gn_ref[:, gs]
        out = out + jnp.dot(normed.astype(BF16), wos_ref[gs, :], preferred_element_type=F32)
    gate = mod_ref[0][:, 2 * D_MODEL:3 * D_MODEL]
    o_ref[...] = _lat_or_ctx(xl_ref, xc_ref) + gate * out


def even_out(x_lat, x_ctx, four_lat, four_ctx, mod, yf, yb, u, z, d_skip, ssd_norm, w_out):
    rows = x_lat.shape[0] + x_ctx.shape[0]
    tm = ROW_TILE
    dsk = jnp.repeat(d_skip[0] + d_skip[1], SSD_HEAD_DIM)[None, :]
    wof = w_out[:FNET_WIDTH].astype(BF16)
    wos = w_out[FNET_WIDTH:].astype(BF16)
    row = lambda n: pl.BlockSpec((tm, n), lambda i: (i, 0))
    return pl.pallas_call(
        _even_out_kernel,
        grid=(rows // tm,),
        in_specs=_lat_ctx_specs(D_MODEL) + _lat_ctx_specs(FNET_WIDTH) + [
                  pl.BlockSpec((1, 1, N_MOD * D_MODEL), lambda i: (_mod_row(i, tm), 0, 0)),
                  row(SSD_INNER), row(SSD_INNER), row(SSD_INNER), row(SSD_INNER),
                  _full((1, SSD_INNER)), _full((1, SSD_INNER)), _full(wof.shape), _full(wos.shape)],
        out_specs=row(D_MODEL),
        out_shape=jax.ShapeDtypeStruct((rows, D_MODEL), F32),
        compiler_params=_params("arbitrary"),
        name="even_out",
    )(x_lat, x_ctx, four_lat, four_ctx, mod, yf, yb, u, z, dsk, ssd_norm[None, :], wof, wos)


BF16_ROWS = 2 * SUBLANES
NOT_TOP = 64.0
AFTER_EVERY_ORDER = float(PEER_TOPK * PEER_KEYS)


def _erf_gelu(z):
    return 0.5 * z * (1.0 + lax.erf(z * (1.0 / math.sqrt(2.0))))


def _merge_exchange_pairs(n):
    t = max(1, math.ceil(math.log2(n)))
    p, pairs = 2 ** (t - 1), []
    while p > 0:
        q, r, d = 2 ** (t - 1), 0, p
        while d > 0:
            pairs += [(i, i + d) for i in range(n - d) if i & p == r]
            d, q, r = q - p, q // 2, p
        p //= 2
    return pairs


def _top_values_sorted(score, k):
    groups = [score[r * SUBLANES:(r + 1) * SUBLANES] for r in range(score.shape[0] // SUBLANES)]
    for a, b in _merge_exchange_pairs(len(groups)):
        groups[a], groups[b] = jnp.maximum(groups[a], groups[b]), jnp.minimum(groups[a], groups[b])
    vals = []
    for i in range(k):
        m = jnp.max(groups[0], axis=0, keepdims=True)
        vals.append(m)
        needed = k - 1 - i
        if needed == 0:
            break
        hit = groups[0] == m
        shifted = [jnp.where(hit, groups[d + 1], groups[d]) for d in range(len(groups) - 1)]
        if len(groups) <= needed:
            shifted.append(jnp.where(hit, NEG_INF, groups[-1]))
        groups = shifted[:needed]
    return jnp.concatenate(vals, axis=0)


def _rank_among(values, top, k):
    assert k & (k - 1) == 0
    bits, count, stride = [], jnp.zeros_like(values), k // 2
    while stride >= 1:
        cands = [top[p + stride - 1:p + stride] for p in range(0, k, 2 * stride)]
        for b in reversed(bits):
            cands = [jnp.where(b, cands[2 * i + 1], cands[2 * i]) for i in range(len(cands) // 2)]
        bit = cands[0] > values
        count = count + jnp.where(bit, float(stride), 0.0)
        bits.append(bit)
        stride //= 2
    return jnp.where(values >= top[k - 1:k], count, NOT_TOP)


def _pair_candidates(v1, v2, k):
    assert k == 2 * SUBLANES, "the row grouping below is laid out for k = 16"
    t = v1.shape[1]
    row8 = lax.broadcasted_iota(jnp.int32, (SUBLANES, t), 0)
    rowf = row8.astype(F32)
    sums = [v1[0:1] + v2[0:8], v1[0:1] + v2[8:16], v1[1:2] + v2[0:8], v1[8:16] + v2[0:1]]
    order = [rowf, rowf + 8.0, rowf + float(k), (rowf + 8.0) * float(k)]
    for i in range(2, 8):
        sums.append(jnp.where(row8 < k // (i + 1), v1[i:i + 1] + v2[0:8], NEG_INF))
        order.append(rowf + float(i * k))
    return jnp.concatenate(sums, axis=0), jnp.concatenate(order, axis=0)


def _select_by_value(s1, s2, k):
    v1 = _top_values_sorted(s1, k)
    v2 = _top_values_sorted(s2, k)
    r2 = _rank_among(s2, v2, k)
    cand, _ = _pair_candidates(v1, v2, k)
    tau = _top_values_sorted(cand, k)[k - 1:k, :]
    chosen = cand >= tau
    zsum = jnp.sum(jnp.where(chosen, jnp.exp(cand - (v1[0:1] + v2[0:1])), 0.0), axis=0, keepdims=True)
    in_top = s1 >= v1[k - 1:k]
    a1 = jnp.where(in_top, s1, NEG_INF)
    lim = jnp.zeros_like(s1)
    for j in range(k // 2):
        lim = lim + jnp.where(a1 + v2[j:j + 1] >= tau, 1.0, 0.0)
    best = jnp.zeros_like(tau)
    for j in range(k // 2, k):
        best = best + jnp.where(v1[0:1] + v2[j:j + 1] >= tau, 1.0, 0.0)
    lim = lim + jnp.where(s1 == v1[0:1], best, 0.0)
    count = lambda mask: jnp.sum(jnp.where(mask, 1.0, 0.0), axis=0, keepdims=True)
    most = jnp.maximum(jnp.maximum(count(in_top), count(r2 < k)), count(chosen))
    return lim, jnp.exp(s1 - v1[0:1]) / zsum, r2, jnp.exp(s2 - v2[0:1]), most


def _top_ranks_ordered(score, order, k):
    work, vals = score, []
    rank = jnp.full(score.shape, NOT_TOP, F32)
    for i in range(k):
        m = jnp.max(work, axis=0, keepdims=True)
        first = jnp.min(jnp.where(work == m, order, AFTER_EVERY_ORDER), axis=0, keepdims=True)
        taken = order == first
        vals.append(m)
        rank = jnp.where(taken, float(i), rank)
        work = jnp.where(taken, NEG_INF, work)
    return jnp.concatenate(vals, axis=0), rank


def _select_by_order(s1, s2, k):
    key_index = lax.broadcasted_iota(jnp.int32, s1.shape, 0).astype(F32)
    v1, r1 = _top_ranks_ordered(s1, key_index, k)
    v2, r2 = _top_ranks_ordered(s2, key_index, k)
    cand, position = _pair_candidates(v1, v2, k)
    chosen = _top_ranks_ordered(cand, position, k)[1] < k
    zsum = jnp.sum(jnp.where(chosen, jnp.exp(cand - (v1[0:1] + v2[0:1])), 0.0), axis=0, keepdims=True)
    picks = jnp.where(chosen, 1.0, 0.0)
    group = lambda g: jnp.sum(picks[g * SUBLANES:(g + 1) * SUBLANES], axis=0, keepdims=True)
    per_rank = [group(0) + group(1), group(2)] + [group(g) for g in range(4, 10)]
    per_rank = jnp.concatenate(per_rank + [picks[3 * SUBLANES:4 * SUBLANES]], axis=0)
    lim = jnp.zeros_like(s1)
    for i in range(k):
        lim = lim + jnp.where(r1 == float(i), per_rank[i:i + 1], 0.0)
    return lim, jnp.exp(s1 - v1[0:1]) / zsum, r2, jnp.exp(s2 - v2[0:1])


def _peer_kernel(x_ref, mod_ref, g_ref, wq_ref, keys_ref, u0_ref, *rest, tm, chunk, per_step):
    u_refs = rest[:per_step]
    (vt_ref, o_ref, ht_ref, q_ref, lim_ref, w1_ref, rank2_ref, w2_ref, acc_ref, za_ref, zb_ref) = rest[per_step:]
    c = pl.program_id(1)
    k = PEER_TOPK
    packed = (PEER_KEYS // BF16_ROWS, BF16_ROWS, tm)
    tsel = PEER_SELECT_TOKENS

    @pl.when(c == 0)
    def _prologue():
        h2 = _modulated_norm(x_ref[...], g_ref[...], mod_ref[0], 3, 4)
        ht = h2.T.astype(BF16)
        ht_ref[...] = ht
        q = jnp.dot(wq_ref[...], ht, preferred_element_type=F32)
        q_ref[...] = q.reshape(2 * PEER_HEADS, PEER_QDIM // 2, tm)
        acc_ref[...] = jnp.zeros_like(acc_ref)
        za_ref[...] = jnp.dot(u0_ref[...], ht, preferred_element_type=F32)
        def head_body(h, carry):
            def scores(part):
                ts = slice(part * tsel, (part + 1) * tsel)
                s1 = jnp.dot(keys_ref[2 * h], q_ref[2 * h, :, ts].astype(BF16), preferred_element_type=F32)
                s2 = jnp.dot(keys_ref[2 * h + 1], q_ref[2 * h + 1, :, ts].astype(BF16), preferred_element_type=F32)
                return ts, s1, s2

            def store(ts, lim, w1, r2, w2):
                small = (PEER_KEYS // BF16_ROWS, BF16_ROWS, tsel)
                lim_ref[h, :, ts] = lim
                w1_ref[h, :, ts] = w1
                rank2_ref[h, :, :, ts] = r2.reshape(small).astype(BF16)
                w2_ref[h, :, :, ts] = w2.reshape(small).astype(BF16)

            counts = []
            for part in range(tm // tsel):
                ts, s1, s2 = scores(part)
                lim, w1, r2, w2, count = _select_by_value(s1, s2, k)
                store(ts, lim, w1, r2, w2)
                counts.append(count)

            @pl.when(jnp.max(functools.reduce(jnp.maximum, counts)) > k)
            def _ties():
                for part, count in enumerate(counts):
                    @pl.when(jnp.max(count) > k)
                    def _redo():
                        ts, s1, s2 = scores(part)
                        store(ts, *_select_by_order(s1, s2, k))

            return carry

        lax.fori_loop(0, PEER_HEADS, head_body, 0)

    per_key = PEER_KEYS // BF16_ROWS
    keys_per_sub = PEER_SUB // PEER_KEYS
    step_keys = per_step * chunk // PEER_KEYS
    key0 = pl.multiple_of(c * step_keys, step_keys)
    step_lim = [lim_ref[h, pl.ds(key0, step_keys), :] for h in range(PEER_HEADS)]
    step_w1 = [w1_ref[h, pl.ds(key0, step_keys), :] for h in range(PEER_HEADS)]

    def gated_values(z_ref, which):
        total = None
        for j in range(chunk // PEER_SUB):
            rows = slice(j * PEER_SUB, (j + 1) * PEER_SUB)
            act = _erf_gelu(z_ref[rows, :]).reshape(PEER_SUB // BF16_ROWS, BF16_ROWS, tm).astype(BF16)
            parts = []
            for jj in range(keys_per_sub):
                e1 = which * (chunk // PEER_KEYS) + j * keys_per_sub + jj
                gate = jnp.zeros(packed, BF16)
                for h in range(PEER_HEADS):
                    lim = jnp.broadcast_to(step_lim[h][e1:e1 + 1, :], (BF16_ROWS, tm)).astype(BF16)
                    w1 = jnp.broadcast_to(step_w1[h][e1:e1 + 1, :], (BF16_ROWS, tm)).astype(BF16)
                    picked = jnp.where(rank2_ref[h] < lim[None], w2_ref[h], jnp.zeros(packed, BF16))
                    gate = gate + picked * w1[None]
                parts.append(act[jj * per_key:(jj + 1) * per_key] * gate)
            a = jnp.concatenate(parts, axis=0).reshape(PEER_SUB, tm)
            cols = slice(which * chunk + j * PEER_SUB, which * chunk + (j + 1) * PEER_SUB)
            d = jnp.dot(vt_ref[:, cols], a, preferred_element_type=F32)
            total = d if total is None else total + d
        return total

    bufs = (za_ref, zb_ref)
    total = None
    for i in range(per_step):
        bufs[(i + 1) % 2][...] = jnp.dot(u_refs[i][...], ht_ref[...], preferred_element_type=F32)
        d = gated_values(bufs[i % 2], i)
        total = d if total is None else total + d
    acc_ref[...] += total

    @pl.when(c == pl.num_programs(1) - 1)
    def _epilogue():
        gate2 = mod_ref[0][:, 5 * D_MODEL:6 * D_MODEL]
        o_ref[...] = x_ref[...] + gate2 * acc_ref[...].T


def peer_layer(x_rows, mod, norm2, peer_wq, peer_keys, peer_u, peer_v):
    rows = x_rows.shape[0]
    tm, chunk = PEER_TILE, PEER_CHUNK
    wq_t = peer_wq.T.astype(BF16)
    keys = peer_keys.reshape(2 * PEER_HEADS, PEER_KEYS, PEER_QDIM // 2).astype(BF16)
    u = peer_u.astype(BF16)
    vt = peer_v.astype(BF16).T
    head_f32 = pltpu.VMEM((PEER_HEADS, PEER_KEYS, tm), F32)
    head_bf16 = pltpu.VMEM((PEER_HEADS, PEER_KEYS // BF16_ROWS, BF16_ROWS, tm), BF16)
    n_chunks = PEER_EXPERTS // chunk
    per_step = PEER_CHUNKS_PER_STEP
    ahead = lambda k: pl.BlockSpec((chunk, D_MODEL), lambda i, c: (jnp.minimum(per_step * c + k, n_chunks - 1), 0))
    return pl.pallas_call(
        functools.partial(_peer_kernel, tm=tm, chunk=chunk, per_step=per_step),
        grid=(rows // tm, n_chunks // per_step),
        in_specs=[pl.BlockSpec((tm, D_MODEL), lambda i, c: (i, 0)),
                  pl.BlockSpec((1, 1, N_MOD * D_MODEL), lambda i, c: (_mod_row(i, tm), 0, 0)),
                  _full((1, D_MODEL)), _full(wq_t.shape), _full(keys.shape),
                  pl.BlockSpec((chunk, D_MODEL), lambda i, c: (0, 0))]
                 + [ahead(k) for k in range(1, per_step + 1)]
                 + [pl.BlockSpec((D_MODEL, per_step * chunk), lambda i, c: (0, c))],
        out_specs=pl.BlockSpec((tm, D_MODEL), lambda i, c: (i, 0)),
        out_shape=jax.ShapeDtypeStruct((rows, D_MODEL), F32),
        scratch_shapes=[pltpu.VMEM((D_MODEL, tm), BF16),
                        pltpu.VMEM((2 * PEER_HEADS, PEER_QDIM // 2, tm), F32),
                        head_f32, head_f32, head_bf16, head_bf16,
                        pltpu.VMEM((D_MODEL, tm), F32),
                        pltpu.VMEM((chunk, tm), F32), pltpu.VMEM((chunk, tm), F32)],
        compiler_params=_params("arbitrary", "arbitrary"),
        name="peer",
    )(x_rows, mod, norm2[None, :], wq_t, keys, u, *([u] * per_step), vt)


def _rope_tables(rows):
    l = jnp.arange(SEQ)
    per_axis = MLA_ROPE // 2
    inv = ROPE_THETA ** (-jnp.arange(0, per_axis, 2, dtype=F32) / per_axis)
    ang = jnp.concatenate([(l // GRID_W)[:, None] * inv, (l % GRID_W)[:, None] * inv], axis=-1)
    cos, sin = jnp.cos(ang), jnp.sin(ang)
    one = jnp.ones((SEQ, MLA_NOPE), F32)
    tail = jnp.zeros((SEQ, HEAD_PAD - MLA_QK), F32)
    zero64 = jnp.zeros((SEQ, MLA_NOPE), F32)
    cos_t = jnp.concatenate([one, cos, cos, tail], axis=1)
    sin_t = jnp.concatenate([zero64, -sin, sin, tail], axis=1)
    n_ctx = rows - LAT_ROWS
    ident = jnp.concatenate([jnp.ones((n_ctx, MLA_QK), F32), jnp.zeros((n_ctx, HEAD_PAD - MLA_QK), F32)], axis=1)
    zeros = jnp.zeros((n_ctx, HEAD_PAD), F32)
    tile = lambda t, c: jnp.concatenate([t] * BATCH + [c], axis=0)
    return tile(cos_t, ident), tile(sin_t, zeros)


def _rope_swap_matrix():
    per_axis = MLA_ROPE // 2
    swap = np.zeros((HEAD_PAD, HEAD_PAD), np.float32)
    for j in range(per_axis):
        swap[MLA_NOPE + per_axis + j, MLA_NOPE + j] = 1.0
        swap[MLA_NOPE + j, MLA_NOPE + per_axis + j] = 1.0
    return swap


def _head_norm_rope(t, gain, cos, sin, ones, swap):
    ms = jnp.dot((t * t).astype(BF16), ones, preferred_element_type=F32) * (1.0 / MLA_QK)
    t = t * lax.rsqrt(ms + EPS) * gain
    return t * cos + jnp.dot(t.astype(BF16), swap, preferred_element_type=F32) * sin


def _mla_in_kernel(x_ref, mod_ref, g_ref, wdq_ref, wdkv_ref, wpe_ref, qan_ref, kvan_ref, wuq_ref, wuk_ref, wuv_ref,
                   qn_ref, kn_ref, ones_ref, swap_ref, cos_ref, sin_ref, q_ref, k_ref, v_ref):
    h = _modulated_norm(x_ref[...], g_ref[...], mod_ref[0], 0, 1).astype(BF16)
    dq = jnp.dot(h, wdq_ref[...], preferred_element_type=F32)
    dkv = jnp.dot(h, wdkv_ref[...], preferred_element_type=F32)
    kpe = jnp.dot(h, wpe_ref[...], preferred_element_type=F32)
    qa = dq * lax.rsqrt(jnp.mean(dq * dq, axis=-1, keepdims=True) + EPS) * qan_ref[...]
    kva = (dkv * lax.rsqrt(jnp.mean(dkv * dkv, axis=-1, keepdims=True) + EPS) * kvan_ref[...]).astype(BF16)
    q = jnp.dot(qa.astype(BF16), wuq_ref[...], preferred_element_type=F32)
    kn = jnp.dot(kva, wuk_ref[...], preferred_element_type=F32)
    v_ref[...] = jnp.dot(kva, wuv_ref[...], preferred_element_type=F32).astype(BF16)
    cos, sin, ones, swap = cos_ref[...], sin_ref[...], ones_ref[...], swap_ref[...]
    scale = MLA_QK ** -0.5 * math.log2(math.e)
    for hd in range(MLA_HEADS):
        hs = slice(hd * HEAD_PAD, (hd + 1) * HEAD_PAD)
        q_ref[:, hs] = (_head_norm_rope(q[:, hs], qn_ref[...], cos, sin, ones, swap) * scale).astype(BF16)
        k_ref[:, hs] = _head_norm_rope(kn[:, hs] + kpe, kn_ref[...], cos, sin, ones, swap).astype(BF16)


def mla_in(x_all, mod, p):
    rows = x_all.shape[0]
    tm = ROW_TILE
    w = p["w_dqkv"]
    wdq = w[:, :MLA_Q_LORA].astype(BF16)
    wdkv = w[:, MLA_Q_LORA:MLA_Q_LORA + MLA_KV_LORA].astype(BF16)
    wpe = jnp.pad(w[:, MLA_Q_LORA + MLA_KV_LORA:], ((0, 0), (MLA_NOPE, HEAD_PAD - MLA_QK))).astype(BF16)
    wuq = jnp.pad(p["w_uq"].reshape(MLA_Q_LORA, MLA_HEADS, MLA_QK), ((0, 0), (0, 0), (0, HEAD_PAD - MLA_QK)))
    wuq = wuq.reshape(MLA_Q_LORA, MLA_HEADS * HEAD_PAD).astype(BF16)
    wukv = p["w_ukv"].reshape(MLA_KV_LORA, MLA_HEADS, MLA_NOPE + MLA_V)
    wuk = jnp.pad(wukv[:, :, :MLA_NOPE], ((0, 0), (0, 0), (0, HEAD_PAD - MLA_NOPE)))
    wuk = wuk.reshape(MLA_KV_LORA, MLA_HEADS * HEAD_PAD).astype(BF16)
    wuv = wukv[:, :, MLA_NOPE:].reshape(MLA_KV_LORA, MLA_HEADS * MLA_V).astype(BF16)
    padg = lambda g: jnp.pad(g, (0, HEAD_PAD - MLA_QK))[None, :]
    cos, sin = _rope_tables(rows)
    row = lambda n: pl.BlockSpec((tm, n), lambda i: (i, 0))
    weights = [wdq, wdkv, wpe, p["q_a_norm"][None, :], p["kv_a_norm"][None, :], wuq, wuk, wuv,
               padg(p["q_norm"]), padg(p["k_norm"]),
               jnp.ones((HEAD_PAD, HEAD_PAD), BF16), jnp.asarray(_rope_swap_matrix(), BF16)]
    return pl.pallas_call(
        _mla_in_kernel,
        grid=(rows // tm,),
        in_specs=[row(D_MODEL),
                  pl.BlockSpec((1, 1, N_MOD * D_MODEL), lambda i: (_mod_row(i, tm), 0, 0)),
                  _full((1, D_MODEL))] + [_full(a.shape) for a in weights] + [row(HEAD_PAD)] * 2,
        out_specs=[row(MLA_HEADS * HEAD_PAD), row(MLA_HEADS * HEAD_PAD), row(MLA_HEADS * MLA_V)],
        out_shape=[jax.ShapeDtypeStruct((rows, MLA_HEADS * HEAD_PAD), BF16),
                   jax.ShapeDtypeStruct((rows, MLA_HEADS * HEAD_PAD), BF16),
                   jax.ShapeDtypeStruct((rows, MLA_HEADS * MLA_V), BF16)],
        compiler_params=_params("arbitrary"),
        name="mla_in",
    )(x_all, mod, p["norm1"][None, :], *weights, cos, sin)


ATTN_SAFE_SHIFT = 50.0
ATTN_BOUND_SLACK = 1.001
ATTN_BOUND_MARGIN = 1e-3


def _attn_kernel(q_ref, kc_ref, kl_ref, vc_ref, vl_ref, o_ref, knorm_ref, *, tq, tk):
    heads = (slice(0, HEAD_PAD), slice(HEAD_PAD, 2 * HEAD_PAD))
    qs = [q_ref[:, hs] for hs in heads]
    n_lat = SEQ // tk

    @pl.when(pl.program_id(2) == 0)
    def _key_norms():
        def sq_max(k2, hh):
            kf = k2[:, heads[hh]].astype(F32)
            return jnp.max(jnp.sum(kf * kf, axis=-1, keepdims=True), axis=0, keepdims=True)

        for hh in range(2):
            best = sq_max(kc_ref[...], hh)
            best = lax.fori_loop(
                0, n_lat, lambda j, b: jnp.maximum(b, sq_max(kl_ref[pl.ds(pl.multiple_of(j * tk, tk), tk), :], hh)), best)
            knorm_ref[hh] = jnp.broadcast_to(jnp.sqrt(best), (SUBLANES, LANES))

    def values(v, hh):
        lane = lax.broadcasted_iota(jnp.int32, v.shape, 1)
        own = (lane < MLA_V) if hh == 0 else (lane >= MLA_V)
        return jnp.where(own, v, jnp.ones_like(v))

    def scores(hh, k2):
        return lax.dot_general(qs[hh], k2[:, heads[hh]], (((1,), (1,)), ((), ())), preferred_element_type=F32)

    def finish(accs):
        lane = lax.broadcasted_iota(jnp.int32, (tq, 2 * MLA_V), 1)
        outs = [acc / pltpu.roll(acc, MLA_V, 1) for acc in accs]
        o_ref[...] = jnp.where(lane < MLA_V, outs[0], outs[1]).astype(o_ref.dtype)

    def lat_chunk(j):
        off = pl.multiple_of(j * tk, tk)
        return kl_ref[pl.ds(off, tk), :], vl_ref[pl.ds(off, tk), :]

    bounds = []
    for hh in range(2):
        qf = qs[hh].astype(F32)
        qn = jnp.sqrt(jnp.sum(qf * qf, axis=-1, keepdims=True))
        bounds.append(qn * knorm_ref[hh, 0:1, 0:1] * ATTN_BOUND_SLACK + ATTN_BOUND_MARGIN)
    widest = jnp.max(jnp.maximum(bounds[0], bounds[1]))

    def fixed_shift():
        def step(k2, v, accs):
            return tuple(accs[hh] + jnp.dot(jnp.exp2(scores(hh, k2) - bounds[hh]).astype(BF16), values(v, hh),
                                            preferred_element_type=F32) for hh in range(2))

        zero = jnp.zeros((tq, 2 * MLA_V), F32)
        accs = step(kc_ref[...], vc_ref[...], (zero, zero))
        finish(lax.fori_loop(0, n_lat, lambda j, a: step(*lat_chunk(j), a), accs, unroll=16))

    def running_max():
        def step(k2, v, carry):
            new = []
            for hh in range(2):
                m, acc = carry[hh]
                s = scores(hh, k2)
                m_new = jnp.maximum(m, jnp.max(s, axis=-1, keepdims=True))
                p = jnp.exp2(s - m_new).astype(BF16)
                acc = jnp.exp2(m - m_new) * acc + jnp.dot(p, values(v, hh), preferred_element_type=F32)
                new.append((m_new, acc))
            return tuple(new)

        init = tuple((jnp.full((tq, 1), NEG_INF, F32), jnp.zeros((tq, 2 * MLA_V), F32)) for _ in range(2))
        carry = step(kc_ref[...], vc_ref[...], init)
        carry = lax.fori_loop(0, n_lat, lambda j, c: step(*lat_chunk(j), c), carry)
        finish([acc for _, acc in carry])

    lax.cond(widest <= ATTN_SAFE_SHIFT, fixed_shift, running_max)


def attention(q, k, v):
    tq, tk = ATTN_TQ, ATTN_TK
    nq = SEQ // tq
    ctx0 = LAT_ROWS // CTX_LEN
    return pl.pallas_call(
        functools.partial(_attn_kernel, tq=tq, tk=tk),
        grid=(BATCH, MLA_HEADS // 2, nq),
        in_specs=[pl.BlockSpec((tq, 2 * HEAD_PAD), lambda b, h, i: (b * nq + i, h)),
                  pl.BlockSpec((CTX_LEN, 2 * HEAD_PAD), lambda b, h, i: (ctx0 + b, h)),
                  pl.BlockSpec((SEQ, 2 * HEAD_PAD), lambda b, h, i: (b, h)),
                  pl.BlockSpec((CTX_LEN, 2 * MLA_V), lambda b, h, i: (ctx0 + b, h)),
                  pl.BlockSpec((SEQ, 2 * MLA_V), lambda b, h, i: (b, h))],
        out_specs=pl.BlockSpec((tq, 2 * MLA_V), lambda b, h, i: (b * nq + i, h)),
        out_shape=jax.ShapeDtypeStruct((LAT_ROWS, MLA_HEADS * MLA_V), BF16),
        scratch_shapes=[pltpu.VMEM((2, SUBLANES, LANES), F32)],
        compiler_params=_params("arbitrary", "arbitrary", "arbitrary"),
        name="attention",
    )(q, k, k, v, v)


def _attn_out_kernel(x_ref, mod_ref, o_ref_in, wo_ref, out_ref):
    gate = mod_ref[0][:, 2 * D_MODEL:3 * D_MODEL]
    out_ref[...] = x_ref[...] + gate * jnp.dot(o_ref_in[...], wo_ref[...], preferred_element_type=F32)


def attn_out(x_all, mod, o, w_o):
    tm = ROW_TILE
    row = lambda n: pl.BlockSpec((tm, n), lambda i: (i, 0))
    return pl.pallas_call(
        _attn_out_kernel,
        grid=(LAT_ROWS // tm,),
        in_specs=[row(D_MODEL), pl.BlockSpec((1, 1, N_MOD * D_MODEL), lambda i: (_mod_row(i, tm), 0, 0)),
                  row(MLA_HEADS * MLA_V), _full(w_o.shape)],
        out_specs=row(D_MODEL),
        out_shape=jax.ShapeDtypeStruct((LAT_ROWS, D_MODEL), F32),
        compiler_params=_params("arbitrary"),
        name="attn_out",
    )(x_all, mod, o, w_o.astype(BF16))


def kernel(x, c, ctx, c_ctx, l0_w_ada, l0_b_ada, l0_norm1, l0_w_in, l0_conv_w, l0_conv_b, l0_dt_bias, l0_a_log, l0_d_skip, l0_ssd_norm, l0_w_out, l0_norm2, l0_peer_wq, l0_peer_keys, l0_peer_u, l0_peer_v, l1_w_ada, l1_b_ada, l1_norm1, l1_w_dqkv, l1_q_a_norm, l1_kv_a_norm, l1_w_uq, l1_w_ukv, l1_q_norm, l1_k_norm, l1_w_o, l1_norm2, l1_peer_wq, l1_peer_keys, l1_peer_u, l1_peer_v):
    x_lat, x_ctx = x.reshape(LAT_ROWS, D_MODEL), ctx.reshape(BATCH * CTX_LEN, D_MODEL)

    mod0 = ada_table(c, c_ctx, l0_w_ada, l0_b_ada)
    f, z, xbc, dt = even_in(x_lat, x_ctx, mod0, l0_norm1, l0_w_in)
    u = conv_silu(xbc, l0_conv_w, l0_conv_b)
    yf, yb = ssd_scan(u, dt, l0_dt_bias, l0_a_log)
    four_lat, four_ctx = fourier_mix(f)
    x_all = even_out(x_lat, x_ctx, four_lat, four_ctx, mod0, yf, yb, u, z, l0_d_skip, l0_ssd_norm, l0_w_out)
    x_all = peer_layer(x_all, mod0, l0_norm2, l0_peer_wq, l0_peer_keys, l0_peer_u, l0_peer_v)

    mod1 = ada_table(c, c_ctx, l1_w_ada, l1_b_ada)
    p1 = dict(norm1=l1_norm1, w_dqkv=l1_w_dqkv, q_a_norm=l1_q_a_norm, kv_a_norm=l1_kv_a_norm, w_uq=l1_w_uq,
              w_ukv=l1_w_ukv, q_norm=l1_q_norm, k_norm=l1_k_norm)
    q, k, v = mla_in(x_all, mod1, p1)
    o = attention(q, k, v)
    x_lat = attn_out(x_all, mod1, o, l1_w_o)
    x_lat = peer_layer(x_lat, mod1, l1_norm2, l1_peer_wq, l1_peer_keys, l1_peer_u, l1_peer_v)
    return x_lat.reshape(BATCH, SEQ, D_MODEL)
```
